```python
import jax
import jax.numpy as jnp
from jax import lax
import numpy as np

D_MODEL = 1024
BATCH = 16
SEQ = 256
DEPTH = 4
DEC_BATCH = 2
DEC_SEQ = 4096
PAST_LEN = 512

GRID_W = 64
HEAD_DIM = 64
N_DIR = 2
A_HEADS = 4
A_WIDTH = A_HEADS * HEAD_DIM
A_DECAY_RANK = 64
A_ICLR_RANK = 64
A_GATE_RANK = 128
B_HEADS = 4
B_NOPE = 64
B_ROPE = 32
B_VDIM = 64
B_Q_RANK = 256
B_KV_RANK = 128
B_WIDTH = B_HEADS * B_VDIM
C_BLOCKS = 4
C_WIDTH = C_BLOCKS * HEAD_DIM
C_CONV = 4
C_POW = 8.0
D_HEADS = 4
D_WIDTH = D_HEADS * HEAD_DIM
D_MIX = A_WIDTH + B_WIDTH + C_WIDTH + D_WIDTH
IN_SIZES = (A_WIDTH, A_WIDTH, A_WIDTH, A_DECAY_RANK, A_DECAY_RANK, A_ICLR_RANK, A_ICLR_RANK, A_GATE_RANK,
            B_Q_RANK, B_KV_RANK, B_ROPE,
            C_WIDTH, C_WIDTH,
            D_WIDTH, D_WIDTH, D_WIDTH, D_WIDTH, D_WIDTH)
D_IN = 3 * A_WIDTH + 2 * A_DECAY_RANK + 2 * A_ICLR_RANK + A_GATE_RANK + B_Q_RANK + B_KV_RANK + B_ROPE + 2 * C_WIDTH + 5 * D_WIDTH
D_FF = 2816
N_MOD = 9
CHUNK = 64
Q_BLOCK = 128
ROPE_BASE = 10000.0
LN_EPS = 1e-5
RMS_EPS = 1e-6
RWKV_GN_EPS = 64e-5
DN_ALPHA = (2 * DEPTH) ** 0.25
DN_BETA = (8 * DEPTH) ** -0.25

kernel_name = 'hybrid_diffusion_prefix_step'


def _layer_norm(x, g, b):
    xf = x.astype(jnp.float32)
    mu = jnp.mean(xf, -1, keepdims=True)
    var = jnp.mean(jnp.square(xf - mu), -1, keepdims=True)
    return ((xf - mu) * lax.rsqrt(var + LN_EPS)).astype(x.dtype) * g + b


def _rms_norm(x, g):
    xf = x.astype(jnp.float32)
    return (xf * lax.rsqrt(jnp.mean(jnp.square(xf), -1, keepdims=True) + RMS_EPS)).astype(x.dtype) * g


def _swiglu(h, w_in, w_out):
    gate, up = jnp.split(h @ w_in, 2, axis=-1)
    return (jax.nn.silu(gate) * up) @ w_out


def _axial_rope_tables(n_tokens):
    rows = n_tokens // GRID_W
    row = jnp.repeat(jnp.arange(rows, dtype=jnp.float32), GRID_W)
    col = jnp.tile(jnp.arange(GRID_W, dtype=jnp.float32), rows)
    half = B_ROPE // 2
    inv = jnp.power(ROPE_BASE, -jnp.arange(0, half, 2, dtype=jnp.float32) / half)
    ang_r = row[:, None] * inv
    ang_c = col[:, None] * inv
    ang = jnp.concatenate([ang_r, ang_r, ang_c, ang_c], axis=-1)
    return jnp.cos(ang), jnp.sin(ang)


def _rot_half(x):
    x1, x2 = jnp.split(x, 2, axis=-1)
    return jnp.concatenate([-x2, x1], axis=-1)


def _apply_axial_rope(x, cos, sin):
    xr, xc = jnp.split(x, 2, axis=-1)
    rot = jnp.concatenate([_rot_half(xr), _rot_half(xc)], axis=-1)
    return (x * cos + rot * sin).astype(x.dtype)


def _attend(q, k, v, scale):
    bsz, t, h, dk = q.shape
    dv = v.shape[-1]
    nb = t // Q_BLOCK
    qb = jnp.moveaxis(q.reshape(bsz, nb, Q_BLOCK, h, dk), 1, 0)

    def one_block(qi):
        s = jnp.einsum('bqhd,bkhd->bhqk', qi, k).astype(jnp.float32) * scale
        p = jax.nn.softmax(s, axis=-1).astype(v.dtype)
        return jnp.einsum('bhqk,bkhd->bqhd', p, v)

    o = lax.map(one_block, qb)
    return jnp.moveaxis(o, 0, 1).reshape(bsz, t, h, dv)


def _rwkv_scan(s0, r, w, k, v, kk, a, reverse):
    def step(s, inp):
        r_t, w_t, k_t, v_t, kk_t, a_t = inp
        s_kk = jnp.einsum('bhvk,bhk->bhv', s, kk_t)
        s = (s * w_t[:, :, None, :] - s_kk[..., None] * (kk_t * a_t)[:, :, None, :]
             + v_t[..., None] * k_t[:, :, None, :]).astype(s0.dtype)
        return s, jnp.einsum('bhvk,bhk->bhv', s, r_t)

    xs = tuple(jnp.moveaxis(z, 1, 0) for z in (r, w, k, v, kk, a))
    s_fin, o = lax.scan(step, s0, xs, reverse=reverse)
    return jnp.moveaxis(o, 0, 1), s_fin


def _rwkv_mix(r, k, v, xw, xa, xg, p, s0):
    bsz, t, _ = r.shape
    heads = lambda z: z.reshape(bsz, t, A_HEADS, HEAD_DIM)
    kk = heads(k * p['rwkv_kk']).astype(jnp.float32)
    kk = (kk / jnp.maximum(jnp.sqrt(jnp.sum(jnp.square(kk), -1, keepdims=True)), 1e-12)).astype(k.dtype)
    outs, bonus, finals = [], [], []
    for d in range(N_DIR):
        w_log = -jax.nn.softplus(-(p['rwkv_w0'][d] + jnp.tanh(xw[d]) @ p['rwkv_w2'][d])) - 0.5
        a = jax.nn.sigmoid(p['rwkv_a0'][d] + xa[d] @ p['rwkv_a2'][d])
        k_d = k * (1 + (a - 1) * p['rwkv_ka'])
        o, s_fin = _rwkv_scan(s0[:, d], heads(r), heads(jnp.exp(-jnp.exp(w_log))), heads(k_d), heads(v),
                              kk, heads(a), d == 1)
        outs.append(o)
        finals.append(s_fin)
        bonus.append(jnp.sum(heads(r * k_d * p['rwkv_rk']), -1, keepdims=True) * heads(v))
    of = (outs[0] + outs[1]).astype(jnp.float32)
    mu = jnp.mean(of, -1, keepdims=True)
    var = jnp.mean(jnp.square(of - mu), -1, keepdims=True)
    gn = ((of - mu) * lax.rsqrt(var + RWKV_GN_EPS)).astype(r.dtype) * p['rwkv_gn_g'].reshape(A_HEADS, HEAD_DIM) \
        + p['rwkv_gn_b'].reshape(A_HEADS, HEAD_DIM)
    g = jax.nn.sigmoid(xg) @ p['rwkv_g2']
    y = (gn + bonus[0] + bonus[1]).reshape(bsz, t, A_WIDTH) * g
    return y, jnp.stack(finals, axis=1)


def _lin_combine(e1, e2):
    a1, b1 = e1
    a2, b2 = e2
    return a1 * a2, a2 * b1 + b2


def _rglru_mix(xc, xgate, p, h0):
    bsz, t, _ = xc.shape
    u = lax.conv_general_dilated(xc, p['rglru_conv_w'][:, None, :], window_strides=(1,),
                                 padding=((C_CONV // 2, C_CONV // 2 - 1),),
                                 dimension_numbers=('NWC', 'WIO', 'NWC'),
                                 feature_group_count=C_WIDTH) + p['rglru_conv_b']
    ub = u.reshape(bsz, t, C_BLOCKS, HEAD_DIM)
    bd = lambda w, b: jnp.einsum('btgi,gij->btgj', ub, w).reshape(bsz, t, C_WIDTH) + b
    hs, finals = [], []
    for d in range(N_DIR):
        r = jax.nn.sigmoid(bd(p['rglru_wa'][d], p['rglru_ba'][d]))
        i = jax.nn.sigmoid(bd(p['rglru_wx'][d], p['rglru_bx'][d]))
        log_a = -C_POW * r * jax.nn.softplus(-p['rglru_lam'][d])
        a = jnp.exp(log_a)
        b = jnp.sqrt(-jnp.expm1(2.0 * log_a)) * (i * u)
        a_cum, b_cum = lax.associative_scan(_lin_combine, (a, b), reverse=(d == 1), axis=1)
        h = a_cum * h0[:, d, None, :] + b_cum
        hs.append(h)
        finals.append(h[:, 0] if d == 1 else h[:, -1])
    return (hs[0] + hs[1]) * jax.nn.gelu(xgate), jnp.stack(finals, axis=1)


def _gla_chunkwise(s0, q, k, v, log_g, reverse):
    if reverse:
        q, k, v, log_g = [jnp.flip(z, axis=1) for z in (q, k, v, log_g)]
    bsz, t, h, _ = q.shape
    n = t // CHUNK
    blocks = lambda z: z.reshape(bsz, n, CHUNK, h, z.shape[-1]).transpose(1, 0, 3, 2, 4)
    causal = jnp.tril(jnp.ones((CHUNK, CHUNK), dtype=bool))[:, :, None]

    def step(s, inp):
        qc, kc, vc, gc = inp
        b = jnp.cumsum(gc.astype(jnp.float32), axis=2)
        dec = jnp.exp(jnp.where(causal, b[:, :, :, None, :] - b[:, :, None, :, :], -jnp.inf))
        att = jnp.einsum('bhtk,bhtsk,bhsk->bhts', qc, dec, kc)
        o = jnp.einsum('bhtk,bhkv->bhtv', qc * jnp.exp(b), s) + jnp.einsum('bhts,bhsv->bhtv', att, vc)
        b_last = b[:, :, -1:, :]
        s = (jnp.exp(b_last[:, :, 0, :, None]) * s
             + jnp.einsum('bhsk,bhsv->bhkv', kc * jnp.exp(b_last - b), vc)).astype(s0.dtype)
        return s, o.astype(vc.dtype)

    s_fin, o = lax.scan(step, s0, tuple(blocks(z) for z in (q, k, v, log_g)))
    o = o.transpose(1, 0, 3, 2, 4).reshape(bsz, t, h, v.shape[-1])
    return (jnp.flip(o, axis=1) if reverse else o), s_fin


def _hgrn_mix(xq, xf, xi, xg, p, s0):
    bsz, t, _ = xq.shape
    heads = lambda z: z.reshape(bsz, t, D_HEADS, HEAD_DIM)
    q = heads(jax.nn.silu(xq))
    v = heads(xi)
    lb = p['hgrn_lb']
    outs, finals = [], []
    for d in range(N_DIR):
        g = lb[d] + (1 - lb[d]) * jax.nn.sigmoid(xf[d])
        o, s_fin = _gla_chunkwise(s0[:, d], q, heads(1 - g), v, heads(jnp.log(g)), d == 1)
        outs.append(o)
        finals.append(s_fin)
    y = _rms_norm(outs[0] + outs[1], p['hgrn_gn_g'].reshape(D_HEADS, HEAD_DIM)) * heads(jax.nn.silu(xg))
    return y.reshape(bsz, t, D_WIDTH), jnp.stack(finals, axis=1)


def _mla_project(xcq, xckv, p):
    bsz, t, _ = xcq.shape
    q = (_rms_norm(xcq, p['mla_qn_g']) @ p['mla_w_uq']).reshape(bsz, t, B_HEADS, B_NOPE + B_ROPE)
    return q, _rms_norm(xckv, p['mla_kvn_g'])


def _mla_keys(ckv, kpe, p):
    bsz, s, _ = ckv.shape
    kv = (ckv @ p['mla_w_ukv']).reshape(bsz, s, B_HEADS, B_NOPE + B_VDIM)
    k = jnp.concatenate([kv[..., :B_NOPE], jnp.broadcast_to(kpe[:, :, None, :], (bsz, s, B_HEADS, B_ROPE))], axis=-1)
    return k, kv[..., B_NOPE:]


def _mixer(h, p, ctx):
    bsz, t, _ = h.shape
    points = [int(i) for i in np.cumsum(IN_SIZES)[:-1]]
    (a_r, a_k, a_v, a_wf, a_wb, a_af, a_ab, a_g, b_cq, b_ckv, b_kpe,
     c_x, c_gate, d_q, d_ff, d_fb, d_i, d_g) = jnp.split(h @ p['w_in'], points, axis=-1)
    if ctx is None:
        s_rwkv0 = jnp.zeros((bsz, N_DIR, A_HEADS, HEAD_DIM, HEAD_DIM), h.dtype)
        h_rglru0 = jnp.zeros((bsz, N_DIR, C_WIDTH), h.dtype)
        s_hgrn0 = jnp.zeros((bsz, N_DIR, D_HEADS, HEAD_DIM, HEAD_DIM), h.dtype)
    else:
        ctx_ckv, ctx_kpe, s_rwkv0, h_rglru0, s_hgrn0 = ctx
    ya, s_rwkv = _rwkv_mix(a_r, a_k, a_v, (a_wf, a_wb), (a_af, a_ab), a_g, p, s_rwkv0)
    q, ckv = _mla_project(b_cq, b_ckv, p)
    if ctx is None:
        k, v = _mla_keys(ckv, b_kpe, p)
    else:
        cos, sin = _axial_rope_tables(t)
        q = jnp.concatenate([q[..., :B_NOPE], _apply_axial_rope(q[..., B_NOPE:], cos[:, None], sin[:, None])], axis=-1)
        k_lat, v_lat = _mla_keys(ckv, _apply_axial_rope(b_kpe, cos, sin), p)
        k_ctx, v_ctx = _mla_keys(ctx_ckv, ctx_kpe, p)
        k = jnp.concatenate([k_lat, k_ctx], axis=1)
        v = jnp.concatenate([v_lat, v_ctx], axis=1)
    yb = _attend(q, k, v, (B_NOPE + B_ROPE) ** -0.5).reshape(bsz, t, B_WIDTH)
    yc, h_rglru = _rglru_mix(c_x, c_gate, p, h_rglru0)
    yd, s_hgrn = _hgrn_mix(d_q, (d_ff, d_fb), d_i, d_g, p, s_hgrn0)
    y = jnp.concatenate([ya, yb, yc, yd], axis=-1) @ p['w_out']
    return y, (ckv, b_kpe, s_rwkv, h_rglru, s_hgrn)


def _block(x, cond, p, ctx):
    mods = jnp.split(jax.nn.silu(cond) @ p['w_ada'] + p['b_ada'], N_MOD, axis=-1)
    sh0, sc0, g0, sh1, sc1, g1, sh2, sc2, g2 = [m[:, None, :] for m in mods]
    ffn = lambda hh, j: _swiglu(hh, p['w_ffn_in'][j], p['w_ffn_out'][j])
    x = _layer_norm(DN_ALPHA * x + 0.5 * g0 * ffn(x * (1 + sc0) + sh0, 0), p['ln_g'][0], p['ln_b'][0])
    mix, ctx_out = _mixer(x * (1 + sc1) + sh1, p, ctx)
    x = _layer_norm(DN_ALPHA * x + g1 * mix, p['ln_g'][1], p['ln_b'][1])
    x = _layer_norm(DN_ALPHA * x + 0.5 * g2 * ffn(x * (1 + sc2) + sh2, 1), p['ln_g'][2], p['ln_b'][2])
    return x, ctx_out


def _lower_bounds(lb_logits):
    sm = jax.nn.softmax(lb_logits.astype(jnp.float32), axis=0)
    return (jnp.cumsum(sm, axis=0) - sm[0]).astype(lb_logits.dtype)


def setup_inputs(seed: int = 0) -> dict:
    key = jax.random.key(seed)
    ks = iter(jax.random.split(key, 64))
    nrm = lambda shape, scale=1.0: scale * jax.random.normal(next(ks), shape, jnp.float32)
    unif = lambda shape, lo, hi: jax.random.uniform(next(ks), shape, jnp.float32, lo, hi)
    logit = lambda u: jnp.log(u) - jnp.log1p(-u)
    L = DEPTH
    return {
        'x_prompt': nrm((BATCH, SEQ, D_MODEL)),
        'x_sample': nrm((DEC_BATCH, DEC_SEQ, D_MODEL)),
        'cache_mla_ckv': nrm((DEC_BATCH, L, PAST_LEN, B_KV_RANK)),
        'cache_mla_kpe': nrm((DEC_BATCH, L, PAST_LEN, B_ROPE)),
        'state_rwkv': nrm((DEC_BATCH, L, N_DIR, A_HEADS, HEAD_DIM, HEAD_DIM), 0.3),
        'state_rglru': nrm((DEC_BATCH, L, N_DIR, C_WIDTH), 0.5),
        'state_hgrn': nrm((DEC_BATCH, L, N_DIR, D_HEADS, HEAD_DIM, HEAD_DIM), 0.3),
        'c': nrm((DEC_BATCH, D_MODEL)),
        'c_ctx': nrm((D_MODEL,)),
        'w_ada': nrm((L, D_MODEL, N_MOD * D_MODEL), 0.5 * D_MODEL ** -0.5),
        'b_ada': nrm((L, N_MOD * D_MODEL), 0.02),
        'ln_g': 1.0 + nrm((L, 3, D_MODEL), 0.02),
        'ln_b': nrm((L, 3, D_MODEL), 0.02),
        'w_ffn_in': nrm((L, 2, D_MODEL, 2 * D_FF), D_MODEL ** -0.5),
        'w_ffn_out': nrm((L, 2, D_FF, D_MODEL), DN_BETA * D_FF ** -0.5),
        'w_in': nrm((L, D_MODEL, D_IN), D_MODEL ** -0.5),
        'w_out': nrm((L, D_MIX, D_MODEL), DN_BETA * D_MIX ** -0.5),
        'rwkv_w0': unif((L, N_DIR, A_WIDTH), -6.5, -1.5),
        'rwkv_w2': nrm((L, N_DIR, A_DECAY_RANK, A_WIDTH), 0.1 * A_DECAY_RANK ** -0.5),
        'rwkv_a0': nrm((L, N_DIR, A_WIDTH), 0.1),
        'rwkv_a2': nrm((L, N_DIR, A_ICLR_RANK, A_WIDTH), 0.1 * A_ICLR_RANK ** -0.5),
        'rwkv_g2': nrm((L, A_GATE_RANK, A_WIDTH), A_GATE_RANK ** -0.5),
        'rwkv_kk': 0.85 + nrm((L, A_WIDTH), 0.02),
        'rwkv_ka': 1.0 + nrm((L, A_WIDTH), 0.02),
        'rwkv_rk': nrm((L, A_WIDTH), 0.1),
        'rwkv_gn_g': 1.0 + nrm((L, A_WIDTH), 0.02),
        'rwkv_gn_b': nrm((L, A_WIDTH), 0.02),
        'mla_qn_g': 1.0 + nrm((L, B_Q_RANK), 0.02),
        'mla_w_uq': nrm((L, B_Q_RANK, B_HEADS * (B_NOPE + B_ROPE)), B_Q_RANK ** -0.5),
        'mla_kvn_g': 1.0 + nrm((L, B_KV_RANK), 0.02),
        'mla_w_ukv': nrm((L, B_KV_RANK, B_HEADS * (B_NOPE + B_VDIM)), B_KV_RANK ** -0.5),
        'rglru_conv_w': nrm((L, C_CONV, C_WIDTH), C_CONV ** -0.5),
        'rglru_conv_b': nrm((L, C_WIDTH), 0.02),
        'rglru_wa': nrm((L, N_DIR, C_BLOCKS, HEAD_DIM, HEAD_DIM), HEAD_DIM ** -0.5),
        'rglru_ba': nrm((L, N_DIR, C_WIDTH), 0.02),
        'rglru_wx': nrm((L, N_DIR, C_BLOCKS, HEAD_DIM, HEAD_DIM), HEAD_DIM ** -0.5),
        'rglru_bx': nrm((L, N_DIR, C_WIDTH), 0.02),
        'rglru_lam': logit(unif((L, N_DIR, C_WIDTH), 0.9, 0.999)),
        'hgrn_lb': nrm((L, N_DIR, D_WIDTH), 0.1),
        'hgrn_gn_g': 1.0 + nrm((L, D_WIDTH), 0.02),
    }


def reference(x_prompt, x_sample, cache_mla_ckv, cache_mla_kpe, state_rwkv, state_rglru, state_hgrn, c, c_ctx,
              w_ada, b_ada, ln_g, ln_b, w_ffn_in, w_ffn_out, w_in, w_out,
              rwkv_w0, rwkv_w2, rwkv_a0, rwkv_a2, rwkv_g2, rwkv_kk, rwkv_ka, rwkv_rk, rwkv_gn_g, rwkv_gn_b,
              mla_qn_g, mla_w_uq, mla_kvn_g, mla_w_ukv,
              rglru_conv_w, rglru_conv_b, rglru_wa, rglru_ba, rglru_wx, rglru_bx, rglru_lam,
              hgrn_lb, hgrn_gn_g):
    lbs = _lower_bounds(hgrn_lb)

    def layer_params(l):
        return {'w_ada': w_ada[l], 'b_ada': b_ada[l], 'ln_g': ln_g[l], 'ln_b': ln_b[l],
                'w_ffn_in': w_ffn_in[l], 'w_ffn_out': w_ffn_out[l], 'w_in': w_in[l], 'w_out': w_out[l],
                'rwkv_w0': rwkv_w0[l], 'rwkv_w2': rwkv_w2[l], 'rwkv_a0': rwkv_a0[l], 'rwkv_a2': rwkv_a2[l],
                'rwkv_g2': rwkv_g2[l], 'rwkv_kk': rwkv_kk[l], 'rwkv_ka': rwkv_ka[l], 'rwkv_rk': rwkv_rk[l],
                'rwkv_gn_g': rwkv_gn_g[l], 'rwkv_gn_b': rwkv_gn_b[l],
                'mla_qn_g': mla_qn_g[l], 'mla_w_uq': mla_w_uq[l], 'mla_kvn_g': mla_kvn_g[l], 'mla_w_ukv': mla_w_ukv[l],
                'rglru_conv_w': rglru_conv_w[l], 'rglru_conv_b': rglru_conv_b[l],
                'rglru_wa': rglru_wa[l], 'rglru_ba': rglru_ba[l], 'rglru_wx': rglru_wx[l], 'rglru_bx': rglru_bx[l],
                'rglru_lam': rglru_lam[l], 'hgrn_lb': lbs[l], 'hgrn_gn_g': hgrn_gn_g[l]}

    x = x_prompt
    ctx_tensors = []
    for l in range(DEPTH):
        x, ctx_l = _block(x, c_ctx[None, :], layer_params(l), None)
        ctx_tensors.append(ctx_l)
    y_prompt = x
    new_mla_ckv = jnp.stack([e[0] for e in ctx_tensors], axis=1)
    new_mla_kpe = jnp.stack([e[1] for e in ctx_tensors], axis=1)
    new_rwkv = jnp.stack([e[2] for e in ctx_tensors], axis=1)
    new_rglru = jnp.stack([e[3] for e in ctx_tensors], axis=1)
    new_hgrn = jnp.stack([e[4] for e in ctx_tensors], axis=1)

    x = x_sample
    for l in range(DEPTH):
        cached = (cache_mla_ckv[:, l], cache_mla_kpe[:, l], state_rwkv[:, l], state_rglru[:, l], state_hgrn[:, l])
        x, _ = _block(x, c, layer_params(l), cached)
    y_sample = x
    return (y_prompt, y_sample, new_mla_ckv, new_mla_kpe, new_rwkv, new_rglru, new_hgrn)
```

```python
import functools

import numpy as np
import jax
import jax.numpy as jnp
from jax import lax
from jax.experimental import pallas as pl
from jax.experimental.pallas import tpu as pltpu

D_MODEL = 1024
DEPTH = 4
GRID_W = 64
HEAD_DIM = 64
N_HEADS = 4
WIDTH = N_HEADS * HEAD_DIM
A_DECAY_RANK = 64
A_ICLR_RANK = 64
A_GATE_RANK = 128
B_NOPE = 64
B_ROPE = 32
B_VDIM = 64
B_Q_RANK = 256
B_KV_RANK = 128
C_CONV = 4
C_POW = 8.0
D_FF = 2816
N_MOD = 9
ROPE_BASE = 10000.0
LN_EPS = 1e-5
RMS_EPS = 1e-6
RWKV_GN_EPS = 64e-5
DN_ALPHA = (2 * DEPTH) ** 0.25

LANE = 128
SLAB = 128
CHUNK = 64
LRU_CHUNK = 256
TM = 512
TF = 1408
TQ = 256
VMEM_LIMIT = 56 * 1024 * 1024

F32 = jnp.float32
BF16 = jnp.bfloat16
HI = lax.Precision.HIGHEST
NEG = -1e30


def _cp(sem):
    return pltpu.CompilerParams(dimension_semantics=sem, vmem_limit_bytes=VMEM_LIMIT)


def _dot(a, b, hi=False):
    return jnp.dot(a, b, preferred_element_type=F32, precision=HI if hi else None)


def _dot_nt(a, b, hi=False):
    return lax.dot_general(a, b, (((1,), (1,)), ((), ())), preferred_element_type=F32,
                           precision=HI if hi else None)


def _dot_tn(a, b, hi=False):
    return lax.dot_general(a, b, (((0,), (0,)), ((), ())), preferred_element_type=F32,
                           precision=HI if hi else None)


def _sigmoid(x):
    return 1.0 / (1.0 + jnp.exp(-x))


def _silu(x):
    return x * _sigmoid(x)


def _softplus(x):
    return jnp.maximum(x, 0.0) + jnp.log(1.0 + jnp.exp(-jnp.abs(x)))


def _tile4(y):
    return jnp.concatenate([y, y, y, y], axis=0)


def _head_consts():
    r = np.arange(WIDTH)
    same = (r[:, None] // HEAD_DIM) == (r[None, :] // HEAD_DIM)
    t = np.arange(CHUNK)[:, None]
    j = (np.arange(WIDTH) % CHUNK)[None, :]
    tt = np.arange(CHUNK)
    return {
        'mbd': same.astype(np.float32),
        'low_s': (j < t).astype(np.float32), 'low_i': (j <= t).astype(np.float32),
        'up_s': (j > t).astype(np.float32), 'up_i': (j >= t).astype(np.float32),
        'eye': (j == t).astype(np.float32),
        'tri_f': (tt[None, :] <= tt[:, None]).astype(np.float32),
        'tri_b': (tt[None, :] >= tt[:, None]).astype(np.float32),
    }


def _lb_kernel(x_ref, o_ref):
    x = x_ref[...]
    m = jnp.max(x, axis=0, keepdims=True)
    e = jnp.exp(x - m)
    sm = e / jnp.sum(e, axis=0, keepdims=True)
    run = sm[0:1]
    rows = [run - sm[0:1]]
    for l in range(1, DEPTH):
        run = run + sm[l:l + 1]
        rows.append(run - sm[0:1])
    o_ref[...] = jnp.concatenate(rows, axis=0)


def _lower_bounds(hgrn_lb):
    flat = hgrn_lb.reshape(DEPTH, 2 * WIDTH)
    out = pl.pallas_call(_lb_kernel, out_shape=jax.ShapeDtypeStruct(flat.shape, F32), name='hgrn_lb')(flat)
    return out.reshape(DEPTH, 2, 1, WIDTH)


def _ada_kernel(c_ref, w_ref, b_ref, o_ref):
    h = _silu(c_ref[...]).astype(BF16)
    o_ref[0] = _dot(h, w_ref[0].astype(BF16)) + b_ref[0]


def _ada(cond8, w_ada, b_ada):
    n = N_MOD * D_MODEL
    tn = 2304
    return pl.pallas_call(
        _ada_kernel,
        grid=(DEPTH, n // tn),
        in_specs=[pl.BlockSpec((8, D_MODEL), lambda l, j: (0, 0)),
                  pl.BlockSpec((1, D_MODEL, tn), lambda l, j: (l, 0, j)),
                  pl.BlockSpec((1, 1, tn), lambda l, j: (l, 0, j))],
        out_specs=pl.BlockSpec((1, 8, tn), lambda l, j: (l, 0, j)),
        out_shape=jax.ShapeDtypeStruct((DEPTH, 8, n), F32),
        compiler_params=_cp(("arbitrary", "arbitrary")),
        name='ada',
    )(cond8, w_ada, b_ada.reshape(DEPTH, 1, n))


def _layer_norm_rows(y, g, b):
    mu = jnp.mean(y, axis=-1, keepdims=True)
    d = y - mu
    var = jnp.mean(d * d, axis=-1, keepdims=True)
    return d * lax.rsqrt(var + LN_EPS) * g + b


def _ffn_kernel(x_ref, sh_ref, sc_ref, g_ref, lng_ref, lnb_ref, wg_ref, wu_ref, wo_ref, o_ref, h_scr, acc_scr):
    j = pl.program_id(1)

    @pl.when(j == 0)
    def _():
        h_scr[...] = (x_ref[...] * (1.0 + sc_ref[0]) + sh_ref[0]).astype(BF16)
        acc_scr[...] = jnp.zeros_like(acc_scr)

    h = h_scr[...]
    gate = _dot(h, wg_ref[...])
    up = _dot(h, wu_ref[...])
    acc_scr[...] += _dot((_silu(gate) * up).astype(BF16), wo_ref[...])

    @pl.when(j == pl.num_programs(1) - 1)
    def _():
        y = DN_ALPHA * x_ref[...] + 0.5 * g_ref[0] * acc_scr[...]
        o_ref[...] = _layer_norm_rows(y, lng_ref[...], lnb_ref[...])


def _ffn(x, sh, sc, g, lng, lnb, w_in_bf, w_out_bf, l, f, tiles_per_seg, tm):
    rows = x.shape[0]
    nj = D_FF // TF
    mod = pl.BlockSpec((1, 1, D_MODEL), lambda i, j: (i // tiles_per_seg, 0, 0))
    vec = pl.BlockSpec((1, D_MODEL), lambda i, j: (0, 0))
    return pl.pallas_call(
        _ffn_kernel,
        grid=(rows // tm, nj),
        in_specs=[pl.BlockSpec((tm, D_MODEL), lambda i, j: (i, 0)), mod, mod, mod, vec, vec,
                  pl.BlockSpec((None, None, D_MODEL, TF), lambda i, j: (l, f, 0, j)),
                  pl.BlockSpec((None, None, D_MODEL, TF), lambda i, j: (l, f, 0, nj + j)),
                  pl.BlockSpec((None, None, TF, D_MODEL), lambda i, j: (l, f, j, 0))],
        out_specs=pl.BlockSpec((tm, D_MODEL), lambda i, j: (i, 0)),
        out_shape=jax.ShapeDtypeStruct((rows, D_MODEL), F32),
        scratch_shapes=[pltpu.VMEM((tm, D_MODEL), BF16), pltpu.VMEM((tm, D_MODEL), F32)],
        compiler_params=_cp(("parallel", "arbitrary")),
        name='ffn',
    )(x, sh, sc, g, lng, lnb, w_in_bf, w_in_bf, w_out_bf)


ZA = 3 * WIDTH + 2 * A_DECAY_RANK + 2 * A_ICLR_RANK + A_GATE_RANK
ZB = B_Q_RANK + B_KV_RANK + LANE
ZC = 2 * WIDTH
ZD = 5 * WIDTH
ZTOT = ZA + ZB + ZC + ZD


def _inproj_kernel(x_ref, sh_ref, sc_ref, w_ref, za_ref, zb_ref, zc_ref, zd_ref):
    h = (x_ref[...] * (1.0 + sc_ref[0]) + sh_ref[0]).astype(BF16)
    za_ref[...] = _dot(h, w_ref[:, 0:ZA])
    zb_ref[...] = _dot(h, w_ref[:, ZA:ZA + ZB])
    zc_ref[...] = _dot(h, w_ref[:, ZA + ZB:ZA + ZB + ZC])
    zd_ref[...] = _dot(h, w_ref[:, ZA + ZB + ZC:ZTOT])


def _inproj(x, sh, sc, w_in_p, l, tiles_per_seg, tm):
    rows = x.shape[0]
    mod = pl.BlockSpec((1, 1, D_MODEL), lambda i: (i // tiles_per_seg, 0, 0))
    row = lambda w: pl.BlockSpec((tm, w), lambda i: (i, 0))
    return pl.pallas_call(
        _inproj_kernel,
        grid=(rows // tm,),
        in_specs=[row(D_MODEL), mod, mod, pl.BlockSpec((None, D_MODEL, ZTOT), lambda i: (l, 0, 0))],
        out_specs=[row(ZA), row(ZB), row(ZC), row(ZD)],
        out_shape=[jax.ShapeDtypeStruct((rows, w), F32) for w in (ZA, ZB, ZC, ZD)],
        compiler_params=_cp(("parallel",)),
        name='mixer_in',
    )(x, sh, sc, w_in_p)


def _gelu_tanh(x):
    return 0.5 * x * (1.0 + jnp.tanh(np.sqrt(2.0 / np.pi) * (x + 0.044715 * (x * x * x))))


def _outproj_kernel(x_ref, ya_ref, yb_ref, hf_ref, hb_ref, cg_ref, yd_ref, w_ref, g_ref, lng_ref, lnb_ref, o_ref):
    yc = (hf_ref[...] + hb_ref[...]) * _gelu_tanh(cg_ref[...])
    y = _dot(ya_ref[...].astype(BF16), w_ref[0:WIDTH, :])
    y += _dot(yb_ref[...].astype(BF16), w_ref[WIDTH:2 * WIDTH, :])
    y += _dot(yc.astype(BF16), w_ref[2 * WIDTH:3 * WIDTH, :])
    y += _dot(yd_ref[...].astype(BF16), w_ref[3 * WIDTH:4 * WIDTH, :])
    o_ref[...] = _layer_norm_rows(DN_ALPHA * x_ref[...] + g_ref[0] * y, lng_ref[...], lnb_ref[...])


def _outproj(x, ya, yb, hf, hb, zc, yd, w_out_bf, g, lng, lnb, l, tiles_per_seg, tm):
    rows = x.shape[0]
    mod = pl.BlockSpec((1, 1, D_MODEL), lambda i: (i // tiles_per_seg, 0, 0))
    vec = pl.BlockSpec((1, D_MODEL), lambda i: (0, 0))
    row = lambda w: pl.BlockSpec((tm, w), lambda i: (i, 0))
    return pl.pallas_call(
        _outproj_kernel,
        grid=(rows // tm,),
        in_specs=[row(D_MODEL), row(WIDTH), row(WIDTH), row(WIDTH), row(WIDTH),
                  pl.BlockSpec((tm, WIDTH), lambda i: (i, 1)), row(WIDTH),
                  pl.BlockSpec((None, 4 * WIDTH, D_MODEL), lambda i: (l, 0, 0)), mod, vec, vec],
        out_specs=row(D_MODEL),
        out_shape=jax.ShapeDtypeStruct((rows, D_MODEL), F32),
        compiler_params=_cp(("parallel",)),
        name='mixer_out',
    )(x, ya, yb, hf, hb, zc, yd, w_out_bf, g, lng, lnb)


def _scan_maps(n_ctx_chunks, ctx_n, dec_n, n_ctx_seq):
    def n_of(i):
        return jnp.where(i < n_ctx_chunks, ctx_n, dec_n)

    def fwd(i):
        return i

    def bwd(i):
        n = n_of(i)
        return (i // n) * n + (n - 1 - i % n)

    def seq(i):
        return jnp.where(i < n_ctx_chunks, i // ctx_n, n_ctx_seq + (i - n_ctx_chunks) // dec_n)

    def first(i):
        return i % n_of(i) == 0

    def last(i):
        n = n_of(i)
        return i % n == n - 1

    return fwd, bwd, seq, first, last


def _rwkv_pre_kernel(za_ref, w0_ref, w2_ref, a0_ref, a2_ref, g2_ref, kkp_ref, kap_ref, rkp_ref, ebd_ref,
                     kap_o, lwf_o, lwb_o, kdf_o, kdb_o, bbf_o, bbb_o, bon_o, gate_o):
    r = za_ref[:, 0:WIDTH]
    k = za_ref[:, WIDTH:2 * WIDTH]
    v = za_ref[:, 2 * WIDTH:3 * WIDTH]
    o = 3 * WIDTH
    xw = (za_ref[:, o:o + A_DECAY_RANK], za_ref[:, o + A_DECAY_RANK:o + 2 * A_DECAY_RANK])
    o += 2 * A_DECAY_RANK
    xa = (za_ref[:, o:o + A_ICLR_RANK], za_ref[:, o + A_ICLR_RANK:o + 2 * A_ICLR_RANK])
    o += 2 * A_ICLR_RANK
    xg = za_ref[:, o:o + A_GATE_RANK]
    ebd = ebd_ref[...]

    kk = k * kkp_ref[...]
    nrm = jnp.sqrt(_dot(kk * kk, ebd, hi=True))
    kappa = kk / jnp.maximum(nrm, 1e-12)
    kap_o[...] = kappa
    bonus = jnp.zeros_like(r)
    for d, (lw_o, kd_o, bb_o) in enumerate(((lwf_o, kdf_o, bbf_o), (lwb_o, kdb_o, bbb_o))):
        w_log = -_softplus(-(w0_ref[d] + _dot(jnp.tanh(xw[d]).astype(BF16), w2_ref[d].astype(BF16)))) - 0.5
        lw_o[...] = -jnp.exp(w_log)
        a = _sigmoid(a0_ref[d] + _dot(xa[d].astype(BF16), a2_ref[d].astype(BF16)))
        k_d = k * (1.0 + (a - 1.0) * kap_ref[...])
        kd_o[...] = k_d
        bb_o[...] = kappa * a
        bonus += _dot(r * k_d * rkp_ref[...], ebd, hi=True) * v
    bon_o[...] = bonus
    gate_o[...] = _dot(_sigmoid(xg).astype(BF16), g2_ref[...].astype(BF16))


def _rwkv_pre(za, p, ebd, tm):
    rows = za.shape[0]
    row = lambda w: pl.BlockSpec((tm, w), lambda i: (i, 0))
    full = lambda a: pl.BlockSpec(a.shape, lambda i: (0,) * a.ndim)
    args = (p['w0'], p['w2'], p['a0'], p['a2'], p['g2'], p['kk'], p['ka'], p['rk'], ebd)
    return pl.pallas_call(
        _rwkv_pre_kernel,
        grid=(rows // tm,),
        in_specs=[row(ZA)] + [full(a) for a in args],
        out_specs=[row(WIDTH)] * 9,
        out_shape=[jax.ShapeDtypeStruct((rows, WIDTH), F32)] * 9,
        compiler_params=_cp(("parallel",)),
        name='rwkv_pre',
    )(za, *args)


def _rwkv_chunk(r, v, kap, lw, kd, bb, st, mbd, eye, m_strict, m_incl, tri):
    cum = _dot(tri, lw, hi=True)
    tot = jnp.sum(lw, axis=0, keepdims=True)
    g_in = jnp.exp(cum)
    g_inv = jnp.exp(-cum)
    g_ex = jnp.exp(cum - lw)
    g_end = jnp.exp(tot - cum)
    x = jnp.concatenate([kap * g_ex, r * g_in], axis=0)
    ab = _dot_nt(x, _tile4(bb * g_inv) * mbd, hi=True)
    ak = _dot_nt(x, _tile4(kd * g_inv) * mbd, hi=True)
    c = r.shape[0]
    strict = m_strict > 0.5
    incl = m_incl > 0.5
    a_ub = jnp.where(strict, ab[:c], 0.0)
    a_rb = jnp.where(incl, ab[c:], 0.0)
    a_uk = jnp.where(strict, ak[:c], 0.0)
    a_rk = jnp.where(incl, ak[c:], 0.0)

    def catmul(pc, q):
        return _dot(pc, _tile4(q) * mbd, hi=True)

    xp = -a_ub
    inv = eye + xp
    steps = int(np.log2(c)) - 1
    for _ in range(steps):
        xp = catmul(xp, xp)
        inv = inv + catmul(inv, xp)
    xs = _dot_nt(x, st, hi=True)
    u = -catmul(inv, xs[:c] + catmul(a_uk, v))
    o = xs[c:] + catmul(a_rb, u) + catmul(a_rk, v)
    upd = _dot_tn(jnp.concatenate([u, v], axis=0),
                  jnp.concatenate([bb * g_end, kd * g_end], axis=0), hi=True)
    st_new = st * jnp.exp(tot) + upd * mbd
    return o, st_new


def _rwkv_scan_kernel(first_fn, last_fn,
                      rf_ref, vf_ref, kapf_ref, lwf_ref, kdf_ref, bbf_ref,
                      rb_ref, vb_ref, kapb_ref, lwb_ref, kdb_ref, bbb_ref,
                      s0_ref, mbd_ref, eye_ref, lows_ref, lowi_ref, ups_ref, upi_ref, trif_ref, trib_ref,
                      of_ref, ob_ref, sfin_ref, st_scr):
    i = pl.program_id(0)

    @pl.when(first_fn(i))
    def _():
        st_scr[...] = s0_ref[0]

    mbd = mbd_ref[...]
    eye = eye_ref[...]
    o_f, st_f = _rwkv_chunk(rf_ref[...], vf_ref[...], kapf_ref[...], lwf_ref[...], kdf_ref[...], bbf_ref[...],
                            st_scr[0], mbd, eye, lows_ref[...], lowi_ref[...], trif_ref[...])
    o_b, st_b = _rwkv_chunk(rb_ref[...], vb_ref[...], kapb_ref[...], lwb_ref[...], kdb_ref[...], bbb_ref[...],
                            st_scr[1], mbd, eye, ups_ref[...], upi_ref[...], trib_ref[...])
    of_ref[...] = o_f
    ob_ref[...] = o_b
    st_scr[0] = st_f
    st_scr[1] = st_b

    @pl.when(last_fn(i))
    def _():
        sfin_ref[0] = st_scr[...]


def _rwkv_scan(za, pre, s0, hc, geom):
    kap, lwf, lwb, kdf, kdb, bbf, bbb = pre
    rows = za.shape[0]
    fwd, bwd, seq, first, last = _scan_maps(*geom)
    nseq = s0.shape[0]
    blk = lambda m, col=0: pl.BlockSpec((CHUNK, WIDTH), lambda i: (m(i), col))
    const = lambda a: pl.BlockSpec(a.shape, lambda i: (0,) * a.ndim)
    consts = (hc['mbd'], hc['eye'], hc['low_s'], hc['low_i'], hc['up_s'], hc['up_i'], hc['tri_f'], hc['tri_b'])
    st_spec = pl.BlockSpec((1, 2, WIDTH, WIDTH), lambda i: (seq(i), 0, 0, 0))
    return pl.pallas_call(
        functools.partial(_rwkv_scan_kernel, first, last),
        grid=(rows // CHUNK,),
        in_specs=[blk(fwd, 0), blk(fwd, 2), blk(fwd), blk(fwd), blk(fwd), blk(fwd),
                  blk(bwd, 0), blk(bwd, 2), blk(bwd), blk(bwd), blk(bwd), blk(bwd),
                  st_spec] + [const(a) for a in consts],
        out_specs=[blk(fwd), blk(bwd), st_spec],
        out_shape=[jax.ShapeDtypeStruct((rows, WIDTH), F32), jax.ShapeDtypeStruct((rows, WIDTH), F32),
                   jax.ShapeDtypeStruct((nseq, 2, WIDTH, WIDTH), F32)],
        scratch_shapes=[pltpu.VMEM((2, WIDTH, WIDTH), F32)],
        compiler_params=_cp(("arbitrary",)),
        name='rwkv_scan',
    )(za, za, kap, lwf, kdf, bbf, za, za, kap, lwb, kdb, bbb, s0, *consts)


def _rwkv_post_kernel(of_ref, ob_ref, bon_ref, gate_ref, gng_ref, gnb_ref, ebd_ref, o_ref):
    of = of_ref[...] + ob_ref[...]
    avg = ebd_ref[...] * (1.0 / HEAD_DIM)
    mu = _dot(of, avg, hi=True)
    d = of - mu
    var = _dot(d * d, avg, hi=True)
    gn = d * lax.rsqrt(var + RWKV_GN_EPS) * gng_ref[...] + gnb_ref[...]
    o_ref[...] = (gn + bon_ref[...]) * gate_ref[...]


def _rwkv_post(o_f, o_b, bonus, gate, gn_g, gn_b, ebd, tm):
    rows = o_f.shape[0]
    row = pl.BlockSpec((tm, WIDTH), lambda i: (i, 0))
    vec = pl.BlockSpec((1, WIDTH), lambda i: (0, 0))
    return pl.pallas_call(
        _rwkv_post_kernel,
        grid=(rows // tm,),
        in_specs=[row, row, row, row, vec, vec, pl.BlockSpec((WIDTH, WIDTH), lambda i: (0, 0))],
        out_specs=row,
        out_shape=jax.ShapeDtypeStruct((rows, WIDTH), F32),
        compiler_params=_cp(("parallel",)),
        name='rwkv_post',
    )(o_f, o_b, bonus, gate, gn_g, gn_b, ebd)


def _hgrn_chunk(xq, xf, xi, lb, st, tri, ebd, mbd, p_scr, reverse):
    c = xq.shape[0]
    q = _silu(xq)
    gsig = lb + (1.0 - lb) * _sigmoid(xf)
    kk = 1.0 - gsig
    lg = jnp.log(gsig)
    cum = _dot(tri, lg, hi=True)
    tot = jnp.sum(lg, axis=0, keepdims=True)
    rowid = lax.broadcasted_iota(jnp.int32, (c, WIDTH), 0)

    def fill(s, carry):
        cs = jnp.sum(jnp.where(rowid == s, cum, 0.0), axis=0, keepdims=True)
        ks = jnp.sum(jnp.where(rowid == s, kk, 0.0), axis=0, keepdims=True)
        valid = (rowid <= s) if reverse else (rowid >= s)
        e = jnp.exp(jnp.where(valid, cum - cs, NEG))
        p_scr[pl.ds(pl.multiple_of(s * c, c), c), :] = e * q * ks
        return carry

    lax.fori_loop(0, c, fill, 0)
    p = p_scr[...]
    p_hi = p.astype(BF16)
    p_lo = (p - p_hi.astype(F32)).astype(BF16)
    ebd_bf = ebd.astype(BF16)
    p_scr[...] = _dot(p_hi, ebd_bf) + _dot(p_lo, ebd_bf)

    def acc_fn(s, acc):
        vs = jnp.sum(jnp.where(rowid == s, xi, 0.0), axis=0, keepdims=True)
        return acc + p_scr[pl.ds(pl.multiple_of(s * c, c), c), :] * vs

    intra = lax.fori_loop(0, c, acc_fn, jnp.zeros((c, WIDTH), F32))
    o = _dot_nt(q * jnp.exp(cum), st, hi=True) + intra
    upd = _dot_tn(xi, kk * jnp.exp(tot - cum), hi=True)
    st_new = st * jnp.exp(tot) + upd * mbd
    return o, st_new


def _hgrn_scan_kernel(first_fn, last_fn,
                      qf_ref, ff_ref, if_ref, qb_ref, fb_ref, ib_ref,
                      lb_ref, s0_ref, mbd_ref, ebd_ref, trif_ref, trib_ref,
                      of_ref, ob_ref, sfin_ref, st_scr, p_scr):
    i = pl.program_id(0)

    @pl.when(first_fn(i))
    def _():
        st_scr[...] = s0_ref[0]

    mbd = mbd_ref[...]
    ebd = ebd_ref[...]
    o_f, st_f = _hgrn_chunk(qf_ref[...], ff_ref[...], if_ref[...], lb_ref[0], st_scr[0], trif_ref[...],
                            ebd, mbd, p_scr, False)
    of_ref[...] = o_f
    st_scr[0] = st_f
    o_b, st_b = _hgrn_chunk(qb_ref[...], fb_ref[...], ib_ref[...], lb_ref[1], st_scr[1], trib_ref[...],
                            ebd, mbd, p_scr, True)
    ob_ref[...] = o_b
    st_scr[1] = st_b

    @pl.when(last_fn(i))
    def _():
        sfin_ref[0] = st_scr[...]


def _hgrn_scan(zd, lb, s0, hc, geom):
    rows = zd.shape[0]
    fwd, bwd, seq, first, last = _scan_maps(*geom)
    nseq = s0.shape[0]
    blk = lambda m, col: pl.BlockSpec((CHUNK, WIDTH), lambda i: (m(i), col))
    const = lambda a: pl.BlockSpec(a.shape, lambda i: (0,) * a.ndim)
    consts = (hc['mbd'], hc['mbd'], hc['tri_f'], hc['tri_b'])
    st_spec = pl.BlockSpec((1, 2, WIDTH, WIDTH), lambda i: (seq(i), 0, 0, 0))
    return pl.pallas_call(
        functools.partial(_hgrn_scan_kernel, first, last),
        grid=(rows // CHUNK,),
        in_specs=[blk(fwd, 0), blk(fwd, 1), blk(fwd, 3), blk(bwd, 0), blk(bwd, 2), blk(bwd, 3),
                  const(lb), st_spec] + [const(a) for a in consts],
        out_specs=[blk(fwd, 0), blk(bwd, 0), st_spec],
        out_shape=[jax.ShapeDtypeStruct((rows, WIDTH), F32), jax.ShapeDtypeStruct((rows, WIDTH), F32),
                   jax.ShapeDtypeStruct((nseq, 2, WIDTH, WIDTH), F32)],
        scratch_shapes=[pltpu.VMEM((2, WIDTH, WIDTH), F32), pltpu.VMEM((CHUNK * CHUNK, WIDTH), F32)],
        compiler_params=_cp(("arbitrary",)),
        name='hgrn_scan',
    )(zd, zd, zd, zd, zd, zd, lb, s0, *consts)


def _hgrn_post_kernel(of_ref, ob_ref, xg_ref, gn_ref, ebd_ref, o_ref):
    of = of_ref[...] + ob_ref[...]
    ms = _dot(of * of, ebd_ref[...] * (1.0 / HEAD_DIM), hi=True)
    o_ref[...] = of * lax.rsqrt(ms + RMS_EPS) * gn_ref[...] * _silu(xg_ref[...])


def _hgrn_post(o_f, o_b, zd, gn_g, ebd, tm):
    rows = o_f.shape[0]
    row = pl.BlockSpec((tm, WIDTH), lambda i: (i, 0))
    return pl.pallas_call(
        _hgrn_post_kernel,
        grid=(rows // tm,),
        in_specs=[row, row, pl.BlockSpec((tm, WIDTH), lambda i: (i, 4)),
                  pl.BlockSpec((1, WIDTH), lambda i: (0, 0)), pl.BlockSpec((WIDTH, WIDTH), lambda i: (0, 0))],
        out_specs=row,
        out_shape=jax.ShapeDtypeStruct((rows, WIDTH), F32),
        compiler_params=_cp(("parallel",)),
        name='hgrn_post',
    )(o_f, o_b, zd, gn_g, ebd)


def _lru_pre_kernel(seq_of_tile, xp_ref, x_ref, xn_ref, cw_ref, cb_ref, wg_ref, bg_ref, lam_ref,
                    af_o, bf_o, ab_o, bb_o, pad_scr):
    i = pl.program_id(0)
    tm = x_ref.shape[0]
    seqlen = seq_of_tile(i)
    pad_scr[0:8, :] = xp_ref[:, 0:WIDTH]
    pad_scr[8:8 + tm, :] = x_ref[:, 0:WIDTH]
    pad_scr[8 + tm:16 + tm, :] = xn_ref[:, 0:WIDTH]
    pos = jnp.bitwise_and(lax.broadcasted_iota(jnp.int32, (tm, WIDTH), 0) + i * tm, seqlen - 1)
    u = jnp.zeros((tm, WIDTH), F32) + cb_ref[...]
    for j in range(C_CONV):
        off = j - C_CONV // 2
        tap = pad_scr[pl.ds(8 + off, tm), :]
        ok = jnp.logical_and(pos + off >= 0, pos + off < seqlen)
        u += jnp.where(ok, tap, 0.0) * cw_ref[j:j + 1, :]
    gates = _sigmoid(_dot(u.astype(BF16), wg_ref[...].astype(BF16)) + bg_ref[...])
    for d, (a_o, b_o) in enumerate(((af_o, bf_o), (ab_o, bb_o))):
        r = gates[:, (2 * d) * WIDTH:(2 * d + 1) * WIDTH]
        ig = gates[:, (2 * d + 1) * WIDTH:(2 * d + 2) * WIDTH]
        log_a = -C_POW * r * _softplus(-lam_ref[d])
        a = jnp.exp(log_a)
        a_o[...] = a
        b_o[...] = jnp.sqrt(-jnp.tanh(log_a) * (a * a + 1.0)) * (ig * u)


def _lru_pre(zc, p, ctx_len, dec_len, n_ctx_tiles, tm):
    rows = zc.shape[0]
    nb8 = rows // 8
    per = tm // 8
    seq_of_tile = lambda i: jnp.where(i < n_ctx_tiles, ctx_len, dec_len)
    row = pl.BlockSpec((tm, WIDTH), lambda i: (i, 0))
    full = lambda a: pl.BlockSpec(a.shape, lambda i: (0,) * a.ndim)
    args = (p['conv_w'], p['conv_b'], p['wg'], p['bg'], p['lam'])
    return pl.pallas_call(
        functools.partial(_lru_pre_kernel, seq_of_tile),
        grid=(rows // tm,),
        in_specs=[pl.BlockSpec((8, WIDTH), lambda i: (jnp.maximum(i * per - 1, 0), 0)),
                  row,
                  pl.BlockSpec((8, WIDTH), lambda i: (jnp.minimum((i + 1) * per, nb8 - 1), 0))]
                 + [full(a) for a in args],
        out_specs=[row] * 4,
        out_shape=[jax.ShapeDtypeStruct((rows, WIDTH), F32)] * 4,
        scratch_shapes=[pltpu.VMEM((tm + 16, WIDTH), F32)],
        compiler_params=_cp(("parallel",)),
        name='rglru_pre',
    )(zc, zc, zc, *args)


def _lru_scan_kernel(first_fn, last_fn, af_ref, bf_ref, ab_ref, bb_ref, h0_ref, hf_ref, hb_ref, hfin_ref, h_scr):
    i = pl.program_id(0)
    c = af_ref.shape[0]

    @pl.when(first_fn(i))
    def _():
        h_scr[...] = h0_ref[0]

    def step(t, carry):
        h_f, h_b = carry
        tb = c - 1 - t
        h_f = af_ref[pl.ds(t, 1), :] * h_f + bf_ref[pl.ds(t, 1), :]
        h_b = ab_ref[pl.ds(tb, 1), :] * h_b + bb_ref[pl.ds(tb, 1), :]
        hf_ref[pl.ds(t, 1), :] = h_f
        hb_ref[pl.ds(tb, 1), :] = h_b
        return h_f, h_b

    h_f, h_b = lax.fori_loop(0, c, step, (h_scr[0:1, :], h_scr[1:2, :]))
    h_scr[0:1, :] = h_f
    h_scr[1:2, :] = h_b

    @pl.when(last_fn(i))
    def _():
        hfin_ref[0] = h_scr[...]


def _lru_scan(coef, h0, geom):
    a_f, b_f, a_b, b_b = coef
    rows = a_f.shape[0]
    fwd, bwd, seq, first, last = _scan_maps(*geom)
    nseq = h0.shape[0]
    blk = lambda m: pl.BlockSpec((LRU_CHUNK, WIDTH), lambda i: (m(i), 0))
    st_spec = pl.BlockSpec((1, 2, WIDTH), lambda i: (seq(i), 0, 0))
    return pl.pallas_call(
        functools.partial(_lru_scan_kernel, first, last),
        grid=(rows // LRU_CHUNK,),
        in_specs=[blk(fwd), blk(fwd), blk(bwd), blk(bwd), st_spec],
        out_specs=[blk(fwd), blk(bwd), st_spec],
        out_shape=[jax.ShapeDtypeStruct((rows, WIDTH), F32), jax.ShapeDtypeStruct((rows, WIDTH), F32),
                   jax.ShapeDtypeStruct((nseq, 2, WIDTH), F32)],
        scratch_shapes=[pltpu.VMEM((2, WIDTH), F32)],
        compiler_params=_cp(("arbitrary",)),
        name='rglru_scan',
    )(a_f, b_f, a_b, b_b, h0)


def _rope_slab(x, cos, sin):
    lane = lax.broadcasted_iota(jnp.int32, x.shape, 1)
    rot = jnp.where(lane % (B_ROPE // 2) < B_ROPE // 4, -pltpu.roll(x, LANE - B_ROPE // 4, 1),
                    pltpu.roll(x, B_ROPE // 4, 1))
    return x * cos + rot * sin


def _kv_up(ckv_bf, kpe_slab, wuk_ref, wuv_ref, k_o, v_o):
    kn = _dot(ckv_bf, wuk_ref[...])
    for h in range(N_HEADS):
        k_o[:, h * SLAB:(h + 1) * SLAB] = (kn[:, h * SLAB:(h + 1) * SLAB] + kpe_slab).astype(BF16)
    v_o[...] = _dot(ckv_bf, wuv_ref[...]).astype(BF16)


def _mla_prep_kernel(zb_ref, cos_ref, sin_ref, qn_ref, kvn_ref, wuq_ref, wuk_ref, wuv_ref,
                     ckv_o, q_o, k_o, v_o):
    cq = zb_ref[:, 0:B_Q_RANK]
    ckv = zb_ref[:, B_Q_RANK:B_Q_RANK + B_KV_RANK]
    kpe = zb_ref[:, B_Q_RANK + B_KV_RANK:ZB]
    cos = cos_ref[...]
    sin = sin_ref[...]
    cqn = cq * lax.rsqrt(jnp.mean(cq * cq, axis=-1, keepdims=True) + RMS_EPS) * qn_ref[...]
    ckvn = ckv * lax.rsqrt(jnp.mean(ckv * ckv, axis=-1, keepdims=True) + RMS_EPS) * kvn_ref[...]
    ckv_o[...] = ckvn
    q = _dot(cqn.astype(BF16), wuq_ref[...])
    for h in range(N_HEADS):
        q_o[:, h * SLAB:(h + 1) * SLAB] = _rope_slab(q[:, h * SLAB:(h + 1) * SLAB], cos, sin).astype(BF16)
    _kv_up(ckvn.astype(BF16), _rope_slab(kpe, cos, sin), wuk_ref, wuv_ref, k_o, v_o)


def _mla_prep(zb, cos, sin, p, tm):
    rows = zb.shape[0]
    row = lambda w: pl.BlockSpec((tm, w), lambda i: (i, 0))
    full = lambda a: pl.BlockSpec(a.shape, lambda i: (0,) * a.ndim)
    args = (p['qn_g'], p['kvn_g'], p['wuq'], p['wuk'], p['wuv'])
    return pl.pallas_call(
        _mla_prep_kernel,
        grid=(rows // tm,),
        in_specs=[row(ZB), row(LANE), row(LANE)] + [full(a) for a in args],
        out_specs=[row(B_KV_RANK), row(N_HEADS * SLAB), row(N_HEADS * SLAB), row(WIDTH)],
        out_shape=[jax.ShapeDtypeStruct((rows, B_KV_RANK), F32), jax.ShapeDtypeStruct((rows, N_HEADS * SLAB), BF16),
                   jax.ShapeDtypeStruct((rows, N_HEADS * SLAB), BF16), jax.ShapeDtypeStruct((rows, WIDTH), BF16)],
        compiler_params=_cp(("parallel",)),
        name='mla_prep',
    )(zb, cos, sin, *args)


def _mla_cache_kernel(ckv_ref, kpe_ref, wuk_ref, wuv_ref, k_o, v_o):
    _kv_up(ckv_ref[...].astype(BF16), kpe_ref[...], wuk_ref, wuv_ref, k_o, v_o)


def _mla_cache(ckv, kpe_slab, p):
    rows = ckv.shape[0]
    return pl.pallas_call(
        _mla_cache_kernel,
        out_shape=[jax.ShapeDtypeStruct((rows, N_HEADS * SLAB), BF16), jax.ShapeDtypeStruct((rows, WIDTH), BF16)],
        compiler_params=pltpu.CompilerParams(vmem_limit_bytes=VMEM_LIMIT),
        name='mla_cache',
    )(ckv, kpe_slab, p['wuk'], p['wuv'])


def _attend_heads(q_ref, kv_refs, o_ref):
    scale = (B_NOPE + B_ROPE) ** -0.5
    tq = q_ref.shape[0]
    lane_head = lax.broadcasted_iota(jnp.int32, (tq, WIDTH), 1) // HEAD_DIM
    out = jnp.zeros((tq, WIDTH), F32)
    for h in range(N_HEADS):
        qh = q_ref[:, h * SLAB:(h + 1) * SLAB]
        ss = [_dot_nt(qh, k_ref[:, h * SLAB:(h + 1) * SLAB]) * scale for k_ref, _ in kv_refs]
        m = ss[0].max(axis=-1, keepdims=True)
        for s in ss[1:]:
            m = jnp.maximum(m, s.max(axis=-1, keepdims=True))
        den = jnp.zeros((tq, 1), F32)
        num = jnp.zeros((tq, WIDTH), F32)
        for s, (_, v_ref) in zip(ss, kv_refs):
            e = jnp.exp(s - m)
            den += e.sum(axis=-1, keepdims=True)
            num += _dot(e.astype(BF16), v_ref[...])
        out = jnp.where(lane_head == h, num / den, out)
    o_ref[...] = out


def _attn_ctx_kernel(q_ref, k_ref, v_ref, o_ref):
    _attend_heads(q_ref, [(k_ref, v_ref)], o_ref)


def _attn_dec_kernel(q_ref, k_ref, v_ref, kc_ref, vc_ref, o_ref):
    _attend_heads(q_ref, [(k_ref, v_ref), (kc_ref, vc_ref)], o_ref)


def _attn_ctx(q, k, v, n_seq, seq_len):
    blk = lambda w: pl.BlockSpec((seq_len, w), lambda b: (b, 0))
    return pl.pallas_call(
        _attn_ctx_kernel,
        grid=(n_seq,),
        in_specs=[blk(N_HEADS * SLAB), blk(N_HEADS * SLAB), blk(WIDTH)],
        out_specs=blk(WIDTH),
        out_shape=jax.ShapeDtypeStruct((n_seq * seq_len, WIDTH), F32),
        compiler_params=_cp(("parallel",)),
        name='attn_ctx',
    )(q, k, v)


def _attn_dec(q, k, v, kc, vc, seg0, n_dec, dec_len, past, tq):
    nq = dec_len // tq
    return pl.pallas_call(
        _attn_dec_kernel,
        grid=(n_dec, nq),
        in_specs=[pl.BlockSpec((tq, N_HEADS * SLAB), lambda b, i: ((seg0 + b) * nq + i, 0)),
                  pl.BlockSpec((dec_len, N_HEADS * SLAB), lambda b, i: (seg0 + b, 0)),
                  pl.BlockSpec((dec_len, WIDTH), lambda b, i: (seg0 + b, 0)),
                  pl.BlockSpec((past, N_HEADS * SLAB), lambda b, i: (b, 0)),
                  pl.BlockSpec((past, WIDTH), lambda b, i: (b, 0))],
        out_specs=pl.BlockSpec((tq, WIDTH), lambda b, i: (b * nq + i, 0)),
        out_shape=jax.ShapeDtypeStruct((n_dec * dec_len, WIDTH), F32),
        compiler_params=_cp(("parallel", "arbitrary")),
        name='attn_dec',
    )(q, k, v, kc, vc)


def _rope_tables(n_ctx_rows, n_dec, dec_len):
    rows = dec_len // GRID_W
    row = np.repeat(np.arange(rows, dtype=np.float32), GRID_W)
    col = np.tile(np.arange(GRID_W, dtype=np.float32), rows)
    half = B_ROPE // 2
    inv = jnp.power(ROPE_BASE, -jnp.arange(0, half, 2, dtype=F32) / half)
    ang_r = jnp.asarray(row)[:, None] * inv
    ang_c = jnp.asarray(col)[:, None] * inv
    ang = jnp.concatenate([ang_r, ang_r, ang_c, ang_c], axis=-1)
    cos = jnp.pad(jnp.cos(ang), ((0, 0), (0, LANE - B_ROPE)), constant_values=1.0)
    sin = jnp.pad(jnp.sin(ang), ((0, 0), (0, LANE - B_ROPE)))
    cos = jnp.concatenate([jnp.ones((n_ctx_rows, LANE), F32)] + [cos] * n_dec, axis=0)
    sin = jnp.concatenate([jnp.zeros((n_ctx_rows, LANE), F32)] + [sin] * n_dec, axis=0)
    return cos, sin


def _block_diag(w):
    g, n, m = w.shape[-3:]
    eye = jnp.eye(g, dtype=w.dtype)
    out = w[..., :, :, None, :] * eye[:, None, :, None]
    return out.reshape(w.shape[:-3] + (g * n, g * m))


def _head_blocks(s):
    lead = s.shape[:-2]
    s = s.reshape(lead + (N_HEADS, HEAD_DIM, N_HEADS, HEAD_DIM))
    return jnp.stack([s[..., h, :, h, :] for h in range(N_HEADS)], axis=-3)


def kernel(x_prompt, x_sample, cache_mla_ckv, cache_mla_kpe, state_rwkv, state_rglru, state_hgrn, c, c_ctx,
           w_ada, b_ada, ln_g, ln_b, w_ffn_in, w_ffn_out, w_in, w_out,
           rwkv_w0, rwkv_w2, rwkv_a0, rwkv_a2, rwkv_g2, rwkv_kk, rwkv_ka, rwkv_rk, rwkv_gn_g, rwkv_gn_b,
           mla_qn_g, mla_w_uq, mla_kvn_g, mla_w_ukv,
           rglru_conv_w, rglru_conv_b, rglru_wa, rglru_ba, rglru_wx, rglru_bx, rglru_lam,
           hgrn_lb, hgrn_gn_g):
    n_ctx, ctx_len, _ = x_prompt.shape
    n_dec, dec_len, _ = x_sample.shape
    past = cache_mla_ckv.shape[2]
    seg = n_ctx * ctx_len
    assert seg == dec_len, "context rows must form one segment of the decode sequence length"
    assert ctx_len & (ctx_len - 1) == 0 and dec_len & (dec_len - 1) == 0, "sequence lengths must be powers of two"
    n_seg = 1 + n_dec
    rows = n_seg * seg
    tm = min(TM, seg)
    tq = min(TQ, dec_len)
    tiles_per_seg = seg // tm
    nseq = n_ctx + n_dec
    geom = (seg // CHUNK, ctx_len // CHUNK, dec_len // CHUNK, n_ctx)
    lru_c = min(LRU_CHUNK, ctx_len)
    assert lru_c == LRU_CHUNK
    lru_geom = (seg // LRU_CHUNK, ctx_len // LRU_CHUNK, dec_len // LRU_CHUNK, n_ctx)

    hc = {k: jnp.asarray(v) for k, v in _head_consts().items()}
    ebd = hc['mbd']
    cos, sin = _rope_tables(seg, n_dec, dec_len)

    cond8 = jnp.zeros((8, D_MODEL), F32).at[0].set(c_ctx).at[1:1 + n_dec].set(c)
    mods = _ada(cond8, w_ada, b_ada).reshape(DEPTH, 8, N_MOD, D_MODEL)
    mods = jnp.transpose(mods, (0, 2, 1, 3))[:, :, :n_seg, None, :]

    w_ffn_in_bf = w_ffn_in.astype(BF16)
    w_ffn_out_bf = w_ffn_out.astype(BF16)
    w_out_bf = w_out.astype(BF16)
    kpe_end = ZA + B_Q_RANK + B_KV_RANK + B_ROPE
    w_in_p = jnp.concatenate([w_in[:, :, :kpe_end], jnp.zeros((DEPTH, D_MODEL, LANE - B_ROPE), F32),
                              w_in[:, :, kpe_end:]], axis=-1).astype(BF16)
    wuq = mla_w_uq.reshape(DEPTH, B_Q_RANK, N_HEADS, B_NOPE + B_ROPE)
    wuq_p = jnp.concatenate([wuq[..., B_NOPE:], wuq[..., :B_NOPE],
                             jnp.zeros((DEPTH, B_Q_RANK, N_HEADS, SLAB - B_NOPE - B_ROPE), F32)], axis=-1)
    wuq_p = wuq_p.reshape(DEPTH, B_Q_RANK, N_HEADS * SLAB).astype(BF16)
    wukv = mla_w_ukv.reshape(DEPTH, B_KV_RANK, N_HEADS, B_NOPE + B_VDIM)
    wuk_p = jnp.concatenate([jnp.zeros((DEPTH, B_KV_RANK, N_HEADS, B_ROPE), F32), wukv[..., :B_NOPE],
                             jnp.zeros((DEPTH, B_KV_RANK, N_HEADS, SLAB - B_NOPE - B_ROPE), F32)], axis=-1)
    wuk_p = wuk_p.reshape(DEPTH, B_KV_RANK, N_HEADS * SLAB).astype(BF16)
    wuv_p = wukv[..., B_NOPE:].reshape(DEPTH, B_KV_RANK, WIDTH).astype(BF16)
    lru_wg = jnp.concatenate([_block_diag(rglru_wa[:, 0]), _block_diag(rglru_wx[:, 0]),
                              _block_diag(rglru_wa[:, 1]), _block_diag(rglru_wx[:, 1])], axis=-1)
    lru_bg = jnp.concatenate([rglru_ba[:, 0], rglru_bx[:, 0], rglru_ba[:, 1], rglru_bx[:, 1]], axis=-1)[:, None, :]
    lbs = _lower_bounds(hgrn_lb)

    zeros_st = jnp.zeros((n_ctx, DEPTH, 2, WIDTH, WIDTH), F32)
    rwkv_s0 = jnp.concatenate([zeros_st, _block_diag(state_rwkv)], axis=0)
    hgrn_s0 = jnp.concatenate([zeros_st, _block_diag(jnp.swapaxes(state_hgrn, -1, -2))], axis=0)
    lru_h0 = jnp.concatenate([jnp.zeros((n_ctx, DEPTH, 2, WIDTH), F32), state_rglru], axis=0)
    cache_kpe_slab = jnp.pad(cache_mla_kpe, ((0, 0), (0, 0), (0, 0), (0, LANE - B_ROPE)))

    x = jnp.concatenate([x_prompt.reshape(seg, D_MODEL), x_sample.reshape(n_dec * dec_len, D_MODEL)], axis=0)
    new_ckv, new_kpe, new_rwkv, new_lru, new_hgrn = [], [], [], [], []
    for l in range(DEPTH):
        m = mods[l]
        lng = ln_g[l][:, None, :]
        lnb = ln_b[l][:, None, :]
        x = _ffn(x, m[0], m[1], m[2], lng[0], lnb[0], w_ffn_in_bf, w_ffn_out_bf, l, 0, tiles_per_seg, tm)
        za, zb, zc, zd = _inproj(x, m[3], m[4], w_in_p, l, tiles_per_seg, tm)

        pa = {'w0': rwkv_w0[l][:, None, :], 'w2': rwkv_w2[l], 'a0': rwkv_a0[l][:, None, :], 'a2': rwkv_a2[l],
              'g2': rwkv_g2[l], 'kk': rwkv_kk[l][None, :], 'ka': rwkv_ka[l][None, :], 'rk': rwkv_rk[l][None, :]}
        pre = _rwkv_pre(za, pa, ebd, tm)
        o_f, o_b, s_fin = _rwkv_scan(za, pre[:7], rwkv_s0[:, l], hc, geom)
        ya = _rwkv_post(o_f, o_b, pre[7], pre[8], rwkv_gn_g[l][None, :], rwkv_gn_b[l][None, :], ebd, tm)
        new_rwkv.append(_head_blocks(s_fin[:n_ctx]))

        pb = {'qn_g': mla_qn_g[l][None, :], 'kvn_g': mla_kvn_g[l][None, :], 'wuq': wuq_p[l], 'wuk': wuk_p[l],
              'wuv': wuv_p[l]}
        ckvn, q_all, k_all, v_all = _mla_prep(zb, cos, sin, pb, tm)
        kc, vc = _mla_cache(cache_mla_ckv[:, l].reshape(n_dec * past, B_KV_RANK),
                            cache_kpe_slab[:, l].reshape(n_dec * past, LANE), pb)
        yb = jnp.concatenate([_attn_ctx(q_all, k_all, v_all, n_ctx, ctx_len),
                              _attn_dec(q_all, k_all, v_all, kc, vc, 1, n_dec, dec_len, past, tq)], axis=0)
        new_ckv.append(ckvn[:seg].reshape(n_ctx, ctx_len, B_KV_RANK))
        new_kpe.append(zb[:seg, B_Q_RANK + B_KV_RANK:B_Q_RANK + B_KV_RANK + B_ROPE].reshape(n_ctx, ctx_len, B_ROPE))

        pc = {'conv_w': rglru_conv_w[l], 'conv_b': rglru_conv_b[l][None, :], 'wg': lru_wg[l], 'bg': lru_bg[l],
              'lam': rglru_lam[l][:, None, :]}
        coef = _lru_pre(zc, pc, ctx_len, dec_len, tiles_per_seg, tm)
        h_f, h_b, h_fin = _lru_scan(coef, lru_h0[:, l], lru_geom)
        new_lru.append(h_fin[:n_ctx])

        od_f, od_b, sd_fin = _hgrn_scan(zd, lbs[l], hgrn_s0[:, l], hc, geom)
        yd = _hgrn_post(od_f, od_b, zd, hgrn_gn_g[l][None, :], ebd, tm)
        new_hgrn.append(jnp.swapaxes(_head_blocks(sd_fin[:n_ctx]), -1, -2))

        x = _outproj(x, ya, yb, h_f, h_b, zc, yd, w_out_bf, m[5], lng[1], lnb[1], l, tiles_per_seg, tm)
        x = _ffn(x, m[6], m[7], m[8], lng[2], lnb[2], w_ffn_in_bf, w_ffn_out_bf, l, 1, tiles_per_seg, tm)

    y_prompt = x[:seg].reshape(n_ctx, ctx_len, D_MODEL)
    y_sample = x[seg:].reshape(n_dec, dec_len, D_MODEL)
    return (y_prompt, y_sample, jnp.stack(new_ckv, axis=1), jnp.stack(new_kpe, axis=1),
            jnp.stack(new_rwkv, axis=1), jnp.stack(new_lru, axis=1), jnp.stack(new_hgrn, axis=1))
```

```python
import functools

import numpy as np
import jax
import jax.numpy as jnp
from jax import lax
from jax.experimental import pallas as pl
from jax.experimental.pallas import tpu as pltpu

D_MODEL = 1024
DEPTH = 4
GRID_W = 64
HEAD_DIM = 64
N_HEADS = 4
WIDTH = N_HEADS * HEAD_DIM
A_DECAY_RANK = 64
A_ICLR_RANK = 64
A_GATE_RANK = 128
B_NOPE = 64
B_ROPE = 32
B_VDIM = 64
B_Q_RANK = 256
B_KV_RANK = 128
C_CONV = 4
C_POW = 8.0
D_FF = 2816
N_MOD = 9
ROPE_BASE = 10000.0
LN_EPS = 1e-5
RMS_EPS = 1e-6
RWKV_GN_EPS = 64e-5
DN_ALPHA = (2 * DEPTH) ** 0.25

LANE = 128
SLAB = 128
CHUNK = 64
SUB = 16
LRU_CHUNK = 256
TM = 512
TF = 1408
TQ = 256
VMEM_LIMIT = 56 * 1024 * 1024

F32 = jnp.float32
BF16 = jnp.bfloat16
HI = lax.Precision.HIGHEST
NEG = -1e30
RWKV_HI = False


def _cp(sem):
    return pltpu.CompilerParams(dimension_semantics=sem, vmem_limit_bytes=VMEM_LIMIT)


def _dot(a, b, hi=False):
    return jnp.dot(a, b, preferred_element_type=F32, precision=HI if hi else None)


def _dot_nt(a, b, hi=False):
    return lax.dot_general(a, b, (((1,), (1,)), ((), ())), preferred_element_type=F32,
                           precision=HI if hi else None)


def _dot_tn(a, b, hi=False):
    return lax.dot_general(a, b, (((0,), (0,)), ((), ())), preferred_element_type=F32,
                           precision=HI if hi else None)


def _sigmoid(x):
    return 1.0 / (1.0 + jnp.exp(-x))


def _silu(x):
    return x * _sigmoid(x)


def _softplus(x):
    return jnp.maximum(x, 0.0) + jnp.log(1.0 + jnp.exp(-jnp.abs(x)))


def _tile4(y):
    return jnp.concatenate([y, y, y, y], axis=0)


def _head_consts():
    r = np.arange(WIDTH)
    same = (r[:, None] // HEAD_DIM) == (r[None, :] // HEAD_DIM)
    t = np.arange(CHUNK)[:, None]
    j = (np.arange(WIDTH) % CHUNK)[None, :]
    tt = np.arange(CHUNK)
    return {
        'mbd': same.astype(np.float32),
        'low_s': (j < t).astype(np.float32), 'low_i': (j <= t).astype(np.float32),
        'up_s': (j > t).astype(np.float32), 'up_i': (j >= t).astype(np.float32),
        'eye': (j == t).astype(np.float32),
        'tri_f': (tt[None, :] <= tt[:, None]).astype(np.float32),
        'tri_b': (tt[None, :] >= tt[:, None]).astype(np.float32),
    }


def _lb_kernel(x_ref, o_ref):
    x = x_ref[...]
    m = jnp.max(x, axis=0, keepdims=True)
    e = jnp.exp(x - m)
    sm = e / jnp.sum(e, axis=0, keepdims=True)
    run = sm[0:1]
    rows = [run - sm[0:1]]
    for l in range(1, DEPTH):
        run = run + sm[l:l + 1]
        rows.append(run - sm[0:1])
    o_ref[...] = jnp.concatenate(rows, axis=0)


def _lower_bounds(hgrn_lb):
    flat = hgrn_lb.reshape(DEPTH, 2 * WIDTH)
    out = pl.pallas_call(_lb_kernel, out_shape=jax.ShapeDtypeStruct(flat.shape, F32), name='hgrn_lb')(flat)
    return out.reshape(DEPTH, 2, 1, WIDTH)


def _ada_kernel(c_ref, w_ref, b_ref, o_ref):
    h = _silu(c_ref[...]).astype(BF16)
    o_ref[0] = _dot(h, w_ref[0].astype(BF16)) + b_ref[0]


def _ada(cond8, w_ada, b_ada):
    n = N_MOD * D_MODEL
    tn = 2304
    return pl.pallas_call(
        _ada_kernel,
        grid=(DEPTH, n // tn),
        in_specs=[pl.BlockSpec((8, D_MODEL), lambda l, j: (0, 0)),
                  pl.BlockSpec((1, D_MODEL, tn), lambda l, j: (l, 0, j)),
                  pl.BlockSpec((1, 1, tn), lambda l, j: (l, 0, j))],
        out_specs=pl.BlockSpec((1, 8, tn), lambda l, j: (l, 0, j)),
        out_shape=jax.ShapeDtypeStruct((DEPTH, 8, n), F32),
        compiler_params=_cp(("arbitrary", "arbitrary")),
        name='ada',
    )(cond8, w_ada, b_ada.reshape(DEPTH, 1, n))


def _layer_norm_rows(y, g, b):
    mu = jnp.mean(y, axis=-1, keepdims=True)
    d = y - mu
    var = jnp.mean(d * d, axis=-1, keepdims=True)
    return d * lax.rsqrt(var + LN_EPS) * g + b


def _ffn_kernel(x_ref, sh_ref, sc_ref, g_ref, lng_ref, lnb_ref, wg_ref, wu_ref, wo_ref, o_ref, h_scr, acc_scr):
    j = pl.program_id(1)

    @pl.when(j == 0)
    def _():
        h_scr[...] = (x_ref[...] * (1.0 + sc_ref[0]) + sh_ref[0]).astype(BF16)
        acc_scr[...] = jnp.zeros_like(acc_scr)

    h = h_scr[...]
    gate = _dot(h, wg_ref[...])
    up = _dot(h, wu_ref[...])
    acc_scr[...] += _dot((_silu(gate) * up).astype(BF16), wo_ref[...])

    @pl.when(j == pl.num_programs(1) - 1)
    def _():
        y = DN_ALPHA * x_ref[...] + 0.5 * g_ref[0] * acc_scr[...]
        o_ref[...] = _layer_norm_rows(y, lng_ref[...], lnb_ref[...])


def _ffn(x, sh, sc, g, lng, lnb, w_in_bf, w_out_bf, l, f, tiles_per_seg, tm):
    rows = x.shape[0]
    nj = D_FF // TF
    mod = pl.BlockSpec((1, 1, D_MODEL), lambda i, j: (i // tiles_per_seg, 0, 0))
    vec = pl.BlockSpec((1, D_MODEL), lambda i, j: (0, 0))
    return pl.pallas_call(
        _ffn_kernel,
        grid=(rows // tm, nj),
        in_specs=[pl.BlockSpec((tm, D_MODEL), lambda i, j: (i, 0)), mod, mod, mod, vec, vec,
                  pl.BlockSpec((None, None, D_MODEL, TF), lambda i, j: (l, f, 0, j)),
                  pl.BlockSpec((None, None, D_MODEL, TF), lambda i, j: (l, f, 0, nj + j)),
                  pl.BlockSpec((None, None, TF, D_MODEL), lambda i, j: (l, f, j, 0))],
        out_specs=pl.BlockSpec((tm, D_MODEL), lambda i, j: (i, 0)),
        out_shape=jax.ShapeDtypeStruct((rows, D_MODEL), F32),
        scratch_shapes=[pltpu.VMEM((tm, D_MODEL), BF16), pltpu.VMEM((tm, D_MODEL), F32)],
        compiler_params=_cp(("parallel", "arbitrary")),
        name='ffn',
    )(x, sh, sc, g, lng, lnb, w_in_bf, w_in_bf, w_out_bf)


ZA = 3 * WIDTH + 2 * A_DECAY_RANK + 2 * A_ICLR_RANK + A_GATE_RANK
ZB = B_Q_RANK + B_KV_RANK + LANE
ZC = 2 * WIDTH
ZD = 5 * WIDTH
ZTOT = ZA + ZB + ZC + ZD


def _inproj_kernel(x_ref, sh_ref, sc_ref, w_ref, za_ref, zb_ref, zc_ref, zd_ref):
    h = (x_ref[...] * (1.0 + sc_ref[0]) + sh_ref[0]).astype(BF16)
    za_ref[...] = _dot(h, w_ref[:, 0:ZA])
    zb_ref[...] = _dot(h, w_ref[:, ZA:ZA + ZB])
    zc_ref[...] = _dot(h, w_ref[:, ZA + ZB:ZA + ZB + ZC])
    zd_ref[...] = _dot(h, w_ref[:, ZA + ZB + ZC:ZTOT])


def _inproj(x, sh, sc, w_in_p, l, tiles_per_seg, tm):
    rows = x.shape[0]
    mod = pl.BlockSpec((1, 1, D_MODEL), lambda i: (i // tiles_per_seg, 0, 0))
    row = lambda w: pl.BlockSpec((tm, w), lambda i: (i, 0))
    return pl.pallas_call(
        _inproj_kernel,
        grid=(rows // tm,),
        in_specs=[row(D_MODEL), mod, mod, pl.BlockSpec((None, D_MODEL, ZTOT), lambda i: (l, 0, 0))],
        out_specs=[row(ZA), row(ZB), row(ZC), row(ZD)],
        out_shape=[jax.ShapeDtypeStruct((rows, w), F32) for w in (ZA, ZB, ZC, ZD)],
        compiler_params=_cp(("parallel",)),
        name='mixer_in',
    )(x, sh, sc, w_in_p)


def _gelu_tanh(x):
    return 0.5 * x * (1.0 + jnp.tanh(np.sqrt(2.0 / np.pi) * (x + 0.044715 * (x * x * x))))


def _outproj_kernel(x_ref, ya_ref, yb_ref, hf_ref, hb_ref, cg_ref, yd_ref, w_ref, g_ref, lng_ref, lnb_ref, o_ref):
    yc = (hf_ref[...] + hb_ref[...]) * _gelu_tanh(cg_ref[...])
    y = _dot(ya_ref[...].astype(BF16), w_ref[0:WIDTH, :])
    y += _dot(yb_ref[...].astype(BF16), w_ref[WIDTH:2 * WIDTH, :])
    y += _dot(yc.astype(BF16), w_ref[2 * WIDTH:3 * WIDTH, :])
    y += _dot(yd_ref[...].astype(BF16), w_ref[3 * WIDTH:4 * WIDTH, :])
    o_ref[...] = _layer_norm_rows(DN_ALPHA * x_ref[...] + g_ref[0] * y, lng_ref[...], lnb_ref[...])


def _outproj(x, ya, yb, hf, hb, zc, yd, w_out_bf, g, lng, lnb, l, tiles_per_seg, tm):
    rows = x.shape[0]
    mod = pl.BlockSpec((1, 1, D_MODEL), lambda i: (i // tiles_per_seg, 0, 0))
    vec = pl.BlockSpec((1, D_MODEL), lambda i: (0, 0))
    row = lambda w: pl.BlockSpec((tm, w), lambda i: (i, 0))
    return pl.pallas_call(
        _outproj_kernel,
        grid=(rows // tm,),
        in_specs=[row(D_MODEL), row(WIDTH), row(WIDTH), row(WIDTH), row(WIDTH),
                  pl.BlockSpec((tm, WIDTH), lambda i: (i, 1)), row(WIDTH),
                  pl.BlockSpec((None, 4 * WIDTH, D_MODEL), lambda i: (l, 0, 0)), mod, vec, vec],
        out_specs=row(D_MODEL),
        out_shape=jax.ShapeDtypeStruct((rows, D_MODEL), F32),
        compiler_params=_cp(("parallel",)),
        name='mixer_out',
    )(x, ya, yb, hf, hb, zc, yd, w_out_bf, g, lng, lnb)


def _scan_maps(n_ctx_chunks, ctx_n, dec_n, n_ctx_seq):
    def n_of(i):
        return jnp.where(i < n_ctx_chunks, ctx_n, dec_n)

    def fwd(i):
        return i

    def bwd(i):
        n = n_of(i)
        return (i // n) * n + (n - 1 - i % n)

    def seq(i):
        return jnp.where(i < n_ctx_chunks, i // ctx_n, n_ctx_seq + (i - n_ctx_chunks) // dec_n)

    def first(i):
        return i % n_of(i) == 0

    def last(i):
        n = n_of(i)
        return i % n == n - 1

    return fwd, bwd, seq, first, last


def _rwkv_pre_kernel(za_ref, w0_ref, w2_ref, a0_ref, a2_ref, g2_ref, kkp_ref, kap_ref, rkp_ref, ebd_ref,
                     kap_o, lwf_o, lwb_o, kdf_o, kdb_o, bbf_o, bbb_o, bon_o, gate_o):
    r = za_ref[:, 0:WIDTH]
    k = za_ref[:, WIDTH:2 * WIDTH]
    v = za_ref[:, 2 * WIDTH:3 * WIDTH]
    o = 3 * WIDTH
    xw = (za_ref[:, o:o + A_DECAY_RANK], za_ref[:, o + A_DECAY_RANK:o + 2 * A_DECAY_RANK])
    o += 2 * A_DECAY_RANK
    xa = (za_ref[:, o:o + A_ICLR_RANK], za_ref[:, o + A_ICLR_RANK:o + 2 * A_ICLR_RANK])
    o += 2 * A_ICLR_RANK
    xg = za_ref[:, o:o + A_GATE_RANK]
    ebd = ebd_ref[...]

    kk = k * kkp_ref[...]
    nrm = jnp.sqrt(_dot(kk * kk, ebd, hi=True))
    kappa = kk / jnp.maximum(nrm, 1e-12)
    kap_o[...] = kappa
    bonus = jnp.zeros_like(r)
    for d, (lw_o, kd_o, bb_o) in enumerate(((lwf_o, kdf_o, bbf_o), (lwb_o, kdb_o, bbb_o))):
        w_log = -_softplus(-(w0_ref[d] + _dot(jnp.tanh(xw[d]).astype(BF16), w2_ref[d].astype(BF16)))) - 0.5
        lw_o[...] = -jnp.exp(w_log)
        a = _sigmoid(a0_ref[d] + _dot(xa[d].astype(BF16), a2_ref[d].astype(BF16)))
        k_d = k * (1.0 + (a - 1.0) * kap_ref[...])
        kd_o[...] = k_d
        bb_o[...] = kappa * a
        bonus += _dot(r * k_d * rkp_ref[...], ebd, hi=True) * v
    bon_o[...] = bonus
    gate_o[...] = _dot(_sigmoid(xg).astype(BF16), g2_ref[...].astype(BF16))


def _rwkv_pre(za, p, ebd, tm):
    rows = za.shape[0]
    row = lambda w: pl.BlockSpec((tm, w), lambda i: (i, 0))
    full = lambda a: pl.BlockSpec(a.shape, lambda i: (0,) * a.ndim)
    args = (p['w0'], p['w2'], p['a0'], p['a2'], p['g2'], p['kk'], p['ka'], p['rk'], ebd)
    return pl.pallas_call(
        _rwkv_pre_kernel,
        grid=(rows // tm,),
        in_specs=[row(ZA)] + [full(a) for a in args],
        out_specs=[row(WIDTH)] * 9,
        out_shape=[jax.ShapeDtypeStruct((rows, WIDTH), F32)] * 9,
        compiler_params=_cp(("parallel",)),
        name='rwkv_pre',
    )(za, *args)


def _rwkv_chunk(r, v, kap, lw, kd, bb, st, mbd, eye, m_strict, m_incl, tri):
    cum = _dot(tri, lw, hi=True)
    tot = jnp.sum(lw, axis=0, keepdims=True)
    g_in = jnp.exp(cum)
    g_inv = jnp.exp(-cum)
    g_ex = jnp.exp(cum - lw)
    g_end = jnp.exp(tot - cum)
    cast = (lambda a: a) if RWKV_HI else (lambda a: a.astype(BF16))
    mask = cast(mbd)
    bd = lambda y: _tile4(cast(y)) * mask
    x = cast(jnp.concatenate([kap * g_ex, r * g_in], axis=0))
    ab = _dot_nt(x, bd(bb * g_inv), hi=RWKV_HI)
    ak = _dot_nt(x, bd(kd * g_inv), hi=RWKV_HI)
    c = r.shape[0]
    strict = m_strict > 0.5
    incl = m_incl > 0.5
    a_ub = jnp.where(strict, ab[:c], 0.0)
    a_rb = jnp.where(incl, ab[c:], 0.0)
    a_uk = jnp.where(strict, ak[:c], 0.0)
    a_rk = jnp.where(incl, ak[c:], 0.0)

    def catmul(pc, q):
        return _dot(cast(pc), bd(q), hi=RWKV_HI)

    xp = -a_ub
    inv = eye + xp
    steps = int(np.log2(c)) - 1
    for _ in range(steps):
        xp = catmul(xp, xp)
        inv = inv + catmul(inv, xp)
    xs = _dot_nt(x, cast(st), hi=RWKV_HI)
    u = -catmul(inv, xs[:c] + catmul(a_uk, v))
    o = xs[c:] + catmul(a_rb, u) + catmul(a_rk, v)
    upd = _dot_tn(cast(jnp.concatenate([u, v], axis=0)),
                  cast(jnp.concatenate([bb * g_end, kd * g_end], axis=0)), hi=RWKV_HI)
    st_new = st * jnp.exp(tot) + upd * mbd
    return o, st_new


def _rwkv_scan_kernel(first_fn, last_fn,
                      rf_ref, vf_ref, kapf_ref, lwf_ref, kdf_ref, bbf_ref,
                      rb_ref, vb_ref, kapb_ref, lwb_ref, kdb_ref, bbb_ref,
                      s0_ref, mbd_ref, eye_ref, lows_ref, lowi_ref, ups_ref, upi_ref, trif_ref, trib_ref,
                      of_ref, ob_ref, sfin_ref, st_scr):
    i = pl.program_id(0)

    @pl.when(first_fn(i))
    def _():
        st_scr[...] = s0_ref[0]

    mbd = mbd_ref[...]
    eye = eye_ref[...]
    o_f, st_f = _rwkv_chunk(rf_ref[...], vf_ref[...], kapf_ref[...], lwf_ref[...], kdf_ref[...], bbf_ref[...],
                            st_scr[0], mbd, eye, lows_ref[...], lowi_ref[...], trif_ref[...])
    o_b, st_b = _rwkv_chunk(rb_ref[...], vb_ref[...], kapb_ref[...], lwb_ref[...], kdb_ref[...], bbb_ref[...],
                            st_scr[1], mbd, eye, ups_ref[...], upi_ref[...], trib_ref[...])
    of_ref[...] = o_f
    ob_ref[...] = o_b
    st_scr[0] = st_f
    st_scr[1] = st_b

    @pl.when(last_fn(i))
    def _():
        sfin_ref[0] = st_scr[...]


def _rwkv_scan(za, pre, s0, hc, geom):
    kap, lwf, lwb, kdf, kdb, bbf, bbb = pre
    rows = za.shape[0]
    fwd, bwd, seq, first, last = _scan_maps(*geom)
    nseq = s0.shape[0]
    blk = lambda m, col=0: pl.BlockSpec((CHUNK, WIDTH), lambda i: (m(i), col))
    const = lambda a: pl.BlockSpec(a.shape, lambda i: (0,) * a.ndim)
    consts = (hc['mbd'], hc['eye'], hc['low_s'], hc['low_i'], hc['up_s'], hc['up_i'], hc['tri_f'], hc['tri_b'])
    st_spec = pl.BlockSpec((1, 2, WIDTH, WIDTH), lambda i: (seq(i), 0, 0, 0))
    return pl.pallas_call(
        functools.partial(_rwkv_scan_kernel, first, last),
        grid=(rows // CHUNK,),
        in_specs=[blk(fwd, 0), blk(fwd, 2), blk(fwd), blk(fwd), blk(fwd), blk(fwd),
                  blk(bwd, 0), blk(bwd, 2), blk(bwd), blk(bwd), blk(bwd), blk(bwd),
                  st_spec] + [const(a) for a in consts],
        out_specs=[blk(fwd), blk(bwd), st_spec],
        out_shape=[jax.ShapeDtypeStruct((rows, WIDTH), F32), jax.ShapeDtypeStruct((rows, WIDTH), F32),
                   jax.ShapeDtypeStruct((nseq, 2, WIDTH, WIDTH), F32)],
        scratch_shapes=[pltpu.VMEM((2, WIDTH, WIDTH), F32)],
        compiler_params=_cp(("arbitrary",)),
        name='rwkv_scan',
    )(za, za, kap, lwf, kdf, bbf, za, za, kap, lwb, kdb, bbb, s0, *consts)


def _rwkv_post_kernel(of_ref, ob_ref, bon_ref, gate_ref, gng_ref, gnb_ref, ebd_ref, o_ref):
    of = of_ref[...] + ob_ref[...]
    avg = ebd_ref[...] * (1.0 / HEAD_DIM)
    mu = _dot(of, avg, hi=True)
    d = of - mu
    var = _dot(d * d, avg, hi=True)
    gn = d * lax.rsqrt(var + RWKV_GN_EPS) * gng_ref[...] + gnb_ref[...]
    o_ref[...] = (gn + bon_ref[...]) * gate_ref[...]


def _rwkv_post(o_f, o_b, bonus, gate, gn_g, gn_b, ebd, tm):
    rows = o_f.shape[0]
    row = pl.BlockSpec((tm, WIDTH), lambda i: (i, 0))
    vec = pl.BlockSpec((1, WIDTH), lambda i: (0, 0))
    return pl.pallas_call(
        _rwkv_post_kernel,
        grid=(rows // tm,),
        in_specs=[row, row, row, row, vec, vec, pl.BlockSpec((WIDTH, WIDTH), lambda i: (0, 0))],
        out_specs=row,
        out_shape=jax.ShapeDtypeStruct((rows, WIDTH), F32),
        compiler_params=_cp(("parallel",)),
        name='rwkv_post',
    )(o_f, o_b, bonus, gate, gn_g, gn_b, ebd)


def _hgrn_chunk(xq, xf, xi, lb, st, tri, ebd, mbd, p_scr, reverse):
    c = xq.shape[0]
    nb = c // SUB
    q = _silu(xq)
    gsig = lb + (1.0 - lb) * _sigmoid(xf)
    kk = 1.0 - gsig
    lg = jnp.log(gsig)
    cum = _dot(tri, lg, hi=True)
    tot = jnp.sum(lg, axis=0, keepdims=True)
    lane_s = lax.broadcasted_iota(jnp.int32, (SUB, WIDTH), 1) % c
    row_l = lax.broadcasted_iota(jnp.int32, (SUB, WIDTH), 0)
    blk = lambda a, i: a[i * SUB:(i + 1) * SUB]
    end_row = (lambda j: j * SUB) if reverse else (lambda j: j * SUB + SUB - 1)
    later = (lambda j: range(0, j)) if reverse else (lambda j: range(j + 1, nb))

    k_end = jnp.concatenate([blk(kk, j) * jnp.exp(cum[end_row(j):end_row(j) + 1] - blk(cum, j))
                             for j in range(nb)], axis=0)
    q_parts, where_part = [], {}
    for j in range(nb):
        for i in later(j):
            where_part[(i, j)] = len(q_parts)
            q_parts.append(blk(q, i) * jnp.exp(blk(cum, i) - cum[end_row(j):end_row(j) + 1]))
    cross = _dot_nt(jnp.concatenate(q_parts, axis=0).astype(BF16), (_tile4(k_end) * mbd).astype(BF16))

    for i in range(nb):
        cb, qb = blk(cum, i), blk(q, i)
        for sl in range(SUB):
            s = i * SUB + sl
            valid = (row_l <= sl) if reverse else (row_l >= sl)
            e = jnp.exp(jnp.where(valid, cb - cum[s:s + 1], NEG))
            p_scr[s * SUB:(s + 1) * SUB, :] = (e * qb * kk[s:s + 1]).astype(BF16)
    same = _dot(p_scr[...], ebd.astype(BF16))

    att_rows = []
    for i in range(nb):
        att = jnp.zeros((SUB, WIDTH), F32)
        for sl in range(SUB):
            s = i * SUB + sl
            att = jnp.where(lane_s == s, same[s * SUB:(s + 1) * SUB], att)
        for j in range(nb):
            if (i, j) in where_part:
                n = where_part[(i, j)]
                att = jnp.where(lane_s // SUB == j, cross[n * SUB:(n + 1) * SUB], att)
        att_rows.append(att)
    att = jnp.concatenate(att_rows, axis=0)
    o = (_dot_nt((q * jnp.exp(cum)).astype(BF16), st.astype(BF16))
         + _dot(att.astype(BF16), (_tile4(xi) * mbd).astype(BF16)))
    upd = _dot_tn(xi.astype(BF16), (kk * jnp.exp(tot - cum)).astype(BF16))
    st_new = st * jnp.exp(tot) + upd * mbd
    return o, st_new


def _hgrn_scan_kernel(first_fn, last_fn,
                      qf_ref, ff_ref, if_ref, qb_ref, fb_ref, ib_ref,
                      lb_ref, s0_ref, mbd_ref, ebd_ref, trif_ref, trib_ref,
                      of_ref, ob_ref, sfin_ref, st_scr, p_scr):
    i = pl.program_id(0)

    @pl.when(first_fn(i))
    def _():
        st_scr[...] = s0_ref[0]

    mbd = mbd_ref[...]
    ebd = ebd_ref[...]
    o_f, st_f = _hgrn_chunk(qf_ref[...], ff_ref[...], if_ref[...], lb_ref[0], st_scr[0], trif_ref[...],
                            ebd, mbd, p_scr, False)
    of_ref[...] = o_f
    st_scr[0] = st_f
    o_b, st_b = _hgrn_chunk(qb_ref[...], fb_ref[...], ib_ref[...], lb_ref[1], st_scr[1], trib_ref[...],
                            ebd, mbd, p_scr, True)
    ob_ref[...] = o_b
    st_scr[1] = st_b

    @pl.when(last_fn(i))
    def _():
        sfin_ref[0] = st_scr[...]


def _hgrn_scan(zd, lb, s0, hc, geom):
    rows = zd.shape[0]
    fwd, bwd, seq, first, last = _scan_maps(*geom)
    nseq = s0.shape[0]
    blk = lambda m, col: pl.BlockSpec((CHUNK, WIDTH), lambda i: (m(i), col))
    const = lambda a: pl.BlockSpec(a.shape, lambda i: (0,) * a.ndim)
    consts = (hc['mbd'], hc['mbd'], hc['tri_f'], hc['tri_b'])
    st_spec = pl.BlockSpec((1, 2, WIDTH, WIDTH), lambda i: (seq(i), 0, 0, 0))
    return pl.pallas_call(
        functools.partial(_hgrn_scan_kernel, first, last),
        grid=(rows // CHUNK,),
        in_specs=[blk(fwd, 0), blk(fwd, 1), blk(fwd, 3), blk(bwd, 0), blk(bwd, 2), blk(bwd, 3),
                  const(lb), st_spec] + [const(a) for a in consts],
        out_specs=[blk(fwd, 0), blk(bwd, 0), st_spec],
        out_shape=[jax.ShapeDtypeStruct((rows, WIDTH), F32), jax.ShapeDtypeStruct((rows, WIDTH), F32),
                   jax.ShapeDtypeStruct((nseq, 2, WIDTH, WIDTH), F32)],
        scratch_shapes=[pltpu.VMEM((2, WIDTH, WIDTH), F32), pltpu.VMEM((CHUNK * SUB, WIDTH), BF16)],
        compiler_params=_cp(("arbitrary",)),
        name='hgrn_scan',
    )(zd, zd, zd, zd, zd, zd, lb, s0, *consts)


def _hgrn_post_kernel(of_ref, ob_ref, xg_ref, gn_ref, ebd_ref, o_ref):
    of = of_ref[...] + ob_ref[...]
    ms = _dot(of * of, ebd_ref[...] * (1.0 / HEAD_DIM), hi=True)
    o_ref[...] = of * lax.rsqrt(ms + RMS_EPS) * gn_ref[...] * _silu(xg_ref[...])


def _hgrn_post(o_f, o_b, zd, gn_g, ebd, tm):
    rows = o_f.shape[0]
    row = pl.BlockSpec((tm, WIDTH), lambda i: (i, 0))
    return pl.pallas_call(
        _hgrn_post_kernel,
        grid=(rows // tm,),
        in_specs=[row, row, pl.BlockSpec((tm, WIDTH), lambda i: (i, 4)),
                  pl.BlockSpec((1, WIDTH), lambda i: (0, 0)), pl.BlockSpec((WIDTH, WIDTH), lambda i: (0, 0))],
        out_specs=row,
        out_shape=jax.ShapeDtypeStruct((rows, WIDTH), F32),
        compiler_params=_cp(("parallel",)),
        name='hgrn_post',
    )(o_f, o_b, zd, gn_g, ebd)


def _lru_pre_kernel(seq_of_tile, xp_ref, x_ref, xn_ref, cw_ref, cb_ref, wg_ref, bg_ref, lam_ref,
                    af_o, bf_o, ab_o, bb_o, pad_scr):
    i = pl.program_id(0)
    tm = x_ref.shape[0]
    seqlen = seq_of_tile(i)
    pad_scr[0:8, :] = xp_ref[:, 0:WIDTH]
    pad_scr[8:8 + tm, :] = x_ref[:, 0:WIDTH]
    pad_scr[8 + tm:16 + tm, :] = xn_ref[:, 0:WIDTH]
    pos = jnp.bitwise_and(lax.broadcasted_iota(jnp.int32, (tm, WIDTH), 0) + i * tm, seqlen - 1)
    u = jnp.zeros((tm, WIDTH), F32) + cb_ref[...]
    for j in range(C_CONV):
        off = j - C_CONV // 2
        tap = pad_scr[pl.ds(8 + off, tm), :]
        ok = jnp.logical_and(pos + off >= 0, pos + off < seqlen)
        u += jnp.where(ok, tap, 0.0) * cw_ref[j:j + 1, :]
    gates = _sigmoid(_dot(u.astype(BF16), wg_ref[...].astype(BF16)) + bg_ref[...])
    for d, (a_o, b_o) in enumerate(((af_o, bf_o), (ab_o, bb_o))):
        r = gates[:, (2 * d) * WIDTH:(2 * d + 1) * WIDTH]
        ig = gates[:, (2 * d + 1) * WIDTH:(2 * d + 2) * WIDTH]
        log_a = -C_POW * r * _softplus(-lam_ref[d])
        a = jnp.exp(log_a)
        a_o[...] = a
        b_o[...] = jnp.sqrt(-jnp.tanh(log_a) * (a * a + 1.0)) * (ig * u)


def _lru_pre(zc, p, ctx_len, dec_len, n_ctx_tiles, tm):
    rows = zc.shape[0]
    nb8 = rows // 8
    per = tm // 8
    seq_of_tile = lambda i: jnp.where(i < n_ctx_tiles, ctx_len, dec_len)
    row = pl.BlockSpec((tm, WIDTH), lambda i: (i, 0))
    full = lambda a: pl.BlockSpec(a.shape, lambda i: (0,) * a.ndim)
    args = (p['conv_w'], p['conv_b'], p['wg'], p['bg'], p['lam'])
    return pl.pallas_call(
        functools.partial(_lru_pre_kernel, seq_of_tile),
        grid=(rows // tm,),
        in_specs=[pl.BlockSpec((8, WIDTH), lambda i: (jnp.maximum(i * per - 1, 0), 0)),
                  row,
                  pl.BlockSpec((8, WIDTH), lambda i: (jnp.minimum((i + 1) * per, nb8 - 1), 0))]
                 + [full(a) for a in args],
        out_specs=[row] * 4,
        out_shape=[jax.ShapeDtypeStruct((rows, WIDTH), F32)] * 4,
        scratch_shapes=[pltpu.VMEM((tm + 16, WIDTH), F32)],
        compiler_params=_cp(("parallel",)),
        name='rglru_pre',
    )(zc, zc, zc, *args)


def _lru_scan_kernel(first_fn, last_fn, af_ref, bf_ref, ab_ref, bb_ref, h0_ref, hf_ref, hb_ref, hfin_ref, h_scr):
    i = pl.program_id(0)
    c = af_ref.shape[0]

    @pl.when(first_fn(i))
    def _():
        h_scr[...] = h0_ref[0]

    def step(t, carry):
        h_f, h_b = carry
        tb = c - 1 - t
        h_f = af_ref[pl.ds(t, 1), :] * h_f + bf_ref[pl.ds(t, 1), :]
        h_b = ab_ref[pl.ds(tb, 1), :] * h_b + bb_ref[pl.ds(tb, 1), :]
        hf_ref[pl.ds(t, 1), :] = h_f
        hb_ref[pl.ds(tb, 1), :] = h_b
        return h_f, h_b

    h_f, h_b = lax.fori_loop(0, c, step, (h_scr[0:1, :], h_scr[1:2, :]))
    h_scr[0:1, :] = h_f
    h_scr[1:2, :] = h_b

    @pl.when(last_fn(i))
    def _():
        hfin_ref[0] = h_scr[...]


def _lru_scan(coef, h0, geom):
    a_f, b_f, a_b, b_b = coef
    rows = a_f.shape[0]
    fwd, bwd, seq, first, last = _scan_maps(*geom)
    nseq = h0.shape[0]
    blk = lambda m: pl.BlockSpec((LRU_CHUNK, WIDTH), lambda i: (m(i), 0))
    st_spec = pl.BlockSpec((1, 2, WIDTH), lambda i: (seq(i), 0, 0))
    return pl.pallas_call(
        functools.partial(_lru_scan_kernel, first, last),
        grid=(rows // LRU_CHUNK,),
        in_specs=[blk(fwd), blk(fwd), blk(bwd), blk(bwd), st_spec],
        out_specs=[blk(fwd), blk(bwd), st_spec],
        out_shape=[jax.ShapeDtypeStruct((rows, WIDTH), F32), jax.ShapeDtypeStruct((rows, WIDTH), F32),
                   jax.ShapeDtypeStruct((nseq, 2, WIDTH), F32)],
        scratch_shapes=[pltpu.VMEM((2, WIDTH), F32)],
        compiler_params=_cp(("arbitrary",)),
        name='rglru_scan',
    )(a_f, b_f, a_b, b_b, h0)


def _rope_slab(x, cos, sin):
    lane = lax.broadcasted_iota(jnp.int32, x.shape, 1)
    rot = jnp.where(lane % (B_ROPE // 2) < B_ROPE // 4, -pltpu.roll(x, LANE - B_ROPE // 4, 1),
                    pltpu.roll(x, B_ROPE // 4, 1))
    return x * cos + rot * sin


def _kv_up(ckv_bf, kpe_slab, wuk_ref, wuv_ref, k_o, v_o):
    kn = _dot(ckv_bf, wuk_ref[...])
    for h in range(N_HEADS):
        k_o[:, h * SLAB:(h + 1) * SLAB] = (kn[:, h * SLAB:(h + 1) * SLAB] + kpe_slab).astype(BF16)
    v_o[...] = _dot(ckv_bf, wuv_ref[...]).astype(BF16)


def _mla_prep_kernel(zb_ref, cos_ref, sin_ref, qn_ref, kvn_ref, wuq_ref, wuk_ref, wuv_ref,
                     ckv_o, q_o, k_o, v_o):
    cq = zb_ref[:, 0:B_Q_RANK]
    ckv = zb_ref[:, B_Q_RANK:B_Q_RANK + B_KV_RANK]
    kpe = zb_ref[:, B_Q_RANK + B_KV_RANK:ZB]
    cos = cos_ref[...]
    sin = sin_ref[...]
    cqn = cq * lax.rsqrt(jnp.mean(cq * cq, axis=-1, keepdims=True) + RMS_EPS) * qn_ref[...]
    ckvn = ckv * lax.rsqrt(jnp.mean(ckv * ckv, axis=-1, keepdims=True) + RMS_EPS) * kvn_ref[...]
    ckv_o[...] = ckvn
    q = _dot(cqn.astype(BF16), wuq_ref[...])
    for h in range(N_HEADS):
        q_o[:, h * SLAB:(h + 1) * SLAB] = _rope_slab(q[:, h * SLAB:(h + 1) * SLAB], cos, sin).astype(BF16)
    _kv_up(ckvn.astype(BF16), _rope_slab(kpe, cos, sin), wuk_ref, wuv_ref, k_o, v_o)


def _mla_prep(zb, cos, sin, p, tm):
    rows = zb.shape[0]
    row = lambda w: pl.BlockSpec((tm, w), lambda i: (i, 0))
    full = lambda a: pl.BlockSpec(a.shape, lambda i: (0,) * a.ndim)
    args = (p['qn_g'], p['kvn_g'], p['wuq'], p['wuk'], p['wuv'])
    return pl.pallas_call(
        _mla_prep_kernel,
        grid=(rows // tm,),
        in_specs=[row(ZB), row(LANE), row(LANE)] + [full(a) for a in args],
        out_specs=[row(B_KV_RANK), row(N_HEADS * SLAB), row(N_HEADS * SLAB), row(WIDTH)],
        out_shape=[jax.ShapeDtypeStruct((rows, B_KV_RANK), F32), jax.ShapeDtypeStruct((rows, N_HEADS * SLAB), BF16),
                   jax.ShapeDtypeStruct((rows, N_HEADS * SLAB), BF16), jax.ShapeDtypeStruct((rows, WIDTH), BF16)],
        compiler_params=_cp(("parallel",)),
        name='mla_prep',
    )(zb, cos, sin, *args)


def _mla_cache_kernel(ckv_ref, kpe_ref, wuk_ref, wuv_ref, k_o, v_o):
    _kv_up(ckv_ref[...].astype(BF16), kpe_ref[...], wuk_ref, wuv_ref, k_o, v_o)


def _mla_cache(ckv, kpe_slab, p):
    rows = ckv.shape[0]
    return pl.pallas_call(
        _mla_cache_kernel,
        out_shape=[jax.ShapeDtypeStruct((rows, N_HEADS * SLAB), BF16), jax.ShapeDtypeStruct((rows, WIDTH), BF16)],
        compiler_params=pltpu.CompilerParams(vmem_limit_bytes=VMEM_LIMIT),
        name='mla_cache',
    )(ckv, kpe_slab, p['wuk'], p['wuv'])


def _attend_heads(q_ref, kv_refs, o_ref):
    scale = (B_NOPE + B_ROPE) ** -0.5
    tq = q_ref.shape[0]
    lane_head = lax.broadcasted_iota(jnp.int32, (tq, WIDTH), 1) // HEAD_DIM
    out = jnp.zeros((tq, WIDTH), F32)
    for h in range(N_HEADS):
        qh = q_ref[:, h * SLAB:(h + 1) * SLAB]
        ss = [_dot_nt(qh, k_ref[:, h * SLAB:(h + 1) * SLAB]) * scale for k_ref, _ in kv_refs]
        m = ss[0].max(axis=-1, keepdims=True)
        for s in ss[1:]:
            m = jnp.maximum(m, s.max(axis=-1, keepdims=True))
        den = jnp.zeros((tq, 1), F32)
        num = jnp.zeros((tq, WIDTH), F32)
        for s, (_, v_ref) in zip(ss, kv_refs):
            e = jnp.exp(s - m)
            den += e.sum(axis=-1, keepdims=True)
            num += _dot(e.astype(BF16), v_ref[...])
        out = jnp.where(lane_head == h, num / den, out)
    o_ref[...] = out


def _attn_ctx_kernel(q_ref, k_ref, v_ref, o_ref):
    _attend_heads(q_ref, [(k_ref, v_ref)], o_ref)


def _attn_dec_kernel(q_ref, k_ref, v_ref, kc_ref, vc_ref, o_ref):
    _attend_heads(q_ref, [(k_ref, v_ref), (kc_ref, vc_ref)], o_ref)


def _attn_ctx(q, k, v, n_seq, seq_len):
    blk = lambda w: pl.BlockSpec((seq_len, w), lambda b: (b, 0))
    return pl.pallas_call(
        _attn_ctx_kernel,
        grid=(n_seq,),
        in_specs=[blk(N_HEADS * SLAB), blk(N_HEADS * SLAB), blk(WIDTH)],
        out_specs=blk(WIDTH),
        out_shape=jax.ShapeDtypeStruct((n_seq * seq_len, WIDTH), F32),
        compiler_params=_cp(("parallel",)),
        name='attn_ctx',
    )(q, k, v)


def _attn_dec(q, k, v, kc, vc, seg0, n_dec, dec_len, past, tq):
    nq = dec_len // tq
    return pl.pallas_call(
        _attn_dec_kernel,
        grid=(n_dec, nq),
        in_specs=[pl.BlockSpec((tq, N_HEADS * SLAB), lambda b, i: ((seg0 + b) * nq + i, 0)),
                  pl.BlockSpec((dec_len, N_HEADS * SLAB), lambda b, i: (seg0 + b, 0)),
                  pl.BlockSpec((dec_len, WIDTH), lambda b, i: (seg0 + b, 0)),
                  pl.BlockSpec((past, N_HEADS * SLAB), lambda b, i: (b, 0)),
                  pl.BlockSpec((past, WIDTH), lambda b, i: (b, 0))],
        out_specs=pl.BlockSpec((tq, WIDTH), lambda b, i: (b * nq + i, 0)),
        out_shape=jax.ShapeDtypeStruct((n_dec * dec_len, WIDTH), F32),
        compiler_params=_cp(("parallel", "arbitrary")),
        name='attn_dec',
    )(q, k, v, kc, vc)


def _rope_tables(n_ctx_rows, n_dec, dec_len):
    rows = dec_len // GRID_W
    row = np.repeat(np.arange(rows, dtype=np.float32), GRID_W)
    col = np.tile(np.arange(GRID_W, dtype=np.float32), rows)
    half = B_ROPE // 2
    inv = jnp.power(ROPE_BASE, -jnp.arange(0, half, 2, dtype=F32) / half)
    ang_r = jnp.asarray(row)[:, None] * inv
    ang_c = jnp.asarray(col)[:, None] * inv
    ang = jnp.concatenate([ang_r, ang_r, ang_c, ang_c], axis=-1)
    cos = jnp.pad(jnp.cos(ang), ((0, 0), (0, LANE - B_ROPE)), constant_values=1.0)
    sin = jnp.pad(jnp.sin(ang), ((0, 0), (0, LANE - B_ROPE)))
    cos = jnp.concatenate([jnp.ones((n_ctx_rows, LANE), F32)] + [cos] * n_dec, axis=0)
    sin = jnp.concatenate([jnp.zeros((n_ctx_rows, LANE), F32)] + [sin] * n_dec, axis=0)
    return cos, sin


def _block_diag(w):
    g, n, m = w.shape[-3:]
    eye = jnp.eye(g, dtype=w.dtype)
    out = w[..., :, :, None, :] * eye[:, None, :, None]
    return out.reshape(w.shape[:-3] + (g * n, g * m))


def _head_blocks(s):
    lead = s.shape[:-2]
    s = s.reshape(lead + (N_HEADS, HEAD_DIM, N_HEADS, HEAD_DIM))
    return jnp.stack([s[..., h, :, h, :] for h in range(N_HEADS)], axis=-3)


def kernel(x_prompt, x_sample, cache_mla_ckv, cache_mla_kpe, state_rwkv, state_rglru, state_hgrn, c, c_ctx,
           w_ada, b_ada, ln_g, ln_b, w_ffn_in, w_ffn_out, w_in, w_out,
           rwkv_w0, rwkv_w2, rwkv_a0, rwkv_a2, rwkv_g2, rwkv_kk, rwkv_ka, rwkv_rk, rwkv_gn_g, rwkv_gn_b,
           mla_qn_g, mla_w_uq, mla_kvn_g, mla_w_ukv,
           rglru_conv_w, rglru_conv_b, rglru_wa, rglru_ba, rglru_wx, rglru_bx, rglru_lam,
           hgrn_lb, hgrn_gn_g):
    n_ctx, ctx_len, _ = x_prompt.shape
    n_dec, dec_len, _ = x_sample.shape
    past = cache_mla_ckv.shape[2]
    seg = n_ctx * ctx_len
    assert seg == dec_len, "context rows must form one segment of the decode sequence length"
    assert ctx_len & (ctx_len - 1) == 0 and dec_len & (dec_len - 1) == 0, "sequence lengths must be powers of two"
    n_seg = 1 + n_dec
    rows = n_seg * seg
    tm = min(TM, seg)
    tq = min(TQ, dec_len)
    tiles_per_seg = seg // tm
    nseq = n_ctx + n_dec
    geom = (seg // CHUNK, ctx_len // CHUNK, dec_len // CHUNK, n_ctx)
    lru_c = min(LRU_CHUNK, ctx_len)
    assert lru_c == LRU_CHUNK
    lru_geom = (seg // LRU_CHUNK, ctx_len // LRU_CHUNK, dec_len // LRU_CHUNK, n_ctx)

    hc = {k: jnp.asarray(v) for k, v in _head_consts().items()}
    ebd = hc['mbd']
    cos, sin = _rope_tables(seg, n_dec, dec_len)

    cond8 = jnp.zeros((8, D_MODEL), F32).at[0].set(c_ctx).at[1:1 + n_dec].set(c)
    mods = _ada(cond8, w_ada, b_ada).reshape(DEPTH, 8, N_MOD, D_MODEL)
    mods = jnp.transpose(mods, (0, 2, 1, 3))[:, :, :n_seg, None, :]

    w_ffn_in_bf = w_ffn_in.astype(BF16)
    w_ffn_out_bf = w_ffn_out.astype(BF16)
    w_out_bf = w_out.astype(BF16)
    kpe_end = ZA + B_Q_RANK + B_KV_RANK + B_ROPE
    w_in_p = jnp.concatenate([w_in[:, :, :kpe_end], jnp.zeros((DEPTH, D_MODEL, LANE - B_ROPE), F32),
                              w_in[:, :, kpe_end:]], axis=-1).astype(BF16)
    wuq = mla_w_uq.reshape(DEPTH, B_Q_RANK, N_HEADS, B_NOPE + B_ROPE)
    wuq_p = jnp.concatenate([wuq[..., B_NOPE:], wuq[..., :B_NOPE],
                             jnp.zeros((DEPTH, B_Q_RANK, N_HEADS, SLAB - B_NOPE - B_ROPE), F32)], axis=-1)
    wuq_p = wuq_p.reshape(DEPTH, B_Q_RANK, N_HEADS * SLAB).astype(BF16)
    wukv = mla_w_ukv.reshape(DEPTH, B_KV_RANK, N_HEADS, B_NOPE + B_VDIM)
    wuk_p = jnp.concatenate([jnp.zeros((DEPTH, B_KV_RANK, N_HEADS, B_ROPE), F32), wukv[..., :B_NOPE],
                             jnp.zeros((DEPTH, B_KV_RANK, N_HEADS, SLAB - B_NOPE - B_ROPE), F32)], axis=-1)
    wuk_p = wuk_p.reshape(DEPTH, B_KV_RANK, N_HEADS * SLAB).astype(BF16)
    wuv_p = wukv[..., B_NOPE:].reshape(DEPTH, B_KV_RANK, WIDTH).astype(BF16)
    lru_wg = jnp.concatenate([_block_diag(rglru_wa[:, 0]), _block_diag(rglru_wx[:, 0]),
                              _block_diag(rglru_wa[:, 1]), _block_diag(rglru_wx[:, 1])], axis=-1)
    lru_bg = jnp.concatenate([rglru_ba[:, 0], rglru_bx[:, 0], rglru_ba[:, 1], rglru_bx[:, 1]], axis=-1)[:, None, :]
    lbs = _lower_bounds(hgrn_lb)

    zeros_st = jnp.zeros((n_ctx, DEPTH, 2, WIDTH, WIDTH), F32)
    rwkv_s0 = jnp.concatenate([zeros_st, _block_diag(state_rwkv)], axis=0)
    hgrn_s0 = jnp.concatenate([zeros_st, _block_diag(jnp.swapaxes(state_hgrn, -1, -2))], axis=0)
    lru_h0 = jnp.concatenate([jnp.zeros((n_ctx, DEPTH, 2, WIDTH), F32), state_rglru], axis=0)
    cache_kpe_slab = jnp.pad(cache_mla_kpe, ((0, 0), (0, 0), (0, 0), (0, LANE - B_ROPE)))

    x = jnp.concatenate([x_prompt.reshape(seg, D_MODEL), x_sample.reshape(n_dec * dec_len, D_MODEL)], axis=0)
    new_ckv, new_kpe, new_rwkv, new_lru, new_hgrn = [], [], [], [], []
    for l in range(DEPTH):
        m = mods[l]
        lng = ln_g[l][:, None, :]
        lnb = ln_b[l][:, None, :]
        x = _ffn(x, m[0], m[1], m[2], lng[0], lnb[0], w_ffn_in_bf, w_ffn_out_bf, l, 0, tiles_per_seg, tm)
        za, zb, zc, zd = _inproj(x, m[3], m[4], w_in_p, l, tiles_per_seg, tm)

        pa = {'w0': rwkv_w0[l][:, None, :], 'w2': rwkv_w2[l], 'a0': rwkv_a0[l][:, None, :], 'a2': rwkv_a2[l],
              'g2': rwkv_g2[l], 'kk': rwkv_kk[l][None, :], 'ka': rwkv_ka[l][None, :], 'rk': rwkv_rk[l][None, :]}
        pre = _rwkv_pre(za, pa, ebd, tm)
        o_f, o_b, s_fin = _rwkv_scan(za, pre[:7], rwkv_s0[:, l], hc, geom)
        ya = _rwkv_post(o_f, o_b, pre[7], pre[8], rwkv_gn_g[l][None, :], rwkv_gn_b[l][None, :], ebd, tm)
        new_rwkv.append(_head_blocks(s_fin[:n_ctx]))

        pb = {'qn_g': mla_qn_g[l][None, :], 'kvn_g': mla_kvn_g[l][None, :], 'wuq': wuq_p[l], 'wuk': wuk_p[l],
              'wuv': wuv_p[l]}
        ckvn, q_all, k_all, v_all = _mla_prep(zb, cos, sin, pb, tm)
        kc, vc = _mla_cache(cache_mla_ckv[:, l].reshape(n_dec * past, B_KV_RANK),
                            cache_kpe_slab[:, l].reshape(n_dec * past, LANE), pb)
        yb = jnp.concatenate([_attn_ctx(q_all, k_all, v_all, n_ctx, ctx_len),
                              _attn_dec(q_all, k_all, v_all, kc, vc, 1, n_dec, dec_len, past, tq)], axis=0)
        new_ckv.append(ckvn[:seg].reshape(n_ctx, ctx_len, B_KV_RANK))
        new_kpe.append(zb[:seg, B_Q_RANK + B_KV_RANK:B_Q_RANK + B_KV_RANK + B_ROPE].reshape(n_ctx, ctx_len, B_ROPE))

        pc = {'conv_w': rglru_conv_w[l], 'conv_b': rglru_conv_b[l][None, :], 'wg': lru_wg[l], 'bg': lru_bg[l],
              'lam': rglru_lam[l][:, None, :]}
        coef = _lru_pre(zc, pc, ctx_len, dec_len, tiles_per_seg, tm)
        h_f, h_b, h_fin = _lru_scan(coef, lru_h0[:, l], lru_geom)
        new_lru.append(h_fin[:n_ctx])

        od_f, od_b, sd_fin = _hgrn_scan(zd, lbs[l], hgrn_s0[:, l], hc, geom)
        yd = _hgrn_post(od_f, od_b, zd, hgrn_gn_g[l][None, :], ebd, tm)
        new_hgrn.append(jnp.swapaxes(_head_blocks(sd_fin[:n_ctx]), -1, -2))

        x = _outproj(x, ya, yb, h_f, h_b, zc, yd, w_out_bf, m[5], lng[1], lnb[1], l, tiles_per_seg, tm)
        x = _ffn(x, m[6], m[7], m[8], lng[2], lnb[2], w_ffn_in_bf, w_ffn_out_bf, l, 1, tiles_per_seg, tm)

    y_prompt = x[:seg].reshape(n_ctx, ctx_len, D_MODEL)
    y_sample = x[seg:].reshape(n_dec, dec_len, D_MODEL)
    return (y_prompt, y_sample, jnp.stack(new_ckv, axis=1), jnp.stack(new_kpe, axis=1),
            jnp.stack(new_rwkv, axis=1), jnp.stack(new_lru, axis=1), jnp.stack(new_hgrn, axis=1))
```

```python
import functools

import numpy as np
import jax
import jax.numpy as jnp
from jax import lax
from jax.experimental import pallas as pl
from jax.experimental.pallas import tpu as pltpu

D_MODEL = 1024
DEPTH = 4
GRID_W = 64
HEAD_DIM = 64
N_HEADS = 4
WIDTH = N_HEADS * HEAD_DIM
A_DECAY_RANK = 64
A_ICLR_RANK = 64
A_GATE_RANK = 128
B_NOPE = 64
B_ROPE = 32
B_VDIM = 64
B_Q_RANK = 256
B_KV_RANK = 128
C_CONV = 4
C_POW = 8.0
D_FF = 2816
N_MOD = 9
ROPE_BASE = 10000.0
LN_EPS = 1e-5
RMS_EPS = 1e-6
RWKV_GN_EPS = 64e-5
DN_ALPHA = (2 * DEPTH) ** 0.25

LANE = 128
SLAB = 128
CHUNK = 64
SUB = 16
LRU_CHUNK = 256
LRU_ROWS = 8
TM = 512
TF = 1408
TQ = 256
VMEM_LIMIT = 56 * 1024 * 1024

F32 = jnp.float32
BF16 = jnp.bfloat16
HI = lax.Precision.HIGHEST
NEG = -1e30
RWKV_HI = False


def _cp(sem):
    return pltpu.CompilerParams(dimension_semantics=sem, vmem_limit_bytes=VMEM_LIMIT)


def _dot(a, b, hi=False):
    return jnp.dot(a, b, preferred_element_type=F32, precision=HI if hi else None)


def _dot_nt(a, b, hi=False):
    return lax.dot_general(a, b, (((1,), (1,)), ((), ())), preferred_element_type=F32,
                           precision=HI if hi else None)


def _dot_tn(a, b, hi=False):
    return lax.dot_general(a, b, (((0,), (0,)), ((), ())), preferred_element_type=F32,
                           precision=HI if hi else None)


def _sigmoid(x):
    return 1.0 / (1.0 + jnp.exp(-x))


def _silu(x):
    return x * _sigmoid(x)


def _softplus(x):
    return jnp.maximum(x, 0.0) + jnp.log(1.0 + jnp.exp(-jnp.abs(x)))


def _tile4(y):
    return jnp.concatenate([y, y, y, y], axis=0)


def _head_consts():
    r = np.arange(WIDTH)
    same = (r[:, None] // HEAD_DIM) == (r[None, :] // HEAD_DIM)
    t = np.arange(CHUNK)[:, None]
    j = (np.arange(WIDTH) % CHUNK)[None, :]
    tt = np.arange(CHUNK)
    return {
        'mbd': same.astype(np.float32),
        'low_s': (j < t).astype(np.float32), 'low_i': (j <= t).astype(np.float32),
        'up_s': (j > t).astype(np.float32), 'up_i': (j >= t).astype(np.float32),
        'eye': (j == t).astype(np.float32),
        'tri_f': (tt[None, :] <= tt[:, None]).astype(np.float32),
        'tri_b': (tt[None, :] >= tt[:, None]).astype(np.float32),
    }


def _lb_kernel(x_ref, o_ref):
    x = x_ref[...]
    m = jnp.max(x, axis=0, keepdims=True)
    e = jnp.exp(x - m)
    sm = e / jnp.sum(e, axis=0, keepdims=True)
    run = sm[0:1]
    rows = [run - sm[0:1]]
    for l in range(1, DEPTH):
        run = run + sm[l:l + 1]
        rows.append(run - sm[0:1])
    o_ref[...] = jnp.concatenate(rows, axis=0)


def _lower_bounds(hgrn_lb):
    flat = hgrn_lb.reshape(DEPTH, 2 * WIDTH)
    out = pl.pallas_call(_lb_kernel, out_shape=jax.ShapeDtypeStruct(flat.shape, F32), name='hgrn_lb')(flat)
    return out.reshape(DEPTH, 2, 1, WIDTH)


def _ada_kernel(c_ref, w_ref, b_ref, o_ref):
    h = _silu(c_ref[...]).astype(BF16)
    o_ref[0] = _dot(h, w_ref[0].astype(BF16)) + b_ref[0]


def _ada(cond8, w_ada, b_ada):
    n = N_MOD * D_MODEL
    tn = 2304
    return pl.pallas_call(
        _ada_kernel,
        grid=(DEPTH, n // tn),
        in_specs=[pl.BlockSpec((8, D_MODEL), lambda l, j: (0, 0)),
                  pl.BlockSpec((1, D_MODEL, tn), lambda l, j: (l, 0, j)),
                  pl.BlockSpec((1, 1, tn), lambda l, j: (l, 0, j))],
        out_specs=pl.BlockSpec((1, 8, tn), lambda l, j: (l, 0, j)),
        out_shape=jax.ShapeDtypeStruct((DEPTH, 8, n), F32),
        compiler_params=_cp(("arbitrary", "arbitrary")),
        name='ada',
    )(cond8, w_ada, b_ada.reshape(DEPTH, 1, n))


def _layer_norm_rows(y, g, b):
    mu = jnp.mean(y, axis=-1, keepdims=True)
    d = y - mu
    var = jnp.mean(d * d, axis=-1, keepdims=True)
    return d * lax.rsqrt(var + LN_EPS) * g + b


def _ffn_kernel(x_ref, sh_ref, sc_ref, g_ref, lng_ref, lnb_ref, wg_ref, wu_ref, wo_ref, o_ref, h_scr, acc_scr):
    j = pl.program_id(1)

    @pl.when(j == 0)
    def _():
        h_scr[...] = (x_ref[...] * (1.0 + sc_ref[0]) + sh_ref[0]).astype(BF16)
        acc_scr[...] = jnp.zeros_like(acc_scr)

    h = h_scr[...]
    gate = _dot(h, wg_ref[...])
    up = _dot(h, wu_ref[...])
    acc_scr[...] += _dot((_silu(gate) * up).astype(BF16), wo_ref[...])

    @pl.when(j == pl.num_programs(1) - 1)
    def _():
        y = DN_ALPHA * x_ref[...] + 0.5 * g_ref[0] * acc_scr[...]
        o_ref[...] = _layer_norm_rows(y, lng_ref[...], lnb_ref[...])


def _ffn(x, sh, sc, g, lng, lnb, w_in_bf, w_out_bf, l, f, tiles_per_seg, tm):
    rows = x.shape[0]
    nj = D_FF // TF
    mod = pl.BlockSpec((1, 1, D_MODEL), lambda i, j: (i // tiles_per_seg, 0, 0))
    vec = pl.BlockSpec((1, D_MODEL), lambda i, j: (0, 0))
    return pl.pallas_call(
        _ffn_kernel,
        grid=(rows // tm, nj),
        in_specs=[pl.BlockSpec((tm, D_MODEL), lambda i, j: (i, 0)), mod, mod, mod, vec, vec,
                  pl.BlockSpec((None, None, D_MODEL, TF), lambda i, j: (l, f, 0, j)),
                  pl.BlockSpec((None, None, D_MODEL, TF), lambda i, j: (l, f, 0, nj + j)),
                  pl.BlockSpec((None, None, TF, D_MODEL), lambda i, j: (l, f, j, 0))],
        out_specs=pl.BlockSpec((tm, D_MODEL), lambda i, j: (i, 0)),
        out_shape=jax.ShapeDtypeStruct((rows, D_MODEL), F32),
        scratch_shapes=[pltpu.VMEM((tm, D_MODEL), BF16), pltpu.VMEM((tm, D_MODEL), F32)],
        compiler_params=_cp(("parallel", "arbitrary")),
        name='ffn',
    )(x, sh, sc, g, lng, lnb, w_in_bf, w_in_bf, w_out_bf)


ZA = 3 * WIDTH + 2 * A_DECAY_RANK + 2 * A_ICLR_RANK + A_GATE_RANK
ZB = B_Q_RANK + B_KV_RANK + LANE
ZC = 2 * WIDTH
ZD = 5 * WIDTH
ZTOT = ZA + ZB + ZC + ZD


def _inproj_kernel(x_ref, sh_ref, sc_ref, w_ref, za_ref, zb_ref, zc_ref, zd_ref):
    h = (x_ref[...] * (1.0 + sc_ref[0]) + sh_ref[0]).astype(BF16)
    za_ref[...] = _dot(h, w_ref[:, 0:ZA])
    zb_ref[...] = _dot(h, w_ref[:, ZA:ZA + ZB])
    zc_ref[...] = _dot(h, w_ref[:, ZA + ZB:ZA + ZB + ZC])
    zd_ref[...] = _dot(h, w_ref[:, ZA + ZB + ZC:ZTOT])


def _inproj(x, sh, sc, w_in_p, l, tiles_per_seg, tm):
    rows = x.shape[0]
    mod = pl.BlockSpec((1, 1, D_MODEL), lambda i: (i // tiles_per_seg, 0, 0))
    row = lambda w: pl.BlockSpec((tm, w), lambda i: (i, 0))
    return pl.pallas_call(
        _inproj_kernel,
        grid=(rows // tm,),
        in_specs=[row(D_MODEL), mod, mod, pl.BlockSpec((None, D_MODEL, ZTOT), lambda i: (l, 0, 0))],
        out_specs=[row(ZA), row(ZB), row(ZC), row(ZD)],
        out_shape=[jax.ShapeDtypeStruct((rows, w), F32) for w in (ZA, ZB, ZC, ZD)],
        compiler_params=_cp(("parallel",)),
        name='mixer_in',
    )(x, sh, sc, w_in_p)


def _gelu_tanh(x):
    return 0.5 * x * (1.0 + jnp.tanh(np.sqrt(2.0 / np.pi) * (x + 0.044715 * (x * x * x))))


def _outproj_kernel(x_ref, ya_ref, yb_ref, hf_ref, hb_ref, cg_ref, yd_ref, w_ref, g_ref, lng_ref, lnb_ref, o_ref):
    h = jnp.concatenate([hf_ref[p] + hb_ref[p] for p in range(WIDTH // LANE)], axis=-1)
    yc = h * _gelu_tanh(cg_ref[...])
    y = _dot(ya_ref[...].astype(BF16), w_ref[0:WIDTH, :])
    y += _dot(yb_ref[...].astype(BF16), w_ref[WIDTH:2 * WIDTH, :])
    y += _dot(yc.astype(BF16), w_ref[2 * WIDTH:3 * WIDTH, :])
    y += _dot(yd_ref[...].astype(BF16), w_ref[3 * WIDTH:4 * WIDTH, :])
    o_ref[...] = _layer_norm_rows(DN_ALPHA * x_ref[...] + g_ref[0] * y, lng_ref[...], lnb_ref[...])


def _outproj(x, ya, yb, hf, hb, zc, yd, w_out_bf, g, lng, lnb, l, tiles_per_seg, tm):
    rows = x.shape[0]
    mod = pl.BlockSpec((1, 1, D_MODEL), lambda i: (i // tiles_per_seg, 0, 0))
    vec = pl.BlockSpec((1, D_MODEL), lambda i: (0, 0))
    row = lambda w: pl.BlockSpec((tm, w), lambda i: (i, 0))
    halves = pl.BlockSpec((WIDTH // LANE, tm, LANE), lambda i: (0, i, 0))
    return pl.pallas_call(
        _outproj_kernel,
        grid=(rows // tm,),
        in_specs=[row(D_MODEL), row(WIDTH), row(WIDTH), halves, halves,
                  pl.BlockSpec((tm, WIDTH), lambda i: (i, 1)), row(WIDTH),
                  pl.BlockSpec((None, 4 * WIDTH, D_MODEL), lambda i: (l, 0, 0)), mod, vec, vec],
        out_specs=row(D_MODEL),
        out_shape=jax.ShapeDtypeStruct((rows, D_MODEL), F32),
        compiler_params=_cp(("parallel",)),
        name='mixer_out',
    )(x, ya, yb, hf, hb, zc, yd, w_out_bf, g, lng, lnb)


class _Streams:
    def __init__(self, n_seg, n_ctx, seg_rows, ctx_len, chunk):
        self.n_seg, self.n_ctx, self.chunk = n_seg, n_ctx, chunk
        self.n = seg_rows // chunk
        self.ctx_n = ctx_len // chunk

    def view(self, a):
        return a.reshape(self.n_seg, -1, a.shape[-1])

    def fwd(self, width, col=0):
        return pl.BlockSpec((self.n_seg, self.chunk, width), lambda c: (0, c, col))

    def bwd(self, width, col=0):
        n = self.n
        return pl.BlockSpec((self.n_seg, self.chunk, width), lambda c: (0, n - 1 - c, col))

    def fin_fwd(self, shape):
        ctx_n = self.ctx_n
        return pl.BlockSpec((1,) + shape, lambda c: (c // ctx_n,) + (0,) * len(shape))

    def fin_bwd(self, shape):
        ctx_n, n_ctx = self.ctx_n, self.n_ctx
        return pl.BlockSpec((1,) + shape, lambda c: (n_ctx - 1 - c // ctx_n,) + (0,) * len(shape))

    def ctx_first(self, c):
        return c % self.ctx_n == 0

    def ctx_last(self, c):
        return c % self.ctx_n == self.ctx_n - 1


def _rwkv_pre_kernel(za_ref, w0_ref, w2_ref, a0_ref, a2_ref, g2_ref, kkp_ref, kap_ref, rkp_ref, ebd_ref,
                     kap_o, lwf_o, lwb_o, kdf_o, kdb_o, bbf_o, bbb_o, bon_o, gate_o):
    r = za_ref[:, 0:WIDTH]
    k = za_ref[:, WIDTH:2 * WIDTH]
    v = za_ref[:, 2 * WIDTH:3 * WIDTH]
    o = 3 * WIDTH
    xw = (za_ref[:, o:o + A_DECAY_RANK], za_ref[:, o + A_DECAY_RANK:o + 2 * A_DECAY_RANK])
    o += 2 * A_DECAY_RANK
    xa = (za_ref[:, o:o + A_ICLR_RANK], za_ref[:, o + A_ICLR_RANK:o + 2 * A_ICLR_RANK])
    o += 2 * A_ICLR_RANK
    xg = za_ref[:, o:o + A_GATE_RANK]
    ebd = ebd_ref[...]

    kk = k * kkp_ref[...]
    nrm = jnp.sqrt(_dot(kk * kk, ebd, hi=True))
    kappa = kk / jnp.maximum(nrm, 1e-12)
    kap_o[...] = kappa
    bonus = jnp.zeros_like(r)
    for d, (lw_o, kd_o, bb_o) in enumerate(((lwf_o, kdf_o, bbf_o), (lwb_o, kdb_o, bbb_o))):
        w_log = -_softplus(-(w0_ref[d] + _dot(jnp.tanh(xw[d]).astype(BF16), w2_ref[d].astype(BF16)))) - 0.5
        lw_o[...] = -jnp.exp(w_log)
        a = _sigmoid(a0_ref[d] + _dot(xa[d].astype(BF16), a2_ref[d].astype(BF16)))
        k_d = k * (1.0 + (a - 1.0) * kap_ref[...])
        kd_o[...] = k_d
        bb_o[...] = kappa * a
        bonus += _dot(r * k_d * rkp_ref[...], ebd, hi=True) * v
    bon_o[...] = bonus
    gate_o[...] = _dot(_sigmoid(xg).astype(BF16), g2_ref[...].astype(BF16))


def _rwkv_pre(za, p, ebd, tm):
    rows = za.shape[0]
    row = lambda w: pl.BlockSpec((tm, w), lambda i: (i, 0))
    full = lambda a: pl.BlockSpec(a.shape, lambda i: (0,) * a.ndim)
    args = (p['w0'], p['w2'], p['a0'], p['a2'], p['g2'], p['kk'], p['ka'], p['rk'], ebd)
    return pl.pallas_call(
        _rwkv_pre_kernel,
        grid=(rows // tm,),
        in_specs=[row(ZA)] + [full(a) for a in args],
        out_specs=[row(WIDTH)] * 9,
        out_shape=[jax.ShapeDtypeStruct((rows, WIDTH), F32)] * 9,
        compiler_params=_cp(("parallel",)),
        name='rwkv_pre',
    )(za, *args)


def _rwkv_chunk(r, v, kap, lw, kd, bb, st, mbd, eye, m_strict, m_incl, tri):
    cum = _dot(tri, lw, hi=True)
    yield
    tot = jnp.sum(lw, axis=0, keepdims=True)
    g_in = jnp.exp(cum)
    g_inv = jnp.exp(-cum)
    g_ex = jnp.exp(cum - lw)
    g_end = jnp.exp(tot - cum)
    cast = (lambda a: a) if RWKV_HI else (lambda a: a.astype(BF16))
    mask = cast(mbd)
    bd = lambda y: _tile4(cast(y)) * mask
    x = cast(jnp.concatenate([kap * g_ex, r * g_in], axis=0))
    ab = _dot_nt(x, bd(bb * g_inv), hi=RWKV_HI)
    ak = _dot_nt(x, bd(kd * g_inv), hi=RWKV_HI)
    xs = _dot_nt(x, cast(st), hi=RWKV_HI)
    yield
    c = r.shape[0]
    strict = m_strict > 0.5
    incl = m_incl > 0.5
    a_ub = jnp.where(strict, ab[:c], 0.0)
    a_rb = jnp.where(incl, ab[c:], 0.0)
    a_uk = jnp.where(strict, ak[:c], 0.0)
    a_rk = jnp.where(incl, ak[c:], 0.0)

    def catmul(pc, q):
        return _dot(cast(pc), bd(q), hi=RWKV_HI)

    rhs = xs[:c] + catmul(a_uk, v)
    o_v = xs[c:] + catmul(a_rk, v)
    xp = -a_ub
    inv = eye + xp
    steps = int(np.log2(c)) - 1
    for _ in range(steps):
        xp = catmul(xp, xp)
        yield
        inv = inv + catmul(inv, xp)
    yield
    u = -catmul(inv, rhs)
    yield
    o = o_v + catmul(a_rb, u)
    upd = _dot_tn(cast(jnp.concatenate([u, v], axis=0)),
                  cast(jnp.concatenate([bb * g_end, kd * g_end], axis=0)), hi=RWKV_HI)
    st_new = st * jnp.exp(tot) + upd * mbd
    return o, st_new


def _interleave(gens):
    results = [None] * len(gens)
    live = list(range(len(gens)))
    while live:
        still = []
        for k in live:
            try:
                next(gens[k])
                still.append(k)
            except StopIteration as done:
                results[k] = done.value
        live = still
    return results


def _init_states(st, c, st_scr, s0_ref):
    @pl.when(c == 0)
    def _():
        st_scr[1:] = s0_ref[...]

    @pl.when(st.ctx_first(c))
    def _():
        st_scr[0] = jnp.zeros(st_scr.shape[1:], F32)


def _rwkv_scan_kernel(st,
                      rf_ref, vf_ref, kapf_ref, lwf_ref, kdf_ref, bbf_ref,
                      rb_ref, vb_ref, kapb_ref, lwb_ref, kdb_ref, bbb_ref,
                      s0_ref, mbd_ref, eye_ref, lows_ref, lowi_ref, ups_ref, upi_ref, trif_ref, trib_ref,
                      of_ref, ob_ref, finf_ref, finb_ref, st_scr):
    c = pl.program_id(0)
    _init_states(st, c, st_scr, s0_ref)
    mbd = mbd_ref[...]
    eye = eye_ref[...]
    gens = []
    for s in range(st.n_seg):
        gens.append(_rwkv_chunk(rf_ref[s], vf_ref[s], kapf_ref[s], lwf_ref[s], kdf_ref[s], bbf_ref[s],
                                st_scr[s, 0], mbd, eye, lows_ref[...], lowi_ref[...], trif_ref[...]))
        gens.append(_rwkv_chunk(rb_ref[s], vb_ref[s], kapb_ref[s], lwb_ref[s], kdb_ref[s], bbb_ref[s],
                                st_scr[s, 1], mbd, eye, ups_ref[...], upi_ref[...], trib_ref[...]))
    for k, (o, st_new) in enumerate(_interleave(gens)):
        (of_ref, ob_ref)[k % 2][k // 2] = o
        st_scr[k // 2, k % 2] = st_new

    @pl.when(st.ctx_last(c))
    def _():
        finf_ref[0] = st_scr[0, 0]
        finb_ref[0] = st_scr[0, 1]


def _rwkv_scan(za, pre, s0_dec, hc, st):
    kap, lwf, lwb, kdf, kdb, bbf, bbb = (st.view(a) for a in pre)
    za3 = st.view(za)
    rows = za.shape[0]
    const = lambda a: pl.BlockSpec(a.shape, lambda c: (0,) * a.ndim)
    consts = (hc['mbd'], hc['eye'], hc['low_s'], hc['low_i'], hc['up_s'], hc['up_i'], hc['tri_f'], hc['tri_b'])
    w = WIDTH
    o_f, o_b, fin_f, fin_b = pl.pallas_call(
        functools.partial(_rwkv_scan_kernel, st),
        grid=(st.n,),
        in_specs=[st.fwd(w, 0), st.fwd(w, 2), st.fwd(w), st.fwd(w), st.fwd(w), st.fwd(w),
                  st.bwd(w, 0), st.bwd(w, 2), st.bwd(w), st.bwd(w), st.bwd(w), st.bwd(w),
                  const(s0_dec)] + [const(a) for a in consts],
        out_specs=[st.fwd(w), st.bwd(w), st.fin_fwd((w, w)), st.fin_bwd((w, w))],
        out_shape=[jax.ShapeDtypeStruct(kap.shape, F32), jax.ShapeDtypeStruct(kap.shape, F32),
                   jax.ShapeDtypeStruct((st.n_ctx, w, w), F32), jax.ShapeDtypeStruct((st.n_ctx, w, w), F32)],
        scratch_shapes=[pltpu.VMEM((st.n_seg, 2, w, w), F32)],
        compiler_params=_cp(("arbitrary",)),
        name='rwkv_scan',
    )(za3, za3, kap, lwf, kdf, bbf, za3, za3, kap, lwb, kdb, bbb, s0_dec, *consts)
    return o_f.reshape(rows, w), o_b.reshape(rows, w), jnp.stack([fin_f, fin_b], axis=1)


def _rwkv_post_kernel(of_ref, ob_ref, bon_ref, gate_ref, gng_ref, gnb_ref, ebd_ref, o_ref):
    of = of_ref[...] + ob_ref[...]
    avg = ebd_ref[...] * (1.0 / HEAD_DIM)
    mu = _dot(of, avg, hi=True)
    d = of - mu
    var = _dot(d * d, avg, hi=True)
    gn = d * lax.rsqrt(var + RWKV_GN_EPS) * gng_ref[...] + gnb_ref[...]
    o_ref[...] = (gn + bon_ref[...]) * gate_ref[...]


def _rwkv_post(o_f, o_b, bonus, gate, gn_g, gn_b, ebd, tm):
    rows = o_f.shape[0]
    row = pl.BlockSpec((tm, WIDTH), lambda i: (i, 0))
    vec = pl.BlockSpec((1, WIDTH), lambda i: (0, 0))
    return pl.pallas_call(
        _rwkv_post_kernel,
        grid=(rows // tm,),
        in_specs=[row, row, row, row, vec, vec, pl.BlockSpec((WIDTH, WIDTH), lambda i: (0, 0))],
        out_specs=row,
        out_shape=jax.ShapeDtypeStruct((rows, WIDTH), F32),
        compiler_params=_cp(("parallel",)),
        name='rwkv_post',
    )(o_f, o_b, bonus, gate, gn_g, gn_b, ebd)


def _hgrn_chunk(xq, xf, xi, lb, st, tri, ebd, mbd, p_scr, reverse):
    c = xq.shape[0]
    nb = c // SUB
    q = _silu(xq)
    gsig = lb + (1.0 - lb) * _sigmoid(xf)
    kk = 1.0 - gsig
    lg = jnp.log(gsig)
    cum = _dot(tri, lg, hi=True)
    yield
    tot = jnp.sum(lg, axis=0, keepdims=True)
    lane_s =lax.broadcasted_iota(jnp.int32, (SUB, WIDTH), 1) % c
    row_l = lax.broadcasted_iota(jnp.int32, (SUB, WIDTH), 0)
    blk = lambda a, i: a[i * SUB:(i + 1) * SUB]
    end_row = (lambda j: j * SUB) if reverse else (lambda j: j * SUB + SUB - 1)
    later = (lambda j: range(0, j)) if reverse else (lambda j: range(j + 1, nb))

    k_end = jnp.concatenate([blk(kk, j) * jnp.exp(cum[end_row(j):end_row(j) + 1] - blk(cum, j))
                             for j in range(nb)], axis=0)
    q_parts, where_part = [], {}
    for j in range(nb):
        for i in later(j):
            where_part[(i, j)] = len(q_parts)
            q_parts.append(blk(q, i) * jnp.exp(blk(cum, i) - cum[end_row(j):end_row(j) + 1]))
    cross = _dot_nt(jnp.concatenate(q_parts, axis=0).astype(BF16), (_tile4(k_end) * mbd).astype(BF16))
    o_state = _dot_nt((q * jnp.exp(cum)).astype(BF16), st.astype(BF16))
    upd = _dot_tn(xi.astype(BF16), (kk * jnp.exp(tot - cum)).astype(BF16))
    yield

    for i in range(nb):
        cb, qb = blk(cum, i), blk(q, i)
        for sl in range(SUB):
            s = i * SUB + sl
            valid = (row_l <= sl) if reverse else (row_l >= sl)
            e = jnp.exp(jnp.where(valid, cb - cum[s:s + 1], NEG))
            p_scr[s * SUB:(s + 1) * SUB, :] = (e * qb * kk[s:s + 1]).astype(BF16)
    same = _dot(p_scr[...], ebd.astype(BF16))
    yield

    att_rows = []
    for i in range(nb):
        att = jnp.zeros((SUB, WIDTH), F32)
        for sl in range(SUB):
            s = i * SUB + sl
            att = jnp.where(lane_s == s, same[s * SUB:(s + 1) * SUB], att)
        for j in range(nb):
            if (i, j) in where_part:
                n = where_part[(i, j)]
                att = jnp.where(lane_s // SUB == j, cross[n * SUB:(n + 1) * SUB], att)
        att_rows.append(att)
    att = jnp.concatenate(att_rows, axis=0)
    o = o_state + _dot(att.astype(BF16), (_tile4(xi) * mbd).astype(BF16))
    st_new = st * jnp.exp(tot) + upd * mbd
    return o, st_new


def _hgrn_scan_kernel(st,
                      qf_ref, ff_ref, if_ref, qb_ref, fb_ref, ib_ref,
                      lb_ref, s0_ref, mbd_ref, trif_ref, trib_ref,
                      of_ref, ob_ref, finf_ref, finb_ref, st_scr, p_scr):
    c = pl.program_id(0)
    _init_states(st, c, st_scr, s0_ref)
    mbd = mbd_ref[...]
    gens = []
    for s in range(st.n_seg):
        gens.append(_hgrn_chunk(qf_ref[s], ff_ref[s], if_ref[s], lb_ref[0], st_scr[s, 0], trif_ref[...],
                                mbd, mbd, p_scr.at[s, 0], False))
        gens.append(_hgrn_chunk(qb_ref[s], fb_ref[s], ib_ref[s], lb_ref[1], st_scr[s, 1], trib_ref[...],
                                mbd, mbd, p_scr.at[s, 1], True))
    for k, (o, st_new) in enumerate(_interleave(gens)):
        (of_ref, ob_ref)[k % 2][k // 2] = o
        st_scr[k // 2, k % 2] = st_new

    @pl.when(st.ctx_last(c))
    def _():
        finf_ref[0] = st_scr[0, 0]
        finb_ref[0] = st_scr[0, 1]


def _hgrn_scan(zd, lb, s0_dec, hc, st):
    rows = zd.shape[0]
    zd3 = st.view(zd)
    const = lambda a: pl.BlockSpec(a.shape, lambda c: (0,) * a.ndim)
    consts = (hc['mbd'], hc['tri_f'], hc['tri_b'])
    w = WIDTH
    o_shape = jax.ShapeDtypeStruct((st.n_seg, rows // st.n_seg, w), F32)
    o_f, o_b, fin_f, fin_b = pl.pallas_call(
        functools.partial(_hgrn_scan_kernel, st),
        grid=(st.n,),
        in_specs=[st.fwd(w, 0), st.fwd(w, 1), st.fwd(w, 3), st.bwd(w, 0), st.bwd(w, 2), st.bwd(w, 3),
                  const(lb), const(s0_dec)] + [const(a) for a in consts],
        out_specs=[st.fwd(w), st.bwd(w), st.fin_fwd((w, w)), st.fin_bwd((w, w))],
        out_shape=[o_shape, o_shape,
                   jax.ShapeDtypeStruct((st.n_ctx, w, w), F32), jax.ShapeDtypeStruct((st.n_ctx, w, w), F32)],
        scratch_shapes=[pltpu.VMEM((st.n_seg, 2, w, w), F32),
                        pltpu.VMEM((st.n_seg, 2, CHUNK * SUB, w), BF16)],
        compiler_params=_cp(("arbitrary",)),
        name='hgrn_scan',
    )(zd3, zd3, zd3, zd3, zd3, zd3, lb, s0_dec, *consts)
    return o_f.reshape(rows, w), o_b.reshape(rows, w), jnp.stack([fin_f, fin_b], axis=1)


def _hgrn_post_kernel(of_ref, ob_ref, xg_ref, gn_ref, ebd_ref, o_ref):
    of = of_ref[...] + ob_ref[...]
    ms = _dot(of * of, ebd_ref[...] * (1.0 / HEAD_DIM), hi=True)
    o_ref[...] = of * lax.rsqrt(ms + RMS_EPS) * gn_ref[...] * _silu(xg_ref[...])


def _hgrn_post(o_f, o_b, zd, gn_g, ebd, tm):
    rows = o_f.shape[0]
    row = pl.BlockSpec((tm, WIDTH), lambda i: (i, 0))
    return pl.pallas_call(
        _hgrn_post_kernel,
        grid=(rows // tm,),
        in_specs=[row, row, pl.BlockSpec((tm, WIDTH), lambda i: (i, 4)),
                  pl.BlockSpec((1, WIDTH), lambda i: (0, 0)), pl.BlockSpec((WIDTH, WIDTH), lambda i: (0, 0))],
        out_specs=row,
        out_shape=jax.ShapeDtypeStruct((rows, WIDTH), F32),
        compiler_params=_cp(("parallel",)),
        name='hgrn_post',
    )(o_f, o_b, zd, gn_g, ebd)


def _lru_pre_kernel(seq_of_tile, xp_ref, x_ref, xn_ref, cw_ref, cb_ref, wg_ref, bg_ref, lam_ref,
                    af_o, bf_o, ab_o, bb_o, pad_scr):
    i = pl.program_id(0)
    tm = x_ref.shape[0]
    seqlen = seq_of_tile(i)
    pad_scr[0:8, :] = xp_ref[:, 0:WIDTH]
    pad_scr[8:8 + tm, :] = x_ref[:, 0:WIDTH]
    pad_scr[8 + tm:16 + tm, :] = xn_ref[:, 0:WIDTH]
    pos = jnp.bitwise_and(lax.broadcasted_iota(jnp.int32, (tm, WIDTH), 0) + i * tm, seqlen - 1)
    u = jnp.zeros((tm, WIDTH), F32) + cb_ref[...]
    for j in range(C_CONV):
        off = j - C_CONV // 2
        tap = pad_scr[pl.ds(8 + off, tm), :]
        ok = jnp.logical_and(pos + off >= 0, pos + off < seqlen)
        u += jnp.where(ok, tap, 0.0) * cw_ref[j:j + 1, :]
    gates = _sigmoid(_dot(u.astype(BF16), wg_ref[...].astype(BF16)) + bg_ref[...])
    for d, (a_o, b_o) in enumerate(((af_o, bf_o), (ab_o, bb_o))):
        r = gates[:, (2 * d) * WIDTH:(2 * d + 1) * WIDTH]
        ig = gates[:, (2 * d + 1) * WIDTH:(2 * d + 2) * WIDTH]
        log_a = -C_POW * r * _softplus(-lam_ref[d])
        a = jnp.exp(log_a)
        b = jnp.sqrt(-jnp.tanh(log_a) * (a * a + 1.0)) * (ig * u)
        for half in range(WIDTH // LANE):
            a_o[half] = a[:, half * LANE:(half + 1) * LANE]
            b_o[half] = b[:, half * LANE:(half + 1) * LANE]


def _lru_pre(zc, p, ctx_len, dec_len, n_ctx_tiles, tm):
    rows = zc.shape[0]
    nb8 = rows // 8
    per = tm // 8
    seq_of_tile = lambda i: jnp.where(i < n_ctx_tiles, ctx_len, dec_len)
    row = pl.BlockSpec((tm, WIDTH), lambda i: (i, 0))
    full = lambda a: pl.BlockSpec(a.shape, lambda i: (0,) * a.ndim)
    args = (p['conv_w'], p['conv_b'], p['wg'], p['bg'], p['lam'])
    return pl.pallas_call(
        functools.partial(_lru_pre_kernel, seq_of_tile),
        grid=(rows // tm,),
        in_specs=[pl.BlockSpec((8, WIDTH), lambda i: (jnp.maximum(i * per - 1, 0), 0)),
                  row,
                  pl.BlockSpec((8, WIDTH), lambda i: (jnp.minimum((i + 1) * per, nb8 - 1), 0))]
                 + [full(a) for a in args],
        out_specs=[pl.BlockSpec((WIDTH // LANE, tm, LANE), lambda i: (0, i, 0))] * 4,
        out_shape=[jax.ShapeDtypeStruct((WIDTH // LANE, rows, LANE), F32)] * 4,
        scratch_shapes=[pltpu.VMEM((tm + 16, WIDTH), F32)],
        compiler_params=_cp(("parallel",)),
        name='rglru_pre',
    )(zc, zc, zc, *args)


def _lru_scan_kernel(st, af_ref, bf_ref, ab_ref, bb_ref, h0_ref, hf_ref, hb_ref, finf_ref, finb_ref,
                     h_scr, loc_scr, car_scr):
    c = pl.program_id(0)
    nh = WIDTH // LANE

    @pl.when(c == 0)
    def _():
        h_scr[:, 1:] = h0_ref[...]

    @pl.when(st.ctx_first(c))
    def _():
        h_scr[:, 0] = jnp.zeros((nh, 2, 1, LANE), F32)

    rr = LRU_ROWS
    ng = af_ref.shape[2] // rr
    chains = [(p, s, d) for s in range(st.n_seg) for d in range(2) for p in range(nh)]
    refs = ((af_ref, bf_ref, hf_ref), (ab_ref, bb_ref, hb_ref))
    order = (list(range(rr)), list(range(rr - 1, -1, -1)))

    for k, (p, s, d) in enumerate(chains):
        a_ref, b_ref, _ = refs[d]
        hloc = ploc = None
        for r in order[d]:
            a = a_ref[p, s, pl.ds(r, ng, stride=rr), :]
            b = b_ref[p, s, pl.ds(r, ng, stride=rr), :]
            hloc = b if hloc is None else a * hloc + b
            ploc = a if ploc is None else a * ploc
            loc_scr[k, 0, r] = hloc
            loc_scr[k, 1, r] = ploc

    def carry_step(j, carries):
        out = []
        for k, (p, s, d) in enumerate(chains):
            g = j if d == 0 else ng - 1 - j
            r_end = order[d][-1]
            car_scr[k, pl.ds(g, 1), :] = carries[k]
            out.append(loc_scr[k, 1, r_end, pl.ds(g, 1), :] * carries[k] + loc_scr[k, 0, r_end, pl.ds(g, 1), :])
        return tuple(out)

    carries = lax.fori_loop(0, ng, carry_step, tuple(h_scr[p, s, d] for p, s, d in chains))

    for k, (p, s, d) in enumerate(chains):
        h_scr[p, s, d] = carries[k]
        o_ref = refs[d][2]
        car = car_scr[k]
        for r in range(rr):
            o_ref[p, s, pl.ds(r, ng, stride=rr), :] = loc_scr[k, 0, r] + loc_scr[k, 1, r] * car

    @pl.when(st.ctx_last(c))
    def _():
        for p in range(nh):
            finf_ref[0, :, p * LANE:(p + 1) * LANE] = h_scr[p, 0, 0]
            finb_ref[0, :, p * LANE:(p + 1) * LANE] = h_scr[p, 0, 1]


def _lru_scan(coef, h0_dec, st):
    nh = WIDTH // LANE
    a_f, b_f, a_b, b_b = (a.reshape(nh, st.n_seg, -1, LANE) for a in coef)
    n_dec = h0_dec.shape[0]
    h0 = jnp.transpose(h0_dec.reshape(n_dec, 2, nh, 1, LANE), (2, 0, 1, 3, 4))
    n_chain = 2 * st.n_seg * nh
    ng = st.chunk // LRU_ROWS
    n = st.n
    fwd = pl.BlockSpec((nh, st.n_seg, st.chunk, LANE), lambda c: (0, 0, c, 0))
    bwd = pl.BlockSpec((nh, st.n_seg, st.chunk, LANE), lambda c: (0, 0, n - 1 - c, 0))
    h_f, h_b, fin_f, fin_b = pl.pallas_call(
        functools.partial(_lru_scan_kernel, st),
        grid=(st.n,),
        in_specs=[fwd, fwd, bwd, bwd, pl.BlockSpec(h0.shape, lambda c: (0,) * h0.ndim)],
        out_specs=[fwd, bwd, st.fin_fwd((1, WIDTH)), st.fin_bwd((1, WIDTH))],
        out_shape=[jax.ShapeDtypeStruct(a_f.shape, F32), jax.ShapeDtypeStruct(a_f.shape, F32),
                   jax.ShapeDtypeStruct((st.n_ctx, 1, WIDTH), F32), jax.ShapeDtypeStruct((st.n_ctx, 1, WIDTH), F32)],
        scratch_shapes=[pltpu.VMEM((nh, st.n_seg, 2, 1, LANE), F32),
                        pltpu.VMEM((n_chain, 2, LRU_ROWS, ng, LANE), F32),
                        pltpu.VMEM((n_chain, ng, LANE), F32)],
        compiler_params=_cp(("arbitrary",)),
        name='rglru_scan',
    )(a_f, b_f, a_b, b_b, h0)
    return (h_f.reshape(coef[0].shape), h_b.reshape(coef[0].shape), jnp.concatenate([fin_f, fin_b], axis=1))


def _rope_slab(x, cos, sin):
    lane = lax.broadcasted_iota(jnp.int32, x.shape, 1)
    rot = jnp.where(lane % (B_ROPE // 2) < B_ROPE // 4, -pltpu.roll(x, LANE - B_ROPE // 4, 1),
                    pltpu.roll(x, B_ROPE // 4, 1))
    return x * cos + rot * sin


def _kv_up(ckv_bf, kpe_slab, wuk_ref, wuv_ref, k_o, v_o):
    kn = _dot(ckv_bf, wuk_ref[...])
    for h in range(N_HEADS):
        k_o[:, h * SLAB:(h + 1) * SLAB] = (kn[:, h * SLAB:(h + 1) * SLAB] + kpe_slab).astype(BF16)
    v_o[...] = _dot(ckv_bf, wuv_ref[...]).astype(BF16)


def _mla_prep_kernel(zb_ref, cos_ref, sin_ref, qn_ref, kvn_ref, wuq_ref, wuk_ref, wuv_ref,
                     ckv_o, q_o, k_o, v_o):
    cq = zb_ref[:, 0:B_Q_RANK]
    ckv = zb_ref[:, B_Q_RANK:B_Q_RANK + B_KV_RANK]
    kpe = zb_ref[:, B_Q_RANK + B_KV_RANK:ZB]
    cos = cos_ref[...]
    sin = sin_ref[...]
    cqn = cq * lax.rsqrt(jnp.mean(cq * cq, axis=-1, keepdims=True) + RMS_EPS) * qn_ref[...]
    ckvn = ckv * lax.rsqrt(jnp.mean(ckv * ckv, axis=-1, keepdims=True) + RMS_EPS) * kvn_ref[...]
    ckv_o[...] = ckvn
    q = _dot(cqn.astype(BF16), wuq_ref[...])
    for h in range(N_HEADS):
        q_o[:, h * SLAB:(h + 1) * SLAB] = _rope_slab(q[:, h * SLAB:(h + 1) * SLAB], cos, sin).astype(BF16)
    _kv_up(ckvn.astype(BF16), _rope_slab(kpe, cos, sin), wuk_ref, wuv_ref, k_o, v_o)


def _mla_prep(zb, cos, sin, p, tm):
    rows = zb.shape[0]
    row = lambda w: pl.BlockSpec((tm, w), lambda i: (i, 0))
    full = lambda a: pl.BlockSpec(a.shape, lambda i: (0,) * a.ndim)
    args = (p['qn_g'], p['kvn_g'], p['wuq'], p['wuk'], p['wuv'])
    return pl.pallas_call(
        _mla_prep_kernel,
        grid=(rows // tm,),
        in_specs=[row(ZB), row(LANE), row(LANE)] + [full(a) for a in args],
        out_specs=[row(B_KV_RANK), row(N_HEADS * SLAB), row(N_HEADS * SLAB), row(WIDTH)],
        out_shape=[jax.ShapeDtypeStruct((rows, B_KV_RANK), F32), jax.ShapeDtypeStruct((rows, N_HEADS * SLAB), BF16),
                   jax.ShapeDtypeStruct((rows, N_HEADS * SLAB), BF16), jax.ShapeDtypeStruct((rows, WIDTH), BF16)],
        compiler_params=_cp(("parallel",)),
        name='mla_prep',
    )(zb, cos, sin, *args)


def _mla_cache_kernel(ckv_ref, kpe_ref, wuk_ref, wuv_ref, k_o, v_o):
    _kv_up(ckv_ref[...].astype(BF16), kpe_ref[...], wuk_ref, wuv_ref, k_o, v_o)


def _mla_cache(ckv, kpe_slab, p):
    rows = ckv.shape[0]
    return pl.pallas_call(
        _mla_cache_kernel,
        out_shape=[jax.ShapeDtypeStruct((rows, N_HEADS * SLAB), BF16), jax.ShapeDtypeStruct((rows, WIDTH), BF16)],
        compiler_params=pltpu.CompilerParams(vmem_limit_bytes=VMEM_LIMIT),
        name='mla_cache',
    )(ckv, kpe_slab, p['wuk'], p['wuv'])


def _attend_heads(q_ref, kv_refs, o_ref):
    log2e_scale = float((B_NOPE + B_ROPE) ** -0.5 * np.log2(np.e))
    tq = q_ref.shape[0]
    lane_head = lax.broadcasted_iota(jnp.int32, (tq, WIDTH), 1) // HEAD_DIM
    out = jnp.zeros((tq, WIDTH), F32)
    for h in range(N_HEADS):
        qh = q_ref[:, h * SLAB:(h + 1) * SLAB]
        ss = [_dot_nt(qh, k_ref[:, h * SLAB:(h + 1) * SLAB]) for k_ref, _ in kv_refs]
        m = ss[0].max(axis=-1, keepdims=True)
        for s in ss[1:]:
            m = jnp.maximum(m, s.max(axis=-1, keepdims=True))
        den = jnp.zeros((tq, LANE), F32)
        num = jnp.zeros((tq, WIDTH), F32)
        for s, (_, v_ref) in zip(ss, kv_refs):
            e = jnp.exp2(((s - m) * log2e_scale).astype(BF16))
            den += _dot(e, jnp.ones((s.shape[1], LANE), BF16))
            num += _dot(e, v_ref[...])
        out = jnp.where(lane_head == h, num / den[:, 0:1], out)
    o_ref[...] = out


def _attn_ctx_kernel(q_ref, k_ref, v_ref, o_ref):
    _attend_heads(q_ref, [(k_ref, v_ref)], o_ref)


def _attn_dec_kernel(q_ref, k_ref, v_ref, kc_ref, vc_ref, o_ref):
    _attend_heads(q_ref, [(k_ref, v_ref), (kc_ref, vc_ref)], o_ref)


def _attn_ctx(q, k, v, n_seq, seq_len):
    blk = lambda w: pl.BlockSpec((seq_len, w), lambda b: (b, 0))
    return pl.pallas_call(
        _attn_ctx_kernel,
        grid=(n_seq,),
        in_specs=[blk(N_HEADS * SLAB), blk(N_HEADS * SLAB), blk(WIDTH)],
        out_specs=blk(WIDTH),
        out_shape=jax.ShapeDtypeStruct((n_seq * seq_len, WIDTH), F32),
        compiler_params=_cp(("parallel",)),
        name='attn_ctx',
    )(q, k, v)


def _attn_dec(q, k, v, kc, vc, seg0, n_dec, dec_len, past, tq):
    nq = dec_len // tq
    return pl.pallas_call(
        _attn_dec_kernel,
        grid=(n_dec, nq),
        in_specs=[pl.BlockSpec((tq, N_HEADS * SLAB), lambda b, i: ((seg0 + b) * nq + i, 0)),
                  pl.BlockSpec((dec_len, N_HEADS * SLAB), lambda b, i: (seg0 + b, 0)),
                  pl.BlockSpec((dec_len, WIDTH), lambda b, i: (seg0 + b, 0)),
                  pl.BlockSpec((past, N_HEADS * SLAB), lambda b, i: (b, 0)),
                  pl.BlockSpec((past, WIDTH), lambda b, i: (b, 0))],
        out_specs=pl.BlockSpec((tq, WIDTH), lambda b, i: (b * nq + i, 0)),
        out_shape=jax.ShapeDtypeStruct((n_dec * dec_len, WIDTH), F32),
        compiler_params=_cp(("parallel", "arbitrary")),
        name='attn_dec',
    )(q, k, v, kc, vc)


def _rope_tables(n_ctx_rows, n_dec, dec_len):
    rows = dec_len // GRID_W
    row = np.repeat(np.arange(rows, dtype=np.float32), GRID_W)
    col = np.tile(np.arange(GRID_W, dtype=np.float32), rows)
    half = B_ROPE // 2
    inv = jnp.power(ROPE_BASE, -jnp.arange(0, half, 2, dtype=F32) / half)
    ang_r = jnp.asarray(row)[:, None] * inv
    ang_c = jnp.asarray(col)[:, None] * inv
    ang = jnp.concatenate([ang_r, ang_r, ang_c, ang_c], axis=-1)
    cos = jnp.pad(jnp.cos(ang), ((0, 0), (0, LANE - B_ROPE)), constant_values=1.0)
    sin = jnp.pad(jnp.sin(ang), ((0, 0), (0, LANE - B_ROPE)))
    cos = jnp.concatenate([jnp.ones((n_ctx_rows, LANE), F32)] + [cos] * n_dec, axis=0)
    sin = jnp.concatenate([jnp.zeros((n_ctx_rows, LANE), F32)] + [sin] * n_dec, axis=0)
    return cos, sin


def _block_diag(w):
    g, n, m = w.shape[-3:]
    eye = jnp.eye(g, dtype=w.dtype)
    out = w[..., :, :, None, :] * eye[:, None, :, None]
    return out.reshape(w.shape[:-3] + (g * n, g * m))


def _head_blocks(s):
    lead = s.shape[:-2]
    s = s.reshape(lead + (N_HEADS, HEAD_DIM, N_HEADS, HEAD_DIM))
    return jnp.stack([s[..., h, :, h, :] for h in range(N_HEADS)], axis=-3)


def kernel(x_prompt, x_sample, cache_mla_ckv, cache_mla_kpe, state_rwkv, state_rglru, state_hgrn, c, c_ctx,
           w_ada, b_ada, ln_g, ln_b, w_ffn_in, w_ffn_out, w_in, w_out,
           rwkv_w0, rwkv_w2, rwkv_a0, rwkv_a2, rwkv_g2, rwkv_kk, rwkv_ka, rwkv_rk, rwkv_gn_g, rwkv_gn_b,
           mla_qn_g, mla_w_uq, mla_kvn_g, mla_w_ukv,
           rglru_conv_w, rglru_conv_b, rglru_wa, rglru_ba, rglru_wx, rglru_bx, rglru_lam,
           hgrn_lb, hgrn_gn_g):
    n_ctx, ctx_len, _ = x_prompt.shape
    n_dec, dec_len, _ = x_sample.shape
    past = cache_mla_ckv.shape[2]
    seg = n_ctx * ctx_len
    assert seg == dec_len, "context rows must form one segment of the decode sequence length"
    assert ctx_len & (ctx_len - 1) == 0 and dec_len & (dec_len - 1) == 0, "sequence lengths must be powers of two"
    n_seg = 1 + n_dec
    rows = n_seg * seg
    tm = min(TM, seg)
    tq = min(TQ, dec_len)
    tiles_per_seg = seg // tm
    nseq = n_ctx + n_dec
    assert ctx_len % LRU_CHUNK == 0
    st = _Streams(n_seg, n_ctx, seg, ctx_len, CHUNK)
    lru_st = _Streams(n_seg, n_ctx, seg, ctx_len, LRU_CHUNK)

    hc = {k: jnp.asarray(v) for k, v in _head_consts().items()}
    ebd = hc['mbd']
    cos, sin = _rope_tables(seg, n_dec, dec_len)

    cond8 = jnp.zeros((8, D_MODEL), F32).at[0].set(c_ctx).at[1:1 + n_dec].set(c)
    mods = _ada(cond8, w_ada, b_ada).reshape(DEPTH, 8, N_MOD, D_MODEL)
    mods = jnp.transpose(mods, (0, 2, 1, 3))[:, :, :n_seg, None, :]

    w_ffn_in_bf = w_ffn_in.astype(BF16)
    w_ffn_out_bf = w_ffn_out.astype(BF16)
    w_out_bf = w_out.astype(BF16)
    kpe_end = ZA + B_Q_RANK + B_KV_RANK + B_ROPE
    w_in_p = jnp.concatenate([w_in[:, :, :kpe_end], jnp.zeros((DEPTH, D_MODEL, LANE - B_ROPE), F32),
                              w_in[:, :, kpe_end:]], axis=-1).astype(BF16)
    wuq = mla_w_uq.reshape(DEPTH, B_Q_RANK, N_HEADS, B_NOPE + B_ROPE)
    wuq_p = jnp.concatenate([wuq[..., B_NOPE:], wuq[..., :B_NOPE],
                             jnp.zeros((DEPTH, B_Q_RANK, N_HEADS, SLAB - B_NOPE - B_ROPE), F32)], axis=-1)
    wuq_p = wuq_p.reshape(DEPTH, B_Q_RANK, N_HEADS * SLAB).astype(BF16)
    wukv = mla_w_ukv.reshape(DEPTH, B_KV_RANK, N_HEADS, B_NOPE + B_VDIM)
    wuk_p = jnp.concatenate([jnp.zeros((DEPTH, B_KV_RANK, N_HEADS, B_ROPE), F32), wukv[..., :B_NOPE],
                             jnp.zeros((DEPTH, B_KV_RANK, N_HEADS, SLAB - B_NOPE - B_ROPE), F32)], axis=-1)
    wuk_p = wuk_p.reshape(DEPTH, B_KV_RANK, N_HEADS * SLAB).astype(BF16)
    wuv_p = wukv[..., B_NOPE:].reshape(DEPTH, B_KV_RANK, WIDTH).astype(BF16)
    lru_wg = jnp.concatenate([_block_diag(rglru_wa[:, 0]), _block_diag(rglru_wx[:, 0]),
                              _block_diag(rglru_wa[:, 1]), _block_diag(rglru_wx[:, 1])], axis=-1)
    lru_bg = jnp.concatenate([rglru_ba[:, 0], rglru_bx[:, 0], rglru_ba[:, 1], rglru_bx[:, 1]], axis=-1)[:, None, :]
    lbs = _lower_bounds(hgrn_lb)

    rwkv_s0 = _block_diag(state_rwkv)
    hgrn_s0 = _block_diag(jnp.swapaxes(state_hgrn, -1, -2))
    cache_kpe_slab = jnp.pad(cache_mla_kpe, ((0, 0), (0, 0), (0, 0), (0, LANE - B_ROPE)))

    x = jnp.concatenate([x_prompt.reshape(seg, D_MODEL), x_sample.reshape(n_dec * dec_len, D_MODEL)], axis=0)
    new_ckv, new_kpe, new_rwkv, new_lru, new_hgrn = [], [], [], [], []
    for l in range(DEPTH):
        m = mods[l]
        lng = ln_g[l][:, None, :]
        lnb = ln_b[l][:, None, :]
        x = _ffn(x, m[0], m[1], m[2], lng[0], lnb[0], w_ffn_in_bf, w_ffn_out_bf, l, 0, tiles_per_seg, tm)
        za, zb, zc, zd = _inproj(x, m[3], m[4], w_in_p, l, tiles_per_seg, tm)

        pa = {'w0': rwkv_w0[l][:, None, :], 'w2': rwkv_w2[l], 'a0': rwkv_a0[l][:, None, :], 'a2': rwkv_a2[l],
              'g2': rwkv_g2[l], 'kk': rwkv_kk[l][None, :], 'ka': rwkv_ka[l][None, :], 'rk': rwkv_rk[l][None, :]}
        pre = _rwkv_pre(za, pa, ebd, tm)
        o_f, o_b, s_fin = _rwkv_scan(za, pre[:7], rwkv_s0[:, l], hc, st)
        ya = _rwkv_post(o_f, o_b, pre[7], pre[8], rwkv_gn_g[l][None, :], rwkv_gn_b[l][None, :], ebd, tm)
        new_rwkv.append(_head_blocks(s_fin))

        pb = {'qn_g': mla_qn_g[l][None, :], 'kvn_g': mla_kvn_g[l][None, :], 'wuq': wuq_p[l], 'wuk': wuk_p[l],
              'wuv': wuv_p[l]}
        ckvn, q_all, k_all, v_all = _mla_prep(zb, cos, sin, pb, tm)
        kc, vc = _mla_cache(cache_mla_ckv[:, l].reshape(n_dec * past, B_KV_RANK),
                            cache_kpe_slab[:, l].reshape(n_dec * past, LANE), pb)
        yb = jnp.concatenate([_attn_ctx(q_all, k_all, v_all, n_ctx, ctx_len),
                              _attn_dec(q_all, k_all, v_all, kc, vc, 1, n_dec, dec_len, past, tq)], axis=0)
        new_ckv.append(ckvn[:seg].reshape(n_ctx, ctx_len, B_KV_RANK))
        new_kpe.append(zb[:seg, B_Q_RANK + B_KV_RANK:B_Q_RANK + B_KV_RANK + B_ROPE].reshape(n_ctx, ctx_len, B_ROPE))

        pc = {'conv_w': rglru_conv_w[l], 'conv_b': rglru_conv_b[l][None, :], 'wg': lru_wg[l], 'bg': lru_bg[l],
              'lam': rglru_lam[l][:, None, :]}
        coef = _lru_pre(zc, pc, ctx_len, dec_len, tiles_per_seg, tm)
        h_f, h_b, h_fin = _lru_scan(coef, state_rglru[:, l], lru_st)
        new_lru.append(h_fin)

        od_f, od_b, sd_fin = _hgrn_scan(zd, lbs[l], hgrn_s0[:, l], hc, st)
        yd = _hgrn_post(od_f, od_b, zd, hgrn_gn_g[l][None, :], ebd, tm)
        new_hgrn.append(jnp.swapaxes(_head_blocks(sd_fin), -1, -2))

        x = _outproj(x, ya, yb, h_f, h_b, zc, yd, w_out_bf, m[5], lng[1], lnb[1], l, tiles_per_seg, tm)
        x = _ffn(x, m[6], m[7], m[8], lng[2], lnb[2], w_ffn_in_bf, w_ffn_out_bf, l, 1, tiles_per_seg, tm)

    y_prompt = x[:seg].reshape(n_ctx, ctx_len, D_MODEL)
    y_sample = x[seg:].reshape(n_dec, dec_len, D_MODEL)
    return (y_prompt, y_sample, jnp.stack(new_ckv, axis=1), jnp.stack(new_kpe, axis=1),
            jnp.stack(new_rwkv, axis=1), jnp.stack(new_lru, axis=1), jnp.stack(new_hgrn, axis=1))
```

```python
import functools

import numpy as np
import jax
import jax.numpy as jnp
from jax import lax
from jax.experimental import pallas as pl
from jax.experimental.pallas import tpu as pltpu

D_MODEL = 1024
DEPTH = 4
GRID_W = 64
HEAD_DIM = 64
N_HEADS = 4
WIDTH = N_HEADS * HEAD_DIM
A_DECAY_RANK = 64
A_ICLR_RANK = 64
A_GATE_RANK = 128
B_NOPE = 64
B_ROPE = 32
B_VDIM = 64
B_Q_RANK = 256
B_KV_RANK = 128
C_CONV = 4
C_POW = 8.0
D_FF = 2816
N_MOD = 9
ROPE_BASE = 10000.0
LN_EPS = 1e-5
RMS_EPS = 1e-6
RWKV_GN_EPS = 64e-5
DN_ALPHA = (2 * DEPTH) ** 0.25

LANE = 128
SLAB = 128
CHUNK = 64
SUB = 16
LRU_CHUNK = 256
LRU_ROWS = 8
TM = 512
TF = 1408
TQ = 256
VMEM_LIMIT = 56 * 1024 * 1024

F32 = jnp.float32
BF16 = jnp.bfloat16
HI = lax.Precision.HIGHEST
NEG = -1e30
LOG2E = float(np.log2(np.e))
RWKV_HI = False


def _cp(sem):
    return pltpu.CompilerParams(dimension_semantics=sem, vmem_limit_bytes=VMEM_LIMIT)


def _dot(a, b, hi=False):
    return jnp.dot(a, b, preferred_element_type=F32, precision=HI if hi else None)


def _dot_nt(a, b, hi=False):
    return lax.dot_general(a, b, (((1,), (1,)), ((), ())), preferred_element_type=F32,
                           precision=HI if hi else None)


def _dot_tn(a, b, hi=False):
    return lax.dot_general(a, b, (((0,), (0,)), ((), ())), preferred_element_type=F32,
                           precision=HI if hi else None)


def _head_sums(x, ebd_bf):
    hi = x.astype(BF16)
    lo = (x - hi.astype(F32)).astype(BF16)
    return _dot(hi, ebd_bf) + _dot(lo, ebd_bf)


def _sigmoid(x):
    return 1.0 / (1.0 + jnp.exp(-x))


def _silu(x):
    return x * _sigmoid(x)


def _softplus(x):
    return jnp.maximum(x, 0.0) + jnp.log(1.0 + jnp.exp(-jnp.abs(x)))


def _tile4(y):
    return jnp.concatenate([y, y, y, y], axis=0)


def _head_consts():
    r = np.arange(WIDTH)
    same = (r[:, None] // HEAD_DIM) == (r[None, :] // HEAD_DIM)
    t = np.arange(CHUNK)[:, None]
    j = (np.arange(WIDTH) % CHUNK)[None, :]
    tt = np.arange(CHUNK)
    return {
        'mbd': same.astype(np.float32),
        'low_s': (j < t).astype(np.float32), 'low_i': (j <= t).astype(np.float32),
        'up_s': (j > t).astype(np.float32), 'up_i': (j >= t).astype(np.float32),
        'eye': (j == t).astype(np.float32),
        'tri_f': (tt[None, :] <= tt[:, None]).astype(np.float32),
        'tri_b': (tt[None, :] >= tt[:, None]).astype(np.float32),
    }


def _lb_kernel(x_ref, o_ref):
    x = x_ref[...]
    m = jnp.max(x, axis=0, keepdims=True)
    e = jnp.exp(x - m)
    sm = e / jnp.sum(e, axis=0, keepdims=True)
    run = sm[0:1]
    rows = [run - sm[0:1]]
    for l in range(1, DEPTH):
        run = run + sm[l:l + 1]
        rows.append(run - sm[0:1])
    o_ref[...] = jnp.concatenate(rows, axis=0)


def _lower_bounds(hgrn_lb):
    flat = hgrn_lb.reshape(DEPTH, 2 * WIDTH)
    out = pl.pallas_call(_lb_kernel, out_shape=jax.ShapeDtypeStruct(flat.shape, F32), name='hgrn_lb')(flat)
    return out.reshape(DEPTH, 2, 1, WIDTH)


def _ada_kernel(c_ref, w_ref, b_ref, o_ref):
    h = _silu(c_ref[...]).astype(BF16)
    o_ref[0] = _dot(h, w_ref[0].astype(BF16)) + b_ref[0]


def _ada(cond8, w_ada, b_ada):
    n = N_MOD * D_MODEL
    tn = 2304
    return pl.pallas_call(
        _ada_kernel,
        grid=(DEPTH, n // tn),
        in_specs=[pl.BlockSpec((8, D_MODEL), lambda l, j: (0, 0)),
                  pl.BlockSpec((1, D_MODEL, tn), lambda l, j: (l, 0, j)),
                  pl.BlockSpec((1, 1, tn), lambda l, j: (l, 0, j))],
        out_specs=pl.BlockSpec((1, 8, tn), lambda l, j: (l, 0, j)),
        out_shape=jax.ShapeDtypeStruct((DEPTH, 8, n), F32),
        compiler_params=_cp(("arbitrary", "arbitrary")),
        name='ada',
    )(cond8, w_ada, b_ada.reshape(DEPTH, 1, n))


def _layer_norm_rows(y, g, b):
    mu = jnp.mean(y, axis=-1, keepdims=True)
    d = y - mu
    var = jnp.mean(d * d, axis=-1, keepdims=True)
    return d * lax.rsqrt(var + LN_EPS) * g + b


def _ffn_kernel(x_ref, sh_ref, sc_ref, g_ref, lng_ref, lnb_ref, wg_ref, wu_ref, wo_ref, o_ref, h_scr, acc_scr):
    j = pl.program_id(1)

    @pl.when(j == 0)
    def _():
        h_scr[...] = (x_ref[...] * (1.0 + sc_ref[0]) + sh_ref[0]).astype(BF16)
        acc_scr[...] = jnp.zeros_like(acc_scr)

    h = h_scr[...]
    gate = _dot(h, wg_ref[...])
    up = _dot(h, wu_ref[...])
    acc_scr[...] += _dot((_silu(gate) * up).astype(BF16), wo_ref[...])

    @pl.when(j == pl.num_programs(1) - 1)
    def _():
        y = DN_ALPHA * x_ref[...] + 0.5 * g_ref[0] * acc_scr[...]
        o_ref[...] = _layer_norm_rows(y, lng_ref[...], lnb_ref[...])


def _ffn(x, sh, sc, g, lng, lnb, w_in_bf, w_out_bf, l, f, tiles_per_seg, tm):
    rows = x.shape[0]
    nj = D_FF // TF
    mod = pl.BlockSpec((1, 1, D_MODEL), lambda i, j: (i // tiles_per_seg, 0, 0))
    vec = pl.BlockSpec((1, D_MODEL), lambda i, j: (0, 0))
    return pl.pallas_call(
        _ffn_kernel,
        grid=(rows // tm, nj),
        in_specs=[pl.BlockSpec((tm, D_MODEL), lambda i, j: (i, 0)), mod, mod, mod, vec, vec,
                  pl.BlockSpec((None, None, D_MODEL, TF), lambda i, j: (l, f, 0, j)),
                  pl.BlockSpec((None, None, D_MODEL, TF), lambda i, j: (l, f, 0, nj + j)),
                  pl.BlockSpec((None, None, TF, D_MODEL), lambda i, j: (l, f, j, 0))],
        out_specs=pl.BlockSpec((tm, D_MODEL), lambda i, j: (i, 0)),
        out_shape=jax.ShapeDtypeStruct((rows, D_MODEL), F32),
        scratch_shapes=[pltpu.VMEM((tm, D_MODEL), BF16), pltpu.VMEM((tm, D_MODEL), F32)],
        compiler_params=_cp(("parallel", "arbitrary")),
        name='ffn',
    )(x, sh, sc, g, lng, lnb, w_in_bf, w_in_bf, w_out_bf)


ZA = 3 * WIDTH + 2 * A_DECAY_RANK + 2 * A_ICLR_RANK + A_GATE_RANK
ZB = B_Q_RANK + B_KV_RANK + LANE
ZC = 2 * WIDTH
ZD = 5 * WIDTH
ZTOT = ZA + ZB + ZC + ZD


def _inproj_kernel(x_ref, sh_ref, sc_ref, w_ref, za_ref, zb_ref, zc_ref, zd_ref):
    h = (x_ref[...] * (1.0 + sc_ref[0]) + sh_ref[0]).astype(BF16)
    za_ref[...] = _dot(h, w_ref[:, 0:ZA])
    zb_ref[...] = _dot(h, w_ref[:, ZA:ZA + ZB])
    zc_ref[...] = _dot(h, w_ref[:, ZA + ZB:ZA + ZB + ZC])
    zd_ref[...] = _dot(h, w_ref[:, ZA + ZB + ZC:ZTOT])


def _inproj(x, sh, sc, w_in_p, l, tiles_per_seg, tm):
    rows = x.shape[0]
    mod = pl.BlockSpec((1, 1, D_MODEL), lambda i: (i // tiles_per_seg, 0, 0))
    row = lambda w: pl.BlockSpec((tm, w), lambda i: (i, 0))
    return pl.pallas_call(
        _inproj_kernel,
        grid=(rows // tm,),
        in_specs=[row(D_MODEL), mod, mod, pl.BlockSpec((None, D_MODEL, ZTOT), lambda i: (l, 0, 0))],
        out_specs=[row(ZA), row(ZB), row(ZC), row(ZD)],
        out_shape=[jax.ShapeDtypeStruct((rows, w), F32) for w in (ZA, ZB, ZC, ZD)],
        compiler_params=_cp(("parallel",)),
        name='mixer_in',
    )(x, sh, sc, w_in_p)


def _gelu_tanh(x):
    return 0.5 * x * (1.0 + jnp.tanh(np.sqrt(2.0 / np.pi) * (x + 0.044715 * (x * x * x))))


def _outproj_kernel(x_ref, ya_ref, yb_ref, hf_ref, hb_ref, cg_ref, yd_ref, w_ref, g_ref, lng_ref, lnb_ref, o_ref):
    h = jnp.concatenate([hf_ref[p] + hb_ref[p] for p in range(WIDTH // LANE)], axis=-1)
    yc = h * _gelu_tanh(cg_ref[...])
    y = _dot(ya_ref[...].astype(BF16), w_ref[0:WIDTH, :])
    y += _dot(yb_ref[...].astype(BF16), w_ref[WIDTH:2 * WIDTH, :])
    y += _dot(yc.astype(BF16), w_ref[2 * WIDTH:3 * WIDTH, :])
    y += _dot(yd_ref[...].astype(BF16), w_ref[3 * WIDTH:4 * WIDTH, :])
    o_ref[...] = _layer_norm_rows(DN_ALPHA * x_ref[...] + g_ref[0] * y, lng_ref[...], lnb_ref[...])


def _outproj(x, ya, yb, hf, hb, zc, yd, w_out_bf, g, lng, lnb, l, tiles_per_seg, tm):
    rows = x.shape[0]
    mod = pl.BlockSpec((1, 1, D_MODEL), lambda i: (i // tiles_per_seg, 0, 0))
    vec = pl.BlockSpec((1, D_MODEL), lambda i: (0, 0))
    row = lambda w: pl.BlockSpec((tm, w), lambda i: (i, 0))
    halves = pl.BlockSpec((WIDTH // LANE, tm, LANE), lambda i: (0, i, 0))
    return pl.pallas_call(
        _outproj_kernel,
        grid=(rows // tm,),
        in_specs=[row(D_MODEL), row(WIDTH), row(WIDTH), halves, halves,
                  pl.BlockSpec((tm, WIDTH), lambda i: (i, 1)), row(WIDTH),
                  pl.BlockSpec((None, 4 * WIDTH, D_MODEL), lambda i: (l, 0, 0)), mod, vec, vec],
        out_specs=row(D_MODEL),
        out_shape=jax.ShapeDtypeStruct((rows, D_MODEL), F32),
        compiler_params=_cp(("parallel",)),
        name='mixer_out',
    )(x, ya, yb, hf, hb, zc, yd, w_out_bf, g, lng, lnb)


class _Streams:
    def __init__(self, n_seg, n_ctx, seg_rows, ctx_len, chunk):
        self.n_seg, self.n_ctx, self.chunk = n_seg, n_ctx, chunk
        self.n = seg_rows // chunk
        self.ctx_n = ctx_len // chunk

    def view(self, a):
        return a.reshape(self.n_seg, -1, a.shape[-1])

    def fwd(self, width, col=0):
        return pl.BlockSpec((self.n_seg, self.chunk, width), lambda c: (0, c, col))

    def bwd(self, width, col=0):
        n = self.n
        return pl.BlockSpec((self.n_seg, self.chunk, width), lambda c: (0, n - 1 - c, col))

    def fin_fwd(self, shape):
        ctx_n = self.ctx_n
        return pl.BlockSpec((1,) + shape, lambda c: (c // ctx_n,) + (0,) * len(shape))

    def fin_bwd(self, shape):
        ctx_n, n_ctx = self.ctx_n, self.n_ctx
        return pl.BlockSpec((1,) + shape, lambda c: (n_ctx - 1 - c // ctx_n,) + (0,) * len(shape))

    def ctx_first(self, c):
        return c % self.ctx_n == 0

    def ctx_last(self, c):
        return c % self.ctx_n == self.ctx_n - 1


def _rwkv_pre_kernel(za_ref, w0_ref, w2_ref, a0_ref, a2_ref, g2_ref, kkp_ref, kap_ref, rkp_ref, ebd_ref,
                     kap_o, lwf_o, lwb_o, kdf_o, kdb_o, bbf_o, bbb_o, bon_o, gate_o):
    r = za_ref[:, 0:WIDTH]
    k = za_ref[:, WIDTH:2 * WIDTH]
    v = za_ref[:, 2 * WIDTH:3 * WIDTH]
    o = 3 * WIDTH
    xw = (za_ref[:, o:o + A_DECAY_RANK], za_ref[:, o + A_DECAY_RANK:o + 2 * A_DECAY_RANK])
    o += 2 * A_DECAY_RANK
    xa = (za_ref[:, o:o + A_ICLR_RANK], za_ref[:, o + A_ICLR_RANK:o + 2 * A_ICLR_RANK])
    o += 2 * A_ICLR_RANK
    xg = za_ref[:, o:o + A_GATE_RANK]
    ebd = ebd_ref[...].astype(BF16)

    kk = k * kkp_ref[...]
    nrm = jnp.sqrt(_head_sums(kk * kk, ebd))
    kappa = kk / jnp.maximum(nrm, 1e-12)
    kap_o[...] = kappa
    k_sum = jnp.zeros_like(r)
    for d, (lw_o, kd_o, bb_o) in enumerate(((lwf_o, kdf_o, bbf_o), (lwb_o, kdb_o, bbb_o))):
        w_log = -_softplus(-(w0_ref[d] + _dot(jnp.tanh(xw[d]).astype(BF16), w2_ref[d].astype(BF16)))) - 0.5
        lw_o[...] = -jnp.exp(w_log)
        a = _sigmoid(a0_ref[d] + _dot(xa[d].astype(BF16), a2_ref[d].astype(BF16)))
        k_d = k * (1.0 + (a - 1.0) * kap_ref[...])
        kd_o[...] = k_d
        bb_o[...] = kappa * a
        k_sum += k_d
    bon_o[...] = _head_sums(r * k_sum * rkp_ref[...], ebd) * v
    gate_o[...] = _dot(_sigmoid(xg).astype(BF16), g2_ref[...].astype(BF16))


def _rwkv_pre(za, p, ebd, tm):
    rows = za.shape[0]
    row = lambda w: pl.BlockSpec((tm, w), lambda i: (i, 0))
    full = lambda a: pl.BlockSpec(a.shape, lambda i: (0,) * a.ndim)
    args = (p['w0'], p['w2'], p['a0'], p['a2'], p['g2'], p['kk'], p['ka'], p['rk'], ebd)
    return pl.pallas_call(
        _rwkv_pre_kernel,
        grid=(rows // tm,),
        in_specs=[row(ZA)] + [full(a) for a in args],
        out_specs=[row(WIDTH)] * 9,
        out_shape=[jax.ShapeDtypeStruct((rows, WIDTH), F32)] * 9,
        compiler_params=_cp(("parallel",)),
        name='rwkv_pre',
    )(za, *args)


def _rwkv_chunk(r, v, kap, lw, kd, bb, st, mbd, eye, m_strict, m_incl, tri):
    cum = _dot(tri, lw, hi=True)
    yield
    tot = jnp.sum(lw, axis=0, keepdims=True)
    g_in = jnp.exp(cum)
    g_inv = jnp.exp(-cum)
    g_ex = jnp.exp(cum - lw)
    g_end = jnp.exp(tot - cum)
    cast = (lambda a: a) if RWKV_HI else (lambda a: a.astype(BF16))
    mask = cast(mbd)
    bd = lambda y: _tile4(cast(y)) * mask
    x = cast(jnp.concatenate([kap * g_ex, r * g_in], axis=0))
    ab = _dot_nt(x, bd(bb * g_inv), hi=RWKV_HI)
    ak = _dot_nt(x, bd(kd * g_inv), hi=RWKV_HI)
    xs = _dot_nt(x, cast(st), hi=RWKV_HI)
    yield
    c = r.shape[0]
    strict = m_strict > 0.5
    incl = m_incl > 0.5
    a_ub = jnp.where(strict, ab[:c], 0.0)
    a_rb = jnp.where(incl, ab[c:], 0.0)
    a_uk = jnp.where(strict, ak[:c], 0.0)
    a_rk = jnp.where(incl, ak[c:], 0.0)

    def catmul(pc, q):
        return _dot(cast(pc), bd(q), hi=RWKV_HI)

    xs = xs + catmul(jnp.concatenate([a_uk, a_rk], axis=0), v)
    rhs, o_v = xs[:c], xs[c:]
    xp = -a_ub
    inv = eye + xp
    xp = catmul(xp, xp)
    yield
    for _ in range(int(np.log2(c)) - 2):
        both = catmul(jnp.concatenate([inv, xp], axis=0), xp)
        yield
        inv = inv + both[:c]
        xp = both[c:]
    inv = inv + catmul(inv, xp)
    yield
    u = -catmul(inv, rhs)
    yield
    o = o_v + catmul(a_rb, u)
    upd = _dot_tn(cast(jnp.concatenate([u, v], axis=0)),
                  cast(jnp.concatenate([bb * g_end, kd * g_end], axis=0)), hi=RWKV_HI)
    st_new = st * jnp.exp(tot) + upd * mbd
    return o, st_new


def _interleave(gens):
    results = [None] * len(gens)
    live = list(range(len(gens)))
    while live:
        still = []
        for k in live:
            try:
                next(gens[k])
                still.append(k)
            except StopIteration as done:
                results[k] = done.value
        live = still
    return results


def _init_states(st, c, st_scr, s0_ref):
    @pl.when(c == 0)
    def _():
        st_scr[1:] = s0_ref[...]

    @pl.when(st.ctx_first(c))
    def _():
        st_scr[0] = jnp.zeros(st_scr.shape[1:], F32)


def _rwkv_scan_kernel(st,
                      rf_ref, vf_ref, kapf_ref, lwf_ref, kdf_ref, bbf_ref,
                      rb_ref, vb_ref, kapb_ref, lwb_ref, kdb_ref, bbb_ref,
                      s0_ref, mbd_ref, eye_ref, lows_ref, lowi_ref, ups_ref, upi_ref, trif_ref, trib_ref,
                      of_ref, ob_ref, finf_ref, finb_ref, st_scr):
    c = pl.program_id(0)
    _init_states(st, c, st_scr, s0_ref)
    mbd = mbd_ref[...]
    eye = eye_ref[...]
    gens = []
    for s in range(st.n_seg):
        gens.append(_rwkv_chunk(rf_ref[s], vf_ref[s], kapf_ref[s], lwf_ref[s], kdf_ref[s], bbf_ref[s],
                                st_scr[s, 0], mbd, eye, lows_ref[...], lowi_ref[...], trif_ref[...]))
        gens.append(_rwkv_chunk(rb_ref[s], vb_ref[s], kapb_ref[s], lwb_ref[s], kdb_ref[s], bbb_ref[s],
                                st_scr[s, 1], mbd, eye, ups_ref[...], upi_ref[...], trib_ref[...]))
    for k, (o, st_new) in enumerate(_interleave(gens)):
        (of_ref, ob_ref)[k % 2][k // 2] = o
        st_scr[k // 2, k % 2] = st_new

    @pl.when(st.ctx_last(c))
    def _():
        finf_ref[0] = st_scr[0, 0]
        finb_ref[0] = st_scr[0, 1]


def _rwkv_scan(za, pre, s0_dec, hc, st):
    kap, lwf, lwb, kdf, kdb, bbf, bbb = (st.view(a) for a in pre)
    za3 = st.view(za)
    rows = za.shape[0]
    const = lambda a: pl.BlockSpec(a.shape, lambda c: (0,) * a.ndim)
    consts = (hc['mbd'], hc['eye'], hc['low_s'], hc['low_i'], hc['up_s'], hc['up_i'], hc['tri_f'], hc['tri_b'])
    w = WIDTH
    o_f, o_b, fin_f, fin_b = pl.pallas_call(
        functools.partial(_rwkv_scan_kernel, st),
        grid=(st.n,),
        in_specs=[st.fwd(w, 0), st.fwd(w, 2), st.fwd(w), st.fwd(w), st.fwd(w), st.fwd(w),
                  st.bwd(w, 0), st.bwd(w, 2), st.bwd(w), st.bwd(w), st.bwd(w), st.bwd(w),
                  const(s0_dec)] + [const(a) for a in consts],
        out_specs=[st.fwd(w), st.bwd(w), st.fin_fwd((w, w)), st.fin_bwd((w, w))],
        out_shape=[jax.ShapeDtypeStruct(kap.shape, F32), jax.ShapeDtypeStruct(kap.shape, F32),
                   jax.ShapeDtypeStruct((st.n_ctx, w, w), F32), jax.ShapeDtypeStruct((st.n_ctx, w, w), F32)],
        scratch_shapes=[pltpu.VMEM((st.n_seg, 2, w, w), F32)],
        compiler_params=_cp(("arbitrary",)),
        name='rwkv_scan',
    )(za3, za3, kap, lwf, kdf, bbf, za3, za3, kap, lwb, kdb, bbb, s0_dec, *consts)
    return o_f.reshape(rows, w), o_b.reshape(rows, w), jnp.stack([fin_f, fin_b], axis=1)


def _rwkv_post_kernel(of_ref, ob_ref, bon_ref, gate_ref, gng_ref, gnb_ref, ebd_ref, o_ref):
    of = of_ref[...] + ob_ref[...]
    avg = (ebd_ref[...] * (1.0 / HEAD_DIM)).astype(BF16)
    mu = _head_sums(of, avg)
    d = of - mu
    var = _head_sums(d * d, avg)
    gn = d * lax.rsqrt(var + RWKV_GN_EPS) * gng_ref[...] + gnb_ref[...]
    o_ref[...] = (gn + bon_ref[...]) * gate_ref[...]


def _rwkv_post(o_f, o_b, bonus, gate, gn_g, gn_b, ebd, tm):
    rows = o_f.shape[0]
    row = pl.BlockSpec((tm, WIDTH), lambda i: (i, 0))
    vec = pl.BlockSpec((1, WIDTH), lambda i: (0, 0))
    return pl.pallas_call(
        _rwkv_post_kernel,
        grid=(rows // tm,),
        in_specs=[row, row, row, row, vec, vec, pl.BlockSpec((WIDTH, WIDTH), lambda i: (0, 0))],
        out_specs=row,
        out_shape=jax.ShapeDtypeStruct((rows, WIDTH), F32),
        compiler_params=_cp(("parallel",)),
        name='rwkv_post',
    )(o_f, o_b, bonus, gate, gn_g, gn_b, ebd)


def _hgrn_chunk(xq, xf, xi, lb, st, tri, ebd, mbd, p_scr, reverse):
    c = xq.shape[0]
    nb = c // SUB
    q = _silu(xq)
    gsig = lb + (1.0 - lb) * _sigmoid(xf)
    kk = 1.0 - gsig
    lg = jnp.log(gsig)
    cum = _dot(tri, lg, hi=True)
    yield
    cum = cum * LOG2E
    tot = jnp.sum(lg, axis=0, keepdims=True) * LOG2E
    lane_s = lax.broadcasted_iota(jnp.int32, (SUB, WIDTH), 1) % c
    half = SUB // 2
    row_h = lax.broadcasted_iota(jnp.int32, (half, WIDTH), 0)
    blk = lambda a, i: a[i * SUB:(i + 1) * SUB]
    end_row = (lambda j: j * SUB) if reverse else (lambda j: j * SUB + SUB - 1)
    later = (lambda j: range(0, j)) if reverse else (lambda j: range(j + 1, nb))

    k_end = jnp.concatenate([blk(kk, j) * jnp.exp2(cum[end_row(j):end_row(j) + 1] - blk(cum, j))
                             for j in range(nb)], axis=0)
    q_parts, where_part = [], {}
    for j in range(nb):
        for i in later(j):
            where_part[(i, j)] = len(q_parts)
            q_parts.append(blk(q, i) * jnp.exp2(blk(cum, i) - cum[end_row(j):end_row(j) + 1]))
    cross = _dot_nt(jnp.concatenate(q_parts, axis=0).astype(BF16), (_tile4(k_end) * mbd).astype(BF16))
    o_state = _dot_nt((q * jnp.exp2(cum)).astype(BF16), st.astype(BF16))
    upd = _dot_tn(xi.astype(BF16), (kk * jnp.exp2(tot - cum)).astype(BF16))
    yield

    for i in range(nb):
        cb, qb = blk(cum, i), blk(q, i)
        for sl in range(SUB):
            s = i * SUB + sl
            parts = []
            for lo in (0, half):
                hi = lo + half - 1
                none_valid = lo > sl if reverse else hi < sl
                all_valid = hi <= sl if reverse else lo >= sl
                if none_valid:
                    parts.append(jnp.zeros((half, WIDTH), F32))
                    continue
                d = cb[lo:lo + half] - cum[s:s + 1]
                if not all_valid:
                    valid = (row_h + lo <= sl) if reverse else (row_h + lo >= sl)
                    d = jnp.where(valid, d, NEG)
                parts.append(jnp.exp2(d) * qb[lo:lo + half] * kk[s:s + 1])
            p_scr[s * SUB:(s + 1) * SUB, :] = jnp.concatenate(parts, axis=0).astype(BF16)
    same = _dot(p_scr[...], ebd.astype(BF16))
    yield

    att_rows = []
    for i in range(nb):
        att = jnp.zeros((SUB, WIDTH), F32)
        for sl in range(SUB):
            s = i * SUB + sl
            att = jnp.where(lane_s == s, same[s * SUB:(s + 1) * SUB], att)
        for j in range(nb):
            if (i, j) in where_part:
                n = where_part[(i, j)]
                att = jnp.where(lane_s // SUB == j, cross[n * SUB:(n + 1) * SUB], att)
        att_rows.append(att)
    att = jnp.concatenate(att_rows, axis=0)
    o = o_state + _dot(att.astype(BF16), (_tile4(xi) * mbd).astype(BF16))
    st_new = st * jnp.exp2(tot) + upd * mbd
    return o, st_new


def _hgrn_scan_kernel(st,
                      qf_ref, ff_ref, if_ref, qb_ref, fb_ref, ib_ref,
                      lb_ref, s0_ref, mbd_ref, trif_ref, trib_ref,
                      of_ref, ob_ref, finf_ref, finb_ref, st_scr, p_scr):
    c = pl.program_id(0)
    _init_states(st, c, st_scr, s0_ref)
    mbd = mbd_ref[...]
    gens = []
    for s in range(st.n_seg):
        gens.append(_hgrn_chunk(qf_ref[s], ff_ref[s], if_ref[s], lb_ref[0], st_scr[s, 0], trif_ref[...],
                                mbd, mbd, p_scr.at[s, 0], False))
        gens.append(_hgrn_chunk(qb_ref[s], fb_ref[s], ib_ref[s], lb_ref[1], st_scr[s, 1], trib_ref[...],
                                mbd, mbd, p_scr.at[s, 1], True))
    for k, (o, st_new) in enumerate(_interleave(gens)):
        (of_ref, ob_ref)[k % 2][k // 2] = o
        st_scr[k // 2, k % 2] = st_new

    @pl.when(st.ctx_last(c))
    def _():
        finf_ref[0] = st_scr[0, 0]
        finb_ref[0] = st_scr[0, 1]


def _hgrn_scan(zd, lb, s0_dec, hc, st):
    rows = zd.shape[0]
    zd3 = st.view(zd)
    const = lambda a: pl.BlockSpec(a.shape, lambda c: (0,) * a.ndim)
    consts = (hc['mbd'], hc['tri_f'], hc['tri_b'])
    w = WIDTH
    o_shape = jax.ShapeDtypeStruct((st.n_seg, rows // st.n_seg, w), F32)
    o_f, o_b, fin_f, fin_b = pl.pallas_call(
        functools.partial(_hgrn_scan_kernel, st),
        grid=(st.n,),
        in_specs=[st.fwd(w, 0), st.fwd(w, 1), st.fwd(w, 3), st.bwd(w, 0), st.bwd(w, 2), st.bwd(w, 3),
                  const(lb), const(s0_dec)] + [const(a) for a in consts],
        out_specs=[st.fwd(w), st.bwd(w), st.fin_fwd((w, w)), st.fin_bwd((w, w))],
        out_shape=[o_shape, o_shape,
                   jax.ShapeDtypeStruct((st.n_ctx, w, w), F32), jax.ShapeDtypeStruct((st.n_ctx, w, w), F32)],
        scratch_shapes=[pltpu.VMEM((st.n_seg, 2, w, w), F32),
                        pltpu.VMEM((st.n_seg, 2, CHUNK * SUB, w), BF16)],
        compiler_params=_cp(("arbitrary",)),
        name='hgrn_scan',
    )(zd3, zd3, zd3, zd3, zd3, zd3, lb, s0_dec, *consts)
    return o_f.reshape(rows, w), o_b.reshape(rows, w), jnp.stack([fin_f, fin_b], axis=1)


def _hgrn_post_kernel(of_ref, ob_ref, xg_ref, gn_ref, ebd_ref, o_ref):
    of = of_ref[...] + ob_ref[...]
    ms = _head_sums(of * of, (ebd_ref[...] * (1.0 / HEAD_DIM)).astype(BF16))
    o_ref[...] = of * lax.rsqrt(ms + RMS_EPS) * gn_ref[...] * _silu(xg_ref[...])


def _hgrn_post(o_f, o_b, zd, gn_g, ebd, tm):
    rows = o_f.shape[0]
    row = pl.BlockSpec((tm, WIDTH), lambda i: (i, 0))
    return pl.pallas_call(
        _hgrn_post_kernel,
        grid=(rows // tm,),
        in_specs=[row, row, pl.BlockSpec((tm, WIDTH), lambda i: (i, 4)),
                  pl.BlockSpec((1, WIDTH), lambda i: (0, 0)), pl.BlockSpec((WIDTH, WIDTH), lambda i: (0, 0))],
        out_specs=row,
        out_shape=jax.ShapeDtypeStruct((rows, WIDTH), F32),
        compiler_params=_cp(("parallel",)),
        name='hgrn_post',
    )(o_f, o_b, zd, gn_g, ebd)


def _lru_pre_kernel(seq_of_tile, xp_ref, x_ref, xn_ref, cw_ref, cb_ref, wg_ref, bg_ref, lam_ref,
                    af_o, bf_o, ab_o, bb_o, pad_scr):
    i = pl.program_id(0)
    tm = x_ref.shape[0]
    seqlen = seq_of_tile(i)
    pad_scr[0:8, :] = xp_ref[:, 0:WIDTH]
    pad_scr[8:8 + tm, :] = x_ref[:, 0:WIDTH]
    pad_scr[8 + tm:16 + tm, :] = xn_ref[:, 0:WIDTH]
    pos = jnp.bitwise_and(lax.broadcasted_iota(jnp.int32, (tm, WIDTH), 0) + i * tm, seqlen - 1)
    u = jnp.zeros((tm, WIDTH), F32) + cb_ref[...]
    for j in range(C_CONV):
        off = j - C_CONV // 2
        tap = pad_scr[pl.ds(8 + off, tm), :]
        ok = jnp.logical_and(pos + off >= 0, pos + off < seqlen)
        u += jnp.where(ok, tap, 0.0) * cw_ref[j:j + 1, :]
    gates = _sigmoid(_dot(u.astype(BF16), wg_ref[...].astype(BF16)) + bg_ref[...])
    for d, (a_o, b_o) in enumerate(((af_o, bf_o), (ab_o, bb_o))):
        r = gates[:, (2 * d) * WIDTH:(2 * d + 1) * WIDTH]
        ig = gates[:, (2 * d + 1) * WIDTH:(2 * d + 2) * WIDTH]
        log_a = -C_POW * r * _softplus(-lam_ref[d])
        a = jnp.exp(log_a)
        b = jnp.sqrt(-jnp.tanh(log_a) * (a * a + 1.0)) * (ig * u)
        for half in range(WIDTH // LANE):
            a_o[half] = a[:, half * LANE:(half + 1) * LANE]
            b_o[half] = b[:, half * LANE:(half + 1) * LANE]


def _lru_pre(zc, p, ctx_len, dec_len, n_ctx_tiles, tm):
    rows = zc.shape[0]
    nb8 = rows // 8
    per = tm // 8
    seq_of_tile = lambda i: jnp.where(i < n_ctx_tiles, ctx_len, dec_len)
    row = pl.BlockSpec((tm, WIDTH), lambda i: (i, 0))
    full = lambda a: pl.BlockSpec(a.shape, lambda i: (0,) * a.ndim)
    args = (p['conv_w'], p['conv_b'], p['wg'], p['bg'], p['lam'])
    return pl.pallas_call(
        functools.partial(_lru_pre_kernel, seq_of_tile),
        grid=(rows // tm,),
        in_specs=[pl.BlockSpec((8, WIDTH), lambda i: (jnp.maximum(i * per - 1, 0), 0)),
                  row,
                  pl.BlockSpec((8, WIDTH), lambda i: (jnp.minimum((i + 1) * per, nb8 - 1), 0))]
                 + [full(a) for a in args],
        out_specs=[pl.BlockSpec((WIDTH // LANE, tm, LANE), lambda i: (0, i, 0))] * 4,
        out_shape=[jax.ShapeDtypeStruct((WIDTH // LANE, rows, LANE), F32)] * 4,
        scratch_shapes=[pltpu.VMEM((tm + 16, WIDTH), F32)],
        compiler_params=_cp(("parallel",)),
        name='rglru_pre',
    )(zc, zc, zc, *args)


def _lru_scan_kernel(st, af_ref, bf_ref, ab_ref, bb_ref, h0_ref, hf_ref, hb_ref, finf_ref, finb_ref,
                     h_scr, loc_scr, car_scr):
    c = pl.program_id(0)
    nh = WIDTH // LANE

    @pl.when(c == 0)
    def _():
        h_scr[:, 1:] = h0_ref[...]

    @pl.when(st.ctx_first(c))
    def _():
        h_scr[:, 0] = jnp.zeros((nh, 2, 1, LANE), F32)

    rr = LRU_ROWS
    ng = af_ref.shape[2] // rr
    chains = [(p, s, d) for s in range(st.n_seg) for d in range(2) for p in range(nh)]
    refs = ((af_ref, bf_ref, hf_ref), (ab_ref, bb_ref, hb_ref))
    order = (list(range(rr)), list(range(rr - 1, -1, -1)))

    for k, (p, s, d) in enumerate(chains):
        a_ref, b_ref, _ = refs[d]
        hloc = ploc = None
        for r in order[d]:
            a = a_ref[p, s, pl.ds(r, ng, stride=rr), :]
            b = b_ref[p, s, pl.ds(r, ng, stride=rr), :]
            hloc = b if hloc is None else a * hloc + b
            ploc = a if ploc is None else a * ploc
            loc_scr[k, 0, r] = hloc
            loc_scr[k, 1, r] = ploc

    def carry_step(j, carries):
        out = []
        for k, (p, s, d) in enumerate(chains):
            g = j if d == 0 else ng - 1 - j
            r_end = order[d][-1]
            car_scr[k, pl.ds(g, 1), :] = carries[k]
            out.append(loc_scr[k, 1, r_end, pl.ds(g, 1), :] * carries[k] + loc_scr[k, 0, r_end, pl.ds(g, 1), :])
        return tuple(out)

    carries = lax.fori_loop(0, ng, carry_step, tuple(h_scr[p, s, d] for p, s, d in chains))

    for k, (p, s, d) in enumerate(chains):
        h_scr[p, s, d] = carries[k]
        o_ref = refs[d][2]
        car = car_scr[k]
        for r in range(rr):
            o_ref[p, s, pl.ds(r, ng, stride=rr), :] = loc_scr[k, 0, r] + loc_scr[k, 1, r] * car

    @pl.when(st.ctx_last(c))
    def _():
        for p in range(nh):
            finf_ref[0, :, p * LANE:(p + 1) * LANE] = h_scr[p, 0, 0]
            finb_ref[0, :, p * LANE:(p + 1) * LANE] = h_scr[p, 0, 1]


def _lru_scan(coef, h0_dec, st):
    nh = WIDTH // LANE
    a_f, b_f, a_b, b_b = (a.reshape(nh, st.n_seg, -1, LANE) for a in coef)
    n_dec = h0_dec.shape[0]
    h0 = jnp.transpose(h0_dec.reshape(n_dec, 2, nh, 1, LANE), (2, 0, 1, 3, 4))
    n_chain = 2 * st.n_seg * nh
    ng = st.chunk // LRU_ROWS
    n = st.n
    fwd = pl.BlockSpec((nh, st.n_seg, st.chunk, LANE), lambda c: (0, 0, c, 0))
    bwd = pl.BlockSpec((nh, st.n_seg, st.chunk, LANE), lambda c: (0, 0, n - 1 - c, 0))
    h_f, h_b, fin_f, fin_b = pl.pallas_call(
        functools.partial(_lru_scan_kernel, st),
        grid=(st.n,),
        in_specs=[fwd, fwd, bwd, bwd, pl.BlockSpec(h0.shape, lambda c: (0,) * h0.ndim)],
        out_specs=[fwd, bwd, st.fin_fwd((1, WIDTH)), st.fin_bwd((1, WIDTH))],
        out_shape=[jax.ShapeDtypeStruct(a_f.shape, F32), jax.ShapeDtypeStruct(a_f.shape, F32),
                   jax.ShapeDtypeStruct((st.n_ctx, 1, WIDTH), F32), jax.ShapeDtypeStruct((st.n_ctx, 1, WIDTH), F32)],
        scratch_shapes=[pltpu.VMEM((nh, st.n_seg, 2, 1, LANE), F32),
                        pltpu.VMEM((n_chain, 2, LRU_ROWS, ng, LANE), F32),
                        pltpu.VMEM((n_chain, ng, LANE), F32)],
        compiler_params=_cp(("arbitrary",)),
        name='rglru_scan',
    )(a_f, b_f, a_b, b_b, h0)
    return (h_f.reshape(coef[0].shape), h_b.reshape(coef[0].shape), jnp.concatenate([fin_f, fin_b], axis=1))


def _rope_slab(x, cos, sin):
    lane = lax.broadcasted_iota(jnp.int32, x.shape, 1)
    rot = jnp.where(lane % (B_ROPE // 2) < B_ROPE // 4, -pltpu.roll(x, LANE - B_ROPE // 4, 1),
                    pltpu.roll(x, B_ROPE // 4, 1))
    return x * cos + rot * sin


def _kv_up(ckv_bf, kpe_slab, wuk_ref, wuv_ref, k_o, v_o):
    kn = _dot(ckv_bf, wuk_ref[...])
    for h in range(N_HEADS):
        k_o[:, h * SLAB:(h + 1) * SLAB] = (kn[:, h * SLAB:(h + 1) * SLAB] + kpe_slab).astype(BF16)
    lane = lax.broadcasted_iota(jnp.int32, (1, N_HEADS * SLAB), 1)
    v_o[...] = (_dot(ckv_bf, wuv_ref[...]) + jnp.where(lane % SLAB >= B_VDIM, 1.0, 0.0)).astype(BF16)


def _mla_prep_kernel(zb_ref, cos_ref, sin_ref, qn_ref, kvn_ref, wuq_ref, wuk_ref, wuv_ref,
                     ckv_o, q_o, k_o, v_o):
    cq = zb_ref[:, 0:B_Q_RANK]
    ckv = zb_ref[:, B_Q_RANK:B_Q_RANK + B_KV_RANK]
    kpe = zb_ref[:, B_Q_RANK + B_KV_RANK:ZB]
    cos = cos_ref[...]
    sin = sin_ref[...]
    cqn = cq * lax.rsqrt(jnp.mean(cq * cq, axis=-1, keepdims=True) + RMS_EPS) * qn_ref[...]
    ckvn = ckv * lax.rsqrt(jnp.mean(ckv * ckv, axis=-1, keepdims=True) + RMS_EPS) * kvn_ref[...]
    ckv_o[...] = ckvn
    q = _dot(cqn.astype(BF16), wuq_ref[...])
    for h in range(N_HEADS):
        q_o[:, h * SLAB:(h + 1) * SLAB] = _rope_slab(q[:, h * SLAB:(h + 1) * SLAB], cos, sin).astype(BF16)
    _kv_up(ckvn.astype(BF16), _rope_slab(kpe, cos, sin), wuk_ref, wuv_ref, k_o, v_o)


def _mla_prep(zb, cos, sin, p, tm):
    rows = zb.shape[0]
    row = lambda w: pl.BlockSpec((tm, w), lambda i: (i, 0))
    full = lambda a: pl.BlockSpec(a.shape, lambda i: (0,) * a.ndim)
    args = (p['qn_g'], p['kvn_g'], p['wuq'], p['wuk'], p['wuv'])
    return pl.pallas_call(
        _mla_prep_kernel,
        grid=(rows // tm,),
        in_specs=[row(ZB), row(LANE), row(LANE)] + [full(a) for a in args],
        out_specs=[row(B_KV_RANK), row(N_HEADS * SLAB), row(N_HEADS * SLAB), row(N_HEADS * SLAB)],
        out_shape=[jax.ShapeDtypeStruct((rows, B_KV_RANK), F32), jax.ShapeDtypeStruct((rows, N_HEADS * SLAB), BF16),
                   jax.ShapeDtypeStruct((rows, N_HEADS * SLAB), BF16),
                   jax.ShapeDtypeStruct((rows, N_HEADS * SLAB), BF16)],
        compiler_params=_cp(("parallel",)),
        name='mla_prep',
    )(zb, cos, sin, *args)


def _mla_cache_kernel(ckv_ref, kpe_ref, wuk_ref, wuv_ref, k_o, v_o):
    _kv_up(ckv_ref[...].astype(BF16), kpe_ref[...], wuk_ref, wuv_ref, k_o, v_o)


def _mla_cache(ckv, kpe_slab, p):
    rows = ckv.shape[0]
    return pl.pallas_call(
        _mla_cache_kernel,
        out_shape=[jax.ShapeDtypeStruct((rows, N_HEADS * SLAB), BF16)] * 2,
        compiler_params=pltpu.CompilerParams(vmem_limit_bytes=VMEM_LIMIT),
        name='mla_cache',
    )(ckv, kpe_slab, p['wuk'], p['wuv'])


def _attend_heads(q_ref, kv_refs, o_ref):
    log2e_scale = float((B_NOPE + B_ROPE) ** -0.5 * np.log2(np.e))
    tq = q_ref.shape[0]
    low = lax.broadcasted_iota(jnp.int32, (tq, SLAB), 1) < B_VDIM
    heads = []
    for h in range(N_HEADS):
        qh = q_ref[:, h * SLAB:(h + 1) * SLAB]
        ss = [_dot_nt(qh, k_ref[:, h * SLAB:(h + 1) * SLAB]) for k_ref, _ in kv_refs]
        m = ss[0].max(axis=-1, keepdims=True)
        for s in ss[1:]:
            m = jnp.maximum(m, s.max(axis=-1, keepdims=True))
        acc = jnp.zeros((tq, SLAB), F32)
        for s, (_, v_ref) in zip(ss, kv_refs):
            e = jnp.exp2(((s - m) * log2e_scale).astype(BF16))
            acc += _dot(e, v_ref[:, h * SLAB:(h + 1) * SLAB])
        heads.append(acc / acc[:, B_VDIM:B_VDIM + 1])
    pairs = [jnp.where(low, heads[h], pltpu.roll(heads[h + 1], B_VDIM, 1)) for h in range(0, N_HEADS, 2)]
    o_ref[...] = jnp.concatenate(pairs, axis=-1)


def _attn_ctx_kernel(q_ref, k_ref, v_ref, o_ref):
    _attend_heads(q_ref, [(k_ref, v_ref)], o_ref)


def _attn_dec_kernel(q_ref, k_ref, v_ref, kc_ref, vc_ref, o_ref):
    _attend_heads(q_ref, [(k_ref, v_ref), (kc_ref, vc_ref)], o_ref)


def _attn_ctx(q, k, v, n_seq, seq_len):
    blk = lambda w: pl.BlockSpec((seq_len, w), lambda b: (b, 0))
    return pl.pallas_call(
        _attn_ctx_kernel,
        grid=(n_seq,),
        in_specs=[blk(N_HEADS * SLAB), blk(N_HEADS * SLAB), blk(N_HEADS * SLAB)],
        out_specs=blk(WIDTH),
        out_shape=jax.ShapeDtypeStruct((n_seq * seq_len, WIDTH), F32),
        compiler_params=_cp(("parallel",)),
        name='attn_ctx',
    )(q, k, v)


def _attn_dec(q, k, v, kc, vc, seg0, n_dec, dec_len, past, tq):
    nq = dec_len // tq
    return pl.pallas_call(
        _attn_dec_kernel,
        grid=(n_dec, nq),
        in_specs=[pl.BlockSpec((tq, N_HEADS * SLAB), lambda b, i: ((seg0 + b) * nq + i, 0)),
                  pl.BlockSpec((dec_len, N_HEADS * SLAB), lambda b, i: (seg0 + b, 0)),
                  pl.BlockSpec((dec_len, N_HEADS * SLAB), lambda b, i: (seg0 + b, 0)),
                  pl.BlockSpec((past, N_HEADS * SLAB), lambda b, i: (b, 0)),
                  pl.BlockSpec((past, N_HEADS * SLAB), lambda b, i: (b, 0))],
        out_specs=pl.BlockSpec((tq, WIDTH), lambda b, i: (b * nq + i, 0)),
        out_shape=jax.ShapeDtypeStruct((n_dec * dec_len, WIDTH), F32),
        compiler_params=_cp(("parallel", "arbitrary")),
        name='attn_dec',
    )(q, k, v, kc, vc)


def _rope_tables(n_ctx_rows, n_dec, dec_len):
    rows = dec_len // GRID_W
    row = np.repeat(np.arange(rows, dtype=np.float32), GRID_W)
    col = np.tile(np.arange(GRID_W, dtype=np.float32), rows)
    half = B_ROPE // 2
    inv = jnp.power(ROPE_BASE, -jnp.arange(0, half, 2, dtype=F32) / half)
    ang_r = jnp.asarray(row)[:, None] * inv
    ang_c = jnp.asarray(col)[:, None] * inv
    ang = jnp.concatenate([ang_r, ang_r, ang_c, ang_c], axis=-1)
    cos = jnp.pad(jnp.cos(ang), ((0, 0), (0, LANE - B_ROPE)), constant_values=1.0)
    sin = jnp.pad(jnp.sin(ang), ((0, 0), (0, LANE - B_ROPE)))
    cos = jnp.concatenate([jnp.ones((n_ctx_rows, LANE), F32)] + [cos] * n_dec, axis=0)
    sin = jnp.concatenate([jnp.zeros((n_ctx_rows, LANE), F32)] + [sin] * n_dec, axis=0)
    return cos, sin


def _block_diag(w):
    g, n, m = w.shape[-3:]
    eye = jnp.eye(g, dtype=w.dtype)
    out = w[..., :, :, None, :] * eye[:, None, :, None]
    return out.reshape(w.shape[:-3] + (g * n, g * m))


def _head_blocks(s):
    lead = s.shape[:-2]
    s = s.reshape(lead + (N_HEADS, HEAD_DIM, N_HEADS, HEAD_DIM))
    return jnp.stack([s[..., h, :, h, :] for h in range(N_HEADS)], axis=-3)


def kernel(x_prompt, x_sample, cache_mla_ckv, cache_mla_kpe, state_rwkv, state_rglru, state_hgrn, c, c_ctx,
           w_ada, b_ada, ln_g, ln_b, w_ffn_in, w_ffn_out, w_in, w_out,
           rwkv_w0, rwkv_w2, rwkv_a0, rwkv_a2, rwkv_g2, rwkv_kk, rwkv_ka, rwkv_rk, rwkv_gn_g, rwkv_gn_b,
           mla_qn_g, mla_w_uq, mla_kvn_g, mla_w_ukv,
           rglru_conv_w, rglru_conv_b, rglru_wa, rglru_ba, rglru_wx, rglru_bx, rglru_lam,
           hgrn_lb, hgrn_gn_g):
    n_ctx, ctx_len, _ = x_prompt.shape
    n_dec, dec_len, _ = x_sample.shape
    past = cache_mla_ckv.shape[2]
    seg = n_ctx * ctx_len
    assert seg == dec_len, "context rows must form one segment of the decode sequence length"
    assert ctx_len & (ctx_len - 1) == 0 and dec_len & (dec_len - 1) == 0, "sequence lengths must be powers of two"
    n_seg = 1 + n_dec
    rows = n_seg * seg
    tm = min(TM, seg)
    tq = min(TQ, dec_len)
    tiles_per_seg = seg // tm
    nseq = n_ctx + n_dec
    assert ctx_len % LRU_CHUNK == 0
    st = _Streams(n_seg, n_ctx, seg, ctx_len, CHUNK)
    lru_st = _Streams(n_seg, n_ctx, seg, ctx_len, LRU_CHUNK)

    hc = {k: jnp.asarray(v) for k, v in _head_consts().items()}
    ebd = hc['mbd']
    cos, sin = _rope_tables(seg, n_dec, dec_len)

    cond8 = jnp.zeros((8, D_MODEL), F32).at[0].set(c_ctx).at[1:1 + n_dec].set(c)
    mods = _ada(cond8, w_ada, b_ada).reshape(DEPTH, 8, N_MOD, D_MODEL)
    mods = jnp.transpose(mods, (0, 2, 1, 3))[:, :, :n_seg, None, :]

    w_ffn_in_bf = w_ffn_in.astype(BF16)
    w_ffn_out_bf = w_ffn_out.astype(BF16)
    w_out_bf = w_out.astype(BF16)
    kpe_end = ZA + B_Q_RANK + B_KV_RANK + B_ROPE
    w_in_p = jnp.concatenate([w_in[:, :, :kpe_end], jnp.zeros((DEPTH, D_MODEL, LANE - B_ROPE), F32),
                              w_in[:, :, kpe_end:]], axis=-1).astype(BF16)
    wuq = mla_w_uq.reshape(DEPTH, B_Q_RANK, N_HEADS, B_NOPE + B_ROPE)
    wuq_p = jnp.concatenate([wuq[..., B_NOPE:], wuq[..., :B_NOPE],
                             jnp.zeros((DEPTH, B_Q_RANK, N_HEADS, SLAB - B_NOPE - B_ROPE), F32)], axis=-1)
    wuq_p = wuq_p.reshape(DEPTH, B_Q_RANK, N_HEADS * SLAB).astype(BF16)
    wukv = mla_w_ukv.reshape(DEPTH, B_KV_RANK, N_HEADS, B_NOPE + B_VDIM)
    wuk_p = jnp.concatenate([jnp.zeros((DEPTH, B_KV_RANK, N_HEADS, B_ROPE), F32), wukv[..., :B_NOPE],
                             jnp.zeros((DEPTH, B_KV_RANK, N_HEADS, SLAB - B_NOPE - B_ROPE), F32)], axis=-1)
    wuk_p = wuk_p.reshape(DEPTH, B_KV_RANK, N_HEADS * SLAB).astype(BF16)
    wuv_p = jnp.concatenate([wukv[..., B_NOPE:], jnp.zeros((DEPTH, B_KV_RANK, N_HEADS, SLAB - B_VDIM), F32)], axis=-1)
    wuv_p = wuv_p.reshape(DEPTH, B_KV_RANK, N_HEADS * SLAB).astype(BF16)
    lru_wg = jnp.concatenate([_block_diag(rglru_wa[:, 0]), _block_diag(rglru_wx[:, 0]),
                              _block_diag(rglru_wa[:, 1]), _block_diag(rglru_wx[:, 1])], axis=-1)
    lru_bg = jnp.concatenate([rglru_ba[:, 0], rglru_bx[:, 0], rglru_ba[:, 1], rglru_bx[:, 1]], axis=-1)[:, None, :]
    lbs = _lower_bounds(hgrn_lb)

    rwkv_s0 = _block_diag(state_rwkv)
    hgrn_s0 = _block_diag(jnp.swapaxes(state_hgrn, -1, -2))
    cache_kpe_slab = jnp.pad(cache_mla_kpe, ((0, 0), (0, 0), (0, 0), (0, LANE - B_ROPE)))

    x = jnp.concatenate([x_prompt.reshape(seg, D_MODEL), x_sample.reshape(n_dec * dec_len, D_MODEL)], axis=0)
    new_ckv, new_kpe, new_rwkv, new_lru, new_hgrn = [], [], [], [], []
    for l in range(DEPTH):
        m = mods[l]
        lng = ln_g[l][:, None, :]
        lnb = ln_b[l][:, None, :]
        x = _ffn(x, m[0], m[1], m[2], lng[0], lnb[0], w_ffn_in_bf, w_ffn_out_bf, l, 0, tiles_per_seg, tm)
        za, zb, zc, zd = _inproj(x, m[3], m[4], w_in_p, l, tiles_per_seg, tm)

        pa = {'w0': rwkv_w0[l][:, None, :], 'w2': rwkv_w2[l], 'a0': rwkv_a0[l][:, None, :], 'a2': rwkv_a2[l],
              'g2': rwkv_g2[l], 'kk': rwkv_kk[l][None, :], 'ka': rwkv_ka[l][None, :], 'rk': rwkv_rk[l][None, :]}
        pre = _rwkv_pre(za, pa, ebd, tm)
        o_f, o_b, s_fin = _rwkv_scan(za, pre[:7], rwkv_s0[:, l], hc, st)
        ya = _rwkv_post(o_f, o_b, pre[7], pre[8], rwkv_gn_g[l][None, :], rwkv_gn_b[l][None, :], ebd, tm)
        new_rwkv.append(_head_blocks(s_fin))

        pb = {'qn_g': mla_qn_g[l][None, :], 'kvn_g': mla_kvn_g[l][None, :], 'wuq': wuq_p[l], 'wuk': wuk_p[l],
              'wuv': wuv_p[l]}
        ckvn, q_all, k_all, v_all = _mla_prep(zb, cos, sin, pb, tm)
        kc, vc = _mla_cache(cache_mla_ckv[:, l].reshape(n_dec * past, B_KV_RANK),
                            cache_kpe_slab[:, l].reshape(n_dec * past, LANE), pb)
        yb = jnp.concatenate([_attn_ctx(q_all, k_all, v_all, n_ctx, ctx_len),
                              _attn_dec(q_all, k_all, v_all, kc, vc, 1, n_dec, dec_len, past, tq)], axis=0)
        new_ckv.append(ckvn[:seg].reshape(n_ctx, ctx_len, B_KV_RANK))
        new_kpe.append(zb[:seg, B_Q_RANK + B_KV_RANK:B_Q_RANK + B_KV_RANK + B_ROPE].reshape(n_ctx, ctx_len, B_ROPE))

        pc = {'conv_w': rglru_conv_w[l], 'conv_b': rglru_conv_b[l][None, :], 'wg': lru_wg[l], 'bg': lru_bg[l],
              'lam': rglru_lam[l][:, None, :]}
        coef = _lru_pre(zc, pc, ctx_len, dec_len, tiles_per_seg, tm)
        h_f, h_b, h_fin = _lru_scan(coef, state_rglru[:, l], lru_st)
        new_lru.append(h_fin)

        od_f, od_b, sd_fin = _hgrn_scan(zd, lbs[l], hgrn_s0[:, l], hc, st)
        yd = _hgrn_post(od_f, od_b, zd, hgrn_gn_g[l][None, :], ebd, tm)
        new_hgrn.append(jnp.swapaxes(_head_blocks(sd_fin), -1, -2))

        x = _outproj(x, ya, yb, h_f, h_b, zc, yd, w_out_bf, m[5], lng[1], lnb[1], l, tiles_per_seg, tm)
        x = _ffn(x, m[6], m[7], m[8], lng[2], lnb[2], w_ffn_in_bf, w_ffn_out_bf, l, 1, tiles_per_seg, tm)

    y_prompt = x[:seg].reshape(n_ctx, ctx_len, D_MODEL)
    y_sample = x[seg:].reshape(n_dec, dec_len, D_MODEL)
    return (y_prompt, y_sample, jnp.stack(new_ckv, axis=1), jnp.stack(new_kpe, axis=1),
            jnp.stack(new_rwkv, axis=1), jnp.stack(new_lru, axis=1), jnp.stack(new_hgrn, axis=1))
```

```python
import functools

import numpy as np
import jax
import jax.numpy as jnp
from jax import lax
from jax.experimental import pallas as pl
from jax.experimental.pallas import tpu as pltpu

D_MODEL = 1024
DEPTH = 4
GRID_W = 64
HEAD_DIM = 64
N_HEADS = 4
WIDTH = N_HEADS * HEAD_DIM
A_DECAY_RANK = 64
A_ICLR_RANK = 64
A_GATE_RANK = 128
B_NOPE = 64
B_ROPE = 32
B_VDIM = 64
B_Q_RANK = 256
B_KV_RANK = 128
C_CONV = 4
C_POW = 8.0
D_FF = 2816
N_MOD = 9
ROPE_BASE = 10000.0
LN_EPS = 1e-5
RMS_EPS = 1e-6
RWKV_GN_EPS = 64e-5
DN_ALPHA = (2 * DEPTH) ** 0.25

LANE = 128
SLAB = 128
CHUNK = 64
SUB = 16
LRU_CHUNK = 256
LRU_ROWS = 8
TM = 512
TF = 256
TQ = 256
VMEM_LIMIT = 56 * 1024 * 1024

F32 = jnp.float32
BF16 = jnp.bfloat16
HI = lax.Precision.HIGHEST
NEG = -1e30
LOG2E = float(np.log2(np.e))
RWKV_HI = False


def _cp(sem):
    return pltpu.CompilerParams(dimension_semantics=sem, vmem_limit_bytes=VMEM_LIMIT)


def _dot(a, b, hi=False):
    return jnp.dot(a, b, preferred_element_type=F32, precision=HI if hi else None)


def _dot_nt(a, b, hi=False):
    return lax.dot_general(a, b, (((1,), (1,)), ((), ())), preferred_element_type=F32,
                           precision=HI if hi else None)


def _dot_tn(a, b, hi=False):
    return lax.dot_general(a, b, (((0,), (0,)), ((), ())), preferred_element_type=F32,
                           precision=HI if hi else None)


def _head_sums(x, ebd_bf):
    hi = x.astype(BF16)
    lo = (x - hi.astype(F32)).astype(BF16)
    return _dot(hi, ebd_bf) + _dot(lo, ebd_bf)


def _sigmoid(x):
    return 1.0 / (1.0 + jnp.exp(-x))


def _silu(x):
    return x * _sigmoid(x)


def _softplus(x):
    return jnp.maximum(x, 0.0) + jnp.log(1.0 + jnp.exp(-jnp.abs(x)))


def _tile4(y):
    return jnp.concatenate([y, y, y, y], axis=0)


def _head_consts():
    r = np.arange(WIDTH)
    same = (r[:, None] // HEAD_DIM) == (r[None, :] // HEAD_DIM)
    t = np.arange(CHUNK)[:, None]
    j = (np.arange(WIDTH) % CHUNK)[None, :]
    tt = np.arange(CHUNK)
    return {
        'mbd': same.astype(np.float32),
        'low_s': (j < t).astype(np.float32), 'low_i': (j <= t).astype(np.float32),
        'up_s': (j > t).astype(np.float32), 'up_i': (j >= t).astype(np.float32),
        'eye': (j == t).astype(np.float32),
        'tri_f': (tt[None, :] <= tt[:, None]).astype(np.float32),
        'tri_b': (tt[None, :] >= tt[:, None]).astype(np.float32),
    }


def _lb_kernel(x_ref, o_ref):
    x = x_ref[...]
    m = jnp.max(x, axis=0, keepdims=True)
    e = jnp.exp(x - m)
    sm = e / jnp.sum(e, axis=0, keepdims=True)
    run = sm[0:1]
    rows = [run - sm[0:1]]
    for l in range(1, DEPTH):
        run = run + sm[l:l + 1]
        rows.append(run - sm[0:1])
    o_ref[...] = jnp.concatenate(rows, axis=0)


def _lower_bounds(hgrn_lb):
    flat = hgrn_lb.reshape(DEPTH, 2 * WIDTH)
    out = pl.pallas_call(_lb_kernel, out_shape=jax.ShapeDtypeStruct(flat.shape, F32), name='hgrn_lb')(flat)
    return out.reshape(DEPTH, 2, 1, WIDTH)


def _ada_kernel(c_ref, w_ref, b_ref, o_ref):
    h = _silu(c_ref[...]).astype(BF16)
    o_ref[0] = _dot(h, w_ref[0].astype(BF16)) + b_ref[0]


def _ada(cond8, w_ada, b_ada):
    n = N_MOD * D_MODEL
    tn = 2304
    return pl.pallas_call(
        _ada_kernel,
        grid=(DEPTH, n // tn),
        in_specs=[pl.BlockSpec((8, D_MODEL), lambda l, j: (0, 0)),
                  pl.BlockSpec((1, D_MODEL, tn), lambda l, j: (l, 0, j)),
                  pl.BlockSpec((1, 1, tn), lambda l, j: (l, 0, j))],
        out_specs=pl.BlockSpec((1, 8, tn), lambda l, j: (l, 0, j)),
        out_shape=jax.ShapeDtypeStruct((DEPTH, 8, n), F32),
        compiler_params=_cp(("arbitrary", "arbitrary")),
        name='ada',
    )(cond8, w_ada, b_ada.reshape(DEPTH, 1, n))


def _layer_norm_rows(y, g, b):
    mu = jnp.mean(y, axis=-1, keepdims=True)
    d = y - mu
    var = jnp.mean(d * d, axis=-1, keepdims=True)
    return d * lax.rsqrt(var + LN_EPS) * g + b


def _ffn_kernel(x_ref, sh_ref, sc_ref, g_ref, lng_ref, lnb_ref, win_ref, wout_ref, o_ref):
    x = x_ref[...]
    h = (x * (1.0 + sc_ref[0]) + sh_ref[0]).astype(BF16)
    n_chunks = D_FF // TF

    def gate_up(c):
        return (_dot(h, win_ref[:, c * TF:(c + 1) * TF]),
                _dot(h, win_ref[:, D_FF + c * TF:D_FF + (c + 1) * TF]))

    y = jnp.zeros(x.shape, F32)
    cur = gate_up(0)
    for c in range(n_chunks):
        nxt = gate_up(c + 1) if c + 1 < n_chunks else None
        a = (_silu(cur[0]) * cur[1]).astype(BF16)
        y = y + _dot(a, wout_ref[c * TF:(c + 1) * TF, :])
        cur = nxt
    o_ref[...] = _layer_norm_rows(DN_ALPHA * x + 0.5 * g_ref[0] * y, lng_ref[...], lnb_ref[...])


def _ffn(x, sh, sc, g, lng, lnb, w_in_bf, w_out_bf, l, f, tiles_per_seg, tm):
    rows = x.shape[0]
    mod = pl.BlockSpec((1, 1, D_MODEL), lambda i: (i // tiles_per_seg, 0, 0))
    vec = pl.BlockSpec((1, D_MODEL), lambda i: (0, 0))
    return pl.pallas_call(
        _ffn_kernel,
        grid=(rows // tm,),
        in_specs=[pl.BlockSpec((tm, D_MODEL), lambda i: (i, 0)), mod, mod, mod, vec, vec,
                  pl.BlockSpec((None, None, D_MODEL, 2 * D_FF), lambda i: (l, f, 0, 0)),
                  pl.BlockSpec((None, None, D_FF, D_MODEL), lambda i: (l, f, 0, 0))],
        out_specs=pl.BlockSpec((tm, D_MODEL), lambda i: (i, 0)),
        out_shape=jax.ShapeDtypeStruct((rows, D_MODEL), F32),
        compiler_params=_cp(("parallel",)),
        name='ffn',
    )(x, sh, sc, g, lng, lnb, w_in_bf, w_out_bf)


ZA = 3 * WIDTH + 2 * A_DECAY_RANK + 2 * A_ICLR_RANK + A_GATE_RANK
ZB = B_Q_RANK + B_KV_RANK + LANE
ZC = 2 * WIDTH
ZD = 5 * WIDTH
ZTOT = ZA + ZB + ZC + ZD


def _inproj_kernel(x_ref, sh_ref, sc_ref, w_ref, za_ref, zb_ref, zc_ref, zd_ref):
    h = (x_ref[...] * (1.0 + sc_ref[0]) + sh_ref[0]).astype(BF16)
    za_ref[...] = _dot(h, w_ref[:, 0:ZA])
    zb_ref[...] = _dot(h, w_ref[:, ZA:ZA + ZB])
    zc_ref[...] = _dot(h, w_ref[:, ZA + ZB:ZA + ZB + ZC])
    zd_ref[...] = _dot(h, w_ref[:, ZA + ZB + ZC:ZTOT])


def _inproj(x, sh, sc, w_in_p, l, tiles_per_seg, tm):
    rows = x.shape[0]
    mod = pl.BlockSpec((1, 1, D_MODEL), lambda i: (i // tiles_per_seg, 0, 0))
    row = lambda w: pl.BlockSpec((tm, w), lambda i: (i, 0))
    return pl.pallas_call(
        _inproj_kernel,
        grid=(rows // tm,),
        in_specs=[row(D_MODEL), mod, mod, pl.BlockSpec((None, D_MODEL, ZTOT), lambda i: (l, 0, 0))],
        out_specs=[row(ZA), row(ZB), row(ZC), row(ZD)],
        out_shape=[jax.ShapeDtypeStruct((rows, w), F32) for w in (ZA, ZB, ZC, ZD)],
        compiler_params=_cp(("parallel",)),
        name='mixer_in',
    )(x, sh, sc, w_in_p)


def _gelu_tanh(x):
    return 0.5 * x * (1.0 + jnp.tanh(np.sqrt(2.0 / np.pi) * (x + 0.044715 * (x * x * x))))


def _outproj_kernel(x_ref, ya_ref, yb_ref, hf_ref, hb_ref, cg_ref, yd_ref, w_ref, g_ref, lng_ref, lnb_ref, o_ref):
    h = jnp.concatenate([hf_ref[p] + hb_ref[p] for p in range(WIDTH // LANE)], axis=-1)
    yc = h * _gelu_tanh(cg_ref[...])
    y = _dot(ya_ref[...].astype(BF16), w_ref[0:WIDTH, :])
    y += _dot(yb_ref[...].astype(BF16), w_ref[WIDTH:2 * WIDTH, :])
    y += _dot(yc.astype(BF16), w_ref[2 * WIDTH:3 * WIDTH, :])
    y += _dot(yd_ref[...].astype(BF16), w_ref[3 * WIDTH:4 * WIDTH, :])
    o_ref[...] = _layer_norm_rows(DN_ALPHA * x_ref[...] + g_ref[0] * y, lng_ref[...], lnb_ref[...])


def _outproj(x, ya, yb, hf, hb, zc, yd, w_out_bf, g, lng, lnb, l, tiles_per_seg, tm):
    rows = x.shape[0]
    mod = pl.BlockSpec((1, 1, D_MODEL), lambda i: (i // tiles_per_seg, 0, 0))
    vec = pl.BlockSpec((1, D_MODEL), lambda i: (0, 0))
    row = lambda w: pl.BlockSpec((tm, w), lambda i: (i, 0))
    halves = pl.BlockSpec((WIDTH // LANE, tm, LANE), lambda i: (0, i, 0))
    return pl.pallas_call(
        _outproj_kernel,
        grid=(rows // tm,),
        in_specs=[row(D_MODEL), row(WIDTH), row(WIDTH), halves, halves,
                  pl.BlockSpec((tm, WIDTH), lambda i: (i, 1)), row(WIDTH),
                  pl.BlockSpec((None, 4 * WIDTH, D_MODEL), lambda i: (l, 0, 0)), mod, vec, vec],
        out_specs=row(D_MODEL),
        out_shape=jax.ShapeDtypeStruct((rows, D_MODEL), F32),
        compiler_params=_cp(("parallel",)),
        name='mixer_out',
    )(x, ya, yb, hf, hb, zc, yd, w_out_bf, g, lng, lnb)


class _Streams:
    def __init__(self, n_seg, n_ctx, seg_rows, ctx_len, chunk):
        self.n_seg, self.n_ctx, self.chunk = n_seg, n_ctx, chunk
        self.n = seg_rows // chunk
        self.ctx_n = ctx_len // chunk

    def view(self, a):
        return a.reshape(self.n_seg, -1, a.shape[-1])

    def fwd(self, width, col=0):
        return pl.BlockSpec((self.n_seg, self.chunk, width), lambda c: (0, c, col))

    def bwd(self, width, col=0):
        n = self.n
        return pl.BlockSpec((self.n_seg, self.chunk, width), lambda c: (0, n - 1 - c, col))

    def fin_fwd(self, shape):
        ctx_n = self.ctx_n
        return pl.BlockSpec((1,) + shape, lambda c: (c // ctx_n,) + (0,) * len(shape))

    def fin_bwd(self, shape):
        ctx_n, n_ctx = self.ctx_n, self.n_ctx
        return pl.BlockSpec((1,) + shape, lambda c: (n_ctx - 1 - c // ctx_n,) + (0,) * len(shape))

    def ctx_first(self, c):
        return c % self.ctx_n == 0

    def ctx_last(self, c):
        return c % self.ctx_n == self.ctx_n - 1


def _rwkv_pre_kernel(za_ref, w0_ref, w2_ref, a0_ref, a2_ref, g2_ref, kkp_ref, kap_ref, rkp_ref, ebd_ref,
                     kap_o, lwf_o, lwb_o, kdf_o, kdb_o, bbf_o, bbb_o, bon_o, gate_o):
    r = za_ref[:, 0:WIDTH]
    k = za_ref[:, WIDTH:2 * WIDTH]
    v = za_ref[:, 2 * WIDTH:3 * WIDTH]
    o = 3 * WIDTH
    xw = (za_ref[:, o:o + A_DECAY_RANK], za_ref[:, o + A_DECAY_RANK:o + 2 * A_DECAY_RANK])
    o += 2 * A_DECAY_RANK
    xa = (za_ref[:, o:o + A_ICLR_RANK], za_ref[:, o + A_ICLR_RANK:o + 2 * A_ICLR_RANK])
    o += 2 * A_ICLR_RANK
    xg = za_ref[:, o:o + A_GATE_RANK]
    ebd = ebd_ref[...].astype(BF16)

    kk = k * kkp_ref[...]
    nrm = jnp.sqrt(_head_sums(kk * kk, ebd))
    kappa = kk / jnp.maximum(nrm, 1e-12)
    kap_o[...] = kappa
    k_sum = jnp.zeros_like(r)
    for d, (lw_o, kd_o, bb_o) in enumerate(((lwf_o, kdf_o, bbf_o), (lwb_o, kdb_o, bbb_o))):
        w_log = -_softplus(-(w0_ref[d] + _dot(jnp.tanh(xw[d]).astype(BF16), w2_ref[d].astype(BF16)))) - 0.5
        lw_o[...] = -jnp.exp(w_log)
        a = _sigmoid(a0_ref[d] + _dot(xa[d].astype(BF16), a2_ref[d].astype(BF16)))
        k_d = k * (1.0 + (a - 1.0) * kap_ref[...])
        kd_o[...] = k_d
        bb_o[...] = kappa * a
        k_sum += k_d
    bon_o[...] = _head_sums(r * k_sum * rkp_ref[...], ebd) * v
    gate_o[...] = _dot(_sigmoid(xg).astype(BF16), g2_ref[...].astype(BF16))


def _rwkv_pre(za, p, ebd, tm):
    rows = za.shape[0]
    row = lambda w: pl.BlockSpec((tm, w), lambda i: (i, 0))
    full = lambda a: pl.BlockSpec(a.shape, lambda i: (0,) * a.ndim)
    args = (p['w0'], p['w2'], p['a0'], p['a2'], p['g2'], p['kk'], p['ka'], p['rk'], ebd)
    return pl.pallas_call(
        _rwkv_pre_kernel,
        grid=(rows // tm,),
        in_specs=[row(ZA)] + [full(a) for a in args],
        out_specs=[row(WIDTH)] * 9,
        out_shape=[jax.ShapeDtypeStruct((rows, WIDTH), F32)] * 9,
        compiler_params=_cp(("parallel",)),
        name='rwkv_pre',
    )(za, *args)


def _rwkv_chunk(r, v, kap, lw, kd, bb, st, mbd, eye, m_strict, m_incl, tri):
    cum = _dot(tri, lw, hi=True)
    yield
    tot = jnp.sum(lw, axis=0, keepdims=True)
    g_in = jnp.exp(cum)
    g_inv = jnp.exp(-cum)
    g_ex = jnp.exp(cum - lw)
    g_end = jnp.exp(tot - cum)
    cast = (lambda a: a) if RWKV_HI else (lambda a: a.astype(BF16))
    mask = cast(mbd)
    bd = lambda y: _tile4(cast(y)) * mask
    x = cast(jnp.concatenate([kap * g_ex, r * g_in], axis=0))
    ab = _dot_nt(x, bd(bb * g_inv), hi=RWKV_HI)
    ak = _dot_nt(x, bd(kd * g_inv), hi=RWKV_HI)
    xs = _dot_nt(x, cast(st), hi=RWKV_HI)
    yield
    c = r.shape[0]
    strict = m_strict > 0.5
    incl = m_incl > 0.5
    a_ub = jnp.where(strict, ab[:c], 0.0)
    a_rb = jnp.where(incl, ab[c:], 0.0)
    a_uk = jnp.where(strict, ak[:c], 0.0)
    a_rk = jnp.where(incl, ak[c:], 0.0)

    def catmul(pc, q):
        return _dot(cast(pc), bd(q), hi=RWKV_HI)

    xs = xs + catmul(jnp.concatenate([a_uk, a_rk], axis=0), v)
    rhs, o_v = xs[:c], xs[c:]
    xp = -a_ub
    inv = eye + xp
    xp = catmul(xp, xp)
    yield
    for _ in range(int(np.log2(c)) - 2):
        both = catmul(jnp.concatenate([inv, xp], axis=0), xp)
        yield
        inv = inv + both[:c]
        xp = both[c:]
    inv = inv + catmul(inv, xp)
    yield
    u = -catmul(inv, rhs)
    yield
    o = o_v + catmul(a_rb, u)
    upd = _dot_tn(cast(jnp.concatenate([u, v], axis=0)),
                  cast(jnp.concatenate([bb * g_end, kd * g_end], axis=0)), hi=RWKV_HI)
    st_new = st * jnp.exp(tot) + upd * mbd
    return o, st_new


def _interleave(gens):
    results = [None] * len(gens)
    live = list(range(len(gens)))
    while live:
        still = []
        for k in live:
            try:
                next(gens[k])
                still.append(k)
            except StopIteration as done:
                results[k] = done.value
        live = still
    return results


def _init_states(st, c, st_scr, s0_ref):
    @pl.when(c == 0)
    def _():
        st_scr[1:] = s0_ref[...]

    @pl.when(st.ctx_first(c))
    def _():
        st_scr[0] = jnp.zeros(st_scr.shape[1:], F32)


def _rwkv_scan_kernel(st,
                      rf_ref, vf_ref, kapf_ref, lwf_ref, kdf_ref, bbf_ref,
                      rb_ref, vb_ref, kapb_ref, lwb_ref, kdb_ref, bbb_ref,
                      s0_ref, mbd_ref, eye_ref, lows_ref, lowi_ref, ups_ref, upi_ref, trif_ref, trib_ref,
                      of_ref, ob_ref, finf_ref, finb_ref, st_scr):
    c = pl.program_id(0)
    _init_states(st, c, st_scr, s0_ref)
    mbd = mbd_ref[...]
    eye = eye_ref[...]
    gens = []
    for s in range(st.n_seg):
        gens.append(_rwkv_chunk(rf_ref[s], vf_ref[s], kapf_ref[s], lwf_ref[s], kdf_ref[s], bbf_ref[s],
                                st_scr[s, 0], mbd, eye, lows_ref[...], lowi_ref[...], trif_ref[...]))
        gens.append(_rwkv_chunk(rb_ref[s], vb_ref[s], kapb_ref[s], lwb_ref[s], kdb_ref[s], bbb_ref[s],
                                st_scr[s, 1], mbd, eye, ups_ref[...], upi_ref[...], trib_ref[...]))
    for k, (o, st_new) in enumerate(_interleave(gens)):
        (of_ref, ob_ref)[k % 2][k // 2] = o
        st_scr[k // 2, k % 2] = st_new

    @pl.when(st.ctx_last(c))
    def _():
        finf_ref[0] = st_scr[0, 0]
        finb_ref[0] = st_scr[0, 1]


def _rwkv_scan(za, pre, s0_dec, hc, st):
    kap, lwf, lwb, kdf, kdb, bbf, bbb = (st.view(a) for a in pre)
    za3 = st.view(za)
    rows = za.shape[0]
    const = lambda a: pl.BlockSpec(a.shape, lambda c: (0,) * a.ndim)
    consts = (hc['mbd'], hc['eye'], hc['low_s'], hc['low_i'], hc['up_s'], hc['up_i'], hc['tri_f'], hc['tri_b'])
    w = WIDTH
    o_f, o_b, fin_f, fin_b = pl.pallas_call(
        functools.partial(_rwkv_scan_kernel, st),
        grid=(st.n,),
        in_specs=[st.fwd(w, 0), st.fwd(w, 2), st.fwd(w), st.fwd(w), st.fwd(w), st.fwd(w),
                  st.bwd(w, 0), st.bwd(w, 2), st.bwd(w), st.bwd(w), st.bwd(w), st.bwd(w),
                  const(s0_dec)] + [const(a) for a in consts],
        out_specs=[st.fwd(w), st.bwd(w), st.fin_fwd((w, w)), st.fin_bwd((w, w))],
        out_shape=[jax.ShapeDtypeStruct(kap.shape, F32), jax.ShapeDtypeStruct(kap.shape, F32),
                   jax.ShapeDtypeStruct((st.n_ctx, w, w), F32), jax.ShapeDtypeStruct((st.n_ctx, w, w), F32)],
        scratch_shapes=[pltpu.VMEM((st.n_seg, 2, w, w), F32)],
        compiler_params=_cp(("arbitrary",)),
        name='rwkv_scan',
    )(za3, za3, kap, lwf, kdf, bbf, za3, za3, kap, lwb, kdb, bbb, s0_dec, *consts)
    return o_f.reshape(rows, w), o_b.reshape(rows, w), jnp.stack([fin_f, fin_b], axis=1)


def _rwkv_post_kernel(of_ref, ob_ref, bon_ref, gate_ref, gng_ref, gnb_ref, ebd_ref, o_ref):
    of = of_ref[...] + ob_ref[...]
    avg = (ebd_ref[...] * (1.0 / HEAD_DIM)).astype(BF16)
    mu = _head_sums(of, avg)
    d = of - mu
    var = _head_sums(d * d, avg)
    gn = d * lax.rsqrt(var + RWKV_GN_EPS) * gng_ref[...] + gnb_ref[...]
    o_ref[...] = (gn + bon_ref[...]) * gate_ref[...]


def _rwkv_post(o_f, o_b, bonus, gate, gn_g, gn_b, ebd, tm):
    rows = o_f.shape[0]
    row = pl.BlockSpec((tm, WIDTH), lambda i: (i, 0))
    vec = pl.BlockSpec((1, WIDTH), lambda i: (0, 0))
    return pl.pallas_call(
        _rwkv_post_kernel,
        grid=(rows // tm,),
        in_specs=[row, row, row, row, vec, vec, pl.BlockSpec((WIDTH, WIDTH), lambda i: (0, 0))],
        out_specs=row,
        out_shape=jax.ShapeDtypeStruct((rows, WIDTH), F32),
        compiler_params=_cp(("parallel",)),
        name='rwkv_post',
    )(o_f, o_b, bonus, gate, gn_g, gn_b, ebd)


def _hgrn_chunk(xq, xf, xi, lb, st, tri, ebd, mbd, p_scr, reverse):
    c = xq.shape[0]
    nb = c // SUB
    q = _silu(xq)
    gsig = lb + (1.0 - lb) * _sigmoid(xf)
    kk = 1.0 - gsig
    lg = jnp.log(gsig)
    cum = _dot(tri, lg, hi=True)
    yield
    cum = cum * LOG2E
    tot = jnp.sum(lg, axis=0, keepdims=True) * LOG2E
    lane_s = lax.broadcasted_iota(jnp.int32, (SUB, WIDTH), 1) % c
    half = SUB // 2
    row_h = lax.broadcasted_iota(jnp.int32, (half, WIDTH), 0)
    blk = lambda a, i: a[i * SUB:(i + 1) * SUB]
    end_row = (lambda j: j * SUB) if reverse else (lambda j: j * SUB + SUB - 1)
    later = (lambda j: range(0, j)) if reverse else (lambda j: range(j + 1, nb))

    k_end = jnp.concatenate([blk(kk, j) * jnp.exp2(cum[end_row(j):end_row(j) + 1] - blk(cum, j))
                             for j in range(nb)], axis=0)
    q_parts, where_part = [], {}
    for j in range(nb):
        for i in later(j):
            where_part[(i, j)] = len(q_parts)
            q_parts.append(blk(q, i) * jnp.exp2(blk(cum, i) - cum[end_row(j):end_row(j) + 1]))
    cross = _dot_nt(jnp.concatenate(q_parts, axis=0).astype(BF16), (_tile4(k_end) * mbd).astype(BF16))
    o_state = _dot_nt((q * jnp.exp2(cum)).astype(BF16), st.astype(BF16))
    upd = _dot_tn(xi.astype(BF16), (kk * jnp.exp2(tot - cum)).astype(BF16))
    yield

    for i in range(nb):
        cb, qb = blk(cum, i), blk(q, i)
        for sl in range(SUB):
            s = i * SUB + sl
            parts = []
            for lo in (0, half):
                hi = lo + half - 1
                none_valid = lo > sl if reverse else hi < sl
                all_valid = hi <= sl if reverse else lo >= sl
                if none_valid:
                    parts.append(jnp.zeros((half, WIDTH), F32))
                    continue
                d = cb[lo:lo + half] - cum[s:s + 1]
                if not all_valid:
                    valid = (row_h + lo <= sl) if reverse else (row_h + lo >= sl)
                    d = jnp.where(valid, d, NEG)
                parts.append(jnp.exp2(d) * qb[lo:lo + half] * kk[s:s + 1])
            p_scr[s * SUB:(s + 1) * SUB, :] = jnp.concatenate(parts, axis=0).astype(BF16)
    same = _dot(p_scr[...], ebd.astype(BF16))
    yield

    att_rows = []
    for i in range(nb):
        att = jnp.zeros((SUB, WIDTH), F32)
        for sl in range(SUB):
            s = i * SUB + sl
            att = jnp.where(lane_s == s, same[s * SUB:(s + 1) * SUB], att)
        for j in range(nb):
            if (i, j) in where_part:
                n = where_part[(i, j)]
                att = jnp.where(lane_s // SUB == j, cross[n * SUB:(n + 1) * SUB], att)
        att_rows.append(att)
    att = jnp.concatenate(att_rows, axis=0)
    o = o_state + _dot(att.astype(BF16), (_tile4(xi) * mbd).astype(BF16))
    st_new = st * jnp.exp2(tot) + upd * mbd
    return o, st_new


def _hgrn_scan_kernel(st,
                      qf_ref, ff_ref, if_ref, qb_ref, fb_ref, ib_ref,
                      lb_ref, s0_ref, mbd_ref, trif_ref, trib_ref,
                      of_ref, ob_ref, finf_ref, finb_ref, st_scr, p_scr):
    c = pl.program_id(0)
    _init_states(st, c, st_scr, s0_ref)
    mbd = mbd_ref[...]
    gens = []
    for s in range(st.n_seg):
        gens.append(_hgrn_chunk(qf_ref[s], ff_ref[s], if_ref[s], lb_ref[0], st_scr[s, 0], trif_ref[...],
                                mbd, mbd, p_scr.at[s, 0], False))
        gens.append(_hgrn_chunk(qb_ref[s], fb_ref[s], ib_ref[s], lb_ref[1], st_scr[s, 1], trib_ref[...],
                                mbd, mbd, p_scr.at[s, 1], True))
    for k, (o, st_new) in enumerate(_interleave(gens)):
        (of_ref, ob_ref)[k % 2][k // 2] = o
        st_scr[k // 2, k % 2] = st_new

    @pl.when(st.ctx_last(c))
    def _():
        finf_ref[0] = st_scr[0, 0]
        finb_ref[0] = st_scr[0, 1]


def _hgrn_scan(zd, lb, s0_dec, hc, st):
    rows = zd.shape[0]
    zd3 = st.view(zd)
    const = lambda a: pl.BlockSpec(a.shape, lambda c: (0,) * a.ndim)
    consts = (hc['mbd'], hc['tri_f'], hc['tri_b'])
    w = WIDTH
    o_shape = jax.ShapeDtypeStruct((st.n_seg, rows // st.n_seg, w), F32)
    o_f, o_b, fin_f, fin_b = pl.pallas_call(
        functools.partial(_hgrn_scan_kernel, st),
        grid=(st.n,),
        in_specs=[st.fwd(w, 0), st.fwd(w, 1), st.fwd(w, 3), st.bwd(w, 0), st.bwd(w, 2), st.bwd(w, 3),
                  const(lb), const(s0_dec)] + [const(a) for a in consts],
        out_specs=[st.fwd(w), st.bwd(w), st.fin_fwd((w, w)), st.fin_bwd((w, w))],
        out_shape=[o_shape, o_shape,
                   jax.ShapeDtypeStruct((st.n_ctx, w, w), F32), jax.ShapeDtypeStruct((st.n_ctx, w, w), F32)],
        scratch_shapes=[pltpu.VMEM((st.n_seg, 2, w, w), F32),
                        pltpu.VMEM((st.n_seg, 2, CHUNK * SUB, w), BF16)],
        compiler_params=_cp(("arbitrary",)),
        name='hgrn_scan',
    )(zd3, zd3, zd3, zd3, zd3, zd3, lb, s0_dec, *consts)
    return o_f.reshape(rows, w), o_b.reshape(rows, w), jnp.stack([fin_f, fin_b], axis=1)


def _hgrn_post_kernel(of_ref, ob_ref, xg_ref, gn_ref, ebd_ref, o_ref):
    of = of_ref[...] + ob_ref[...]
    ms = _head_sums(of * of, (ebd_ref[...] * (1.0 / HEAD_DIM)).astype(BF16))
    o_ref[...] = of * lax.rsqrt(ms + RMS_EPS) * gn_ref[...] * _silu(xg_ref[...])


def _hgrn_post(o_f, o_b, zd, gn_g, ebd, tm):
    rows = o_f.shape[0]
    row = pl.BlockSpec((tm, WIDTH), lambda i: (i, 0))
    return pl.pallas_call(
        _hgrn_post_kernel,
        grid=(rows // tm,),
        in_specs=[row, row, pl.BlockSpec((tm, WIDTH), lambda i: (i, 4)),
                  pl.BlockSpec((1, WIDTH), lambda i: (0, 0)), pl.BlockSpec((WIDTH, WIDTH), lambda i: (0, 0))],
        out_specs=row,
        out_shape=jax.ShapeDtypeStruct((rows, WIDTH), F32),
        compiler_params=_cp(("parallel",)),
        name='hgrn_post',
    )(o_f, o_b, zd, gn_g, ebd)


def _lru_pre_kernel(seq_of_tile, xp_ref, x_ref, xn_ref, cw_ref, cb_ref, wg_ref, bg_ref, lam_ref,
                    af_o, bf_o, ab_o, bb_o, pad_scr):
    i = pl.program_id(0)
    tm = x_ref.shape[0]
    seqlen = seq_of_tile(i)
    pad_scr[0:8, :] = xp_ref[:, 0:WIDTH]
    pad_scr[8:8 + tm, :] = x_ref[:, 0:WIDTH]
    pad_scr[8 + tm:16 + tm, :] = xn_ref[:, 0:WIDTH]
    pos = jnp.bitwise_and(lax.broadcasted_iota(jnp.int32, (tm, WIDTH), 0) + i * tm, seqlen - 1)
    u = jnp.zeros((tm, WIDTH), F32) + cb_ref[...]
    for j in range(C_CONV):
        off = j - C_CONV // 2
        tap = pad_scr[pl.ds(8 + off, tm), :]
        ok = jnp.logical_and(pos + off >= 0, pos + off < seqlen)
        u += jnp.where(ok, tap, 0.0) * cw_ref[j:j + 1, :]
    gates = _sigmoid(_dot(u.astype(BF16), wg_ref[...].astype(BF16)) + bg_ref[...])
    for d, (a_o, b_o) in enumerate(((af_o, bf_o), (ab_o, bb_o))):
        r = gates[:, (2 * d) * WIDTH:(2 * d + 1) * WIDTH]
        ig = gates[:, (2 * d + 1) * WIDTH:(2 * d + 2) * WIDTH]
        log_a = -C_POW * r * _softplus(-lam_ref[d])
        a = jnp.exp(log_a)
        b = jnp.sqrt(-jnp.tanh(log_a) * (a * a + 1.0)) * (ig * u)
        for half in range(WIDTH // LANE):
            a_o[half] = a[:, half * LANE:(half + 1) * LANE]
            b_o[half] = b[:, half * LANE:(half + 1) * LANE]


def _lru_pre(zc, p, ctx_len, dec_len, n_ctx_tiles, tm):
    rows = zc.shape[0]
    nb8 = rows // 8
    per = tm // 8
    seq_of_tile = lambda i: jnp.where(i < n_ctx_tiles, ctx_len, dec_len)
    row = pl.BlockSpec((tm, WIDTH), lambda i: (i, 0))
    full = lambda a: pl.BlockSpec(a.shape, lambda i: (0,) * a.ndim)
    args = (p['conv_w'], p['conv_b'], p['wg'], p['bg'], p['lam'])
    return pl.pallas_call(
        functools.partial(_lru_pre_kernel, seq_of_tile),
        grid=(rows // tm,),
        in_specs=[pl.BlockSpec((8, WIDTH), lambda i: (jnp.maximum(i * per - 1, 0), 0)),
                  row,
                  pl.BlockSpec((8, WIDTH), lambda i: (jnp.minimum((i + 1) * per, nb8 - 1), 0))]
                 + [full(a) for a in args],
        out_specs=[pl.BlockSpec((WIDTH // LANE, tm, LANE), lambda i: (0, i, 0))] * 4,
        out_shape=[jax.ShapeDtypeStruct((WIDTH // LANE, rows, LANE), F32)] * 4,
        scratch_shapes=[pltpu.VMEM((tm + 16, WIDTH), F32)],
        compiler_params=_cp(("parallel",)),
        name='rglru_pre',
    )(zc, zc, zc, *args)


def _lru_scan_kernel(st, af_ref, bf_ref, ab_ref, bb_ref, h0_ref, hf_ref, hb_ref, finf_ref, finb_ref,
                     h_scr, loc_scr, car_scr):
    c = pl.program_id(0)
    nh = WIDTH // LANE

    @pl.when(c == 0)
    def _():
        h_scr[:, 1:] = h0_ref[...]

    @pl.when(st.ctx_first(c))
    def _():
        h_scr[:, 0] = jnp.zeros((nh, 2, 1, LANE), F32)

    rr = LRU_ROWS
    ng = af_ref.shape[2] // rr
    chains = [(p, s, d) for s in range(st.n_seg) for d in range(2) for p in range(nh)]
    refs = ((af_ref, bf_ref, hf_ref), (ab_ref, bb_ref, hb_ref))
    order = (list(range(rr)), list(range(rr - 1, -1, -1)))

    for k, (p, s, d) in enumerate(chains):
        a_ref, b_ref, _ = refs[d]
        hloc = ploc = None
        for r in order[d]:
            a = a_ref[p, s, pl.ds(r, ng, stride=rr), :]
            b = b_ref[p, s, pl.ds(r, ng, stride=rr), :]
            hloc = b if hloc is None else a * hloc + b
            ploc = a if ploc is None else a * ploc
            loc_scr[k, 0, r] = hloc
            loc_scr[k, 1, r] = ploc

    def carry_step(j, carries):
        out = []
        for k, (p, s, d) in enumerate(chains):
            g = j if d == 0 else ng - 1 - j
            r_end = order[d][-1]
            car_scr[k, pl.ds(g, 1), :] = carries[k]
            out.append(loc_scr[k, 1, r_end, pl.ds(g, 1), :] * carries[k] + loc_scr[k, 0, r_end, pl.ds(g, 1), :])
        return tuple(out)

    carries = lax.fori_loop(0, ng, carry_step, tuple(h_scr[p, s, d] for p, s, d in chains))

    for k, (p, s, d) in enumerate(chains):
        h_scr[p, s, d] = carries[k]
        o_ref = refs[d][2]
        car = car_scr[k]
        for r in range(rr):
            o_ref[p, s, pl.ds(r, ng, stride=rr), :] = loc_scr[k, 0, r] + loc_scr[k, 1, r] * car

    @pl.when(st.ctx_last(c))
    def _():
        for p in range(nh):
            finf_ref[0, :, p * LANE:(p + 1) * LANE] = h_scr[p, 0, 0]
            finb_ref[0, :, p * LANE:(p + 1) * LANE] = h_scr[p, 0, 1]


def _lru_scan(coef, h0_dec, st):
    nh = WIDTH // LANE
    a_f, b_f, a_b, b_b = (a.reshape(nh, st.n_seg, -1, LANE) for a in coef)
    n_dec = h0_dec.shape[0]
    h0 = jnp.transpose(h0_dec.reshape(n_dec, 2, nh, 1, LANE), (2, 0, 1, 3, 4))
    n_chain = 2 * st.n_seg * nh
    ng = st.chunk // LRU_ROWS
    n = st.n
    fwd = pl.BlockSpec((nh, st.n_seg, st.chunk, LANE), lambda c: (0, 0, c, 0))
    bwd = pl.BlockSpec((nh, st.n_seg, st.chunk, LANE), lambda c: (0, 0, n - 1 - c, 0))
    h_f, h_b, fin_f, fin_b = pl.pallas_call(
        functools.partial(_lru_scan_kernel, st),
        grid=(st.n,),
        in_specs=[fwd, fwd, bwd, bwd, pl.BlockSpec(h0.shape, lambda c: (0,) * h0.ndim)],
        out_specs=[fwd, bwd, st.fin_fwd((1, WIDTH)), st.fin_bwd((1, WIDTH))],
        out_shape=[jax.ShapeDtypeStruct(a_f.shape, F32), jax.ShapeDtypeStruct(a_f.shape, F32),
                   jax.ShapeDtypeStruct((st.n_ctx, 1, WIDTH), F32), jax.ShapeDtypeStruct((st.n_ctx, 1, WIDTH), F32)],
        scratch_shapes=[pltpu.VMEM((nh, st.n_seg, 2, 1, LANE), F32),
                        pltpu.VMEM((n_chain, 2, LRU_ROWS, ng, LANE), F32),
                        pltpu.VMEM((n_chain, ng, LANE), F32)],
        compiler_params=_cp(("arbitrary",)),
        name='rglru_scan',
    )(a_f, b_f, a_b, b_b, h0)
    return (h_f.reshape(coef[0].shape), h_b.reshape(coef[0].shape), jnp.concatenate([fin_f, fin_b], axis=1))


def _rope_slab(x, cos, sin):
    lane = lax.broadcasted_iota(jnp.int32, x.shape, 1)
    rot = jnp.where(lane % (B_ROPE // 2) < B_ROPE // 4, -pltpu.roll(x, LANE - B_ROPE // 4, 1),
                    pltpu.roll(x, B_ROPE // 4, 1))
    return x * cos + rot * sin


def _kv_up(ckv_bf, kpe_slab, wuk_ref, wuv_ref, k_o, v_o):
    kn = _dot(ckv_bf, wuk_ref[...])
    for h in range(N_HEADS):
        k_o[:, h * SLAB:(h + 1) * SLAB] = (kn[:, h * SLAB:(h + 1) * SLAB] + kpe_slab).astype(BF16)
    lane = lax.broadcasted_iota(jnp.int32, (1, N_HEADS * SLAB), 1)
    v_o[...] = (_dot(ckv_bf, wuv_ref[...]) + jnp.where(lane % SLAB >= B_VDIM, 1.0, 0.0)).astype(BF16)


def _mla_prep_kernel(zb_ref, cos_ref, sin_ref, qn_ref, kvn_ref, wuq_ref, wuk_ref, wuv_ref,
                     ckv_o, q_o, k_o, v_o):
    cq = zb_ref[:, 0:B_Q_RANK]
    ckv = zb_ref[:, B_Q_RANK:B_Q_RANK + B_KV_RANK]
    kpe = zb_ref[:, B_Q_RANK + B_KV_RANK:ZB]
    cos = cos_ref[...]
    sin = sin_ref[...]
    cqn = cq * lax.rsqrt(jnp.mean(cq * cq, axis=-1, keepdims=True) + RMS_EPS) * qn_ref[...]
    ckvn = ckv * lax.rsqrt(jnp.mean(ckv * ckv, axis=-1, keepdims=True) + RMS_EPS) * kvn_ref[...]
    ckv_o[...] = ckvn
    q = _dot(cqn.astype(BF16), wuq_ref[...])
    for h in range(N_HEADS):
        q_o[:, h * SLAB:(h + 1) * SLAB] = _rope_slab(q[:, h * SLAB:(h + 1) * SLAB], cos, sin).astype(BF16)
    _kv_up(ckvn.astype(BF16), _rope_slab(kpe, cos, sin), wuk_ref, wuv_ref, k_o, v_o)


def _mla_prep(zb, cos, sin, p, tm):
    rows = zb.shape[0]
    row = lambda w: pl.BlockSpec((tm, w), lambda i: (i, 0))
    full = lambda a: pl.BlockSpec(a.shape, lambda i: (0,) * a.ndim)
    args = (p['qn_g'], p['kvn_g'], p['wuq'], p['wuk'], p['wuv'])
    return pl.pallas_call(
        _mla_prep_kernel,
        grid=(rows // tm,),
        in_specs=[row(ZB), row(LANE), row(LANE)] + [full(a) for a in args],
        out_specs=[row(B_KV_RANK), row(N_HEADS * SLAB), row(N_HEADS * SLAB), row(N_HEADS * SLAB)],
        out_shape=[jax.ShapeDtypeStruct((rows, B_KV_RANK), F32), jax.ShapeDtypeStruct((rows, N_HEADS * SLAB), BF16),
                   jax.ShapeDtypeStruct((rows, N_HEADS * SLAB), BF16),
                   jax.ShapeDtypeStruct((rows, N_HEADS * SLAB), BF16)],
        compiler_params=_cp(("parallel",)),
        name='mla_prep',
    )(zb, cos, sin, *args)


def _mla_cache_kernel(ckv_ref, kpe_ref, wuk_ref, wuv_ref, k_o, v_o):
    _kv_up(ckv_ref[...].astype(BF16), kpe_ref[...], wuk_ref, wuv_ref, k_o, v_o)


def _mla_cache(ckv, kpe_slab, p):
    rows = ckv.shape[0]
    return pl.pallas_call(
        _mla_cache_kernel,
        out_shape=[jax.ShapeDtypeStruct((rows, N_HEADS * SLAB), BF16)] * 2,
        compiler_params=pltpu.CompilerParams(vmem_limit_bytes=VMEM_LIMIT),
        name='mla_cache',
    )(ckv, kpe_slab, p['wuk'], p['wuv'])


def _attend_heads(q_ref, kv_refs, o_ref):
    log2e_scale = float((B_NOPE + B_ROPE) ** -0.5 * np.log2(np.e))
    tq = q_ref.shape[0]
    low = lax.broadcasted_iota(jnp.int32, (tq, SLAB), 1) < B_VDIM

    def head(h):
        qh = q_ref[:, h * SLAB:(h + 1) * SLAB]
        ss = [_dot_nt(qh, k_ref[:, h * SLAB:(h + 1) * SLAB]) for k_ref, _ in kv_refs]
        yield
        m = ss[0].max(axis=-1, keepdims=True)
        for s in ss[1:]:
            m = jnp.maximum(m, s.max(axis=-1, keepdims=True))
        es = [jnp.exp2(((s - m) * log2e_scale).astype(BF16)) for s in ss]
        yield
        acc = jnp.zeros((tq, SLAB), F32)
        for e, (_, v_ref) in zip(es, kv_refs):
            acc += _dot(e, v_ref[:, h * SLAB:(h + 1) * SLAB])
        return acc / acc[:, B_VDIM:B_VDIM + 1]

    pairs = []
    for h in range(0, N_HEADS, 2):
        h0, h1 = _interleave([head(h), head(h + 1)])
        pairs.append(jnp.where(low, h0, pltpu.roll(h1, B_VDIM, 1)))
    o_ref[...] = jnp.concatenate(pairs, axis=-1)


def _attn_ctx_kernel(q_ref, k_ref, v_ref, o_ref):
    _attend_heads(q_ref, [(k_ref, v_ref)], o_ref)


def _attn_dec_kernel(q_ref, k_ref, v_ref, kc_ref, vc_ref, o_ref):
    _attend_heads(q_ref, [(k_ref, v_ref), (kc_ref, vc_ref)], o_ref)


def _attn_ctx(q, k, v, n_seq, seq_len):
    blk = lambda w: pl.BlockSpec((seq_len, w), lambda b: (b, 0))
    return pl.pallas_call(
        _attn_ctx_kernel,
        grid=(n_seq,),
        in_specs=[blk(N_HEADS * SLAB), blk(N_HEADS * SLAB), blk(N_HEADS * SLAB)],
        out_specs=blk(WIDTH),
        out_shape=jax.ShapeDtypeStruct((n_seq * seq_len, WIDTH), F32),
        compiler_params=_cp(("parallel",)),
        name='attn_ctx',
    )(q, k, v)


def _attn_dec(q, k, v, kc, vc, seg0, n_dec, dec_len, past, tq):
    nq = dec_len // tq
    return pl.pallas_call(
        _attn_dec_kernel,
        grid=(n_dec, nq),
        in_specs=[pl.BlockSpec((tq, N_HEADS * SLAB), lambda b, i: ((seg0 + b) * nq + i, 0)),
                  pl.BlockSpec((dec_len, N_HEADS * SLAB), lambda b, i: (seg0 + b, 0)),
                  pl.BlockSpec((dec_len, N_HEADS * SLAB), lambda b, i: (seg0 + b, 0)),
                  pl.BlockSpec((past, N_HEADS * SLAB), lambda b, i: (b, 0)),
                  pl.BlockSpec((past, N_HEADS * SLAB), lambda b, i: (b, 0))],
        out_specs=pl.BlockSpec((tq, WIDTH), lambda b, i: (b * nq + i, 0)),
        out_shape=jax.ShapeDtypeStruct((n_dec * dec_len, WIDTH), F32),
        compiler_params=_cp(("parallel", "arbitrary")),
        name='attn_dec',
    )(q, k, v, kc, vc)


def _rope_tables(n_ctx_rows, n_dec, dec_len):
    rows = dec_len // GRID_W
    row = np.repeat(np.arange(rows, dtype=np.float32), GRID_W)
    col = np.tile(np.arange(GRID_W, dtype=np.float32), rows)
    half = B_ROPE // 2
    inv = jnp.power(ROPE_BASE, -jnp.arange(0, half, 2, dtype=F32) / half)
    ang_r = jnp.asarray(row)[:, None] * inv
    ang_c = jnp.asarray(col)[:, None] * inv
    ang = jnp.concatenate([ang_r, ang_r, ang_c, ang_c], axis=-1)
    cos = jnp.pad(jnp.cos(ang), ((0, 0), (0, LANE - B_ROPE)), constant_values=1.0)
    sin = jnp.pad(jnp.sin(ang), ((0, 0), (0, LANE - B_ROPE)))
    cos = jnp.concatenate([jnp.ones((n_ctx_rows, LANE), F32)] + [cos] * n_dec, axis=0)
    sin = jnp.concatenate([jnp.zeros((n_ctx_rows, LANE), F32)] + [sin] * n_dec, axis=0)
    return cos, sin


def _block_diag(w):
    g, n, m = w.shape[-3:]
    eye = jnp.eye(g, dtype=w.dtype)
    out = w[..., :, :, None, :] * eye[:, None, :, None]
    return out.reshape(w.shape[:-3] + (g * n, g * m))


def _head_blocks(s):
    lead = s.shape[:-2]
    s = s.reshape(lead + (N_HEADS, HEAD_DIM, N_HEADS, HEAD_DIM))
    return jnp.stack([s[..., h, :, h, :] for h in range(N_HEADS)], axis=-3)


def kernel(x_prompt, x_sample, cache_mla_ckv, cache_mla_kpe, state_rwkv, state_rglru, state_hgrn, c, c_ctx,
           w_ada, b_ada, ln_g, ln_b, w_ffn_in, w_ffn_out, w_in, w_out,
           rwkv_w0, rwkv_w2, rwkv_a0, rwkv_a2, rwkv_g2, rwkv_kk, rwkv_ka, rwkv_rk, rwkv_gn_g, rwkv_gn_b,
           mla_qn_g, mla_w_uq, mla_kvn_g, mla_w_ukv,
           rglru_conv_w, rglru_conv_b, rglru_wa, rglru_ba, rglru_wx, rglru_bx, rglru_lam,
           hgrn_lb, hgrn_gn_g):
    n_ctx, ctx_len, _ = x_prompt.shape
    n_dec, dec_len, _ = x_sample.shape
    past = cache_mla_ckv.shape[2]
    seg = n_ctx * ctx_len
    assert seg == dec_len, "context rows must form one segment of the decode sequence length"
    assert ctx_len & (ctx_len - 1) == 0 and dec_len & (dec_len - 1) == 0, "sequence lengths must be powers of two"
    n_seg = 1 + n_dec
    rows = n_seg * seg
    tm = min(TM, seg)
    tq = min(TQ, dec_len)
    tiles_per_seg = seg // tm
    nseq = n_ctx + n_dec
    assert ctx_len % LRU_CHUNK == 0
    st = _Streams(n_seg, n_ctx, seg, ctx_len, CHUNK)
    lru_st = _Streams(n_seg, n_ctx, seg, ctx_len, LRU_CHUNK)

    hc = {k: jnp.asarray(v) for k, v in _head_consts().items()}
    ebd = hc['mbd']
    cos, sin = _rope_tables(seg, n_dec, dec_len)

    cond8 = jnp.zeros((8, D_MODEL), F32).at[0].set(c_ctx).at[1:1 + n_dec].set(c)
    mods = _ada(cond8, w_ada, b_ada).reshape(DEPTH, 8, N_MOD, D_MODEL)
    mods = jnp.transpose(mods, (0, 2, 1, 3))[:, :, :n_seg, None, :]

    w_ffn_in_bf = w_ffn_in.astype(BF16)
    w_ffn_out_bf = w_ffn_out.astype(BF16)
    w_out_bf = w_out.astype(BF16)
    kpe_end = ZA + B_Q_RANK + B_KV_RANK + B_ROPE
    w_in_p = jnp.concatenate([w_in[:, :, :kpe_end], jnp.zeros((DEPTH, D_MODEL, LANE - B_ROPE), F32),
                              w_in[:, :, kpe_end:]], axis=-1).astype(BF16)
    wuq = mla_w_uq.reshape(DEPTH, B_Q_RANK, N_HEADS, B_NOPE + B_ROPE)
    wuq_p = jnp.concatenate([wuq[..., B_NOPE:], wuq[..., :B_NOPE],
                             jnp.zeros((DEPTH, B_Q_RANK, N_HEADS, SLAB - B_NOPE - B_ROPE), F32)], axis=-1)
    wuq_p = wuq_p.reshape(DEPTH, B_Q_RANK, N_HEADS * SLAB).astype(BF16)
    wukv = mla_w_ukv.reshape(DEPTH, B_KV_RANK, N_HEADS, B_NOPE + B_VDIM)
    wuk_p = jnp.concatenate([jnp.zeros((DEPTH, B_KV_RANK, N_HEADS, B_ROPE), F32), wukv[..., :B_NOPE],
                             jnp.zeros((DEPTH, B_KV_RANK, N_HEADS, SLAB - B_NOPE - B_ROPE), F32)], axis=-1)
    wuk_p = wuk_p.reshape(DEPTH, B_KV_RANK, N_HEADS * SLAB).astype(BF16)
    wuv_p = jnp.concatenate([wukv[..., B_NOPE:], jnp.zeros((DEPTH, B_KV_RANK, N_HEADS, SLAB - B_VDIM), F32)], axis=-1)
    wuv_p = wuv_p.reshape(DEPTH, B_KV_RANK, N_HEADS * SLAB).astype(BF16)
    lru_wg = jnp.concatenate([_block_diag(rglru_wa[:, 0]), _block_diag(rglru_wx[:, 0]),
                              _block_diag(rglru_wa[:, 1]), _block_diag(rglru_wx[:, 1])], axis=-1)
    lru_bg = jnp.concatenate([rglru_ba[:, 0], rglru_bx[:, 0], rglru_ba[:, 1], rglru_bx[:, 1]], axis=-1)[:, None, :]
    lbs = _lower_bounds(hgrn_lb)

    rwkv_s0 = _block_diag(state_rwkv)
    hgrn_s0 = _block_diag(jnp.swapaxes(state_hgrn, -1, -2))
    cache_kpe_slab = jnp.pad(cache_mla_kpe, ((0, 0), (0, 0), (0, 0), (0, LANE - B_ROPE)))

    x = jnp.concatenate([x_prompt.reshape(seg, D_MODEL), x_sample.reshape(n_dec * dec_len, D_MODEL)], axis=0)
    new_ckv, new_kpe, new_rwkv, new_lru, new_hgrn = [], [], [], [], []
    for l in range(DEPTH):
        m = mods[l]
        lng = ln_g[l][:, None, :]
        lnb = ln_b[l][:, None, :]
        x = _ffn(x, m[0], m[1], m[2], lng[0], lnb[0], w_ffn_in_bf, w_ffn_out_bf, l, 0, tiles_per_seg, tm)
        za, zb, zc, zd = _inproj(x, m[3], m[4], w_in_p, l, tiles_per_seg, tm)

        pa = {'w0': rwkv_w0[l][:, None, :], 'w2': rwkv_w2[l], 'a0': rwkv_a0[l][:, None, :], 'a2': rwkv_a2[l],
              'g2': rwkv_g2[l], 'kk': rwkv_kk[l][None, :], 'ka': rwkv_ka[l][None, :], 'rk': rwkv_rk[l][None, :]}
        pre = _rwkv_pre(za, pa, ebd, tm)
        o_f, o_b, s_fin = _rwkv_scan(za, pre[:7], rwkv_s0[:, l], hc, st)
        ya = _rwkv_post(o_f, o_b, pre[7], pre[8], rwkv_gn_g[l][None, :], rwkv_gn_b[l][None, :], ebd, tm)
        new_rwkv.append(_head_blocks(s_fin))

        pb = {'qn_g': mla_qn_g[l][None, :], 'kvn_g': mla_kvn_g[l][None, :], 'wuq': wuq_p[l], 'wuk': wuk_p[l],
              'wuv': wuv_p[l]}
        ckvn, q_all, k_all, v_all = _mla_prep(zb, cos, sin, pb, tm)
        kc, vc = _mla_cache(cache_mla_ckv[:, l].reshape(n_dec * past, B_KV_RANK),
                            cache_kpe_slab[:, l].reshape(n_dec * past, LANE), pb)
        yb = jnp.concatenate([_attn_ctx(q_all, k_all, v_all, n_ctx, ctx_len),
                              _attn_dec(q_all, k_all, v_all, kc, vc, 1, n_dec, dec_len, past, tq)], axis=0)
        new_ckv.append(ckvn[:seg].reshape(n_ctx, ctx_len, B_KV_RANK))
        new_kpe.append(zb[:seg, B_Q_RANK + B_KV_RANK:B_Q_RANK + B_KV_RANK + B_ROPE].reshape(n_ctx, ctx_len, B_ROPE))

        pc = {'conv_w': rglru_conv_w[l], 'conv_b': rglru_conv_b[l][None, :], 'wg': lru_wg[l], 'bg': lru_bg[l],
              'lam': rglru_lam[l][:, None, :]}
        coef = _lru_pre(zc, pc, ctx_len, dec_len, tiles_per_seg, tm)
        h_f, h_b, h_fin = _lru_scan(coef, state_rglru[:, l], lru_st)
        new_lru.append(h_fin)

        od_f, od_b, sd_fin = _hgrn_scan(zd, lbs[l], hgrn_s0[:, l], hc, st)
        yd = _hgrn_post(od_f, od_b, zd, hgrn_gn_g[l][None, :], ebd, tm)
        new_hgrn.append(jnp.swapaxes(_head_blocks(sd_fin), -1, -2))

        x = _outproj(x, ya, yb, h_f, h_b, zc, yd, w_out_bf, m[5], lng[1], lnb[1], l, tiles_per_seg, tm)
        x = _ffn(x, m[6], m[7], m[8], lng[2], lnb[2], w_ffn_in_bf, w_ffn_out_bf, l, 1, tiles_per_seg, tm)

    y_prompt = x[:seg].reshape(n_ctx, ctx_len, D_MODEL)
    y_sample = x[seg:].reshape(n_dec, dec_len, D_MODEL)
    return (y_prompt, y_sample, jnp.stack(new_ckv, axis=1), jnp.stack(new_kpe, axis=1),
            jnp.stack(new_rwkv, axis=1), jnp.stack(new_lru, axis=1), jnp.stack(new_hgrn, axis=1))
```

```python
import functools

import numpy as np
import jax
import jax.numpy as jnp
from jax import lax
from jax.experimental import pallas as pl
from jax.experimental.pallas import tpu as pltpu

D_MODEL = 1024
DEPTH = 4
GRID_W = 64
HEAD_DIM = 64
N_HEADS = 4
WIDTH = N_HEADS * HEAD_DIM
A_DECAY_RANK = 64
A_ICLR_RANK = 64
A_GATE_RANK = 128
B_NOPE = 64
B_ROPE = 32
B_VDIM = 64
B_Q_RANK = 256
B_KV_RANK = 128
C_CONV = 4
C_POW = 8.0
D_FF = 2816
N_MOD = 9
ROPE_BASE = 10000.0
LN_EPS = 1e-5
RMS_EPS = 1e-6
RWKV_GN_EPS = 64e-5
DN_ALPHA = (2 * DEPTH) ** 0.25

LANE = 128
SLAB = 128
CHUNK = 64
SUB = 16
LRU_CHUNK = 256
LRU_ROWS = 8
TM = 512
TF = 256
TQ = 256
VMEM_LIMIT = 56 * 1024 * 1024

F32 = jnp.float32
BF16 = jnp.bfloat16
HI = lax.Precision.HIGHEST
NEG = -1e30
LOG2E = float(np.log2(np.e))
RWKV_HI = False


def _cp(sem):
    return pltpu.CompilerParams(dimension_semantics=sem, vmem_limit_bytes=VMEM_LIMIT)


def _dot(a, b, hi=False):
    return jnp.dot(a, b, preferred_element_type=F32, precision=HI if hi else None)


def _dot_nt(a, b, hi=False):
    return lax.dot_general(a, b, (((1,), (1,)), ((), ())), preferred_element_type=F32,
                           precision=HI if hi else None)


def _dot_tn(a, b, hi=False):
    return lax.dot_general(a, b, (((0,), (0,)), ((), ())), preferred_element_type=F32,
                           precision=HI if hi else None)


def _head_sums(x, ebd_bf):
    hi = x.astype(BF16)
    lo = (x - hi.astype(F32)).astype(BF16)
    return _dot(hi, ebd_bf) + _dot(lo, ebd_bf)


def _sigmoid(x):
    return 1.0 / (1.0 + jnp.exp(-x))


def _silu(x):
    return x * _sigmoid(x)


def _softplus(x):
    return jnp.maximum(x, 0.0) + jnp.log(1.0 + jnp.exp(-jnp.abs(x)))


def _tile4(y):
    return jnp.concatenate([y, y, y, y], axis=0)


def _head_consts():
    r = np.arange(WIDTH)
    same = (r[:, None] // HEAD_DIM) == (r[None, :] // HEAD_DIM)
    t = np.arange(CHUNK)[:, None]
    j = (np.arange(WIDTH) % CHUNK)[None, :]
    tt = np.arange(CHUNK)
    return {
        'mbd': same.astype(np.float32),
        'low_s': (j < t).astype(np.float32), 'low_i': (j <= t).astype(np.float32),
        'up_s': (j > t).astype(np.float32), 'up_i': (j >= t).astype(np.float32),
        'eye': (j == t).astype(np.float32),
        'tri_f': (tt[None, :] <= tt[:, None]).astype(np.float32),
        'tri_b': (tt[None, :] >= tt[:, None]).astype(np.float32),
    }


def _lb_kernel(x_ref, o_ref):
    x = x_ref[...]
    m = jnp.max(x, axis=0, keepdims=True)
    e = jnp.exp(x - m)
    sm = e / jnp.sum(e, axis=0, keepdims=True)
    run = sm[0:1]
    rows = [run - sm[0:1]]
    for l in range(1, DEPTH):
        run = run + sm[l:l + 1]
        rows.append(run - sm[0:1])
    o_ref[...] = jnp.concatenate(rows, axis=0)


def _lower_bounds(hgrn_lb):
    flat = hgrn_lb.reshape(DEPTH, 2 * WIDTH)
    out = pl.pallas_call(_lb_kernel, out_shape=jax.ShapeDtypeStruct(flat.shape, F32), name='hgrn_lb')(flat)
    return out.reshape(DEPTH, 2, 1, WIDTH)


def _ada_kernel(c_ref, w_ref, b_ref, o_ref):
    h = _silu(c_ref[...]).astype(BF16)
    o_ref[0] = _dot(h, w_ref[0].astype(BF16)) + b_ref[0]


def _ada(cond8, w_ada, b_ada):
    n = N_MOD * D_MODEL
    tn = 2304
    return pl.pallas_call(
        _ada_kernel,
        grid=(DEPTH, n // tn),
        in_specs=[pl.BlockSpec((8, D_MODEL), lambda l, j: (0, 0)),
                  pl.BlockSpec((1, D_MODEL, tn), lambda l, j: (l, 0, j)),
                  pl.BlockSpec((1, 1, tn), lambda l, j: (l, 0, j))],
        out_specs=pl.BlockSpec((1, 8, tn), lambda l, j: (l, 0, j)),
        out_shape=jax.ShapeDtypeStruct((DEPTH, 8, n), F32),
        compiler_params=_cp(("arbitrary", "arbitrary")),
        name='ada',
    )(cond8, w_ada, b_ada.reshape(DEPTH, 1, n))


def _layer_norm_rows(y, g, b):
    mu = jnp.mean(y, axis=-1, keepdims=True)
    d = y - mu
    var = jnp.mean(d * d, axis=-1, keepdims=True)
    return d * lax.rsqrt(var + LN_EPS) * g + b


def _ffn_kernel(x_ref, sh_ref, sc_ref, g_ref, lng_ref, lnb_ref, win_ref, wout_ref, o_ref):
    x = x_ref[...]
    h = (x * (1.0 + sc_ref[0]) + sh_ref[0]).astype(BF16)
    n_chunks = D_FF // TF

    def gate_up(c):
        return (_dot(h, win_ref[:, c * TF:(c + 1) * TF]),
                _dot(h, win_ref[:, D_FF + c * TF:D_FF + (c + 1) * TF]))

    y = jnp.zeros(x.shape, F32)
    cur = gate_up(0)
    for c in range(n_chunks):
        nxt = gate_up(c + 1) if c + 1 < n_chunks else None
        a = (_silu(cur[0]) * cur[1]).astype(BF16)
        y = y + _dot(a, wout_ref[c * TF:(c + 1) * TF, :])
        cur = nxt
    o_ref[...] = _layer_norm_rows(DN_ALPHA * x + 0.5 * g_ref[0] * y, lng_ref[...], lnb_ref[...])


class _Tiles:
    def __init__(self, n_seg, seg_rows, tm):
        self.n_seg, self.seg_rows, self.tm = n_seg, seg_rows, tm
        self.per_seg = seg_rows // tm
        self.grid = (n_seg * self.per_seg,)

    def row(self, width, col=0):
        ps = self.per_seg
        return pl.BlockSpec((None, self.tm, width), lambda i: (i // ps, i % ps, col))

    def mod(self):
        ps = self.per_seg
        return pl.BlockSpec((1, 1, D_MODEL), lambda i: (i // ps, 0, 0))

    def shape(self, width, dtype=F32):
        return jax.ShapeDtypeStruct((self.n_seg, self.seg_rows, width), dtype)


def _full(a):
    return pl.BlockSpec(a.shape, lambda i: (0,) * a.ndim)


def _ffn(x, sh, sc, g, lng, lnb, w_in_bf, w_out_bf, l, f, tl):
    mod = tl.mod()
    vec = pl.BlockSpec((1, D_MODEL), lambda i: (0, 0))
    return pl.pallas_call(
        _ffn_kernel,
        grid=tl.grid,
        in_specs=[tl.row(D_MODEL), mod, mod, mod, vec, vec,
                  pl.BlockSpec((None, None, D_MODEL, 2 * D_FF), lambda i: (l, f, 0, 0)),
                  pl.BlockSpec((None, None, D_FF, D_MODEL), lambda i: (l, f, 0, 0))],
        out_specs=tl.row(D_MODEL),
        out_shape=tl.shape(D_MODEL),
        compiler_params=_cp(("parallel",)),
        name='ffn',
    )(x, sh, sc, g, lng, lnb, w_in_bf, w_out_bf)


ZA = 3 * WIDTH + 2 * A_DECAY_RANK + 2 * A_ICLR_RANK + A_GATE_RANK
ZB = B_Q_RANK + B_KV_RANK + LANE
ZC = 2 * WIDTH
ZD = 5 * WIDTH
ZTOT = ZA + ZB + ZC + ZD


def _inproj_kernel(x_ref, sh_ref, sc_ref, w_ref, za_ref, zb_ref, zc_ref, zd_ref):
    h = (x_ref[...] * (1.0 + sc_ref[0]) + sh_ref[0]).astype(BF16)
    za_ref[...] = _dot(h, w_ref[:, 0:ZA])
    zb_ref[...] = _dot(h, w_ref[:, ZA:ZA + ZB])
    zc_ref[...] = _dot(h, w_ref[:, ZA + ZB:ZA + ZB + ZC])
    zd_ref[...] = _dot(h, w_ref[:, ZA + ZB + ZC:ZTOT])


def _inproj(x, sh, sc, w_in_p, l, tl):
    mod = tl.mod()
    return pl.pallas_call(
        _inproj_kernel,
        grid=tl.grid,
        in_specs=[tl.row(D_MODEL), mod, mod, pl.BlockSpec((None, D_MODEL, ZTOT), lambda i: (l, 0, 0))],
        out_specs=[tl.row(ZA), tl.row(ZB), tl.row(ZC), tl.row(ZD)],
        out_shape=[tl.shape(w) for w in (ZA, ZB, ZC, ZD)],
        compiler_params=_cp(("parallel",)),
        name='mixer_in',
    )(x, sh, sc, w_in_p)


def _gelu_tanh(x):
    return 0.5 * x * (1.0 + jnp.tanh(np.sqrt(2.0 / np.pi) * (x + 0.044715 * (x * x * x))))


def _outproj_kernel(x_ref, af_ref, ab_ref, bon_ref, gate_ref, yb_ref, hf_ref, hb_ref, cg_ref, df_ref, db_ref,
                    dg_ref, agn_g_ref, agn_b_ref, dgn_g_ref, ebd_ref, w_ref, g_ref, lng_ref, lnb_ref, o_ref):
    avg = (ebd_ref[...] * (1.0 / HEAD_DIM)).astype(BF16)
    oa = af_ref[...] + ab_ref[...]
    da = oa - _head_sums(oa, avg)
    gn = da * lax.rsqrt(_head_sums(da * da, avg) + RWKV_GN_EPS) * agn_g_ref[...] + agn_b_ref[...]
    ya = (gn + bon_ref[...]) * gate_ref[...]
    h = jnp.concatenate([hf_ref[p] + hb_ref[p] for p in range(WIDTH // LANE)], axis=-1)
    yc = h * _gelu_tanh(cg_ref[...])
    od = df_ref[...] + db_ref[...]
    yd = od * lax.rsqrt(_head_sums(od * od, avg) + RMS_EPS) * dgn_g_ref[...] * _silu(dg_ref[...])
    y = _dot(ya.astype(BF16), w_ref[0:WIDTH, :])
    y += _dot(yb_ref[...].astype(BF16), w_ref[WIDTH:2 * WIDTH, :])
    y += _dot(yc.astype(BF16), w_ref[2 * WIDTH:3 * WIDTH, :])
    y += _dot(yd.astype(BF16), w_ref[3 * WIDTH:4 * WIDTH, :])
    o_ref[...] = _layer_norm_rows(DN_ALPHA * x_ref[...] + g_ref[0] * y, lng_ref[...], lnb_ref[...])


def _outproj(x, oa_f, oa_b, bonus, gate, yb, hf, hb, zc, od_f, od_b, zd, agn_g, agn_b, dgn_g, ebd,
             w_out_bf, g, lng, lnb, l, tl):
    vec = pl.BlockSpec((1, D_MODEL), lambda i: (0, 0))
    wvec = pl.BlockSpec((1, WIDTH), lambda i: (0, 0))
    ps = tl.per_seg
    halves = pl.BlockSpec((WIDTH // LANE, None, tl.tm, LANE), lambda i: (0, i // ps, i % ps, 0))
    w = WIDTH
    return pl.pallas_call(
        _outproj_kernel,
        grid=tl.grid,
        in_specs=[tl.row(D_MODEL), tl.row(w), tl.row(w), tl.row(w), tl.row(w), tl.row(w), halves, halves,
                  tl.row(w, 1), tl.row(w), tl.row(w), tl.row(w, 4), wvec, wvec, wvec, _full(ebd),
                  pl.BlockSpec((None, 4 * WIDTH, D_MODEL), lambda i: (l, 0, 0)), tl.mod(), vec, vec],
        out_specs=tl.row(D_MODEL),
        out_shape=tl.shape(D_MODEL),
        compiler_params=_cp(("parallel",)),
        name='mixer_out',
    )(x, oa_f, oa_b, bonus, gate, yb, hf, hb, zc, od_f, od_b, zd, agn_g, agn_b, dgn_g, ebd, w_out_bf, g, lng, lnb)


class _Streams:
    def __init__(self, n_seg, n_ctx, seg_rows, ctx_len, chunk):
        self.n_seg, self.n_ctx, self.chunk = n_seg, n_ctx, chunk
        self.n = seg_rows // chunk
        self.ctx_n = ctx_len // chunk

    def fwd(self, width, col=0):
        return pl.BlockSpec((self.n_seg, self.chunk, width), lambda c: (0, c, col))

    def bwd(self, width, col=0):
        n = self.n
        return pl.BlockSpec((self.n_seg, self.chunk, width), lambda c: (0, n - 1 - c, col))

    def fin_fwd(self, shape):
        ctx_n = self.ctx_n
        return pl.BlockSpec((1,) + shape, lambda c: (c // ctx_n,) + (0,) * len(shape))

    def fin_bwd(self, shape):
        ctx_n, n_ctx = self.ctx_n, self.n_ctx
        return pl.BlockSpec((1,) + shape, lambda c: (n_ctx - 1 - c // ctx_n,) + (0,) * len(shape))

    def ctx_first(self, c):
        return c % self.ctx_n == 0

    def ctx_last(self, c):
        return c % self.ctx_n == self.ctx_n - 1


def _rwkv_pre_kernel(za_ref, w0_ref, w2_ref, a0_ref, a2_ref, g2_ref, kkp_ref, kap_ref, rkp_ref, ebd_ref,
                     kap_o, lwf_o, lwb_o, kdf_o, kdb_o, bbf_o, bbb_o, bon_o, gate_o):
    r = za_ref[:, 0:WIDTH]
    k = za_ref[:, WIDTH:2 * WIDTH]
    v = za_ref[:, 2 * WIDTH:3 * WIDTH]
    o = 3 * WIDTH
    xw = (za_ref[:, o:o + A_DECAY_RANK], za_ref[:, o + A_DECAY_RANK:o + 2 * A_DECAY_RANK])
    o += 2 * A_DECAY_RANK
    xa = (za_ref[:, o:o + A_ICLR_RANK], za_ref[:, o + A_ICLR_RANK:o + 2 * A_ICLR_RANK])
    o += 2 * A_ICLR_RANK
    xg = za_ref[:, o:o + A_GATE_RANK]
    ebd = ebd_ref[...].astype(BF16)

    kk = k * kkp_ref[...]
    nrm = jnp.sqrt(_head_sums(kk * kk, ebd))
    kappa = kk / jnp.maximum(nrm, 1e-12)
    kap_o[...] = kappa
    k_sum = jnp.zeros_like(r)
    for d, (lw_o, kd_o, bb_o) in enumerate(((lwf_o, kdf_o, bbf_o), (lwb_o, kdb_o, bbb_o))):
        w_log = -_softplus(-(w0_ref[d] + _dot(jnp.tanh(xw[d]).astype(BF16), w2_ref[d].astype(BF16)))) - 0.5
        lw_o[...] = -jnp.exp(w_log)
        a = _sigmoid(a0_ref[d] + _dot(xa[d].astype(BF16), a2_ref[d].astype(BF16)))
        k_d = k * (1.0 + (a - 1.0) * kap_ref[...])
        kd_o[...] = k_d
        bb_o[...] = kappa * a
        k_sum += k_d
    bon_o[...] = _head_sums(r * k_sum * rkp_ref[...], ebd) * v
    gate_o[...] = _dot(_sigmoid(xg).astype(BF16), g2_ref[...].astype(BF16))


def _rwkv_pre(za, p, ebd, tl):
    args = (p['w0'], p['w2'], p['a0'], p['a2'], p['g2'], p['kk'], p['ka'], p['rk'], ebd)
    return pl.pallas_call(
        _rwkv_pre_kernel,
        grid=tl.grid,
        in_specs=[tl.row(ZA)] + [_full(a) for a in args],
        out_specs=[tl.row(WIDTH)] * 9,
        out_shape=[tl.shape(WIDTH)] * 9,
        compiler_params=_cp(("parallel",)),
        name='rwkv_pre',
    )(za, *args)


def _rwkv_chunk(r, v, kap, lw, kd, bb, st, mbd, eye, m_strict, m_incl, tri):
    cum = _dot(tri, lw, hi=True)
    yield
    tot = jnp.sum(lw, axis=0, keepdims=True)
    g_in = jnp.exp(cum)
    g_inv = jnp.exp(-cum)
    g_ex = jnp.exp(cum - lw)
    g_end = jnp.exp(tot - cum)
    cast = (lambda a: a) if RWKV_HI else (lambda a: a.astype(BF16))
    mask = cast(mbd)
    bd = lambda y: _tile4(cast(y)) * mask
    x = cast(jnp.concatenate([kap * g_ex, r * g_in], axis=0))
    ab = _dot_nt(x, bd(bb * g_inv), hi=RWKV_HI)
    ak = _dot_nt(x, bd(kd * g_inv), hi=RWKV_HI)
    xs = _dot_nt(x, cast(st), hi=RWKV_HI)
    yield
    c = r.shape[0]
    strict = m_strict > 0.5
    incl = m_incl > 0.5
    a_ub = jnp.where(strict, ab[:c], 0.0)
    a_rb = jnp.where(incl, ab[c:], 0.0)
    a_uk = jnp.where(strict, ak[:c], 0.0)
    a_rk = jnp.where(incl, ak[c:], 0.0)

    def catmul(pc, q):
        return _dot(cast(pc), bd(q), hi=RWKV_HI)

    xs = xs + catmul(jnp.concatenate([a_uk, a_rk], axis=0), v)
    rhs, o_v = xs[:c], xs[c:]
    xp = -a_ub
    inv = eye + xp
    xp = catmul(xp, xp)
    yield
    for _ in range(int(np.log2(c)) - 2):
        both = catmul(jnp.concatenate([inv, xp], axis=0), xp)
        yield
        inv = inv + both[:c]
        xp = both[c:]
    inv = inv + catmul(inv, xp)
    yield
    u = -catmul(inv, rhs)
    yield
    o = o_v + catmul(a_rb, u)
    upd = _dot_tn(cast(jnp.concatenate([u, v], axis=0)),
                  cast(jnp.concatenate([bb * g_end, kd * g_end], axis=0)), hi=RWKV_HI)
    st_new = st * jnp.exp(tot) + upd * mbd
    return o, st_new


def _interleave(gens):
    results = [None] * len(gens)
    live = list(range(len(gens)))
    while live:
        still = []
        for k in live:
            try:
                next(gens[k])
                still.append(k)
            except StopIteration as done:
                results[k] = done.value
        live = still
    return results


def _init_states(st, c, st_scr, s0_ref):
    @pl.when(c == 0)
    def _():
        st_scr[1:] = s0_ref[...]

    @pl.when(st.ctx_first(c))
    def _():
        st_scr[0] = jnp.zeros(st_scr.shape[1:], F32)


def _hgrn_chunk(xq, xf, xi, lb, st, tri, ebd, mbd, p_scr, reverse):
    c = xq.shape[0]
    nb = c // SUB
    q = _silu(xq)
    gsig = lb + (1.0 - lb) * _sigmoid(xf)
    kk = 1.0 - gsig
    lg = jnp.log(gsig)
    cum = _dot(tri, lg, hi=True)
    yield
    cum = cum * LOG2E
    tot = jnp.sum(lg, axis=0, keepdims=True) * LOG2E
    lane_s = lax.broadcasted_iota(jnp.int32, (SUB, WIDTH), 1) % c
    half = SUB // 2
    row_h = lax.broadcasted_iota(jnp.int32, (half, WIDTH), 0)
    blk = lambda a, i: a[i * SUB:(i + 1) * SUB]
    end_row = (lambda j: j * SUB) if reverse else (lambda j: j * SUB + SUB - 1)
    later = (lambda j: range(0, j)) if reverse else (lambda j: range(j + 1, nb))

    k_end = jnp.concatenate([blk(kk, j) * jnp.exp2(cum[end_row(j):end_row(j) + 1] - blk(cum, j))
                             for j in range(nb)], axis=0)
    q_parts, where_part = [], {}
    for j in range(nb):
        for i in later(j):
            where_part[(i, j)] = len(q_parts)
            q_parts.append(blk(q, i) * jnp.exp2(blk(cum, i) - cum[end_row(j):end_row(j) + 1]))
    cross = _dot_nt(jnp.concatenate(q_parts, axis=0).astype(BF16), (_tile4(k_end) * mbd).astype(BF16))
    o_state = _dot_nt((q * jnp.exp2(cum)).astype(BF16), st.astype(BF16))
    upd = _dot_tn(xi.astype(BF16), (kk * jnp.exp2(tot - cum)).astype(BF16))
    yield

    for i in range(nb):
        cb, qb = blk(cum, i), blk(q, i)
        for sl in range(SUB):
            s = i * SUB + sl
            parts = []
            for lo in (0, half):
                hi = lo + half - 1
                none_valid = lo > sl if reverse else hi < sl
                all_valid = hi <= sl if reverse else lo >= sl
                if none_valid:
                    parts.append(jnp.zeros((half, WIDTH), F32))
                    continue
                d = cb[lo:lo + half] - cum[s:s + 1]
                if not all_valid:
                    valid = (row_h + lo <= sl) if reverse else (row_h + lo >= sl)
                    d = jnp.where(valid, d, NEG)
                parts.append(jnp.exp2(d) * qb[lo:lo + half] * kk[s:s + 1])
            p_scr[s * SUB:(s + 1) * SUB, :] = jnp.concatenate(parts, axis=0).astype(BF16)
    same = _dot(p_scr[...], ebd.astype(BF16))
    yield

    att_rows = []
    for i in range(nb):
        att = jnp.zeros((SUB, WIDTH), F32)
        for sl in range(SUB):
            s = i * SUB + sl
            att = jnp.where(lane_s == s, same[s * SUB:(s + 1) * SUB], att)
        for j in range(nb):
            if (i, j) in where_part:
                n = where_part[(i, j)]
                att = jnp.where(lane_s // SUB == j, cross[n * SUB:(n + 1) * SUB], att)
        att_rows.append(att)
    att = jnp.concatenate(att_rows, axis=0)
    o = o_state + _dot(att.astype(BF16), (_tile4(xi) * mbd).astype(BF16))
    st_new = st * jnp.exp2(tot) + upd * mbd
    return o, st_new


def _mix_scan_kernel(st,
                     rf_ref, vf_ref, kapf_ref, lwf_ref, kdf_ref, bbf_ref,
                     rb_ref, vb_ref, kapb_ref, lwb_ref, kdb_ref, bbb_ref,
                     qf_ref, ff_ref, if_ref, qb_ref, fb_ref, ib_ref, lb_ref,
                     sa0_ref, sd0_ref, mbd_ref, eye_ref, lows_ref, lowi_ref, ups_ref, upi_ref, trif_ref, trib_ref,
                     af_ref, ab_ref, afin_f, afin_b, df_ref, db_ref, dfin_f, dfin_b,
                     sa_scr, sd_scr, p_scr):
    c = pl.program_id(0)
    _init_states(st, c, sa_scr, sa0_ref)
    _init_states(st, c, sd_scr, sd0_ref)
    mbd = mbd_ref[...]
    eye = eye_ref[...]
    gens = []
    for s in range(st.n_seg):
        gens.append(_rwkv_chunk(rf_ref[s], vf_ref[s], kapf_ref[s], lwf_ref[s], kdf_ref[s], bbf_ref[s],
                                sa_scr[s, 0], mbd, eye, lows_ref[...], lowi_ref[...], trif_ref[...]))
        gens.append(_hgrn_chunk(qf_ref[s], ff_ref[s], if_ref[s], lb_ref[0], sd_scr[s, 0], trif_ref[...],
                                mbd, mbd, p_scr.at[s, 0], False))
        gens.append(_rwkv_chunk(rb_ref[s], vb_ref[s], kapb_ref[s], lwb_ref[s], kdb_ref[s], bbb_ref[s],
                                sa_scr[s, 1], mbd, eye, ups_ref[...], upi_ref[...], trib_ref[...]))
        gens.append(_hgrn_chunk(qb_ref[s], fb_ref[s], ib_ref[s], lb_ref[1], sd_scr[s, 1], trib_ref[...],
                                mbd, mbd, p_scr.at[s, 1], True))
    outs = ((af_ref, sa_scr), (df_ref, sd_scr), (ab_ref, sa_scr), (db_ref, sd_scr))
    for k, (o, st_new) in enumerate(_interleave(gens)):
        s, kind = k // 4, k % 4
        o_ref, scr = outs[kind]
        o_ref[s] = o
        scr[s, kind // 2] = st_new

    @pl.when(st.ctx_last(c))
    def _():
        afin_f[0] = sa_scr[0, 0]
        afin_b[0] = sa_scr[0, 1]
        dfin_f[0] = sd_scr[0, 0]
        dfin_b[0] = sd_scr[0, 1]


def _mix_scan(za, pre, zd, lb, sa0_dec, sd0_dec, hc, st):
    kap, lwf, lwb, kdf, kdb, bbf, bbb = pre
    consts = (hc['mbd'], hc['eye'], hc['low_s'], hc['low_i'], hc['up_s'], hc['up_i'], hc['tri_f'], hc['tri_b'])
    w = WIDTH
    o_shape = jax.ShapeDtypeStruct(kap.shape, F32)
    fin_shape = jax.ShapeDtypeStruct((st.n_ctx, w, w), F32)
    fin_specs = [st.fin_fwd((w, w)), st.fin_bwd((w, w))]
    outs = pl.pallas_call(
        functools.partial(_mix_scan_kernel, st),
        grid=(st.n,),
        in_specs=[st.fwd(w, 0), st.fwd(w, 2), st.fwd(w), st.fwd(w), st.fwd(w), st.fwd(w),
                  st.bwd(w, 0), st.bwd(w, 2), st.bwd(w), st.bwd(w), st.bwd(w), st.bwd(w),
                  st.fwd(w, 0), st.fwd(w, 1), st.fwd(w, 3), st.bwd(w, 0), st.bwd(w, 2), st.bwd(w, 3),
                  _full(lb), _full(sa0_dec), _full(sd0_dec)] + [_full(a) for a in consts],
        out_specs=[st.fwd(w), st.bwd(w)] + fin_specs + [st.fwd(w), st.bwd(w)] + fin_specs,
        out_shape=[o_shape, o_shape, fin_shape, fin_shape] * 2,
        scratch_shapes=[pltpu.VMEM((st.n_seg, 2, w, w), F32), pltpu.VMEM((st.n_seg, 2, w, w), F32),
                        pltpu.VMEM((st.n_seg, 2, CHUNK * SUB, w), BF16)],
        compiler_params=_cp(("arbitrary",)),
        name='mix_scan',
    )(za, za, kap, lwf, kdf, bbf, za, za, kap, lwb, kdb, bbb, zd, zd, zd, zd, zd, zd, lb, sa0_dec, sd0_dec, *consts)
    a_f, a_b, afin_f, afin_b, d_f, d_b, dfin_f, dfin_b = outs
    return ((a_f, a_b, jnp.stack([afin_f, afin_b], axis=1)), (d_f, d_b, jnp.stack([dfin_f, dfin_b], axis=1)))


def _lru_pre_kernel(seq_of_tile, xp_ref, x_ref, xn_ref, cw_ref, cb_ref, wg_ref, bg_ref, lam_ref,
                    af_o, bf_o, ab_o, bb_o, pad_scr):
    i = pl.program_id(0)
    tm = x_ref.shape[0]
    seqlen = seq_of_tile(i)
    pad_scr[0:8, :] = xp_ref[:, 0:WIDTH]
    pad_scr[8:8 + tm, :] = x_ref[:, 0:WIDTH]
    pad_scr[8 + tm:16 + tm, :] = xn_ref[:, 0:WIDTH]
    pos = jnp.bitwise_and(lax.broadcasted_iota(jnp.int32, (tm, WIDTH), 0) + i * tm, seqlen - 1)
    u = jnp.zeros((tm, WIDTH), F32) + cb_ref[...]
    for j in range(C_CONV):
        off = j - C_CONV // 2
        tap = pad_scr[pl.ds(8 + off, tm), :]
        ok = jnp.logical_and(pos + off >= 0, pos + off < seqlen)
        u += jnp.where(ok, tap, 0.0) * cw_ref[j:j + 1, :]
    gates = _sigmoid(_dot(u.astype(BF16), wg_ref[...].astype(BF16)) + bg_ref[...])
    for d, (a_o, b_o) in enumerate(((af_o, bf_o), (ab_o, bb_o))):
        r = gates[:, (2 * d) * WIDTH:(2 * d + 1) * WIDTH]
        ig = gates[:, (2 * d + 1) * WIDTH:(2 * d + 2) * WIDTH]
        log_a = -C_POW * r * _softplus(-lam_ref[d])
        a = jnp.exp(log_a)
        b = jnp.sqrt(-jnp.tanh(log_a) * (a * a + 1.0)) * (ig * u)
        for half in range(WIDTH // LANE):
            a_o[half] = a[:, half * LANE:(half + 1) * LANE]
            b_o[half] = b[:, half * LANE:(half + 1) * LANE]


def _lru_pre(zc, p, ctx_len, dec_len, tl):
    ps, tm = tl.per_seg, tl.tm
    per = tm // 8
    last8 = tl.seg_rows // 8 - 1
    seq_of_tile = lambda i: jnp.where(i < ps, ctx_len, dec_len)
    args = (p['conv_w'], p['conv_b'], p['wg'], p['bg'], p['lam'])
    nh = WIDTH // LANE
    prev8 = pl.BlockSpec((None, 8, WIDTH), lambda i: (i // ps, jnp.maximum((i % ps) * per - 1, 0), 0))
    next8 = pl.BlockSpec((None, 8, WIDTH), lambda i: (i // ps, jnp.minimum((i % ps + 1) * per, last8), 0))
    return pl.pallas_call(
        functools.partial(_lru_pre_kernel, seq_of_tile),
        grid=tl.grid,
        in_specs=[prev8, tl.row(WIDTH), next8] + [_full(a) for a in args],
        out_specs=[pl.BlockSpec((nh, None, tm, LANE), lambda i: (0, i // ps, i % ps, 0))] * 4,
        out_shape=[jax.ShapeDtypeStruct((nh, tl.n_seg, tl.seg_rows, LANE), F32)] * 4,
        scratch_shapes=[pltpu.VMEM((tm + 16, WIDTH), F32)],
        compiler_params=_cp(("parallel",)),
        name='rglru_pre',
    )(zc, zc, zc, *args)


def _lru_scan_kernel(st, af_ref, bf_ref, ab_ref, bb_ref, h0_ref, hf_ref, hb_ref, finf_ref, finb_ref,
                     h_scr, loc_scr, car_scr):
    c = pl.program_id(0)
    nh = WIDTH // LANE

    @pl.when(c == 0)
    def _():
        h_scr[:, 1:] = h0_ref[...]

    @pl.when(st.ctx_first(c))
    def _():
        h_scr[:, 0] = jnp.zeros((nh, 2, 1, LANE), F32)

    rr = LRU_ROWS
    ng = af_ref.shape[2] // rr
    chains = [(p, s, d) for s in range(st.n_seg) for d in range(2) for p in range(nh)]
    refs = ((af_ref, bf_ref, hf_ref), (ab_ref, bb_ref, hb_ref))
    order = (list(range(rr)), list(range(rr - 1, -1, -1)))

    for k, (p, s, d) in enumerate(chains):
        a_ref, b_ref, _ = refs[d]
        hloc = ploc = None
        for r in order[d]:
            a = a_ref[p, s, pl.ds(r, ng, stride=rr), :]
            b = b_ref[p, s, pl.ds(r, ng, stride=rr), :]
            hloc = b if hloc is None else a * hloc + b
            ploc = a if ploc is None else a * ploc
            loc_scr[k, 0, r] = hloc
            loc_scr[k, 1, r] = ploc

    def carry_step(j, carries):
        out = []
        for k, (p, s, d) in enumerate(chains):
            g = j if d == 0 else ng - 1 - j
            r_end = order[d][-1]
            car_scr[k, pl.ds(g, 1), :] = carries[k]
            out.append(loc_scr[k, 1, r_end, pl.ds(g, 1), :] * carries[k] + loc_scr[k, 0, r_end, pl.ds(g, 1), :])
        return tuple(out)

    carries = lax.fori_loop(0, ng, carry_step, tuple(h_scr[p, s, d] for p, s, d in chains))

    for k, (p, s, d) in enumerate(chains):
        h_scr[p, s, d] = carries[k]
        o_ref = refs[d][2]
        car = car_scr[k]
        for r in range(rr):
            o_ref[p, s, pl.ds(r, ng, stride=rr), :] = loc_scr[k, 0, r] + loc_scr[k, 1, r] * car

    @pl.when(st.ctx_last(c))
    def _():
        for p in range(nh):
            finf_ref[0, :, p * LANE:(p + 1) * LANE] = h_scr[p, 0, 0]
            finb_ref[0, :, p * LANE:(p + 1) * LANE] = h_scr[p, 0, 1]


def _lru_scan(coef, h0_dec, st):
    nh = WIDTH // LANE
    a_f, b_f, a_b, b_b = coef
    n_dec = h0_dec.shape[0]
    h0 = jnp.transpose(h0_dec.reshape(n_dec, 2, nh, 1, LANE), (2, 0, 1, 3, 4))
    n_chain = 2 * st.n_seg * nh
    ng = st.chunk // LRU_ROWS
    n = st.n
    fwd = pl.BlockSpec((nh, st.n_seg, st.chunk, LANE), lambda c: (0, 0, c, 0))
    bwd = pl.BlockSpec((nh, st.n_seg, st.chunk, LANE), lambda c: (0, 0, n - 1 - c, 0))
    h_f, h_b, fin_f, fin_b = pl.pallas_call(
        functools.partial(_lru_scan_kernel, st),
        grid=(st.n,),
        in_specs=[fwd, fwd, bwd, bwd, pl.BlockSpec(h0.shape, lambda c: (0,) * h0.ndim)],
        out_specs=[fwd, bwd, st.fin_fwd((1, WIDTH)), st.fin_bwd((1, WIDTH))],
        out_shape=[jax.ShapeDtypeStruct(a_f.shape, F32), jax.ShapeDtypeStruct(a_f.shape, F32),
                   jax.ShapeDtypeStruct((st.n_ctx, 1, WIDTH), F32), jax.ShapeDtypeStruct((st.n_ctx, 1, WIDTH), F32)],
        scratch_shapes=[pltpu.VMEM((nh, st.n_seg, 2, 1, LANE), F32),
                        pltpu.VMEM((n_chain, 2, LRU_ROWS, ng, LANE), F32),
                        pltpu.VMEM((n_chain, ng, LANE), F32)],
        compiler_params=_cp(("arbitrary",)),
        name='rglru_scan',
    )(a_f, b_f, a_b, b_b, h0)
    return h_f, h_b, jnp.concatenate([fin_f, fin_b], axis=1)


def _rope_slab(x, cos, sin):
    lane = lax.broadcasted_iota(jnp.int32, x.shape, 1)
    rot = jnp.where(lane % (B_ROPE // 2) < B_ROPE // 4, -pltpu.roll(x, LANE - B_ROPE // 4, 1),
                    pltpu.roll(x, B_ROPE // 4, 1))
    return x * cos + rot * sin


def _kv_up(ckv_bf, kpe_slab, wuk_ref, wuv_ref, k_o, v_o):
    kn = _dot(ckv_bf, wuk_ref[...])
    for h in range(N_HEADS):
        k_o[:, h * SLAB:(h + 1) * SLAB] = (kn[:, h * SLAB:(h + 1) * SLAB] + kpe_slab).astype(BF16)
    lane = lax.broadcasted_iota(jnp.int32, (1, N_HEADS * SLAB), 1)
    v_o[...] = (_dot(ckv_bf, wuv_ref[...]) + jnp.where(lane % SLAB >= B_VDIM, 1.0, 0.0)).astype(BF16)


def _mla_prep_kernel(zb_ref, cos_ref, sin_ref, qn_ref, kvn_ref, wuq_ref, wuk_ref, wuv_ref,
                     ckv_o, q_o, k_o, v_o):
    cq = zb_ref[:, 0:B_Q_RANK]
    ckv = zb_ref[:, B_Q_RANK:B_Q_RANK + B_KV_RANK]
    kpe = zb_ref[:, B_Q_RANK + B_KV_RANK:ZB]
    cos = cos_ref[...]
    sin = sin_ref[...]
    cqn = cq * lax.rsqrt(jnp.mean(cq * cq, axis=-1, keepdims=True) + RMS_EPS) * qn_ref[...]
    ckvn = ckv * lax.rsqrt(jnp.mean(ckv * ckv, axis=-1, keepdims=True) + RMS_EPS) * kvn_ref[...]
    ckv_o[...] = ckvn
    q = _dot(cqn.astype(BF16), wuq_ref[...])
    for h in range(N_HEADS):
        q_o[:, h * SLAB:(h + 1) * SLAB] = _rope_slab(q[:, h * SLAB:(h + 1) * SLAB], cos, sin).astype(BF16)
    _kv_up(ckvn.astype(BF16), _rope_slab(kpe, cos, sin), wuk_ref, wuv_ref, k_o, v_o)


def _mla_prep(zb, cos, sin, p, tl):
    args = (p['qn_g'], p['kvn_g'], p['wuq'], p['wuk'], p['wuv'])
    slabs = N_HEADS * SLAB
    return pl.pallas_call(
        _mla_prep_kernel,
        grid=tl.grid,
        in_specs=[tl.row(ZB), tl.row(LANE), tl.row(LANE)] + [_full(a) for a in args],
        out_specs=[tl.row(B_KV_RANK), tl.row(slabs), tl.row(slabs), tl.row(slabs)],
        out_shape=[tl.shape(B_KV_RANK), tl.shape(slabs, BF16), tl.shape(slabs, BF16), tl.shape(slabs, BF16)],
        compiler_params=_cp(("parallel",)),
        name='mla_prep',
    )(zb, cos, sin, *args)


def _mla_cache_kernel(ckv_ref, kpe_ref, wuk_ref, wuv_ref, k_o, v_o):
    _kv_up(ckv_ref[...].astype(BF16), kpe_ref[...], wuk_ref, wuv_ref, k_o, v_o)


def _mla_cache(ckv, kpe_slab, p):
    rows = ckv.shape[0]
    return pl.pallas_call(
        _mla_cache_kernel,
        out_shape=[jax.ShapeDtypeStruct((rows, N_HEADS * SLAB), BF16)] * 2,
        compiler_params=pltpu.CompilerParams(vmem_limit_bytes=VMEM_LIMIT),
        name='mla_cache',
    )(ckv, kpe_slab, p['wuk'], p['wuv'])


def _attend_heads(q_ref, kv_refs, o_ref):
    log2e_scale = float((B_NOPE + B_ROPE) ** -0.5 * np.log2(np.e))
    tq = q_ref.shape[0]
    low = lax.broadcasted_iota(jnp.int32, (tq, SLAB), 1) < B_VDIM

    def head(h):
        qh = q_ref[:, h * SLAB:(h + 1) * SLAB]
        ss = [_dot_nt(qh, k_ref[:, h * SLAB:(h + 1) * SLAB]) for k_ref, _ in kv_refs]
        yield
        m = ss[0].max(axis=-1, keepdims=True)
        for s in ss[1:]:
            m = jnp.maximum(m, s.max(axis=-1, keepdims=True))
        es = [jnp.exp2(((s - m) * log2e_scale).astype(BF16)) for s in ss]
        yield
        acc = jnp.zeros((tq, SLAB), F32)
        for e, (_, v_ref) in zip(es, kv_refs):
            acc += _dot(e, v_ref[:, h * SLAB:(h + 1) * SLAB])
        return acc / acc[:, B_VDIM:B_VDIM + 1]

    pairs = []
    for h in range(0, N_HEADS, 2):
        h0, h1 = _interleave([head(h), head(h + 1)])
        pairs.append(jnp.where(low, h0, pltpu.roll(h1, B_VDIM, 1)))
    o_ref[...] = jnp.concatenate(pairs, axis=-1)


def _attn_ctx_kernel(q_ref, k_ref, v_ref, o_ref):
    _attend_heads(q_ref, [(k_ref, v_ref)], o_ref)


def _attn_dec_kernel(q_ref, k_ref, v_ref, kc_ref, vc_ref, o_ref):
    _attend_heads(q_ref, [(k_ref, v_ref), (kc_ref, vc_ref)], o_ref)


def _attn_ctx(q, k, v, n_seq, seq_len):
    blk = lambda w: pl.BlockSpec((None, seq_len, w), lambda b: (0, b, 0))
    slabs = N_HEADS * SLAB
    return pl.pallas_call(
        _attn_ctx_kernel,
        grid=(n_seq,),
        in_specs=[blk(slabs), blk(slabs), blk(slabs)],
        out_specs=blk(WIDTH),
        out_shape=jax.ShapeDtypeStruct(q.shape[:2] + (WIDTH,), F32),
        compiler_params=_cp(("parallel",)),
        name='attn_ctx',
    )(q, k, v)


def _attn_dec_aliased_kernel(q_ref, k_ref, v_ref, kc_ref, vc_ref, y_hbm, o_ref):
    del y_hbm
    _attn_dec_kernel(q_ref, k_ref, v_ref, kc_ref, vc_ref, o_ref)


def _attn_dec(q, k, v, kc, vc, yb, n_dec, dec_len, past, tq):
    nq = dec_len // tq
    slabs = N_HEADS * SLAB
    return pl.pallas_call(
        _attn_dec_aliased_kernel,
        grid=(n_dec, nq),
        in_specs=[pl.BlockSpec((None, tq, slabs), lambda b, i: (1 + b, i, 0)),
                  pl.BlockSpec((None, dec_len, slabs), lambda b, i: (1 + b, 0, 0)),
                  pl.BlockSpec((None, dec_len, slabs), lambda b, i: (1 + b, 0, 0)),
                  pl.BlockSpec((past, slabs), lambda b, i: (b, 0)),
                  pl.BlockSpec((past, slabs), lambda b, i: (b, 0)),
                  pl.BlockSpec(memory_space=pl.ANY)],
        out_specs=pl.BlockSpec((None, tq, WIDTH), lambda b, i: (1 + b, i, 0)),
        out_shape=jax.ShapeDtypeStruct(yb.shape, F32),
        input_output_aliases={5: 0},
        compiler_params=_cp(("parallel", "arbitrary")),
        name='attn_dec',
    )(q, k, v, kc, vc, yb)


def _rope_tables(n_ctx_rows, n_dec, dec_len):
    rows = dec_len // GRID_W
    row = np.repeat(np.arange(rows, dtype=np.float32), GRID_W)
    col = np.tile(np.arange(GRID_W, dtype=np.float32), rows)
    half = B_ROPE // 2
    inv = jnp.power(ROPE_BASE, -jnp.arange(0, half, 2, dtype=F32) / half)
    ang_r = jnp.asarray(row)[:, None] * inv
    ang_c = jnp.asarray(col)[:, None] * inv
    ang = jnp.concatenate([ang_r, ang_r, ang_c, ang_c], axis=-1)
    cos = jnp.pad(jnp.cos(ang), ((0, 0), (0, LANE - B_ROPE)), constant_values=1.0)
    sin = jnp.pad(jnp.sin(ang), ((0, 0), (0, LANE - B_ROPE)))
    cos = jnp.concatenate([jnp.ones((n_ctx_rows, LANE), F32)] + [cos] * n_dec, axis=0)
    sin = jnp.concatenate([jnp.zeros((n_ctx_rows, LANE), F32)] + [sin] * n_dec, axis=0)
    return cos, sin


def _block_diag(w):
    g, n, m = w.shape[-3:]
    eye = jnp.eye(g, dtype=w.dtype)
    out = w[..., :, :, None, :] * eye[:, None, :, None]
    return out.reshape(w.shape[:-3] + (g * n, g * m))


def _head_blocks(s):
    lead = s.shape[:-2]
    s = s.reshape(lead + (N_HEADS, HEAD_DIM, N_HEADS, HEAD_DIM))
    return jnp.stack([s[..., h, :, h, :] for h in range(N_HEADS)], axis=-3)


def kernel(x_prompt, x_sample, cache_mla_ckv, cache_mla_kpe, state_rwkv, state_rglru, state_hgrn, c, c_ctx,
           w_ada, b_ada, ln_g, ln_b, w_ffn_in, w_ffn_out, w_in, w_out,
           rwkv_w0, rwkv_w2, rwkv_a0, rwkv_a2, rwkv_g2, rwkv_kk, rwkv_ka, rwkv_rk, rwkv_gn_g, rwkv_gn_b,
           mla_qn_g, mla_w_uq, mla_kvn_g, mla_w_ukv,
           rglru_conv_w, rglru_conv_b, rglru_wa, rglru_ba, rglru_wx, rglru_bx, rglru_lam,
           hgrn_lb, hgrn_gn_g):
    n_ctx, ctx_len, _ = x_prompt.shape
    n_dec, dec_len, _ = x_sample.shape
    past = cache_mla_ckv.shape[2]
    seg = n_ctx * ctx_len
    assert seg == dec_len, "context rows must form one segment of the decode sequence length"
    assert ctx_len & (ctx_len - 1) == 0 and dec_len & (dec_len - 1) == 0, "sequence lengths must be powers of two"
    n_seg = 1 + n_dec
    tl = _Tiles(n_seg, seg, min(TM, seg))
    tq = min(TQ, dec_len)
    assert ctx_len % LRU_CHUNK == 0
    st = _Streams(n_seg, n_ctx, seg, ctx_len, CHUNK)
    lru_st = _Streams(n_seg, n_ctx, seg, ctx_len, LRU_CHUNK)

    hc = {k: jnp.asarray(v) for k, v in _head_consts().items()}
    ebd = hc['mbd']
    cos, sin = (t.reshape(n_seg, seg, LANE) for t in _rope_tables(seg, n_dec, dec_len))

    cond8 = jnp.zeros((8, D_MODEL), F32).at[0].set(c_ctx).at[1:1 + n_dec].set(c)
    mods = _ada(cond8, w_ada, b_ada).reshape(DEPTH, 8, N_MOD, D_MODEL)
    mods = jnp.transpose(mods, (0, 2, 1, 3))[:, :, :n_seg, None, :]

    w_ffn_in_bf = w_ffn_in.astype(BF16)
    w_ffn_out_bf = w_ffn_out.astype(BF16)
    w_out_bf = w_out.astype(BF16)
    kpe_end = ZA + B_Q_RANK + B_KV_RANK + B_ROPE
    w_in_p = jnp.concatenate([w_in[:, :, :kpe_end], jnp.zeros((DEPTH, D_MODEL, LANE - B_ROPE), F32),
                              w_in[:, :, kpe_end:]], axis=-1).astype(BF16)
    wuq = mla_w_uq.reshape(DEPTH, B_Q_RANK, N_HEADS, B_NOPE + B_ROPE)
    wuq_p = jnp.concatenate([wuq[..., B_NOPE:], wuq[..., :B_NOPE],
                             jnp.zeros((DEPTH, B_Q_RANK, N_HEADS, SLAB - B_NOPE - B_ROPE), F32)], axis=-1)
    wuq_p = wuq_p.reshape(DEPTH, B_Q_RANK, N_HEADS * SLAB).astype(BF16)
    wukv = mla_w_ukv.reshape(DEPTH, B_KV_RANK, N_HEADS, B_NOPE + B_VDIM)
    wuk_p = jnp.concatenate([jnp.zeros((DEPTH, B_KV_RANK, N_HEADS, B_ROPE), F32), wukv[..., :B_NOPE],
                             jnp.zeros((DEPTH, B_KV_RANK, N_HEADS, SLAB - B_NOPE - B_ROPE), F32)], axis=-1)
    wuk_p = wuk_p.reshape(DEPTH, B_KV_RANK, N_HEADS * SLAB).astype(BF16)
    wuv_p = jnp.concatenate([wukv[..., B_NOPE:], jnp.zeros((DEPTH, B_KV_RANK, N_HEADS, SLAB - B_VDIM), F32)], axis=-1)
    wuv_p = wuv_p.reshape(DEPTH, B_KV_RANK, N_HEADS * SLAB).astype(BF16)
    lru_wg = jnp.concatenate([_block_diag(rglru_wa[:, 0]), _block_diag(rglru_wx[:, 0]),
                              _block_diag(rglru_wa[:, 1]), _block_diag(rglru_wx[:, 1])], axis=-1)
    lru_bg = jnp.concatenate([rglru_ba[:, 0], rglru_bx[:, 0], rglru_ba[:, 1], rglru_bx[:, 1]], axis=-1)[:, None, :]
    lbs = _lower_bounds(hgrn_lb)

    rwkv_s0 = _block_diag(state_rwkv)
    hgrn_s0 = _block_diag(jnp.swapaxes(state_hgrn, -1, -2))
    cache_kpe_slab = jnp.pad(cache_mla_kpe, ((0, 0), (0, 0), (0, 0), (0, LANE - B_ROPE)))

    x = jnp.concatenate([x_prompt.reshape(1, seg, D_MODEL), x_sample], axis=0)
    new_ckv, new_kpe, new_rwkv, new_lru, new_hgrn = [], [], [], [], []
    for l in range(DEPTH):
        m = mods[l]
        lng = ln_g[l][:, None, :]
        lnb = ln_b[l][:, None, :]
        x = _ffn(x, m[0], m[1], m[2], lng[0], lnb[0], w_ffn_in_bf, w_ffn_out_bf, l, 0, tl)
        za, zb, zc, zd = _inproj(x, m[3], m[4], w_in_p, l, tl)

        pa = {'w0': rwkv_w0[l][:, None, :], 'w2': rwkv_w2[l], 'a0': rwkv_a0[l][:, None, :], 'a2': rwkv_a2[l],
              'g2': rwkv_g2[l], 'kk': rwkv_kk[l][None, :], 'ka': rwkv_ka[l][None, :], 'rk': rwkv_rk[l][None, :]}
        pre = _rwkv_pre(za, pa, ebd, tl)
        (oa_f, oa_b, sa_fin), (od_f, od_b, sd_fin) = _mix_scan(za, pre[:7], zd, lbs[l], rwkv_s0[:, l],
                                                                hgrn_s0[:, l], hc, st)
        new_rwkv.append(_head_blocks(sa_fin))
        new_hgrn.append(jnp.swapaxes(_head_blocks(sd_fin), -1, -2))

        pb = {'qn_g': mla_qn_g[l][None, :], 'kvn_g': mla_kvn_g[l][None, :], 'wuq': wuq_p[l], 'wuk': wuk_p[l],
              'wuv': wuv_p[l]}
        ckvn, q_all, k_all, v_all = _mla_prep(zb, cos, sin, pb, tl)
        kc, vc = _mla_cache(cache_mla_ckv[:, l].reshape(n_dec * past, B_KV_RANK),
                            cache_kpe_slab[:, l].reshape(n_dec * past, LANE), pb)
        yb = _attn_ctx(q_all, k_all, v_all, n_ctx, ctx_len)
        yb = _attn_dec(q_all, k_all, v_all, kc, vc, yb, n_dec, dec_len, past, tq)
        new_ckv.append(ckvn[0].reshape(n_ctx, ctx_len, B_KV_RANK))
        new_kpe.append(zb[0, :, B_Q_RANK + B_KV_RANK:B_Q_RANK + B_KV_RANK + B_ROPE].reshape(n_ctx, ctx_len, B_ROPE))

        pc = {'conv_w': rglru_conv_w[l], 'conv_b': rglru_conv_b[l][None, :], 'wg': lru_wg[l], 'bg': lru_bg[l],
              'lam': rglru_lam[l][:, None, :]}
        coef = _lru_pre(zc, pc, ctx_len, dec_len, tl)
        h_f, h_b, h_fin = _lru_scan(coef, state_rglru[:, l], lru_st)
        new_lru.append(h_fin)

        x = _outproj(x, oa_f, oa_b, pre[7], pre[8], yb, h_f, h_b, zc, od_f, od_b, zd,
                     rwkv_gn_g[l][None, :], rwkv_gn_b[l][None, :], hgrn_gn_g[l][None, :], ebd,
                     w_out_bf, m[5], lng[1], lnb[1], l, tl)
        x = _ffn(x, m[6], m[7], m[8], lng[2], lnb[2], w_ffn_in_bf, w_ffn_out_bf, l, 1, tl)

    y_prompt = x[0].reshape(n_ctx, ctx_len, D_MODEL)
    y_sample = x[1:]
    return (y_prompt, y_sample, jnp.stack(new_ckv, axis=1), jnp.stack(new_kpe, axis=1),
            jnp.stack(new_rwkv, axis=1), jnp.stack(new_lru, axis=1), jnp.stack(new_hgrn, axis=1))
```

```python
import functools

import numpy as np
import jax
import jax.numpy as jnp
from jax import lax
from jax.experimental import pallas as pl
from jax.experimental.pallas import tpu as pltpu

D_MODEL = 1024
DEPTH = 4
GRID_W = 64
HEAD_DIM = 64
N_HEADS = 4
WIDTH = N_HEADS * HEAD_DIM
A_DECAY_RANK = 64
A_ICLR_RANK = 64
A_GATE_RANK = 128
B_NOPE = 64
B_ROPE = 32
B_VDIM = 64
B_Q_RANK = 256
B_KV_RANK = 128
C_CONV = 4
C_POW = 8.0
D_FF = 2816
N_MOD = 9
ROPE_BASE = 10000.0
LN_EPS = 1e-5
RMS_EPS = 1e-6
RWKV_GN_EPS = 64e-5
DN_ALPHA = (2 * DEPTH) ** 0.25

LANE = 128
SLAB = 128
CHUNK = 64
SUB = 16
LRU_CHUNK = 256
LRU_ROWS = 8
TM = 512
TF = 256
TQ = 256
VMEM_LIMIT = 56 * 1024 * 1024

F32 = jnp.float32
BF16 = jnp.bfloat16
HI = lax.Precision.HIGHEST
NEG = -1e30
LOG2E = float(np.log2(np.e))
RWKV_HI = False


def _cp(sem):
    return pltpu.CompilerParams(dimension_semantics=sem, vmem_limit_bytes=VMEM_LIMIT)


def _dot(a, b, hi=False):
    return jnp.dot(a, b, preferred_element_type=F32, precision=HI if hi else None)


def _dot_nt(a, b, hi=False):
    return lax.dot_general(a, b, (((1,), (1,)), ((), ())), preferred_element_type=F32,
                           precision=HI if hi else None)


def _dot_tn(a, b, hi=False):
    return lax.dot_general(a, b, (((0,), (0,)), ((), ())), preferred_element_type=F32,
                           precision=HI if hi else None)


def _head_sums(x, ebd_bf):
    hi = x.astype(BF16)
    lo = (x - hi.astype(F32)).astype(BF16)
    return _dot(hi, ebd_bf) + _dot(lo, ebd_bf)


def _sigmoid(x):
    return 1.0 / (1.0 + jnp.exp(-x))


def _silu(x):
    return x * _sigmoid(x)


def _softplus(x):
    return jnp.maximum(x, 0.0) + jnp.log(1.0 + jnp.exp(-jnp.abs(x)))


def _tile4(y):
    return jnp.concatenate([y, y, y, y], axis=0)


def _head_consts():
    r = np.arange(WIDTH)
    same = (r[:, None] // HEAD_DIM) == (r[None, :] // HEAD_DIM)
    t = np.arange(CHUNK)[:, None]
    j = (np.arange(WIDTH) % CHUNK)[None, :]
    tt = np.arange(CHUNK)
    return {
        'mbd': same.astype(np.float32),
        'low_s': (j < t).astype(np.float32), 'low_i': (j <= t).astype(np.float32),
        'up_s': (j > t).astype(np.float32), 'up_i': (j >= t).astype(np.float32),
        'eye': (j == t).astype(np.float32),
        'tri_f': (tt[None, :] <= tt[:, None]).astype(np.float32),
        'tri_b': (tt[None, :] >= tt[:, None]).astype(np.float32),
    }


def _lb_kernel(x_ref, o_ref):
    x = x_ref[...]
    m = jnp.max(x, axis=0, keepdims=True)
    e = jnp.exp(x - m)
    sm = e / jnp.sum(e, axis=0, keepdims=True)
    run = sm[0:1]
    rows = [run - sm[0:1]]
    for l in range(1, DEPTH):
        run = run + sm[l:l + 1]
        rows.append(run - sm[0:1])
    o_ref[...] = jnp.concatenate(rows, axis=0)


def _lower_bounds(hgrn_lb):
    flat = hgrn_lb.reshape(DEPTH, 2 * WIDTH)
    out = pl.pallas_call(_lb_kernel, out_shape=jax.ShapeDtypeStruct(flat.shape, F32), name='hgrn_lb')(flat)
    return out.reshape(DEPTH, 2, 1, WIDTH)


def _ada_kernel(c_ref, w_ref, b_ref, o_ref):
    h = _silu(c_ref[...]).astype(BF16)
    o_ref[0] = _dot(h, w_ref[0].astype(BF16)) + b_ref[0]


def _ada(cond8, w_ada, b_ada):
    n = N_MOD * D_MODEL
    tn = 2304
    return pl.pallas_call(
        _ada_kernel,
        grid=(DEPTH, n // tn),
        in_specs=[pl.BlockSpec((8, D_MODEL), lambda l, j: (0, 0)),
                  pl.BlockSpec((1, D_MODEL, tn), lambda l, j: (l, 0, j)),
                  pl.BlockSpec((1, 1, tn), lambda l, j: (l, 0, j))],
        out_specs=pl.BlockSpec((1, 8, tn), lambda l, j: (l, 0, j)),
        out_shape=jax.ShapeDtypeStruct((DEPTH, 8, n), F32),
        compiler_params=_cp(("arbitrary", "arbitrary")),
        name='ada',
    )(cond8, w_ada, b_ada.reshape(DEPTH, 1, n))


def _layer_norm_rows(y, g, b):
    mu = jnp.mean(y, axis=-1, keepdims=True)
    d = y - mu
    var = jnp.mean(d * d, axis=-1, keepdims=True)
    return d * lax.rsqrt(var + LN_EPS) * g + b


def _ffn_kernel(x_ref, sh_ref, sc_ref, g_ref, lng_ref, lnb_ref, win_ref, wout_ref, o_ref):
    x = x_ref[...]
    h = (x * (1.0 + sc_ref[0]) + sh_ref[0]).astype(BF16)
    n_chunks = D_FF // TF

    def gate_up(c):
        return (_dot(h, win_ref[:, c * TF:(c + 1) * TF]),
                _dot(h, win_ref[:, D_FF + c * TF:D_FF + (c + 1) * TF]))

    y = jnp.zeros(x.shape, F32)
    cur = gate_up(0)
    for c in range(n_chunks):
        nxt = gate_up(c + 1) if c + 1 < n_chunks else None
        a = (_silu(cur[0]) * cur[1]).astype(BF16)
        y = y + _dot(a, wout_ref[c * TF:(c + 1) * TF, :])
        cur = nxt
    o_ref[...] = _layer_norm_rows(DN_ALPHA * x + 0.5 * g_ref[0] * y, lng_ref[...], lnb_ref[...])


class _Tiles:
    def __init__(self, n_seg, seg_rows, tm):
        self.n_seg, self.seg_rows, self.tm = n_seg, seg_rows, tm
        self.per_seg = seg_rows // tm
        self.grid = (n_seg * self.per_seg,)

    def row(self, width, col=0):
        ps = self.per_seg
        return pl.BlockSpec((None, self.tm, width), lambda i: (i // ps, i % ps, col))

    def mod(self):
        ps = self.per_seg
        return pl.BlockSpec((1, 1, D_MODEL), lambda i: (i // ps, 0, 0))

    def shape(self, width, dtype=F32):
        return jax.ShapeDtypeStruct((self.n_seg, self.seg_rows, width), dtype)


def _full(a):
    return pl.BlockSpec(a.shape, lambda i: (0,) * a.ndim)


def _ffn_into_kernel(x_ref, sh_ref, sc_ref, g_ref, lng_ref, lnb_ref, win_ref, wout_ref, buf_hbm, o_ref):
    del buf_hbm
    _ffn_kernel(x_ref, sh_ref, sc_ref, g_ref, lng_ref, lnb_ref, win_ref, wout_ref, o_ref)


def _ffn(x, sh, sc, g, lng, lnb, w_in_bf, w_out_bf, l, f, tl, segs=None, x_off=0, out_off=0, out_nseg=None,
         into=None):
    first, count = segs if segs is not None else (0, tl.n_seg)
    out_nseg = tl.n_seg if out_nseg is None else out_nseg
    ps, tm = tl.per_seg, tl.tm
    mod = pl.BlockSpec((1, 1, D_MODEL), lambda i: (first + i // ps, 0, 0))
    vec = pl.BlockSpec((1, D_MODEL), lambda i: (0, 0))
    in_specs = [pl.BlockSpec((None, tm, D_MODEL), lambda i: (x_off + i // ps, i % ps, 0)), mod, mod, mod, vec, vec,
                pl.BlockSpec((None, None, D_MODEL, 2 * D_FF), lambda i: (l, f, 0, 0)),
                pl.BlockSpec((None, None, D_FF, D_MODEL), lambda i: (l, f, 0, 0))]
    args = [x, sh, sc, g, lng, lnb, w_in_bf, w_out_bf]
    aliases = {}
    if into is not None:
        in_specs.append(pl.BlockSpec(memory_space=pl.ANY))
        args.append(into)
        aliases = {len(args) - 1: 0}
    return pl.pallas_call(
        _ffn_kernel if into is None else _ffn_into_kernel,
        grid=(count * ps,),
        in_specs=in_specs,
        out_specs=pl.BlockSpec((None, tm, D_MODEL), lambda i: (out_off + i // ps, i % ps, 0)),
        out_shape=jax.ShapeDtypeStruct((out_nseg, tl.seg_rows, D_MODEL), F32),
        input_output_aliases=aliases,
        compiler_params=_cp(("parallel",)),
        name='ffn',
    )(*args)


ZA = 3 * WIDTH + 2 * A_DECAY_RANK + 2 * A_ICLR_RANK + A_GATE_RANK
ZB = B_Q_RANK + B_KV_RANK + LANE
ZC = 2 * WIDTH
ZD = 5 * WIDTH
ZTOT = ZA + ZB + ZC + ZD


def _inproj_kernel(x_ref, sh_ref, sc_ref, w_ref, za_ref, zb_ref, zc_ref, zd_ref):
    h = (x_ref[...] * (1.0 + sc_ref[0]) + sh_ref[0]).astype(BF16)
    za_ref[...] = _dot(h, w_ref[:, 0:ZA])
    zb_ref[...] = _dot(h, w_ref[:, ZA:ZA + ZB])
    zc_ref[...] = _dot(h, w_ref[:, ZA + ZB:ZA + ZB + ZC])
    zd_ref[...] = _dot(h, w_ref[:, ZA + ZB + ZC:ZTOT])


def _inproj(x, sh, sc, w_in_p, l, tl):
    mod = tl.mod()
    return pl.pallas_call(
        _inproj_kernel,
        grid=tl.grid,
        in_specs=[tl.row(D_MODEL), mod, mod, pl.BlockSpec((None, D_MODEL, ZTOT), lambda i: (l, 0, 0))],
        out_specs=[tl.row(ZA), tl.row(ZB), tl.row(ZC), tl.row(ZD)],
        out_shape=[tl.shape(w) for w in (ZA, ZB, ZC, ZD)],
        compiler_params=_cp(("parallel",)),
        name='mixer_in',
    )(x, sh, sc, w_in_p)


def _gelu_tanh(x):
    return 0.5 * x * (1.0 + jnp.tanh(np.sqrt(2.0 / np.pi) * (x + 0.044715 * (x * x * x))))


def _outproj_kernel(x_ref, af_ref, ab_ref, bon_ref, gate_ref, yb_ref, hf_ref, hb_ref, cg_ref, df_ref, db_ref,
                    dg_ref, agn_g_ref, agn_b_ref, dgn_g_ref, ebd_ref, w_ref, g_ref, lng_ref, lnb_ref, o_ref):
    avg = (ebd_ref[...] * (1.0 / HEAD_DIM)).astype(BF16)
    oa = af_ref[...] + ab_ref[...]
    da = oa - _head_sums(oa, avg)
    gn = da * lax.rsqrt(_head_sums(da * da, avg) + RWKV_GN_EPS) * agn_g_ref[...] + agn_b_ref[...]
    ya = (gn + bon_ref[...]) * gate_ref[...]
    h = jnp.concatenate([hf_ref[p] + hb_ref[p] for p in range(WIDTH // LANE)], axis=-1)
    yc = h * _gelu_tanh(cg_ref[...])
    od = df_ref[...] + db_ref[...]
    yd = od * lax.rsqrt(_head_sums(od * od, avg) + RMS_EPS) * dgn_g_ref[...] * _silu(dg_ref[...])
    y = _dot(ya.astype(BF16), w_ref[0:WIDTH, :])
    y += _dot(yb_ref[...].astype(BF16), w_ref[WIDTH:2 * WIDTH, :])
    y += _dot(yc.astype(BF16), w_ref[2 * WIDTH:3 * WIDTH, :])
    y += _dot(yd.astype(BF16), w_ref[3 * WIDTH:4 * WIDTH, :])
    o_ref[...] = _layer_norm_rows(DN_ALPHA * x_ref[...] + g_ref[0] * y, lng_ref[...], lnb_ref[...])


def _outproj(x, oa_f, oa_b, bonus, gate, yb, hf, hb, zc, od_f, od_b, zd, agn_g, agn_b, dgn_g, ebd,
             w_out_bf, g, lng, lnb, l, tl):
    vec = pl.BlockSpec((1, D_MODEL), lambda i: (0, 0))
    wvec = pl.BlockSpec((1, WIDTH), lambda i: (0, 0))
    ps = tl.per_seg
    halves = pl.BlockSpec((WIDTH // LANE, None, tl.tm, LANE), lambda i: (0, i // ps, i % ps, 0))
    w = WIDTH
    return pl.pallas_call(
        _outproj_kernel,
        grid=tl.grid,
        in_specs=[tl.row(D_MODEL), tl.row(w), tl.row(w), tl.row(w), tl.row(w), tl.row(w), halves, halves,
                  tl.row(w, 1), tl.row(w), tl.row(w), tl.row(w, 4), wvec, wvec, wvec, _full(ebd),
                  pl.BlockSpec((None, 4 * WIDTH, D_MODEL), lambda i: (l, 0, 0)), tl.mod(), vec, vec],
        out_specs=tl.row(D_MODEL),
        out_shape=tl.shape(D_MODEL),
        compiler_params=_cp(("parallel",)),
        name='mixer_out',
    )(x, oa_f, oa_b, bonus, gate, yb, hf, hb, zc, od_f, od_b, zd, agn_g, agn_b, dgn_g, ebd, w_out_bf, g, lng, lnb)


class _Streams:
    def __init__(self, n_seg, n_ctx, seg_rows, ctx_len, chunk):
        self.n_seg, self.n_ctx, self.chunk = n_seg, n_ctx, chunk
        self.n = seg_rows // chunk
        self.ctx_n = ctx_len // chunk

    def fwd(self, width, col=0):
        return pl.BlockSpec((self.n_seg, self.chunk, width), lambda c: (0, c, col))

    def bwd(self, width, col=0):
        n = self.n
        return pl.BlockSpec((self.n_seg, self.chunk, width), lambda c: (0, n - 1 - c, col))

    def fin_fwd(self, shape):
        ctx_n = self.ctx_n
        return pl.BlockSpec((1,) + shape, lambda c: (c // ctx_n,) + (0,) * len(shape))

    def fin_bwd(self, shape):
        ctx_n, n_ctx = self.ctx_n, self.n_ctx
        return pl.BlockSpec((1,) + shape, lambda c: (n_ctx - 1 - c // ctx_n,) + (0,) * len(shape))

    def ctx_first(self, c):
        return c % self.ctx_n == 0

    def ctx_last(self, c):
        return c % self.ctx_n == self.ctx_n - 1


def _rwkv_pre_kernel(za_ref, w0_ref, w2_ref, a0_ref, a2_ref, g2_ref, kkp_ref, kap_ref, rkp_ref, ebd_ref,
                     kap_o, lwf_o, lwb_o, kdf_o, kdb_o, bbf_o, bbb_o, bon_o, gate_o):
    r = za_ref[:, 0:WIDTH]
    k = za_ref[:, WIDTH:2 * WIDTH]
    v = za_ref[:, 2 * WIDTH:3 * WIDTH]
    o = 3 * WIDTH
    xw = (za_ref[:, o:o + A_DECAY_RANK], za_ref[:, o + A_DECAY_RANK:o + 2 * A_DECAY_RANK])
    o += 2 * A_DECAY_RANK
    xa = (za_ref[:, o:o + A_ICLR_RANK], za_ref[:, o + A_ICLR_RANK:o + 2 * A_ICLR_RANK])
    o += 2 * A_ICLR_RANK
    xg = za_ref[:, o:o + A_GATE_RANK]
    ebd = ebd_ref[...].astype(BF16)

    kk = k * kkp_ref[...]
    nrm = jnp.sqrt(_head_sums(kk * kk, ebd))
    kappa = kk / jnp.maximum(nrm, 1e-12)
    kap_o[...] = kappa
    k_sum = jnp.zeros_like(r)
    for d, (lw_o, kd_o, bb_o) in enumerate(((lwf_o, kdf_o, bbf_o), (lwb_o, kdb_o, bbb_o))):
        w_log = -_softplus(-(w0_ref[d] + _dot(jnp.tanh(xw[d]).astype(BF16), w2_ref[d].astype(BF16)))) - 0.5
        lw_o[...] = -jnp.exp(w_log)
        a = _sigmoid(a0_ref[d] + _dot(xa[d].astype(BF16), a2_ref[d].astype(BF16)))
        k_d = k * (1.0 + (a - 1.0) * kap_ref[...])
        kd_o[...] = k_d
        bb_o[...] = kappa * a
        k_sum += k_d
    bon_o[...] = _head_sums(r * k_sum * rkp_ref[...], ebd) * v
    gate_o[...] = _dot(_sigmoid(xg).astype(BF16), g2_ref[...].astype(BF16))


def _rwkv_pre(za, p, ebd, tl):
    args = (p['w0'], p['w2'], p['a0'], p['a2'], p['g2'], p['kk'], p['ka'], p['rk'], ebd)
    return pl.pallas_call(
        _rwkv_pre_kernel,
        grid=tl.grid,
        in_specs=[tl.row(ZA)] + [_full(a) for a in args],
        out_specs=[tl.row(WIDTH)] * 9,
        out_shape=[tl.shape(WIDTH)] * 9,
        compiler_params=_cp(("parallel",)),
        name='rwkv_pre',
    )(za, *args)


def _rwkv_chunk(r, v, kap, lw, kd, bb, st, mbd, eye, m_strict, m_incl, tri):
    cum = _dot(tri, lw, hi=True)
    yield
    tot = jnp.sum(lw, axis=0, keepdims=True)
    g_in = jnp.exp(cum)
    g_inv = jnp.exp(-cum)
    g_ex = jnp.exp(cum - lw)
    g_end = jnp.exp(tot - cum)
    cast = (lambda a: a) if RWKV_HI else (lambda a: a.astype(BF16))
    mask = cast(mbd)
    bd = lambda y: _tile4(cast(y)) * mask
    x = cast(jnp.concatenate([kap * g_ex, r * g_in], axis=0))
    ab = _dot_nt(x, bd(bb * g_inv), hi=RWKV_HI)
    ak = _dot_nt(x, bd(kd * g_inv), hi=RWKV_HI)
    xs = _dot_nt(x, cast(st), hi=RWKV_HI)
    yield
    c = r.shape[0]
    strict = m_strict > 0.5
    incl = m_incl > 0.5
    a_ub = jnp.where(strict, ab[:c], 0.0)
    a_rb = jnp.where(incl, ab[c:], 0.0)
    a_uk = jnp.where(strict, ak[:c], 0.0)
    a_rk = jnp.where(incl, ak[c:], 0.0)

    def catmul(pc, q):
        return _dot(cast(pc), bd(q), hi=RWKV_HI)

    xs = xs + catmul(jnp.concatenate([a_uk, a_rk], axis=0), v)
    rhs, o_v = xs[:c], xs[c:]
    xp = -a_ub
    inv = eye + xp
    xp = catmul(xp, xp)
    yield
    for _ in range(int(np.log2(c)) - 2):
        both = catmul(jnp.concatenate([inv, xp], axis=0), xp)
        yield
        inv = inv + both[:c]
        xp = both[c:]
    inv = inv + catmul(inv, xp)
    yield
    u = -catmul(inv, rhs)
    yield
    o = o_v + catmul(a_rb, u)
    upd = _dot_tn(cast(jnp.concatenate([u, v], axis=0)),
                  cast(jnp.concatenate([bb * g_end, kd * g_end], axis=0)), hi=RWKV_HI)
    st_new = st * jnp.exp(tot) + upd * mbd
    return o, st_new


def _interleave(gens):
    results = [None] * len(gens)
    live = list(range(len(gens)))
    while live:
        still = []
        for k in live:
            try:
                next(gens[k])
                still.append(k)
            except StopIteration as done:
                results[k] = done.value
        live = still
    return results


def _init_states(st, c, st_scr, s0_ref):
    @pl.when(c == 0)
    def _():
        st_scr[1:] = s0_ref[...]

    @pl.when(st.ctx_first(c))
    def _():
        st_scr[0] = jnp.zeros(st_scr.shape[1:], F32)


def _hgrn_chunk(xq, xf, xi, lb, st, tri, ebd, mbd, p_scr, reverse):
    c = xq.shape[0]
    nb = c // SUB
    q = _silu(xq)
    gsig = lb + (1.0 - lb) * _sigmoid(xf)
    kk = 1.0 - gsig
    lg = jnp.log(gsig)
    cum = _dot(tri, lg, hi=True)
    yield
    cum = cum * LOG2E
    tot = jnp.sum(lg, axis=0, keepdims=True) * LOG2E
    lane_s = lax.broadcasted_iota(jnp.int32, (SUB, WIDTH), 1) % c
    half = SUB // 2
    row_h = lax.broadcasted_iota(jnp.int32, (half, WIDTH), 0)
    blk = lambda a, i: a[i * SUB:(i + 1) * SUB]
    end_row = (lambda j: j * SUB) if reverse else (lambda j: j * SUB + SUB - 1)
    later = (lambda j: range(0, j)) if reverse else (lambda j: range(j + 1, nb))

    k_end = jnp.concatenate([blk(kk, j) * jnp.exp2(cum[end_row(j):end_row(j) + 1] - blk(cum, j))
                             for j in range(nb)], axis=0)
    q_parts, where_part = [], {}
    for j in range(nb):
        for i in later(j):
            where_part[(i, j)] = len(q_parts)
            q_parts.append(blk(q, i) * jnp.exp2(blk(cum, i) - cum[end_row(j):end_row(j) + 1]))
    cross = _dot_nt(jnp.concatenate(q_parts, axis=0).astype(BF16), (_tile4(k_end) * mbd).astype(BF16))
    o_state = _dot_nt((q * jnp.exp2(cum)).astype(BF16), st.astype(BF16))
    upd = _dot_tn(xi.astype(BF16), (kk * jnp.exp2(tot - cum)).astype(BF16))
    yield

    for i in range(nb):
        cb, qb = blk(cum, i), blk(q, i)
        for sl in range(SUB):
            s = i * SUB + sl
            parts = []
            for lo in (0, half):
                hi = lo + half - 1
                none_valid = lo > sl if reverse else hi < sl
                all_valid = hi <= sl if reverse else lo >= sl
                if none_valid:
                    parts.append(jnp.zeros((half, WIDTH), F32))
                    continue
                d = cb[lo:lo + half] - cum[s:s + 1]
                if not all_valid:
                    valid = (row_h + lo <= sl) if reverse else (row_h + lo >= sl)
                    d = jnp.where(valid, d, NEG)
                parts.append(jnp.exp2(d) * qb[lo:lo + half] * kk[s:s + 1])
            p_scr[s * SUB:(s + 1) * SUB, :] = jnp.concatenate(parts, axis=0).astype(BF16)
    same = _dot(p_scr[...], ebd.astype(BF16))
    yield

    att_rows = []
    for i in range(nb):
        att = jnp.zeros((SUB, WIDTH), F32)
        for sl in range(SUB):
            s = i * SUB + sl
            att = jnp.where(lane_s == s, same[s * SUB:(s + 1) * SUB], att)
        for j in range(nb):
            if (i, j) in where_part:
                n = where_part[(i, j)]
                att = jnp.where(lane_s // SUB == j, cross[n * SUB:(n + 1) * SUB], att)
        att_rows.append(att)
    att = jnp.concatenate(att_rows, axis=0)
    o = o_state + _dot(att.astype(BF16), (_tile4(xi) * mbd).astype(BF16))
    st_new = st * jnp.exp2(tot) + upd * mbd
    return o, st_new


def _mix_scan_kernel(st,
                     rf_ref, vf_ref, kapf_ref, lwf_ref, kdf_ref, bbf_ref,
                     rb_ref, vb_ref, kapb_ref, lwb_ref, kdb_ref, bbb_ref,
                     qf_ref, ff_ref, if_ref, qb_ref, fb_ref, ib_ref, lb_ref,
                     sa0_ref, sd0_ref, mbd_ref, eye_ref, lows_ref, lowi_ref, ups_ref, upi_ref, trif_ref, trib_ref,
                     af_ref, ab_ref, afin_f, afin_b, df_ref, db_ref, dfin_f, dfin_b,
                     sa_scr, sd_scr, p_scr):
    c = pl.program_id(0)
    _init_states(st, c, sa_scr, sa0_ref)
    _init_states(st, c, sd_scr, sd0_ref)
    mbd = mbd_ref[...]
    eye = eye_ref[...]
    gens = []
    for s in range(st.n_seg):
        gens.append(_rwkv_chunk(rf_ref[s], vf_ref[s], kapf_ref[s], lwf_ref[s], kdf_ref[s], bbf_ref[s],
                                sa_scr[s, 0], mbd, eye, lows_ref[...], lowi_ref[...], trif_ref[...]))
        gens.append(_hgrn_chunk(qf_ref[s], ff_ref[s], if_ref[s], lb_ref[0], sd_scr[s, 0], trif_ref[...],
                                mbd, mbd, p_scr.at[s, 0], False))
        gens.append(_rwkv_chunk(rb_ref[s], vb_ref[s], kapb_ref[s], lwb_ref[s], kdb_ref[s], bbb_ref[s],
                                sa_scr[s, 1], mbd, eye, ups_ref[...], upi_ref[...], trib_ref[...]))
        gens.append(_hgrn_chunk(qb_ref[s], fb_ref[s], ib_ref[s], lb_ref[1], sd_scr[s, 1], trib_ref[...],
                                mbd, mbd, p_scr.at[s, 1], True))
    outs = ((af_ref, sa_scr), (df_ref, sd_scr), (ab_ref, sa_scr), (db_ref, sd_scr))
    for k, (o, st_new) in enumerate(_interleave(gens)):
        s, kind = k // 4, k % 4
        o_ref, scr = outs[kind]
        o_ref[s] = o
        scr[s, kind // 2] = st_new

    @pl.when(st.ctx_last(c))
    def _():
        for h in range(N_HEADS):
            hs = slice(h * HEAD_DIM, (h + 1) * HEAD_DIM)
            afin_f[0, h] = sa_scr[0, 0, hs, hs]
            afin_b[0, h] = sa_scr[0, 1, hs, hs]
            dfin_f[0, h] = sd_scr[0, 0, hs, hs].T
            dfin_b[0, h] = sd_scr[0, 1, hs, hs].T


def _mix_scan(za, pre, zd, lb, sa0_dec, sd0_dec, hc, st):
    kap, lwf, lwb, kdf, kdb, bbf, bbb = pre
    consts = (hc['mbd'], hc['eye'], hc['low_s'], hc['low_i'], hc['up_s'], hc['up_i'], hc['tri_f'], hc['tri_b'])
    w = WIDTH
    o_shape = jax.ShapeDtypeStruct(kap.shape, F32)
    head_blocks = (N_HEADS, HEAD_DIM, HEAD_DIM)
    fin_shape = jax.ShapeDtypeStruct((st.n_ctx,) + head_blocks, F32)
    fin_specs = [st.fin_fwd(head_blocks), st.fin_bwd(head_blocks)]
    outs = pl.pallas_call(
        functools.partial(_mix_scan_kernel, st),
        grid=(st.n,),
        in_specs=[st.fwd(w, 0), st.fwd(w, 2), st.fwd(w), st.fwd(w), st.fwd(w), st.fwd(w),
                  st.bwd(w, 0), st.bwd(w, 2), st.bwd(w), st.bwd(w), st.bwd(w), st.bwd(w),
                  st.fwd(w, 0), st.fwd(w, 1), st.fwd(w, 3), st.bwd(w, 0), st.bwd(w, 2), st.bwd(w, 3),
                  _full(lb), _full(sa0_dec), _full(sd0_dec)] + [_full(a) for a in consts],
        out_specs=[st.fwd(w), st.bwd(w)] + fin_specs + [st.fwd(w), st.bwd(w)] + fin_specs,
        out_shape=[o_shape, o_shape, fin_shape, fin_shape] * 2,
        scratch_shapes=[pltpu.VMEM((st.n_seg, 2, w, w), F32), pltpu.VMEM((st.n_seg, 2, w, w), F32),
                        pltpu.VMEM((st.n_seg, 2, CHUNK * SUB, w), BF16)],
        compiler_params=_cp(("arbitrary",)),
        name='mix_scan',
    )(za, za, kap, lwf, kdf, bbf, za, za, kap, lwb, kdb, bbb, zd, zd, zd, zd, zd, zd, lb, sa0_dec, sd0_dec, *consts)
    a_f, a_b, afin_f, afin_b, d_f, d_b, dfin_f, dfin_b = outs
    return ((a_f, a_b, jnp.stack([afin_f, afin_b], axis=1)), (d_f, d_b, jnp.stack([dfin_f, dfin_b], axis=1)))


def _lru_pre_kernel(seq_of_tile, xp_ref, x_ref, xn_ref, cw_ref, cb_ref, wg_ref, bg_ref, lam_ref,
                    af_o, bf_o, ab_o, bb_o, pad_scr):
    i = pl.program_id(0)
    tm = x_ref.shape[0]
    seqlen = seq_of_tile(i)
    pad_scr[0:8, :] = xp_ref[:, 0:WIDTH]
    pad_scr[8:8 + tm, :] = x_ref[:, 0:WIDTH]
    pad_scr[8 + tm:16 + tm, :] = xn_ref[:, 0:WIDTH]
    pos = jnp.bitwise_and(lax.broadcasted_iota(jnp.int32, (tm, WIDTH), 0) + i * tm, seqlen - 1)
    u = jnp.zeros((tm, WIDTH), F32) + cb_ref[...]
    for j in range(C_CONV):
        off = j - C_CONV // 2
        tap = pad_scr[pl.ds(8 + off, tm), :]
        ok = jnp.logical_and(pos + off >= 0, pos + off < seqlen)
        u += jnp.where(ok, tap, 0.0) * cw_ref[j:j + 1, :]
    gates = _sigmoid(_dot(u.astype(BF16), wg_ref[...].astype(BF16)) + bg_ref[...])
    for d, (a_o, b_o) in enumerate(((af_o, bf_o), (ab_o, bb_o))):
        r = gates[:, (2 * d) * WIDTH:(2 * d + 1) * WIDTH]
        ig = gates[:, (2 * d + 1) * WIDTH:(2 * d + 2) * WIDTH]
        log_a = -C_POW * r * _softplus(-lam_ref[d])
        a = jnp.exp(log_a)
        b = jnp.sqrt(-jnp.tanh(log_a) * (a * a + 1.0)) * (ig * u)
        for half in range(WIDTH // LANE):
            a_o[half] = a[:, half * LANE:(half + 1) * LANE]
            b_o[half] = b[:, half * LANE:(half + 1) * LANE]


def _lru_pre(zc, p, ctx_len, dec_len, tl):
    ps, tm = tl.per_seg, tl.tm
    per = tm // 8
    last8 = tl.seg_rows // 8 - 1
    seq_of_tile = lambda i: jnp.where(i < ps, ctx_len, dec_len)
    args = (p['conv_w'], p['conv_b'], p['wg'], p['bg'], p['lam'])
    nh = WIDTH // LANE
    prev8 = pl.BlockSpec((None, 8, WIDTH), lambda i: (i // ps, jnp.maximum((i % ps) * per - 1, 0), 0))
    next8 = pl.BlockSpec((None, 8, WIDTH), lambda i: (i // ps, jnp.minimum((i % ps + 1) * per, last8), 0))
    return pl.pallas_call(
        functools.partial(_lru_pre_kernel, seq_of_tile),
        grid=tl.grid,
        in_specs=[prev8, tl.row(WIDTH), next8] + [_full(a) for a in args],
        out_specs=[pl.BlockSpec((nh, None, tm, LANE), lambda i: (0, i // ps, i % ps, 0))] * 4,
        out_shape=[jax.ShapeDtypeStruct((nh, tl.n_seg, tl.seg_rows, LANE), F32)] * 4,
        scratch_shapes=[pltpu.VMEM((tm + 16, WIDTH), F32)],
        compiler_params=_cp(("parallel",)),
        name='rglru_pre',
    )(zc, zc, zc, *args)


def _lru_scan_kernel(st, af_ref, bf_ref, ab_ref, bb_ref, h0_ref, hf_ref, hb_ref, finf_ref, finb_ref,
                     h_scr, loc_scr, car_scr):
    c = pl.program_id(0)
    nh = WIDTH // LANE

    @pl.when(c == 0)
    def _():
        h_scr[:, 1:] = h0_ref[...]

    @pl.when(st.ctx_first(c))
    def _():
        h_scr[:, 0] = jnp.zeros((nh, 2, 1, LANE), F32)

    rr = LRU_ROWS
    ng = af_ref.shape[2] // rr
    chains = [(p, s, d) for s in range(st.n_seg) for d in range(2) for p in range(nh)]
    refs = ((af_ref, bf_ref, hf_ref), (ab_ref, bb_ref, hb_ref))
    order = (list(range(rr)), list(range(rr - 1, -1, -1)))

    for k, (p, s, d) in enumerate(chains):
        a_ref, b_ref, _ = refs[d]
        hloc = ploc = None
        for r in order[d]:
            a = a_ref[p, s, pl.ds(r, ng, stride=rr), :]
            b = b_ref[p, s, pl.ds(r, ng, stride=rr), :]
            hloc = b if hloc is None else a * hloc + b
            ploc = a if ploc is None else a * ploc
            loc_scr[k, 0, r] = hloc
            loc_scr[k, 1, r] = ploc

    def carry_step(j, carries):
        out = []
        for k, (p, s, d) in enumerate(chains):
            g = j if d == 0 else ng - 1 - j
            r_end = order[d][-1]
            car_scr[k, pl.ds(g, 1), :] = carries[k]
            out.append(loc_scr[k, 1, r_end, pl.ds(g, 1), :] * carries[k] + loc_scr[k, 0, r_end, pl.ds(g, 1), :])
        return tuple(out)

    carries = lax.fori_loop(0, ng, carry_step, tuple(h_scr[p, s, d] for p, s, d in chains))

    for k, (p, s, d) in enumerate(chains):
        h_scr[p, s, d] = carries[k]
        o_ref = refs[d][2]
        car = car_scr[k]
        for r in range(rr):
            o_ref[p, s, pl.ds(r, ng, stride=rr), :] = loc_scr[k, 0, r] + loc_scr[k, 1, r] * car

    @pl.when(st.ctx_last(c))
    def _():
        for p in range(nh):
            finf_ref[0, :, p * LANE:(p + 1) * LANE] = h_scr[p, 0, 0]
            finb_ref[0, :, p * LANE:(p + 1) * LANE] = h_scr[p, 0, 1]


def _lru_scan(coef, h0_dec, st):
    nh = WIDTH // LANE
    a_f, b_f, a_b, b_b = coef
    n_dec = h0_dec.shape[0]
    h0 = jnp.transpose(h0_dec.reshape(n_dec, 2, nh, 1, LANE), (2, 0, 1, 3, 4))
    n_chain = 2 * st.n_seg * nh
    ng = st.chunk // LRU_ROWS
    n = st.n
    fwd = pl.BlockSpec((nh, st.n_seg, st.chunk, LANE), lambda c: (0, 0, c, 0))
    bwd = pl.BlockSpec((nh, st.n_seg, st.chunk, LANE), lambda c: (0, 0, n - 1 - c, 0))
    h_f, h_b, fin_f, fin_b = pl.pallas_call(
        functools.partial(_lru_scan_kernel, st),
        grid=(st.n,),
        in_specs=[fwd, fwd, bwd, bwd, pl.BlockSpec(h0.shape, lambda c: (0,) * h0.ndim)],
        out_specs=[fwd, bwd, st.fin_fwd((1, WIDTH)), st.fin_bwd((1, WIDTH))],
        out_shape=[jax.ShapeDtypeStruct(a_f.shape, F32), jax.ShapeDtypeStruct(a_f.shape, F32),
                   jax.ShapeDtypeStruct((st.n_ctx, 1, WIDTH), F32), jax.ShapeDtypeStruct((st.n_ctx, 1, WIDTH), F32)],
        scratch_shapes=[pltpu.VMEM((nh, st.n_seg, 2, 1, LANE), F32),
                        pltpu.VMEM((n_chain, 2, LRU_ROWS, ng, LANE), F32),
                        pltpu.VMEM((n_chain, ng, LANE), F32)],
        compiler_params=_cp(("arbitrary",)),
        name='rglru_scan',
    )(a_f, b_f, a_b, b_b, h0)
    return h_f, h_b, jnp.concatenate([fin_f, fin_b], axis=1)


def _rope_slab(x, cos, sin):
    lane = lax.broadcasted_iota(jnp.int32, x.shape, 1)
    rot = jnp.where(lane % (B_ROPE // 2) < B_ROPE // 4, -pltpu.roll(x, LANE - B_ROPE // 4, 1),
                    pltpu.roll(x, B_ROPE // 4, 1))
    return x * cos + rot * sin


def _kv_up(ckv_bf, kpe_slab, wuk_ref, wuv_ref, k_o, v_o):
    kn = _dot(ckv_bf, wuk_ref[...])
    for h in range(N_HEADS):
        k_o[:, h * SLAB:(h + 1) * SLAB] = (kn[:, h * SLAB:(h + 1) * SLAB] + kpe_slab).astype(BF16)
    lane = lax.broadcasted_iota(jnp.int32, (1, N_HEADS * SLAB), 1)
    v_o[...] = (_dot(ckv_bf, wuv_ref[...]) + jnp.where(lane % SLAB >= B_VDIM, 1.0, 0.0)).astype(BF16)


def _mla_prep_kernel(zb_ref, cos_ref, sin_ref, qn_ref, kvn_ref, wuq_ref, wuk_ref, wuv_ref,
                     ckv_o, q_o, k_o, v_o):
    cq = zb_ref[:, 0:B_Q_RANK]
    ckv = zb_ref[:, B_Q_RANK:B_Q_RANK + B_KV_RANK]
    kpe = zb_ref[:, B_Q_RANK + B_KV_RANK:ZB]
    cos = cos_ref[...]
    sin = sin_ref[...]
    cqn = cq * lax.rsqrt(jnp.mean(cq * cq, axis=-1, keepdims=True) + RMS_EPS) * qn_ref[...]
    ckvn = ckv * lax.rsqrt(jnp.mean(ckv * ckv, axis=-1, keepdims=True) + RMS_EPS) * kvn_ref[...]
    ckv_o[...] = ckvn
    q = _dot(cqn.astype(BF16), wuq_ref[...])
    for h in range(N_HEADS):
        q_o[:, h * SLAB:(h + 1) * SLAB] = _rope_slab(q[:, h * SLAB:(h + 1) * SLAB], cos, sin).astype(BF16)
    _kv_up(ckvn.astype(BF16), _rope_slab(kpe, cos, sin), wuk_ref, wuv_ref, k_o, v_o)


def _mla_prep(zb, cos, sin, p, tl):
    args = (p['qn_g'], p['kvn_g'], p['wuq'], p['wuk'], p['wuv'])
    slabs = N_HEADS * SLAB
    return pl.pallas_call(
        _mla_prep_kernel,
        grid=tl.grid,
        in_specs=[tl.row(ZB), tl.row(LANE), tl.row(LANE)] + [_full(a) for a in args],
        out_specs=[tl.row(B_KV_RANK), tl.row(slabs), tl.row(slabs), tl.row(slabs)],
        out_shape=[tl.shape(B_KV_RANK), tl.shape(slabs, BF16), tl.shape(slabs, BF16), tl.shape(slabs, BF16)],
        compiler_params=_cp(("parallel",)),
        name='mla_prep',
    )(zb, cos, sin, *args)


def _mla_cache_kernel(ckv_ref, kpe_ref, wuk_ref, wuv_ref, k_o, v_o):
    _kv_up(ckv_ref[...].astype(BF16), kpe_ref[...], wuk_ref, wuv_ref, k_o, v_o)


def _mla_cache(ckv, kpe_slab, p):
    rows = ckv.shape[0]
    return pl.pallas_call(
        _mla_cache_kernel,
        out_shape=[jax.ShapeDtypeStruct((rows, N_HEADS * SLAB), BF16)] * 2,
        compiler_params=pltpu.CompilerParams(vmem_limit_bytes=VMEM_LIMIT),
        name='mla_cache',
    )(ckv, kpe_slab, p['wuk'], p['wuv'])


def _attend_heads(q_ref, kv_refs, o_ref):
    log2e_scale = float((B_NOPE + B_ROPE) ** -0.5 * np.log2(np.e))
    tq = q_ref.shape[0]
    low = lax.broadcasted_iota(jnp.int32, (tq, SLAB), 1) < B_VDIM

    def head(h):
        qh = q_ref[:, h * SLAB:(h + 1) * SLAB]
        ss = [_dot_nt(qh, k_ref[:, h * SLAB:(h + 1) * SLAB]) for k_ref, _ in kv_refs]
        yield
        m = ss[0].max(axis=-1, keepdims=True)
        for s in ss[1:]:
            m = jnp.maximum(m, s.max(axis=-1, keepdims=True))
        es = [jnp.exp2(((s - m) * log2e_scale).astype(BF16)) for s in ss]
        yield
        acc = jnp.zeros((tq, SLAB), F32)
        for e, (_, v_ref) in zip(es, kv_refs):
            acc += _dot(e, v_ref[:, h * SLAB:(h + 1) * SLAB])
        return acc / acc[:, B_VDIM:B_VDIM + 1]

    pairs = []
    for h in range(0, N_HEADS, 2):
        h0, h1 = _interleave([head(h), head(h + 1)])
        pairs.append(jnp.where(low, h0, pltpu.roll(h1, B_VDIM, 1)))
    o_ref[...] = jnp.concatenate(pairs, axis=-1)


def _attn_ctx_kernel(q_ref, k_ref, v_ref, o_ref):
    _attend_heads(q_ref, [(k_ref, v_ref)], o_ref)


def _attn_dec_kernel(q_ref, k_ref, v_ref, kc_ref, vc_ref, o_ref):
    _attend_heads(q_ref, [(k_ref, v_ref), (kc_ref, vc_ref)], o_ref)


def _attn_ctx(q, k, v, n_seq, seq_len):
    blk = lambda w: pl.BlockSpec((None, seq_len, w), lambda b: (0, b, 0))
    slabs = N_HEADS * SLAB
    return pl.pallas_call(
        _attn_ctx_kernel,
        grid=(n_seq,),
        in_specs=[blk(slabs), blk(slabs), blk(slabs)],
        out_specs=blk(WIDTH),
        out_shape=jax.ShapeDtypeStruct(q.shape[:2] + (WIDTH,), F32),
        compiler_params=_cp(("parallel",)),
        name='attn_ctx',
    )(q, k, v)


def _attn_dec_aliased_kernel(q_ref, k_ref, v_ref, kc_ref, vc_ref, y_hbm, o_ref):
    del y_hbm
    _attn_dec_kernel(q_ref, k_ref, v_ref, kc_ref, vc_ref, o_ref)


def _attn_dec(q, k, v, kc, vc, yb, n_dec, dec_len, past, tq):
    nq = dec_len // tq
    slabs = N_HEADS * SLAB
    return pl.pallas_call(
        _attn_dec_aliased_kernel,
        grid=(n_dec, nq),
        in_specs=[pl.BlockSpec((None, tq, slabs), lambda b, i: (1 + b, i, 0)),
                  pl.BlockSpec((None, dec_len, slabs), lambda b, i: (1 + b, 0, 0)),
                  pl.BlockSpec((None, dec_len, slabs), lambda b, i: (1 + b, 0, 0)),
                  pl.BlockSpec((past, slabs), lambda b, i: (b, 0)),
                  pl.BlockSpec((past, slabs), lambda b, i: (b, 0)),
                  pl.BlockSpec(memory_space=pl.ANY)],
        out_specs=pl.BlockSpec((None, tq, WIDTH), lambda b, i: (1 + b, i, 0)),
        out_shape=jax.ShapeDtypeStruct(yb.shape, F32),
        input_output_aliases={5: 0},
        compiler_params=_cp(("parallel", "arbitrary")),
        name='attn_dec',
    )(q, k, v, kc, vc, yb)


def _rope_tables(n_ctx_rows, n_dec, dec_len):
    rows = dec_len // GRID_W
    row = np.repeat(np.arange(rows, dtype=np.float32), GRID_W)
    col = np.tile(np.arange(GRID_W, dtype=np.float32), rows)
    half = B_ROPE // 2
    inv = jnp.power(ROPE_BASE, -jnp.arange(0, half, 2, dtype=F32) / half)
    ang_r = jnp.asarray(row)[:, None] * inv
    ang_c = jnp.asarray(col)[:, None] * inv
    ang = jnp.concatenate([ang_r, ang_r, ang_c, ang_c], axis=-1)
    cos = jnp.pad(jnp.cos(ang), ((0, 0), (0, LANE - B_ROPE)), constant_values=1.0)
    sin = jnp.pad(jnp.sin(ang), ((0, 0), (0, LANE - B_ROPE)))
    cos = jnp.concatenate([jnp.ones((n_ctx_rows, LANE), F32)] + [cos] * n_dec, axis=0)
    sin = jnp.concatenate([jnp.zeros((n_ctx_rows, LANE), F32)] + [sin] * n_dec, axis=0)
    return cos, sin


def _block_diag(w):
    g, n, m = w.shape[-3:]
    eye = jnp.eye(g, dtype=w.dtype)
    out = w[..., :, :, None, :] * eye[:, None, :, None]
    return out.reshape(w.shape[:-3] + (g * n, g * m))


def kernel(x_prompt, x_sample, cache_mla_ckv, cache_mla_kpe, state_rwkv, state_rglru, state_hgrn, c, c_ctx,
           w_ada, b_ada, ln_g, ln_b, w_ffn_in, w_ffn_out, w_in, w_out,
           rwkv_w0, rwkv_w2, rwkv_a0, rwkv_a2, rwkv_g2, rwkv_kk, rwkv_ka, rwkv_rk, rwkv_gn_g, rwkv_gn_b,
           mla_qn_g, mla_w_uq, mla_kvn_g, mla_w_ukv,
           rglru_conv_w, rglru_conv_b, rglru_wa, rglru_ba, rglru_wx, rglru_bx, rglru_lam,
           hgrn_lb, hgrn_gn_g):
    n_ctx, ctx_len, _ = x_prompt.shape
    n_dec, dec_len, _ = x_sample.shape
    past = cache_mla_ckv.shape[2]
    seg = n_ctx * ctx_len
    assert seg == dec_len, "context rows must form one segment of the decode sequence length"
    assert ctx_len & (ctx_len - 1) == 0 and dec_len & (dec_len - 1) == 0, "sequence lengths must be powers of two"
    n_seg = 1 + n_dec
    tl = _Tiles(n_seg, seg, min(TM, seg))
    tq = min(TQ, dec_len)
    assert ctx_len % LRU_CHUNK == 0
    st = _Streams(n_seg, n_ctx, seg, ctx_len, CHUNK)
    lru_st = _Streams(n_seg, n_ctx, seg, ctx_len, LRU_CHUNK)

    hc = {k: jnp.asarray(v) for k, v in _head_consts().items()}
    ebd = hc['mbd']
    cos, sin = (t.reshape(n_seg, seg, LANE) for t in _rope_tables(seg, n_dec, dec_len))

    cond8 = jnp.zeros((8, D_MODEL), F32).at[0].set(c_ctx).at[1:1 + n_dec].set(c)
    mods = _ada(cond8, w_ada, b_ada).reshape(DEPTH, 8, N_MOD, D_MODEL)
    mods = jnp.transpose(mods, (0, 2, 1, 3))[:, :, :n_seg, None, :]

    w_ffn_in_bf = w_ffn_in.astype(BF16)
    w_ffn_out_bf = w_ffn_out.astype(BF16)
    w_out_bf = w_out.astype(BF16)
    kpe_end = ZA + B_Q_RANK + B_KV_RANK + B_ROPE
    w_in_p = jnp.concatenate([w_in[:, :, :kpe_end], jnp.zeros((DEPTH, D_MODEL, LANE - B_ROPE), F32),
                              w_in[:, :, kpe_end:]], axis=-1).astype(BF16)
    wuq = mla_w_uq.reshape(DEPTH, B_Q_RANK, N_HEADS, B_NOPE + B_ROPE)
    wuq_p = jnp.concatenate([wuq[..., B_NOPE:], wuq[..., :B_NOPE],
                             jnp.zeros((DEPTH, B_Q_RANK, N_HEADS, SLAB - B_NOPE - B_ROPE), F32)], axis=-1)
    wuq_p = wuq_p.reshape(DEPTH, B_Q_RANK, N_HEADS * SLAB).astype(BF16)
    wukv = mla_w_ukv.reshape(DEPTH, B_KV_RANK, N_HEADS, B_NOPE + B_VDIM)
    wuk_p = jnp.concatenate([jnp.zeros((DEPTH, B_KV_RANK, N_HEADS, B_ROPE), F32), wukv[..., :B_NOPE],
                             jnp.zeros((DEPTH, B_KV_RANK, N_HEADS, SLAB - B_NOPE - B_ROPE), F32)], axis=-1)
    wuk_p = wuk_p.reshape(DEPTH, B_KV_RANK, N_HEADS * SLAB).astype(BF16)
    wuv_p = jnp.concatenate([wukv[..., B_NOPE:], jnp.zeros((DEPTH, B_KV_RANK, N_HEADS, SLAB - B_VDIM), F32)], axis=-1)
    wuv_p = wuv_p.reshape(DEPTH, B_KV_RANK, N_HEADS * SLAB).astype(BF16)
    lru_wg = jnp.concatenate([_block_diag(rglru_wa[:, 0]), _block_diag(rglru_wx[:, 0]),
                              _block_diag(rglru_wa[:, 1]), _block_diag(rglru_wx[:, 1])], axis=-1)
    lru_bg = jnp.concatenate([rglru_ba[:, 0], rglru_bx[:, 0], rglru_ba[:, 1], rglru_bx[:, 1]], axis=-1)[:, None, :]
    lbs = _lower_bounds(hgrn_lb)

    rwkv_s0 = _block_diag(state_rwkv)
    hgrn_s0 = _block_diag(jnp.swapaxes(state_hgrn, -1, -2))
    cache_kpe_slab = jnp.pad(cache_mla_kpe, ((0, 0), (0, 0), (0, 0), (0, LANE - B_ROPE)))

    new_ckv, new_kpe, new_rwkv, new_lru, new_hgrn = [], [], [], [], []
    for l in range(DEPTH):
        m = mods[l]
        lng = ln_g[l][:, None, :]
        lnb = ln_b[l][:, None, :]
        ffn0 = functools.partial(_ffn, sh=m[0], sc=m[1], g=m[2], lng=lng[0], lnb=lnb[0], w_in_bf=w_ffn_in_bf,
                                 w_out_bf=w_ffn_out_bf, l=l, f=0, tl=tl)
        if l == 0:
            x = ffn0(x_prompt.reshape(1, seg, D_MODEL), segs=(0, 1))
            x = ffn0(x_sample, segs=(1, n_dec), out_off=1, into=x)
        else:
            x = ffn0(x)
        za, zb, zc, zd = _inproj(x, m[3], m[4], w_in_p, l, tl)

        pa = {'w0': rwkv_w0[l][:, None, :], 'w2': rwkv_w2[l], 'a0': rwkv_a0[l][:, None, :], 'a2': rwkv_a2[l],
              'g2': rwkv_g2[l], 'kk': rwkv_kk[l][None, :], 'ka': rwkv_ka[l][None, :], 'rk': rwkv_rk[l][None, :]}
        pre = _rwkv_pre(za, pa, ebd, tl)
        (oa_f, oa_b, sa_fin), (od_f, od_b, sd_fin) = _mix_scan(za, pre[:7], zd, lbs[l], rwkv_s0[:, l],
                                                                hgrn_s0[:, l], hc, st)
        new_rwkv.append(sa_fin)
        new_hgrn.append(sd_fin)

        pb = {'qn_g': mla_qn_g[l][None, :], 'kvn_g': mla_kvn_g[l][None, :], 'wuq': wuq_p[l], 'wuk': wuk_p[l],
              'wuv': wuv_p[l]}
        ckvn, q_all, k_all, v_all = _mla_prep(zb, cos, sin, pb, tl)
        kc, vc = _mla_cache(cache_mla_ckv[:, l].reshape(n_dec * past, B_KV_RANK),
                            cache_kpe_slab[:, l].reshape(n_dec * past, LANE), pb)
        yb = _attn_ctx(q_all, k_all, v_all, n_ctx, ctx_len)
        yb = _attn_dec(q_all, k_all, v_all, kc, vc, yb, n_dec, dec_len, past, tq)
        new_ckv.append(ckvn[0].reshape(n_ctx, ctx_len, B_KV_RANK))
        new_kpe.append(zb[0, :, B_Q_RANK + B_KV_RANK:B_Q_RANK + B_KV_RANK + B_ROPE].reshape(n_ctx, ctx_len, B_ROPE))

        pc = {'conv_w': rglru_conv_w[l], 'conv_b': rglru_conv_b[l][None, :], 'wg': lru_wg[l], 'bg': lru_bg[l],
              'lam': rglru_lam[l][:, None, :]}
        coef = _lru_pre(zc, pc, ctx_len, dec_len, tl)
        h_f, h_b, h_fin = _lru_scan(coef, state_rglru[:, l], lru_st)
        new_lru.append(h_fin)

        x = _outproj(x, oa_f, oa_b, pre[7], pre[8], yb, h_f, h_b, zc, od_f, od_b, zd,
                     rwkv_gn_g[l][None, :], rwkv_gn_b[l][None, :], hgrn_gn_g[l][None, :], ebd,
                     w_out_bf, m[5], lng[1], lnb[1], l, tl)
        ffn1 = functools.partial(_ffn, sh=m[6], sc=m[7], g=m[8], lng=lng[2], lnb=lnb[2], w_in_bf=w_ffn_in_bf,
                                 w_out_bf=w_ffn_out_bf, l=l, f=1, tl=tl)
        if l == DEPTH - 1:
            y_prompt = ffn1(x, segs=(0, 1), out_nseg=1).reshape(n_ctx, ctx_len, D_MODEL)
            y_sample = ffn1(x, segs=(1, n_dec), x_off=1, out_nseg=n_dec)
        else:
            x = ffn1(x)
    return (y_prompt, y_sample, jnp.stack(new_ckv, axis=1), jnp.stack(new_kpe, axis=1),
            jnp.stack(new_rwkv, axis=1), jnp.stack(new_lru, axis=1), jnp.stack(new_hgrn, axis=1))
```

```python
import functools

import numpy as np
import jax
import jax.numpy as jnp
from jax import lax
from jax.experimental import pallas as pl
from jax.experimental.pallas import tpu as pltpu

D_MODEL = 1024
DEPTH = 4
GRID_W = 64
HEAD_DIM = 64
N_HEADS = 4
WIDTH = N_HEADS * HEAD_DIM
A_DECAY_RANK = 64
A_ICLR_RANK = 64
A_GATE_RANK = 128
B_NOPE = 64
B_ROPE = 32
B_VDIM = 64
B_Q_RANK = 256
B_KV_RANK = 128
C_CONV = 4
C_POW = 8.0
D_FF = 2816
N_MOD = 9
ROPE_BASE = 10000.0
LN_EPS = 1e-5
RMS_EPS = 1e-6
RWKV_GN_EPS = 64e-5
DN_ALPHA = (2 * DEPTH) ** 0.25

LANE = 128
SLAB = 128
CHUNK = 64
SUB = 16
LRU_CHUNK = 256
LRU_ROWS = 8
TM = 512
TF = 256
TQ = 256
VMEM_LIMIT = 56 * 1024 * 1024

F32 = jnp.float32
BF16 = jnp.bfloat16
HI = lax.Precision.HIGHEST
NEG = -1e30
LOG2E = float(np.log2(np.e))
ATTN_LOG2_SCALE = float((B_NOPE + B_ROPE) ** -0.5) * LOG2E
RWKV_HI = False


def _cp(sem):
    return pltpu.CompilerParams(dimension_semantics=sem, vmem_limit_bytes=VMEM_LIMIT)


def _dot(a, b, hi=False):
    return jnp.dot(a, b, preferred_element_type=F32, precision=HI if hi else None)


def _dot_nt(a, b, hi=False):
    return lax.dot_general(a, b, (((1,), (1,)), ((), ())), preferred_element_type=F32,
                           precision=HI if hi else None)


def _dot_tn(a, b, hi=False):
    return lax.dot_general(a, b, (((0,), (0,)), ((), ())), preferred_element_type=F32,
                           precision=HI if hi else None)


def _head_sums(x, ebd_bf):
    hi = x.astype(BF16)
    lo = (x - hi.astype(F32)).astype(BF16)
    return _dot(hi, ebd_bf) + _dot(lo, ebd_bf)


def _sigmoid(x):
    return 1.0 / (1.0 + jnp.exp(-x))


def _silu(x):
    return x * _sigmoid(x)


def _softplus(x):
    return jnp.maximum(x, 0.0) + jnp.log(1.0 + jnp.exp(-jnp.abs(x)))


def _tile4(y):
    return jnp.concatenate([y, y, y, y], axis=0)


def _head_consts():
    r = np.arange(WIDTH)
    same = (r[:, None] // HEAD_DIM) == (r[None, :] // HEAD_DIM)
    t = np.arange(CHUNK)[:, None]
    j = (np.arange(WIDTH) % CHUNK)[None, :]
    tt = np.arange(CHUNK)
    return {
        'mbd': same.astype(np.float32),
        'low_s': (j < t).astype(np.float32), 'low_i': (j <= t).astype(np.float32),
        'up_s': (j > t).astype(np.float32), 'up_i': (j >= t).astype(np.float32),
        'eye': (j == t).astype(np.float32),
        'tri_f': (tt[None, :] <= tt[:, None]).astype(np.float32),
        'tri_b': (tt[None, :] >= tt[:, None]).astype(np.float32),
    }


def _lb_kernel(x_ref, o_ref):
    x = x_ref[...]
    m = jnp.max(x, axis=0, keepdims=True)
    e = jnp.exp(x - m)
    sm = e / jnp.sum(e, axis=0, keepdims=True)
    run = sm[0:1]
    rows = [run - sm[0:1]]
    for l in range(1, DEPTH):
        run = run + sm[l:l + 1]
        rows.append(run - sm[0:1])
    o_ref[...] = jnp.concatenate(rows, axis=0)


def _lower_bounds(hgrn_lb):
    flat = hgrn_lb.reshape(DEPTH, 2 * WIDTH)
    out = pl.pallas_call(_lb_kernel, out_shape=jax.ShapeDtypeStruct(flat.shape, F32), name='hgrn_lb')(flat)
    return out.reshape(DEPTH, 2, 1, WIDTH)


def _ada_kernel(c_ref, w_ref, b_ref, o_ref):
    h = _silu(c_ref[...]).astype(BF16)
    o_ref[0] = _dot(h, w_ref[0].astype(BF16)) + b_ref[0]


def _ada(cond8, w_ada, b_ada):
    n = N_MOD * D_MODEL
    tn = 2304
    return pl.pallas_call(
        _ada_kernel,
        grid=(DEPTH, n // tn),
        in_specs=[pl.BlockSpec((8, D_MODEL), lambda l, j: (0, 0)),
                  pl.BlockSpec((1, D_MODEL, tn), lambda l, j: (l, 0, j)),
                  pl.BlockSpec((1, 1, tn), lambda l, j: (l, 0, j))],
        out_specs=pl.BlockSpec((1, 8, tn), lambda l, j: (l, 0, j)),
        out_shape=jax.ShapeDtypeStruct((DEPTH, 8, n), F32),
        compiler_params=_cp(("arbitrary", "arbitrary")),
        name='ada',
    )(cond8, w_ada, b_ada.reshape(DEPTH, 1, n))


def _layer_norm_rows(y, g, b):
    mu = jnp.mean(y, axis=-1, keepdims=True)
    d = y - mu
    var = jnp.mean(d * d, axis=-1, keepdims=True)
    return d * lax.rsqrt(var + LN_EPS) * g + b


def _ffn_kernel(x_ref, sh_ref, sc_ref, g_ref, lng_ref, lnb_ref, win_ref, wout_ref, o_ref):
    x = x_ref[...]
    h = (x * (1.0 + sc_ref[0]) + sh_ref[0]).astype(BF16)
    n_chunks = D_FF // TF

    def gate_up(c):
        return (_dot(h, win_ref[:, c * TF:(c + 1) * TF]),
                _dot(h, win_ref[:, D_FF + c * TF:D_FF + (c + 1) * TF]))

    y = jnp.zeros(x.shape, F32)
    cur = gate_up(0)
    for c in range(n_chunks):
        nxt = gate_up(c + 1) if c + 1 < n_chunks else None
        a = (_silu(cur[0]) * cur[1]).astype(BF16)
        y = y + _dot(a, wout_ref[c * TF:(c + 1) * TF, :])
        cur = nxt
    o_ref[...] = _layer_norm_rows(DN_ALPHA * x + 0.5 * g_ref[0] * y, lng_ref[...], lnb_ref[...])


class _Tiles:
    def __init__(self, n_seg, seg_rows, tm):
        self.n_seg, self.seg_rows, self.tm = n_seg, seg_rows, tm
        self.per_seg = seg_rows // tm
        self.grid = (n_seg * self.per_seg,)

    def row(self, width, col=0):
        ps = self.per_seg
        return pl.BlockSpec((None, self.tm, width), lambda i: (i // ps, i % ps, col))

    def mod(self):
        ps = self.per_seg
        return pl.BlockSpec((1, 1, D_MODEL), lambda i: (i // ps, 0, 0))

    def shape(self, width, dtype=F32):
        return jax.ShapeDtypeStruct((self.n_seg, self.seg_rows, width), dtype)


def _full(a):
    return pl.BlockSpec(a.shape, lambda i: (0,) * a.ndim)


def _ffn_into_kernel(x_ref, sh_ref, sc_ref, g_ref, lng_ref, lnb_ref, win_ref, wout_ref, buf_hbm, o_ref):
    del buf_hbm
    _ffn_kernel(x_ref, sh_ref, sc_ref, g_ref, lng_ref, lnb_ref, win_ref, wout_ref, o_ref)


def _ffn(x, sh, sc, g, lng, lnb, w_in_bf, w_out_bf, l, f, tl, segs=None, x_off=0, out_off=0, out_nseg=None,
         into=None):
    first, count = segs if segs is not None else (0, tl.n_seg)
    out_nseg = tl.n_seg if out_nseg is None else out_nseg
    ps, tm = tl.per_seg, tl.tm
    mod = pl.BlockSpec((1, 1, D_MODEL), lambda i: (first + i // ps, 0, 0))
    vec = pl.BlockSpec((1, D_MODEL), lambda i: (0, 0))
    in_specs = [pl.BlockSpec((None, tm, D_MODEL), lambda i: (x_off + i // ps, i % ps, 0)), mod, mod, mod, vec, vec,
                pl.BlockSpec((None, None, D_MODEL, 2 * D_FF), lambda i: (l, f, 0, 0)),
                pl.BlockSpec((None, None, D_FF, D_MODEL), lambda i: (l, f, 0, 0))]
    args = [x, sh, sc, g, lng, lnb, w_in_bf, w_out_bf]
    aliases = {}
    if into is not None:
        in_specs.append(pl.BlockSpec(memory_space=pl.ANY))
        args.append(into)
        aliases = {len(args) - 1: 0}
    return pl.pallas_call(
        _ffn_kernel if into is None else _ffn_into_kernel,
        grid=(count * ps,),
        in_specs=in_specs,
        out_specs=pl.BlockSpec((None, tm, D_MODEL), lambda i: (out_off + i // ps, i % ps, 0)),
        out_shape=jax.ShapeDtypeStruct((out_nseg, tl.seg_rows, D_MODEL), F32),
        input_output_aliases=aliases,
        compiler_params=_cp(("parallel",)),
        name='ffn',
    )(*args)


ZA = 3 * WIDTH + 2 * A_DECAY_RANK + 2 * A_ICLR_RANK + A_GATE_RANK
ZB = B_Q_RANK + B_KV_RANK + LANE
ZC = 2 * WIDTH
ZD = 5 * WIDTH
ZTOT = ZA + ZB + ZC + ZD


N_RWKV_PRE = 9
N_MLA_PREP = 4


def _inproj_kernel(x_ref, sh_ref, sc_ref, w_ref, *rest):
    n_a, n_b = 9, 7
    a_par, b_par = rest[:n_a], rest[n_a:n_a + n_b]
    outs = rest[n_a + n_b:]
    r_o, v_o = outs[0:2]
    a_out = outs[2:2 + N_RWKV_PRE]
    kpe_o = outs[2 + N_RWKV_PRE]
    b_out = outs[3 + N_RWKV_PRE:3 + N_RWKV_PRE + N_MLA_PREP]
    zc_ref, zd_ref, za_scr, zb_scr = outs[3 + N_RWKV_PRE + N_MLA_PREP:]
    h = (x_ref[...] * (1.0 + sc_ref[0]) + sh_ref[0]).astype(BF16)
    za_scr[...] = _dot(h, w_ref[:, 0:ZA])
    zb_scr[...] = _dot(h, w_ref[:, ZA:ZA + ZB])
    zc_ref[...] = _dot(h, w_ref[:, ZA + ZB:ZA + ZB + ZC])
    zd_ref[...] = _dot(h, w_ref[:, ZA + ZB + ZC:ZTOT])
    r_o[...] = za_scr[:, 0:WIDTH]
    v_o[...] = za_scr[:, 2 * WIDTH:3 * WIDTH]
    kpe_o[...] = zb_scr[:, B_Q_RANK + B_KV_RANK:ZB]
    _rwkv_pre_kernel(za_scr, *a_par, *a_out)
    _mla_prep_kernel(zb_scr, *b_par, *b_out)


def _inproj(x, sh, sc, w_in_p, l, pa, pb, ebd, cos, sin, tl):
    mod = tl.mod()
    a_par = (pa['w0'], pa['w2'], pa['a0'], pa['a2'], pa['g2'], pa['kk'], pa['ka'], pa['rk'], ebd)
    b_par = (pb['qn_g'], pb['kvn_g'], pb['wuq'], pb['wuk'], pb['wuv'])
    slabs = N_HEADS * SLAB
    w = WIDTH
    out_specs = ([tl.row(w)] * (2 + N_RWKV_PRE) + [tl.row(LANE), tl.row(B_KV_RANK)] + [tl.row(slabs)] * 3
                 + [tl.row(ZC), tl.row(ZD)])
    out_shape = ([tl.shape(w)] * (2 + N_RWKV_PRE) + [tl.shape(LANE), tl.shape(B_KV_RANK)]
                 + [tl.shape(slabs, BF16)] * 3 + [tl.shape(ZC), tl.shape(ZD)])
    outs = pl.pallas_call(
        _inproj_kernel,
        grid=tl.grid,
        in_specs=[tl.row(D_MODEL), mod, mod, pl.BlockSpec((None, D_MODEL, ZTOT), lambda i: (l, 0, 0))]
                 + [_full(a) for a in a_par] + [tl.row(LANE), tl.row(LANE)] + [_full(a) for a in b_par],
        out_specs=out_specs,
        out_shape=out_shape,
        scratch_shapes=[pltpu.VMEM((tl.tm, ZA), F32), pltpu.VMEM((tl.tm, ZB), F32)],
        compiler_params=_cp(("parallel",)),
        name='mixer_in',
    )(x, sh, sc, w_in_p, *a_par, cos, sin, *b_par)
    r, v = outs[0:2]
    pre = outs[2:2 + N_RWKV_PRE]
    kpe = outs[2 + N_RWKV_PRE]
    mla = outs[3 + N_RWKV_PRE:3 + N_RWKV_PRE + N_MLA_PREP]
    zc, zd = outs[-2:]
    return r, v, pre, kpe, mla, zc, zd


def _gelu_tanh(x):
    return 0.5 * x * (1.0 + jnp.tanh(np.sqrt(2.0 / np.pi) * (x + 0.044715 * (x * x * x))))


def _outproj_kernel(x_ref, af_ref, ab_ref, bon_ref, gate_ref, yb_ref, hf_ref, hb_ref, cg_ref, df_ref, db_ref,
                    dg_ref, agn_g_ref, agn_b_ref, dgn_g_ref, ebd_ref, w_ref, g_ref, lng_ref, lnb_ref, o_ref):
    avg = (ebd_ref[...] * (1.0 / HEAD_DIM)).astype(BF16)
    oa = af_ref[...] + ab_ref[...]
    da = oa - _head_sums(oa, avg)
    gn = da * lax.rsqrt(_head_sums(da * da, avg) + RWKV_GN_EPS) * agn_g_ref[...] + agn_b_ref[...]
    ya = (gn + bon_ref[...]) * gate_ref[...]
    h = jnp.concatenate([hf_ref[p] + hb_ref[p] for p in range(WIDTH // LANE)], axis=-1)
    yc = h * _gelu_tanh(cg_ref[...])
    od = df_ref[...] + db_ref[...]
    yd = od * lax.rsqrt(_head_sums(od * od, avg) + RMS_EPS) * dgn_g_ref[...] * _silu(dg_ref[...])
    y = _dot(ya.astype(BF16), w_ref[0:WIDTH, :])
    y += _dot(yb_ref[...].astype(BF16), w_ref[WIDTH:2 * WIDTH, :])
    y += _dot(yc.astype(BF16), w_ref[2 * WIDTH:3 * WIDTH, :])
    y += _dot(yd.astype(BF16), w_ref[3 * WIDTH:4 * WIDTH, :])
    o_ref[...] = _layer_norm_rows(DN_ALPHA * x_ref[...] + g_ref[0] * y, lng_ref[...], lnb_ref[...])


def _outproj(x, oa_f, oa_b, bonus, gate, yb, hf, hb, zc, od_f, od_b, zd, agn_g, agn_b, dgn_g, ebd,
             w_out_bf, g, lng, lnb, l, tl):
    vec = pl.BlockSpec((1, D_MODEL), lambda i: (0, 0))
    wvec = pl.BlockSpec((1, WIDTH), lambda i: (0, 0))
    ps = tl.per_seg
    halves = pl.BlockSpec((WIDTH // LANE, None, tl.tm, LANE), lambda i: (0, i // ps, i % ps, 0))
    w = WIDTH
    return pl.pallas_call(
        _outproj_kernel,
        grid=tl.grid,
        in_specs=[tl.row(D_MODEL), tl.row(w), tl.row(w), tl.row(w), tl.row(w), tl.row(w), halves, halves,
                  tl.row(w, 1), tl.row(w), tl.row(w), tl.row(w, 4), wvec, wvec, wvec, _full(ebd),
                  pl.BlockSpec((None, 4 * WIDTH, D_MODEL), lambda i: (l, 0, 0)), tl.mod(), vec, vec],
        out_specs=tl.row(D_MODEL),
        out_shape=tl.shape(D_MODEL),
        compiler_params=_cp(("parallel",)),
        name='mixer_out',
    )(x, oa_f, oa_b, bonus, gate, yb, hf, hb, zc, od_f, od_b, zd, agn_g, agn_b, dgn_g, ebd, w_out_bf, g, lng, lnb)


class _Streams:
    def __init__(self, n_seg, n_ctx, seg_rows, ctx_len, chunk):
        self.n_seg, self.n_ctx, self.chunk = n_seg, n_ctx, chunk
        self.n = seg_rows // chunk
        self.ctx_n = ctx_len // chunk

    def fwd(self, width, col=0):
        return pl.BlockSpec((self.n_seg, self.chunk, width), lambda c: (0, c, col))

    def bwd(self, width, col=0):
        n = self.n
        return pl.BlockSpec((self.n_seg, self.chunk, width), lambda c: (0, n - 1 - c, col))

    def fin_fwd(self, shape):
        ctx_n = self.ctx_n
        return pl.BlockSpec((1,) + shape, lambda c: (c // ctx_n,) + (0,) * len(shape))

    def fin_bwd(self, shape):
        ctx_n, n_ctx = self.ctx_n, self.n_ctx
        return pl.BlockSpec((1,) + shape, lambda c: (n_ctx - 1 - c // ctx_n,) + (0,) * len(shape))

    def ctx_first(self, c):
        return c % self.ctx_n == 0

    def ctx_last(self, c):
        return c % self.ctx_n == self.ctx_n - 1


def _rwkv_pre_kernel(za_ref, w0_ref, w2_ref, a0_ref, a2_ref, g2_ref, kkp_ref, kap_ref, rkp_ref, ebd_ref,
                     kap_o, lwf_o, lwb_o, kdf_o, kdb_o, bbf_o, bbb_o, bon_o, gate_o):
    r = za_ref[:, 0:WIDTH]
    k = za_ref[:, WIDTH:2 * WIDTH]
    v = za_ref[:, 2 * WIDTH:3 * WIDTH]
    o = 3 * WIDTH
    xw = (za_ref[:, o:o + A_DECAY_RANK], za_ref[:, o + A_DECAY_RANK:o + 2 * A_DECAY_RANK])
    o += 2 * A_DECAY_RANK
    xa = (za_ref[:, o:o + A_ICLR_RANK], za_ref[:, o + A_ICLR_RANK:o + 2 * A_ICLR_RANK])
    o += 2 * A_ICLR_RANK
    xg = za_ref[:, o:o + A_GATE_RANK]
    ebd = ebd_ref[...].astype(BF16)

    kk = k * kkp_ref[...]
    nrm = jnp.sqrt(_head_sums(kk * kk, ebd))
    kappa = kk / jnp.maximum(nrm, 1e-12)
    kap_o[...] = kappa
    k_sum = jnp.zeros_like(r)
    for d, (lw_o, kd_o, bb_o) in enumerate(((lwf_o, kdf_o, bbf_o), (lwb_o, kdb_o, bbb_o))):
        w_log = -_softplus(-(w0_ref[d] + _dot(jnp.tanh(xw[d]).astype(BF16), w2_ref[d].astype(BF16)))) - 0.5
        lw_o[...] = -jnp.exp(w_log)
        a = _sigmoid(a0_ref[d] + _dot(xa[d].astype(BF16), a2_ref[d].astype(BF16)))
        k_d = k * (1.0 + (a - 1.0) * kap_ref[...])
        kd_o[...] = k_d
        bb_o[...] = kappa * a
        k_sum += k_d
    bon_o[...] = _head_sums(r * k_sum * rkp_ref[...], ebd) * v
    gate_o[...] = _dot(_sigmoid(xg).astype(BF16), g2_ref[...].astype(BF16))


def _rwkv_chunk(r, v, kap, lw, kd, bb, st, mbd, eye, m_strict, m_incl, tri):
    cum = _dot(tri, lw, hi=True)
    yield
    tot = jnp.sum(lw, axis=0, keepdims=True)
    g_in = jnp.exp(cum)
    g_inv = jnp.exp(-cum)
    g_ex = jnp.exp(cum - lw)
    g_end = jnp.exp(tot - cum)
    cast = (lambda a: a) if RWKV_HI else (lambda a: a.astype(BF16))
    mask = cast(mbd)
    bd = lambda y: _tile4(cast(y)) * mask
    x = cast(jnp.concatenate([kap * g_ex, r * g_in], axis=0))
    ab = _dot_nt(x, bd(bb * g_inv), hi=RWKV_HI)
    ak = _dot_nt(x, bd(kd * g_inv), hi=RWKV_HI)
    xs = _dot_nt(x, cast(st), hi=RWKV_HI)
    yield
    c = r.shape[0]
    strict = m_strict > 0.5
    incl = m_incl > 0.5
    a_ub = jnp.where(strict, ab[:c], 0.0)
    a_rb = jnp.where(incl, ab[c:], 0.0)
    a_uk = jnp.where(strict, ak[:c], 0.0)
    a_rk = jnp.where(incl, ak[c:], 0.0)

    def catmul(pc, q):
        return _dot(cast(pc), bd(q), hi=RWKV_HI)

    xs = xs + catmul(jnp.concatenate([a_uk, a_rk], axis=0), v)
    rhs, o_v = xs[:c], xs[c:]
    xp = -a_ub
    inv = eye + xp
    xp = catmul(xp, xp)
    yield
    for _ in range(int(np.log2(c)) - 2):
        both = catmul(jnp.concatenate([inv, xp], axis=0), xp)
        yield
        inv = inv + both[:c]
        xp = both[c:]
    inv = inv + catmul(inv, xp)
    yield
    u = -catmul(inv, rhs)
    yield
    o = o_v + catmul(a_rb, u)
    upd = _dot_tn(cast(jnp.concatenate([u, v], axis=0)),
                  cast(jnp.concatenate([bb * g_end, kd * g_end], axis=0)), hi=RWKV_HI)
    st_new = st * jnp.exp(tot) + upd * mbd
    return o, st_new


def _interleave(gens):
    results = [None] * len(gens)
    live = list(range(len(gens)))
    while live:
        still = []
        for k in live:
            try:
                next(gens[k])
                still.append(k)
            except StopIteration as done:
                results[k] = done.value
        live = still
    return results


def _init_states(st, c, st_scr, s0_ref):
    @pl.when(c == 0)
    def _():
        st_scr[1:] = s0_ref[...]

    @pl.when(st.ctx_first(c))
    def _():
        st_scr[0] = jnp.zeros(st_scr.shape[1:], F32)


def _hgrn_chunk(xq, xf, xi, lb, st, tri, ebd, mbd, p_scr, reverse):
    c = xq.shape[0]
    nb = c // SUB
    q = _silu(xq)
    gsig = lb + (1.0 - lb) * _sigmoid(xf)
    kk = 1.0 - gsig
    lg = jnp.log(gsig)
    cum = _dot(tri, lg, hi=True)
    yield
    cum = cum * LOG2E
    tot = jnp.sum(lg, axis=0, keepdims=True) * LOG2E
    lane_s = lax.broadcasted_iota(jnp.int32, (SUB, WIDTH), 1) % c
    half = SUB // 2
    row_h = lax.broadcasted_iota(jnp.int32, (half, WIDTH), 0)
    blk = lambda a, i: a[i * SUB:(i + 1) * SUB]
    end_row = (lambda j: j * SUB) if reverse else (lambda j: j * SUB + SUB - 1)
    later = (lambda j: range(0, j)) if reverse else (lambda j: range(j + 1, nb))

    k_end = jnp.concatenate([blk(kk, j) * jnp.exp2(cum[end_row(j):end_row(j) + 1] - blk(cum, j))
                             for j in range(nb)], axis=0)
    q_parts, where_part = [], {}
    for j in range(nb):
        for i in later(j):
            where_part[(i, j)] = len(q_parts)
            q_parts.append(blk(q, i) * jnp.exp2(blk(cum, i) - cum[end_row(j):end_row(j) + 1]))
    cross = _dot_nt(jnp.concatenate(q_parts, axis=0).astype(BF16), (_tile4(k_end) * mbd).astype(BF16))
    o_state = _dot_nt((q * jnp.exp2(cum)).astype(BF16), st.astype(BF16))
    upd = _dot_tn(xi.astype(BF16), (kk * jnp.exp2(tot - cum)).astype(BF16))
    yield

    for i in range(nb):
        cb, qb = blk(cum, i), blk(q, i)
        for sl in range(SUB):
            s = i * SUB + sl
            parts = []
            for lo in (0, half):
                hi = lo + half - 1
                none_valid = lo > sl if reverse else hi < sl
                all_valid = hi <= sl if reverse else lo >= sl
                if none_valid:
                    parts.append(jnp.zeros((half, WIDTH), F32))
                    continue
                d = cb[lo:lo + half] - cum[s:s + 1]
                if not all_valid:
                    valid = (row_h + lo <= sl) if reverse else (row_h + lo >= sl)
                    d = jnp.where(valid, d, NEG)
                parts.append(jnp.exp2(d) * qb[lo:lo + half] * kk[s:s + 1])
            p_scr[s * SUB:(s + 1) * SUB, :] = jnp.concatenate(parts, axis=0).astype(BF16)
    same = _dot(p_scr[...], ebd.astype(BF16))
    yield

    att_rows = []
    for i in range(nb):
        att = jnp.zeros((SUB, WIDTH), F32)
        for sl in range(SUB):
            s = i * SUB + sl
            att = jnp.where(lane_s == s, same[s * SUB:(s + 1) * SUB], att)
        for j in range(nb):
            if (i, j) in where_part:
                n = where_part[(i, j)]
                att = jnp.where(lane_s // SUB == j, cross[n * SUB:(n + 1) * SUB], att)
        att_rows.append(att)
    att = jnp.concatenate(att_rows, axis=0)
    o = o_state + _dot(att.astype(BF16), (_tile4(xi) * mbd).astype(BF16))
    st_new = st * jnp.exp2(tot) + upd * mbd
    return o, st_new


def _mix_scan_kernel(st,
                     rf_ref, vf_ref, kapf_ref, lwf_ref, kdf_ref, bbf_ref,
                     rb_ref, vb_ref, kapb_ref, lwb_ref, kdb_ref, bbb_ref,
                     qf_ref, ff_ref, if_ref, qb_ref, fb_ref, ib_ref, lb_ref,
                     sa0_ref, sd0_ref, mbd_ref, eye_ref, lows_ref, lowi_ref, ups_ref, upi_ref, trif_ref, trib_ref,
                     af_ref, ab_ref, afin_f, afin_b, df_ref, db_ref, dfin_f, dfin_b,
                     sa_scr, sd_scr, p_scr):
    c = pl.program_id(0)
    _init_states(st, c, sa_scr, sa0_ref)
    _init_states(st, c, sd_scr, sd0_ref)
    mbd = mbd_ref[...]
    eye = eye_ref[...]
    gens = []
    for s in range(st.n_seg):
        gens.append(_rwkv_chunk(rf_ref[s], vf_ref[s], kapf_ref[s], lwf_ref[s], kdf_ref[s], bbf_ref[s],
                                sa_scr[s, 0], mbd, eye, lows_ref[...], lowi_ref[...], trif_ref[...]))
        gens.append(_hgrn_chunk(qf_ref[s], ff_ref[s], if_ref[s], lb_ref[0], sd_scr[s, 0], trif_ref[...],
                                mbd, mbd, p_scr.at[s, 0], False))
        gens.append(_rwkv_chunk(rb_ref[s], vb_ref[s], kapb_ref[s], lwb_ref[s], kdb_ref[s], bbb_ref[s],
                                sa_scr[s, 1], mbd, eye, ups_ref[...], upi_ref[...], trib_ref[...]))
        gens.append(_hgrn_chunk(qb_ref[s], fb_ref[s], ib_ref[s], lb_ref[1], sd_scr[s, 1], trib_ref[...],
                                mbd, mbd, p_scr.at[s, 1], True))
    outs = ((af_ref, sa_scr), (df_ref, sd_scr), (ab_ref, sa_scr), (db_ref, sd_scr))
    for k, (o, st_new) in enumerate(_interleave(gens)):
        s, kind = k // 4, k % 4
        o_ref, scr = outs[kind]
        o_ref[s] = o
        scr[s, kind // 2] = st_new

    @pl.when(st.ctx_last(c))
    def _():
        for h in range(N_HEADS):
            hs = slice(h * HEAD_DIM, (h + 1) * HEAD_DIM)
            afin_f[0, h] = sa_scr[0, 0, hs, hs]
            afin_b[0, h] = sa_scr[0, 1, hs, hs]
            dfin_f[0, h] = sd_scr[0, 0, hs, hs].T
            dfin_b[0, h] = sd_scr[0, 1, hs, hs].T


def _mix_scan(r, v, pre, zd, lb, sa0_dec, sd0_dec, hc, st):
    kap, lwf, lwb, kdf, kdb, bbf, bbb = pre
    consts = (hc['mbd'], hc['eye'], hc['low_s'], hc['low_i'], hc['up_s'], hc['up_i'], hc['tri_f'], hc['tri_b'])
    w = WIDTH
    o_shape = jax.ShapeDtypeStruct(kap.shape, F32)
    head_blocks = (N_HEADS, HEAD_DIM, HEAD_DIM)
    fin_shape = jax.ShapeDtypeStruct((st.n_ctx,) + head_blocks, F32)
    fin_specs = [st.fin_fwd(head_blocks), st.fin_bwd(head_blocks)]
    outs = pl.pallas_call(
        functools.partial(_mix_scan_kernel, st),
        grid=(st.n,),
        in_specs=[st.fwd(w)] * 6 + [st.bwd(w)] * 6 + [
                  st.fwd(w, 0), st.fwd(w, 1), st.fwd(w, 3), st.bwd(w, 0), st.bwd(w, 2), st.bwd(w, 3),
                  _full(lb), _full(sa0_dec), _full(sd0_dec)] + [_full(a) for a in consts],
        out_specs=[st.fwd(w), st.bwd(w)] + fin_specs + [st.fwd(w), st.bwd(w)] + fin_specs,
        out_shape=[o_shape, o_shape, fin_shape, fin_shape] * 2,
        scratch_shapes=[pltpu.VMEM((st.n_seg, 2, w, w), F32), pltpu.VMEM((st.n_seg, 2, w, w), F32),
                        pltpu.VMEM((st.n_seg, 2, CHUNK * SUB, w), BF16)],
        compiler_params=_cp(("arbitrary",)),
        name='mix_scan',
    )(r, v, kap, lwf, kdf, bbf, r, v, kap, lwb, kdb, bbb, zd, zd, zd, zd, zd, zd, lb, sa0_dec, sd0_dec, *consts)
    a_f, a_b, afin_f, afin_b, d_f, d_b, dfin_f, dfin_b = outs
    return ((a_f, a_b, jnp.stack([afin_f, afin_b], axis=1)), (d_f, d_b, jnp.stack([dfin_f, dfin_b], axis=1)))


def _lru_pre_kernel(seq_of_tile, xp_ref, x_ref, xn_ref, cw_ref, cb_ref, wg_ref, bg_ref, lam_ref,
                    af_o, bf_o, ab_o, bb_o, pad_scr):
    i = pl.program_id(0)
    tm = x_ref.shape[0]
    seqlen = seq_of_tile(i)
    pad_scr[0:8, :] = xp_ref[:, 0:WIDTH]
    pad_scr[8:8 + tm, :] = x_ref[:, 0:WIDTH]
    pad_scr[8 + tm:16 + tm, :] = xn_ref[:, 0:WIDTH]
    pos = jnp.bitwise_and(lax.broadcasted_iota(jnp.int32, (tm, WIDTH), 0) + i * tm, seqlen - 1)
    u = jnp.zeros((tm, WIDTH), F32) + cb_ref[...]
    for j in range(C_CONV):
        off = j - C_CONV // 2
        tap = pad_scr[pl.ds(8 + off, tm), :]
        ok = jnp.logical_and(pos + off >= 0, pos + off < seqlen)
        u += jnp.where(ok, tap, 0.0) * cw_ref[j:j + 1, :]
    gates = _sigmoid(_dot(u.astype(BF16), wg_ref[...].astype(BF16)) + bg_ref[...])
    for d, (a_o, b_o) in enumerate(((af_o, bf_o), (ab_o, bb_o))):
        r = gates[:, (2 * d) * WIDTH:(2 * d + 1) * WIDTH]
        ig = gates[:, (2 * d + 1) * WIDTH:(2 * d + 2) * WIDTH]
        log_a = -C_POW * r * _softplus(-lam_ref[d])
        a = jnp.exp(log_a)
        b = jnp.sqrt(-jnp.tanh(log_a) * (a * a + 1.0)) * (ig * u)
        for half in range(WIDTH // LANE):
            a_o[half] = a[:, half * LANE:(half + 1) * LANE]
            b_o[half] = b[:, half * LANE:(half + 1) * LANE]


def _lru_pre(zc, p, ctx_len, dec_len, tl):
    ps, tm = tl.per_seg, tl.tm
    per = tm // 8
    last8 = tl.seg_rows // 8 - 1
    seq_of_tile = lambda i: jnp.where(i < ps, ctx_len, dec_len)
    args = (p['conv_w'], p['conv_b'], p['wg'], p['bg'], p['lam'])
    nh = WIDTH // LANE
    prev8 = pl.BlockSpec((None, 8, WIDTH), lambda i: (i // ps, jnp.maximum((i % ps) * per - 1, 0), 0))
    next8 = pl.BlockSpec((None, 8, WIDTH), lambda i: (i // ps, jnp.minimum((i % ps + 1) * per, last8), 0))
    return pl.pallas_call(
        functools.partial(_lru_pre_kernel, seq_of_tile),
        grid=tl.grid,
        in_specs=[prev8, tl.row(WIDTH), next8] + [_full(a) for a in args],
        out_specs=[pl.BlockSpec((nh, None, tm, LANE), lambda i: (0, i // ps, i % ps, 0))] * 4,
        out_shape=[jax.ShapeDtypeStruct((nh, tl.n_seg, tl.seg_rows, LANE), F32)] * 4,
        scratch_shapes=[pltpu.VMEM((tm + 16, WIDTH), F32)],
        compiler_params=_cp(("parallel",)),
        name='rglru_pre',
    )(zc, zc, zc, *args)


def _lru_scan_kernel(st, af_ref, bf_ref, ab_ref, bb_ref, h0_ref, hf_ref, hb_ref, finf_ref, finb_ref,
                     h_scr, loc_scr, car_scr):
    c = pl.program_id(0)
    nh = WIDTH // LANE

    @pl.when(c == 0)
    def _():
        h_scr[:, 1:] = h0_ref[...]

    @pl.when(st.ctx_first(c))
    def _():
        h_scr[:, 0] = jnp.zeros((nh, 2, 1, LANE), F32)

    rr = LRU_ROWS
    ng = af_ref.shape[2] // rr
    chains = [(p, s, d) for s in range(st.n_seg) for d in range(2) for p in range(nh)]
    refs = ((af_ref, bf_ref, hf_ref), (ab_ref, bb_ref, hb_ref))
    order = (list(range(rr)), list(range(rr - 1, -1, -1)))

    for k, (p, s, d) in enumerate(chains):
        a_ref, b_ref, _ = refs[d]
        hloc = ploc = None
        for r in order[d]:
            a = a_ref[p, s, pl.ds(r, ng, stride=rr), :]
            b = b_ref[p, s, pl.ds(r, ng, stride=rr), :]
            hloc = b if hloc is None else a * hloc + b
            ploc = a if ploc is None else a * ploc
            loc_scr[k, 0, r] = hloc
            loc_scr[k, 1, r] = ploc

    def carry_step(j, carries):
        out = []
        for k, (p, s, d) in enumerate(chains):
            g = j if d == 0 else ng - 1 - j
            r_end = order[d][-1]
            car_scr[k, pl.ds(g, 1), :] = carries[k]
            out.append(loc_scr[k, 1, r_end, pl.ds(g, 1), :] * carries[k] + loc_scr[k, 0, r_end, pl.ds(g, 1), :])
        return tuple(out)

    carries = lax.fori_loop(0, ng, carry_step, tuple(h_scr[p, s, d] for p, s, d in chains))

    for k, (p, s, d) in enumerate(chains):
        h_scr[p, s, d] = carries[k]
        o_ref = refs[d][2]
        car = car_scr[k]
        for r in range(rr):
            o_ref[p, s, pl.ds(r, ng, stride=rr), :] = loc_scr[k, 0, r] + loc_scr[k, 1, r] * car

    @pl.when(st.ctx_last(c))
    def _():
        for p in range(nh):
            finf_ref[0, :, p * LANE:(p + 1) * LANE] = h_scr[p, 0, 0]
            finb_ref[0, :, p * LANE:(p + 1) * LANE] = h_scr[p, 0, 1]


def _lru_scan(coef, h0_dec, st):
    nh = WIDTH // LANE
    a_f, b_f, a_b, b_b = coef
    n_dec = h0_dec.shape[0]
    h0 = jnp.transpose(h0_dec.reshape(n_dec, 2, nh, 1, LANE), (2, 0, 1, 3, 4))
    n_chain = 2 * st.n_seg * nh
    ng = st.chunk // LRU_ROWS
    n = st.n
    fwd = pl.BlockSpec((nh, st.n_seg, st.chunk, LANE), lambda c: (0, 0, c, 0))
    bwd = pl.BlockSpec((nh, st.n_seg, st.chunk, LANE), lambda c: (0, 0, n - 1 - c, 0))
    h_f, h_b, fin_f, fin_b = pl.pallas_call(
        functools.partial(_lru_scan_kernel, st),
        grid=(st.n,),
        in_specs=[fwd, fwd, bwd, bwd, pl.BlockSpec(h0.shape, lambda c: (0,) * h0.ndim)],
        out_specs=[fwd, bwd, st.fin_fwd((1, WIDTH)), st.fin_bwd((1, WIDTH))],
        out_shape=[jax.ShapeDtypeStruct(a_f.shape, F32), jax.ShapeDtypeStruct(a_f.shape, F32),
                   jax.ShapeDtypeStruct((st.n_ctx, 1, WIDTH), F32), jax.ShapeDtypeStruct((st.n_ctx, 1, WIDTH), F32)],
        scratch_shapes=[pltpu.VMEM((nh, st.n_seg, 2, 1, LANE), F32),
                        pltpu.VMEM((n_chain, 2, LRU_ROWS, ng, LANE), F32),
                        pltpu.VMEM((n_chain, ng, LANE), F32)],
        compiler_params=_cp(("arbitrary",)),
        name='rglru_scan',
    )(a_f, b_f, a_b, b_b, h0)
    return h_f, h_b, jnp.concatenate([fin_f, fin_b], axis=1)


def _rope_slab(x, cos, sin):
    lane = lax.broadcasted_iota(jnp.int32, x.shape, 1)
    rot = jnp.where(lane % (B_ROPE // 2) < B_ROPE // 4, -pltpu.roll(x, LANE - B_ROPE // 4, 1),
                    pltpu.roll(x, B_ROPE // 4, 1))
    return x * cos + rot * sin


def _kv_up(ckv_bf, kpe_slab, wuk_ref, wuv_ref, k_o, v_o):
    kn = _dot(ckv_bf, wuk_ref[...])
    for h in range(N_HEADS):
        k_o[:, h * SLAB:(h + 1) * SLAB] = (kn[:, h * SLAB:(h + 1) * SLAB] + kpe_slab).astype(BF16)
    lane = lax.broadcasted_iota(jnp.int32, (1, N_HEADS * SLAB), 1)
    v_o[...] = (_dot(ckv_bf, wuv_ref[...]) + jnp.where(lane % SLAB >= B_VDIM, 1.0, 0.0)).astype(BF16)


def _mla_prep_kernel(zb_ref, cos_ref, sin_ref, qn_ref, kvn_ref, wuq_ref, wuk_ref, wuv_ref,
                     ckv_o, q_o, k_o, v_o):
    cq = zb_ref[:, 0:B_Q_RANK]
    ckv = zb_ref[:, B_Q_RANK:B_Q_RANK + B_KV_RANK]
    kpe = zb_ref[:, B_Q_RANK + B_KV_RANK:ZB]
    cos = cos_ref[...]
    sin = sin_ref[...]
    cqn = cq * lax.rsqrt(jnp.mean(cq * cq, axis=-1, keepdims=True) + RMS_EPS) * qn_ref[...]
    ckvn = ckv * lax.rsqrt(jnp.mean(ckv * ckv, axis=-1, keepdims=True) + RMS_EPS) * kvn_ref[...]
    ckv_o[...] = ckvn
    q = _dot(cqn.astype(BF16), wuq_ref[...])
    for h in range(N_HEADS):
        q_o[:, h * SLAB:(h + 1) * SLAB] = (_rope_slab(q[:, h * SLAB:(h + 1) * SLAB], cos, sin)
                                           * ATTN_LOG2_SCALE).astype(BF16)
    _kv_up(ckvn.astype(BF16), _rope_slab(kpe, cos, sin), wuk_ref, wuv_ref, k_o, v_o)


def _mla_cache_kernel(ckv_ref, kpe_ref, wuk_ref, wuv_ref, k_o, v_o):
    _kv_up(ckv_ref[...].astype(BF16), kpe_ref[...], wuk_ref, wuv_ref, k_o, v_o)


def _mla_cache(ckv, kpe_slab, p):
    rows = ckv.shape[0]
    return pl.pallas_call(
        _mla_cache_kernel,
        out_shape=[jax.ShapeDtypeStruct((rows, N_HEADS * SLAB), BF16)] * 2,
        compiler_params=pltpu.CompilerParams(vmem_limit_bytes=VMEM_LIMIT),
        name='mla_cache',
    )(ckv, kpe_slab, p['wuk'], p['wuv'])


def _attend_heads(q_ref, kv_refs, o_ref):
    tq = q_ref.shape[0]
    low = lax.broadcasted_iota(jnp.int32, (tq, SLAB), 1) < B_VDIM

    def head(h):
        qh = q_ref[:, h * SLAB:(h + 1) * SLAB]
        ss = [_dot_nt(qh, k_ref[:, h * SLAB:(h + 1) * SLAB]) for k_ref, _ in kv_refs]
        yield
        m = ss[0].max(axis=-1, keepdims=True)
        for s in ss[1:]:
            m = jnp.maximum(m, s.max(axis=-1, keepdims=True))
        es = [jnp.exp2((s - m).astype(BF16)) for s in ss]
        yield
        acc = jnp.zeros((tq, SLAB), F32)
        for e, (_, v_ref) in zip(es, kv_refs):
            acc += _dot(e, v_ref[:, h * SLAB:(h + 1) * SLAB])
        return acc / acc[:, B_VDIM:B_VDIM + 1]

    pairs = []
    for h in range(0, N_HEADS, 2):
        h0, h1 = _interleave([head(h), head(h + 1)])
        pairs.append(jnp.where(low, h0, pltpu.roll(h1, B_VDIM, 1)))
    o_ref[...] = jnp.concatenate(pairs, axis=-1)


def _attn_ctx_kernel(q_ref, k_ref, v_ref, o_ref):
    _attend_heads(q_ref, [(k_ref, v_ref)], o_ref)


def _attn_dec_kernel(q_ref, k_ref, v_ref, kc_ref, vc_ref, o_ref):
    _attend_heads(q_ref, [(k_ref, v_ref), (kc_ref, vc_ref)], o_ref)


def _attn_ctx(q, k, v, n_seq, seq_len):
    blk = lambda w: pl.BlockSpec((None, seq_len, w), lambda b: (0, b, 0))
    slabs = N_HEADS * SLAB
    return pl.pallas_call(
        _attn_ctx_kernel,
        grid=(n_seq,),
        in_specs=[blk(slabs), blk(slabs), blk(slabs)],
        out_specs=blk(WIDTH),
        out_shape=jax.ShapeDtypeStruct(q.shape[:2] + (WIDTH,), F32),
        compiler_params=_cp(("parallel",)),
        name='attn_ctx',
    )(q, k, v)


def _attn_dec_aliased_kernel(q_ref, k_ref, v_ref, kc_ref, vc_ref, y_hbm, o_ref):
    del y_hbm
    _attn_dec_kernel(q_ref, k_ref, v_ref, kc_ref, vc_ref, o_ref)


def _attn_dec(q, k, v, kc, vc, yb, n_dec, dec_len, past, tq):
    nq = dec_len // tq
    slabs = N_HEADS * SLAB
    return pl.pallas_call(
        _attn_dec_aliased_kernel,
        grid=(n_dec, nq),
        in_specs=[pl.BlockSpec((None, tq, slabs), lambda b, i: (1 + b, i, 0)),
                  pl.BlockSpec((None, dec_len, slabs), lambda b, i: (1 + b, 0, 0)),
                  pl.BlockSpec((None, dec_len, slabs), lambda b, i: (1 + b, 0, 0)),
                  pl.BlockSpec((past, slabs), lambda b, i: (b, 0)),
                  pl.BlockSpec((past, slabs), lambda b, i: (b, 0)),
                  pl.BlockSpec(memory_space=pl.ANY)],
        out_specs=pl.BlockSpec((None, tq, WIDTH), lambda b, i: (1 + b, i, 0)),
        out_shape=jax.ShapeDtypeStruct(yb.shape, F32),
        input_output_aliases={5: 0},
        compiler_params=_cp(("parallel", "arbitrary")),
        name='attn_dec',
    )(q, k, v, kc, vc, yb)


def _rope_tables(n_ctx_rows, n_dec, dec_len):
    rows = dec_len // GRID_W
    row = np.repeat(np.arange(rows, dtype=np.float32), GRID_W)
    col = np.tile(np.arange(GRID_W, dtype=np.float32), rows)
    half = B_ROPE // 2
    inv = jnp.power(ROPE_BASE, -jnp.arange(0, half, 2, dtype=F32) / half)
    ang_r = jnp.asarray(row)[:, None] * inv
    ang_c = jnp.asarray(col)[:, None] * inv
    ang = jnp.concatenate([ang_r, ang_r, ang_c, ang_c], axis=-1)
    cos = jnp.pad(jnp.cos(ang), ((0, 0), (0, LANE - B_ROPE)), constant_values=1.0)
    sin = jnp.pad(jnp.sin(ang), ((0, 0), (0, LANE - B_ROPE)))
    cos = jnp.concatenate([jnp.ones((n_ctx_rows, LANE), F32)] + [cos] * n_dec, axis=0)
    sin = jnp.concatenate([jnp.zeros((n_ctx_rows, LANE), F32)] + [sin] * n_dec, axis=0)
    return cos, sin


def _block_diag(w):
    g, n, m = w.shape[-3:]
    eye = jnp.eye(g, dtype=w.dtype)
    out = w[..., :, :, None, :] * eye[:, None, :, None]
    return out.reshape(w.shape[:-3] + (g * n, g * m))


def kernel(x_prompt, x_sample, cache_mla_ckv, cache_mla_kpe, state_rwkv, state_rglru, state_hgrn, c, c_ctx,
           w_ada, b_ada, ln_g, ln_b, w_ffn_in, w_ffn_out, w_in, w_out,
           rwkv_w0, rwkv_w2, rwkv_a0, rwkv_a2, rwkv_g2, rwkv_kk, rwkv_ka, rwkv_rk, rwkv_gn_g, rwkv_gn_b,
           mla_qn_g, mla_w_uq, mla_kvn_g, mla_w_ukv,
           rglru_conv_w, rglru_conv_b, rglru_wa, rglru_ba, rglru_wx, rglru_bx, rglru_lam,
           hgrn_lb, hgrn_gn_g):
    n_ctx, ctx_len, _ = x_prompt.shape
    n_dec, dec_len, _ = x_sample.shape
    past = cache_mla_ckv.shape[2]
    seg = n_ctx * ctx_len
    assert seg == dec_len, "context rows must form one segment of the decode sequence length"
    assert ctx_len & (ctx_len - 1) == 0 and dec_len & (dec_len - 1) == 0, "sequence lengths must be powers of two"
    n_seg = 1 + n_dec
    tl = _Tiles(n_seg, seg, min(TM, seg))
    tq = min(TQ, dec_len)
    assert ctx_len % LRU_CHUNK == 0
    st = _Streams(n_seg, n_ctx, seg, ctx_len, CHUNK)
    lru_st = _Streams(n_seg, n_ctx, seg, ctx_len, LRU_CHUNK)

    hc = {k: jnp.asarray(v) for k, v in _head_consts().items()}
    ebd = hc['mbd']
    cos, sin = (t.reshape(n_seg, seg, LANE) for t in _rope_tables(seg, n_dec, dec_len))

    cond8 = jnp.zeros((8, D_MODEL), F32).at[0].set(c_ctx).at[1:1 + n_dec].set(c)
    mods = _ada(cond8, w_ada, b_ada).reshape(DEPTH, 8, N_MOD, D_MODEL)
    mods = jnp.transpose(mods, (0, 2, 1, 3))[:, :, :n_seg, None, :]

    w_ffn_in_bf = w_ffn_in.astype(BF16)
    w_ffn_out_bf = w_ffn_out.astype(BF16)
    w_out_bf = w_out.astype(BF16)
    kpe_end = ZA + B_Q_RANK + B_KV_RANK + B_ROPE
    w_in_bf = w_in.astype(BF16)
    w_in_p = jnp.concatenate([w_in_bf[:, :, :kpe_end], jnp.zeros((DEPTH, D_MODEL, LANE - B_ROPE), BF16),
                              w_in_bf[:, :, kpe_end:]], axis=-1)
    wuq = mla_w_uq.reshape(DEPTH, B_Q_RANK, N_HEADS, B_NOPE + B_ROPE)
    wuq_p = jnp.concatenate([wuq[..., B_NOPE:], wuq[..., :B_NOPE],
                             jnp.zeros((DEPTH, B_Q_RANK, N_HEADS, SLAB - B_NOPE - B_ROPE), F32)], axis=-1)
    wuq_p = wuq_p.reshape(DEPTH, B_Q_RANK, N_HEADS * SLAB).astype(BF16)
    wukv = mla_w_ukv.reshape(DEPTH, B_KV_RANK, N_HEADS, B_NOPE + B_VDIM)
    wuk_p = jnp.concatenate([jnp.zeros((DEPTH, B_KV_RANK, N_HEADS, B_ROPE), F32), wukv[..., :B_NOPE],
                             jnp.zeros((DEPTH, B_KV_RANK, N_HEADS, SLAB - B_NOPE - B_ROPE), F32)], axis=-1)
    wuk_p = wuk_p.reshape(DEPTH, B_KV_RANK, N_HEADS * SLAB).astype(BF16)
    wuv_p = jnp.concatenate([wukv[..., B_NOPE:], jnp.zeros((DEPTH, B_KV_RANK, N_HEADS, SLAB - B_VDIM), F32)], axis=-1)
    wuv_p = wuv_p.reshape(DEPTH, B_KV_RANK, N_HEADS * SLAB).astype(BF16)
    lru_wg = jnp.concatenate([_block_diag(rglru_wa[:, 0]), _block_diag(rglru_wx[:, 0]),
                              _block_diag(rglru_wa[:, 1]), _block_diag(rglru_wx[:, 1])], axis=-1)
    lru_bg = jnp.concatenate([rglru_ba[:, 0], rglru_bx[:, 0], rglru_ba[:, 1], rglru_bx[:, 1]], axis=-1)[:, None, :]
    lbs = _lower_bounds(hgrn_lb)

    rwkv_s0 = _block_diag(state_rwkv)
    hgrn_s0 = _block_diag(jnp.swapaxes(state_hgrn, -1, -2))
    cache_kpe_slab = jnp.pad(cache_mla_kpe, ((0, 0), (0, 0), (0, 0), (0, LANE - B_ROPE)))

    new_ckv, new_kpe, new_rwkv, new_lru, new_hgrn = [], [], [], [], []
    for l in range(DEPTH):
        m = mods[l]
        lng = ln_g[l][:, None, :]
        lnb = ln_b[l][:, None, :]
        ffn0 = functools.partial(_ffn, sh=m[0], sc=m[1], g=m[2], lng=lng[0], lnb=lnb[0], w_in_bf=w_ffn_in_bf,
                                 w_out_bf=w_ffn_out_bf, l=l, f=0, tl=tl)
        if l == 0:
            x = ffn0(x_prompt.reshape(1, seg, D_MODEL), segs=(0, 1))
            x = ffn0(x_sample, segs=(1, n_dec), out_off=1, into=x)
        else:
            x = ffn0(x)
        pa = {'w0': rwkv_w0[l][:, None, :], 'w2': rwkv_w2[l], 'a0': rwkv_a0[l][:, None, :], 'a2': rwkv_a2[l],
              'g2': rwkv_g2[l], 'kk': rwkv_kk[l][None, :], 'ka': rwkv_ka[l][None, :], 'rk': rwkv_rk[l][None, :]}
        pb = {'qn_g': mla_qn_g[l][None, :], 'kvn_g': mla_kvn_g[l][None, :], 'wuq': wuq_p[l], 'wuk': wuk_p[l],
              'wuv': wuv_p[l]}
        a_r, a_v, pre, kpe_raw, (ckvn, q_all, k_all, v_all), zc, zd = _inproj(x, m[3], m[4], w_in_p, l, pa, pb,
                                                                                ebd, cos, sin, tl)

        (oa_f, oa_b, sa_fin), (od_f, od_b, sd_fin) = _mix_scan(a_r, a_v, pre[:7], zd, lbs[l], rwkv_s0[:, l],
                                                                hgrn_s0[:, l], hc, st)
        new_rwkv.append(sa_fin)
        new_hgrn.append(sd_fin)

        kc, vc = _mla_cache(cache_mla_ckv[:, l].reshape(n_dec * past, B_KV_RANK),
                            cache_kpe_slab[:, l].reshape(n_dec * past, LANE), pb)
        yb = _attn_ctx(q_all, k_all, v_all, n_ctx, ctx_len)
        yb = _attn_dec(q_all, k_all, v_all, kc, vc, yb, n_dec, dec_len, past, tq)
        new_ckv.append(ckvn[0].reshape(n_ctx, ctx_len, B_KV_RANK))
        new_kpe.append(kpe_raw[0, :, :B_ROPE].reshape(n_ctx, ctx_len, B_ROPE))

        pc = {'conv_w': rglru_conv_w[l], 'conv_b': rglru_conv_b[l][None, :], 'wg': lru_wg[l], 'bg': lru_bg[l],
              'lam': rglru_lam[l][:, None, :]}
        coef = _lru_pre(zc, pc, ctx_len, dec_len, tl)
        h_f, h_b, h_fin = _lru_scan(coef, state_rglru[:, l], lru_st)
        new_lru.append(h_fin)

        x = _outproj(x, oa_f, oa_b, pre[7], pre[8], yb, h_f, h_b, zc, od_f, od_b, zd,
                     rwkv_gn_g[l][None, :], rwkv_gn_b[l][None, :], hgrn_gn_g[l][None, :], ebd,
                     w_out_bf, m[5], lng[1], lnb[1], l, tl)
        ffn1 = functools.partial(_ffn, sh=m[6], sc=m[7], g=m[8], lng=lng[2], lnb=lnb[2], w_in_bf=w_ffn_in_bf,
                                 w_out_bf=w_ffn_out_bf, l=l, f=1, tl=tl)
        if l == DEPTH - 1:
            y_prompt = ffn1(x, segs=(0, 1), out_nseg=1).reshape(n_ctx, ctx_len, D_MODEL)
            y_sample = ffn1(x, segs=(1, n_dec), x_off=1, out_nseg=n_dec)
        else:
            x = ffn1(x)
    return (y_prompt, y_sample, jnp.stack(new_ckv, axis=1), jnp.stack(new_kpe, axis=1),
            jnp.stack(new_rwkv, axis=1), jnp.stack(new_lru, axis=1), jnp.stack(new_hgrn, axis=1))
```

```python
import functools

import numpy as np
import jax
import jax.numpy as jnp
from jax import lax
from jax.experimental import pallas as pl
from jax.experimental.pallas import tpu as pltpu

D_MODEL = 1024
DEPTH = 4
GRID_W = 64
HEAD_DIM = 64
N_HEADS = 4
WIDTH = N_HEADS * HEAD_DIM
A_DECAY_RANK = 64
A_ICLR_RANK = 64
A_GATE_RANK = 128
B_NOPE = 64
B_ROPE = 32
B_VDIM = 64
B_Q_RANK = 256
B_KV_RANK = 128
C_CONV = 4
C_POW = 8.0
D_FF = 2816
N_MOD = 9
ROPE_BASE = 10000.0
LN_EPS = 1e-5
RMS_EPS = 1e-6
RWKV_GN_EPS = 64e-5
DN_ALPHA = (2 * DEPTH) ** 0.25

LANE = 128
SLAB = 128
CHUNK = 64
SUB = 16
LRU_CHUNK = 256
LRU_ROWS = 8
TM = 512
TF = 256
VMEM_LIMIT = 56 * 1024 * 1024

F32 = jnp.float32
BF16 = jnp.bfloat16
HI = lax.Precision.HIGHEST
NEG = -1e30
LOG2E = float(np.log2(np.e))
ATTN_LOG2_SCALE = float((B_NOPE + B_ROPE) ** -0.5) * LOG2E
RWKV_HI = False


def _cp(sem):
    return pltpu.CompilerParams(dimension_semantics=sem, vmem_limit_bytes=VMEM_LIMIT)


def _dot(a, b, hi=False):
    return jnp.dot(a, b, preferred_element_type=F32, precision=HI if hi else None)


def _dot_nt(a, b, hi=False):
    return lax.dot_general(a, b, (((1,), (1,)), ((), ())), preferred_element_type=F32,
                           precision=HI if hi else None)


def _dot_tn(a, b, hi=False):
    return lax.dot_general(a, b, (((0,), (0,)), ((), ())), preferred_element_type=F32,
                           precision=HI if hi else None)


def _head_sums(x, ebd_bf):
    hi = x.astype(BF16)
    lo = (x - hi.astype(F32)).astype(BF16)
    return _dot(hi, ebd_bf) + _dot(lo, ebd_bf)


def _sigmoid(x):
    return 1.0 / (1.0 + jnp.exp(-x))


def _silu(x):
    return x * _sigmoid(x)


def _softplus(x):
    return jnp.maximum(x, 0.0) + jnp.log(1.0 + jnp.exp(-jnp.abs(x)))


def _tile4(y):
    return jnp.concatenate([y, y, y, y], axis=0)


def _head_consts():
    r = np.arange(WIDTH)
    same = (r[:, None] // HEAD_DIM) == (r[None, :] // HEAD_DIM)
    t = np.arange(CHUNK)[:, None]
    j = (np.arange(WIDTH) % CHUNK)[None, :]
    tt = np.arange(CHUNK)
    return {
        'mbd': same.astype(np.float32),
        'low_s': (j < t).astype(np.float32), 'low_i': (j <= t).astype(np.float32),
        'up_s': (j > t).astype(np.float32), 'up_i': (j >= t).astype(np.float32),
        'eye': (j == t).astype(np.float32),
        'tri_f': (tt[None, :] <= tt[:, None]).astype(np.float32),
        'tri_b': (tt[None, :] >= tt[:, None]).astype(np.float32),
    }


def _lb_kernel(x_ref, o_ref):
    x = x_ref[...]
    m = jnp.max(x, axis=0, keepdims=True)
    e = jnp.exp(x - m)
    sm = e / jnp.sum(e, axis=0, keepdims=True)
    run = sm[0:1]
    rows = [run - sm[0:1]]
    for l in range(1, DEPTH):
        run = run + sm[l:l + 1]
        rows.append(run - sm[0:1])
    o_ref[...] = jnp.concatenate(rows, axis=0)


def _lower_bounds(hgrn_lb):
    flat = hgrn_lb.reshape(DEPTH, 2 * WIDTH)
    out = pl.pallas_call(_lb_kernel, out_shape=jax.ShapeDtypeStruct(flat.shape, F32), name='hgrn_lb')(flat)
    return out.reshape(DEPTH, 2, 1, WIDTH)


def _ada_kernel(c_ref, w_ref, b_ref, o_ref):
    h = _silu(c_ref[...]).astype(BF16)
    o_ref[0] = _dot(h, w_ref[0].astype(BF16)) + b_ref[0]


def _ada(cond8, w_ada, b_ada):
    n = N_MOD * D_MODEL
    tn = 2304
    return pl.pallas_call(
        _ada_kernel,
        grid=(DEPTH, n // tn),
        in_specs=[pl.BlockSpec((8, D_MODEL), lambda l, j: (0, 0)),
                  pl.BlockSpec((1, D_MODEL, tn), lambda l, j: (l, 0, j)),
                  pl.BlockSpec((1, 1, tn), lambda l, j: (l, 0, j))],
        out_specs=pl.BlockSpec((1, 8, tn), lambda l, j: (l, 0, j)),
        out_shape=jax.ShapeDtypeStruct((DEPTH, 8, n), F32),
        compiler_params=_cp(("arbitrary", "arbitrary")),
        name='ada',
    )(cond8, w_ada, b_ada.reshape(DEPTH, 1, n))


def _layer_norm_rows(y, g, b):
    mu = jnp.mean(y, axis=-1, keepdims=True)
    d = y - mu
    var = jnp.mean(d * d, axis=-1, keepdims=True)
    return d * lax.rsqrt(var + LN_EPS) * g + b


def _ffn_tile(x, sh_ref, sc_ref, g_ref, lng_ref, lnb_ref, win_ref, wout_ref):
    h = (x * (1.0 + sc_ref[0]) + sh_ref[0]).astype(BF16)
    n_chunks = D_FF // TF

    def gate_up(c):
        return (_dot(h, win_ref[:, c * TF:(c + 1) * TF]),
                _dot(h, win_ref[:, D_FF + c * TF:D_FF + (c + 1) * TF]))

    y = jnp.zeros(x.shape, F32)
    cur = gate_up(0)
    for c in range(n_chunks):
        nxt = gate_up(c + 1) if c + 1 < n_chunks else None
        a = (_silu(cur[0]) * cur[1]).astype(BF16)
        y = y + _dot(a, wout_ref[c * TF:(c + 1) * TF, :])
        cur = nxt
    return _layer_norm_rows(DN_ALPHA * x + 0.5 * g_ref[0] * y, lng_ref[...], lnb_ref[...])


def _ffn_kernel(x_ref, *refs):
    o_ref = refs[-1]
    o_ref[...] = _ffn_tile(x_ref[...], *refs[:-1])


def _ffn_first_kernel(ctx_tiles, xp_ref, xs_ref, *refs):
    o_ref = refs[-1]
    x = jnp.where(pl.program_id(0) < ctx_tiles, xp_ref[...], xs_ref[...])
    o_ref[...] = _ffn_tile(x, *refs[:-1])


def _ffn_last_kernel(ctx_tiles, x_ref, *refs):
    yp_ref, ys_ref = refs[-2:]
    y = _ffn_tile(x_ref[...], *refs[:-2])
    is_ctx = pl.program_id(0) < ctx_tiles

    @pl.when(is_ctx)
    def _():
        yp_ref[...] = y

    @pl.when(jnp.logical_not(is_ctx))
    def _():
        ys_ref[...] = y


class _Tiles:
    def __init__(self, n_seg, seg_rows, tm):
        self.n_seg, self.seg_rows, self.tm = n_seg, seg_rows, tm
        self.per_seg = seg_rows // tm
        self.grid = (n_seg * self.per_seg,)

    def row(self, width, col=0):
        ps = self.per_seg
        return pl.BlockSpec((None, self.tm, width), lambda i: (i // ps, i % ps, col))

    def mod(self):
        ps = self.per_seg
        return pl.BlockSpec((1, 1, D_MODEL), lambda i: (i // ps, 0, 0))

    def shape(self, width, dtype=F32):
        return jax.ShapeDtypeStruct((self.n_seg, self.seg_rows, width), dtype)


def _full(a):
    return pl.BlockSpec(a.shape, lambda i: (0,) * a.ndim)


def _ffn(x, sh, sc, g, lng, lnb, w_in_bf, w_out_bf, l, f, tl, split=None):
    ps, tm = tl.per_seg, tl.tm
    mod = tl.mod()
    vec = pl.BlockSpec((1, D_MODEL), lambda i: (0, 0))
    ctx_blk = pl.BlockSpec((None, tm, D_MODEL), lambda i: (0, jnp.minimum(i, ps - 1), 0))
    dec_blk = pl.BlockSpec((None, tm, D_MODEL),
                           lambda i: (jnp.maximum(i - ps, 0) // ps, jnp.maximum(i - ps, 0) % ps, 0))
    weights = [pl.BlockSpec((None, None, D_MODEL, 2 * D_FF), lambda i: (l, f, 0, 0)),
               pl.BlockSpec((None, None, D_FF, D_MODEL), lambda i: (l, f, 0, 0))]
    rest = [mod, mod, mod, vec, vec] + weights
    args = (sh, sc, g, lng, lnb, w_in_bf, w_out_bf)
    n_dec = tl.n_seg - 1
    if split == 'in':
        body, xs, x_specs = functools.partial(_ffn_first_kernel, ps), tuple(x), [ctx_blk, dec_blk]
    else:
        body, xs, x_specs = _ffn_kernel, (x,), [tl.row(D_MODEL)]
    if split == 'out':
        body = functools.partial(_ffn_last_kernel, ps)
        out_specs = [ctx_blk, dec_blk]
        out_shape = [jax.ShapeDtypeStruct((1, tl.seg_rows, D_MODEL), F32),
                     jax.ShapeDtypeStruct((n_dec, tl.seg_rows, D_MODEL), F32)]
    else:
        out_specs, out_shape = tl.row(D_MODEL), tl.shape(D_MODEL)
    return pl.pallas_call(
        body,
        grid=tl.grid,
        in_specs=x_specs + rest,
        out_specs=out_specs,
        out_shape=out_shape,
        compiler_params=_cp(("arbitrary",) if split == 'out' else ("parallel",)),
        name='ffn',
    )(*xs, *args)


ZA = 3 * WIDTH + 2 * A_DECAY_RANK + 2 * A_ICLR_RANK + A_GATE_RANK
ZB = B_Q_RANK + B_KV_RANK + LANE
ZC = 2 * WIDTH
ZD = 5 * WIDTH
ZTOT = ZA + ZB + ZC + ZD


N_RWKV_PRE = 9
N_MLA_PREP = 4


def _inproj_kernel(x_ref, sh_ref, sc_ref, w_ref, *rest):
    n_a, n_b = 9, 7
    a_par, b_par = rest[:n_a], rest[n_a:n_a + n_b]
    outs = rest[n_a + n_b:]
    r_o, v_o = outs[0:2]
    a_out = outs[2:2 + N_RWKV_PRE]
    kpe_o = outs[2 + N_RWKV_PRE]
    b_out = outs[3 + N_RWKV_PRE:3 + N_RWKV_PRE + N_MLA_PREP]
    zc_ref, zd_ref, za_scr, zb_scr = outs[3 + N_RWKV_PRE + N_MLA_PREP:]
    h = (x_ref[...] * (1.0 + sc_ref[0]) + sh_ref[0]).astype(BF16)
    za_scr[...] = _dot(h, w_ref[:, 0:ZA])
    zb_scr[...] = _dot(h, w_ref[:, ZA:ZA + ZB])
    zc_ref[...] = _dot(h, w_ref[:, ZA + ZB:ZA + ZB + ZC])
    zd_ref[...] = _dot(h, w_ref[:, ZA + ZB + ZC:ZTOT])
    r_o[...] = za_scr[:, 0:WIDTH]
    v_o[...] = za_scr[:, 2 * WIDTH:3 * WIDTH]
    kpe_o[...] = zb_scr[:, B_Q_RANK + B_KV_RANK:ZB]
    _rwkv_pre_kernel(za_scr, *a_par, *a_out)
    _mla_prep_kernel(zb_scr, *b_par, *b_out)


def _inproj(x, sh, sc, w_in_p, l, pa, pb, ebd, cos, sin, tl):
    mod = tl.mod()
    a_par = (pa['w0'], pa['w2'], pa['a0'], pa['a2'], pa['g2'], pa['kk'], pa['ka'], pa['rk'], ebd)
    b_par = (pb['qn_g'], pb['kvn_g'], pb['wuq'], pb['wuk'], pb['wuv'])
    slabs = N_HEADS * SLAB
    w = WIDTH
    out_specs = ([tl.row(w)] * (2 + N_RWKV_PRE) + [tl.row(LANE), tl.row(B_KV_RANK)] + [tl.row(slabs)] * 3
                 + [tl.row(ZC), tl.row(ZD)])
    out_shape = ([tl.shape(w)] * (2 + N_RWKV_PRE) + [tl.shape(LANE), tl.shape(B_KV_RANK)]
                 + [tl.shape(slabs, BF16)] * 3 + [tl.shape(ZC), tl.shape(ZD)])
    outs = pl.pallas_call(
        _inproj_kernel,
        grid=tl.grid,
        in_specs=[tl.row(D_MODEL), mod, mod, pl.BlockSpec((None, D_MODEL, ZTOT), lambda i: (l, 0, 0))]
                 + [_full(a) for a in a_par] + [tl.row(LANE), tl.row(LANE)] + [_full(a) for a in b_par],
        out_specs=out_specs,
        out_shape=out_shape,
        scratch_shapes=[pltpu.VMEM((tl.tm, ZA), F32), pltpu.VMEM((tl.tm, ZB), F32)],
        compiler_params=_cp(("parallel",)),
        name='mixer_in',
    )(x, sh, sc, w_in_p, *a_par, cos, sin, *b_par)
    r, v = outs[0:2]
    pre = outs[2:2 + N_RWKV_PRE]
    kpe = outs[2 + N_RWKV_PRE]
    mla = outs[3 + N_RWKV_PRE:3 + N_RWKV_PRE + N_MLA_PREP]
    zc, zd = outs[-2:]
    return r, v, pre, kpe, mla, zc, zd


def _gelu_tanh(x):
    return 0.5 * x * (1.0 + jnp.tanh(np.sqrt(2.0 / np.pi) * (x + 0.044715 * (x * x * x))))


def _outproj_kernel(x_ref, af_ref, ab_ref, bon_ref, gate_ref, yb_ref, hf_ref, hb_ref, cg_ref, df_ref, db_ref,
                    dg_ref, agn_g_ref, agn_b_ref, dgn_g_ref, ebd_ref, w_ref, g_ref, lng_ref, lnb_ref, o_ref):
    avg = (ebd_ref[...] * (1.0 / HEAD_DIM)).astype(BF16)
    oa = af_ref[...] + ab_ref[...]
    da = oa - _head_sums(oa, avg)
    gn = da * lax.rsqrt(_head_sums(da * da, avg) + RWKV_GN_EPS) * agn_g_ref[...] + agn_b_ref[...]
    ya = (gn + bon_ref[...]) * gate_ref[...]
    h = jnp.concatenate([hf_ref[p] + hb_ref[p] for p in range(WIDTH // LANE)], axis=-1)
    yc = h * _gelu_tanh(cg_ref[...])
    od = df_ref[...] + db_ref[...]
    yd = od * lax.rsqrt(_head_sums(od * od, avg) + RMS_EPS) * dgn_g_ref[...] * _silu(dg_ref[...])
    y = _dot(ya.astype(BF16), w_ref[0:WIDTH, :])
    y += _dot(yb_ref[...].astype(BF16), w_ref[WIDTH:2 * WIDTH, :])
    y += _dot(yc.astype(BF16), w_ref[2 * WIDTH:3 * WIDTH, :])
    y += _dot(yd.astype(BF16), w_ref[3 * WIDTH:4 * WIDTH, :])
    o_ref[...] = _layer_norm_rows(DN_ALPHA * x_ref[...] + g_ref[0] * y, lng_ref[...], lnb_ref[...])


def _outproj(x, oa_f, oa_b, bonus, gate, yb, hf, hb, zc, od_f, od_b, zd, agn_g, agn_b, dgn_g, ebd,
             w_out_bf, g, lng, lnb, l, tl):
    vec = pl.BlockSpec((1, D_MODEL), lambda i: (0, 0))
    wvec = pl.BlockSpec((1, WIDTH), lambda i: (0, 0))
    ps = tl.per_seg
    halves = pl.BlockSpec((WIDTH // LANE, None, tl.tm, LANE), lambda i: (0, i // ps, i % ps, 0))
    w = WIDTH
    return pl.pallas_call(
        _outproj_kernel,
        grid=tl.grid,
        in_specs=[tl.row(D_MODEL), tl.row(w), tl.row(w), tl.row(w), tl.row(w), tl.row(w), halves, halves,
                  tl.row(w, 1), tl.row(w), tl.row(w), tl.row(w, 4), wvec, wvec, wvec, _full(ebd),
                  pl.BlockSpec((None, 4 * WIDTH, D_MODEL), lambda i: (l, 0, 0)), tl.mod(), vec, vec],
        out_specs=tl.row(D_MODEL),
        out_shape=tl.shape(D_MODEL),
        compiler_params=_cp(("parallel",)),
        name='mixer_out',
    )(x, oa_f, oa_b, bonus, gate, yb, hf, hb, zc, od_f, od_b, zd, agn_g, agn_b, dgn_g, ebd, w_out_bf, g, lng, lnb)


class _Streams:
    def __init__(self, n_seg, n_ctx, seg_rows, ctx_len, chunk):
        self.n_seg, self.n_ctx, self.chunk = n_seg, n_ctx, chunk
        self.n = seg_rows // chunk
        self.ctx_n = ctx_len // chunk

    def fwd(self, width, col=0):
        return pl.BlockSpec((self.n_seg, self.chunk, width), lambda c: (0, c, col))

    def bwd(self, width, col=0):
        n = self.n
        return pl.BlockSpec((self.n_seg, self.chunk, width), lambda c: (0, n - 1 - c, col))

    def fin_fwd(self, shape):
        ctx_n = self.ctx_n
        return pl.BlockSpec((1,) + shape, lambda c: (c // ctx_n,) + (0,) * len(shape))

    def fin_bwd(self, shape):
        ctx_n, n_ctx = self.ctx_n, self.n_ctx
        return pl.BlockSpec((1,) + shape, lambda c: (n_ctx - 1 - c // ctx_n,) + (0,) * len(shape))

    def ctx_first(self, c):
        return c % self.ctx_n == 0

    def ctx_last(self, c):
        return c % self.ctx_n == self.ctx_n - 1


def _rwkv_pre_kernel(za_ref, w0_ref, w2_ref, a0_ref, a2_ref, g2_ref, kkp_ref, kap_ref, rkp_ref, ebd_ref,
                     kap_o, lwf_o, lwb_o, kdf_o, kdb_o, bbf_o, bbb_o, bon_o, gate_o):
    r = za_ref[:, 0:WIDTH]
    k = za_ref[:, WIDTH:2 * WIDTH]
    v = za_ref[:, 2 * WIDTH:3 * WIDTH]
    o = 3 * WIDTH
    xw = (za_ref[:, o:o + A_DECAY_RANK], za_ref[:, o + A_DECAY_RANK:o + 2 * A_DECAY_RANK])
    o += 2 * A_DECAY_RANK
    xa = (za_ref[:, o:o + A_ICLR_RANK], za_ref[:, o + A_ICLR_RANK:o + 2 * A_ICLR_RANK])
    o += 2 * A_ICLR_RANK
    xg = za_ref[:, o:o + A_GATE_RANK]
    ebd = ebd_ref[...].astype(BF16)

    kk = k * kkp_ref[...]
    nrm = jnp.sqrt(_head_sums(kk * kk, ebd))
    kappa = kk / jnp.maximum(nrm, 1e-12)
    kap_o[...] = kappa
    k_sum = jnp.zeros_like(r)
    for d, (lw_o, kd_o, bb_o) in enumerate(((lwf_o, kdf_o, bbf_o), (lwb_o, kdb_o, bbb_o))):
        w_log = -_softplus(-(w0_ref[d] + _dot(jnp.tanh(xw[d]).astype(BF16), w2_ref[d].astype(BF16)))) - 0.5
        lw_o[...] = -jnp.exp(w_log)
        a = _sigmoid(a0_ref[d] + _dot(xa[d].astype(BF16), a2_ref[d].astype(BF16)))
        k_d = k * (1.0 + (a - 1.0) * kap_ref[...])
        kd_o[...] = k_d
        bb_o[...] = kappa * a
        k_sum += k_d
    bon_o[...] = _head_sums(r * k_sum * rkp_ref[...], ebd) * v
    gate_o[...] = _dot(_sigmoid(xg).astype(BF16), g2_ref[...].astype(BF16))


def _rwkv_chunk(r, v, kap, lw, kd, bb, st, mbd, eye, m_strict, m_incl, tri):
    cum = _dot(tri, lw, hi=True)
    yield
    tot = jnp.sum(lw, axis=0, keepdims=True)
    g_in = jnp.exp(cum)
    g_inv = jnp.exp(-cum)
    g_ex = jnp.exp(cum - lw)
    g_end = jnp.exp(tot - cum)
    cast = (lambda a: a) if RWKV_HI else (lambda a: a.astype(BF16))
    mask = cast(mbd)
    bd = lambda y: _tile4(cast(y)) * mask
    x = cast(jnp.concatenate([kap * g_ex, r * g_in], axis=0))
    ab = _dot_nt(x, bd(bb * g_inv), hi=RWKV_HI)
    ak = _dot_nt(x, bd(kd * g_inv), hi=RWKV_HI)
    xs = _dot_nt(x, cast(st), hi=RWKV_HI)
    yield
    c = r.shape[0]
    strict = m_strict > 0.5
    incl = m_incl > 0.5
    a_ub = jnp.where(strict, ab[:c], 0.0)
    a_rb = jnp.where(incl, ab[c:], 0.0)
    a_uk = jnp.where(strict, ak[:c], 0.0)
    a_rk = jnp.where(incl, ak[c:], 0.0)

    def catmul(pc, q):
        return _dot(cast(pc), bd(q), hi=RWKV_HI)

    xs = xs + catmul(jnp.concatenate([a_uk, a_rk], axis=0), v)
    rhs, o_v = xs[:c], xs[c:]
    xp = -a_ub
    inv = eye + xp
    xp = catmul(xp, xp)
    yield
    for _ in range(int(np.log2(c)) - 2):
        both = catmul(jnp.concatenate([inv, xp], axis=0), xp)
        yield
        inv = inv + both[:c]
        xp = both[c:]
    inv = inv + catmul(inv, xp)
    yield
    u = -catmul(inv, rhs)
    yield
    o = o_v + catmul(a_rb, u)
    upd = _dot_tn(cast(jnp.concatenate([u, v], axis=0)),
                  cast(jnp.concatenate([bb * g_end, kd * g_end], axis=0)), hi=RWKV_HI)
    st_new = st * jnp.exp(tot) + upd * mbd
    return o, st_new


def _interleave(gens):
    results = [None] * len(gens)
    live = list(range(len(gens)))
    while live:
        still = []
        for k in live:
            try:
                next(gens[k])
                still.append(k)
            except StopIteration as done:
                results[k] = done.value
        live = still
    return results


def _init_states(st, c, st_scr, s0_ref):
    @pl.when(c == 0)
    def _():
        st_scr[1:] = s0_ref[...]

    @pl.when(st.ctx_first(c))
    def _():
        st_scr[0] = jnp.zeros(st_scr.shape[1:], F32)


def _hgrn_chunk(xq, xf, xi, lb, st, tri, ebd, mbd, p_scr, reverse):
    c = xq.shape[0]
    nb = c // SUB
    q = _silu(xq)
    gsig = lb + (1.0 - lb) * _sigmoid(xf)
    kk = 1.0 - gsig
    lg = jnp.log(gsig)
    cum = _dot(tri, lg, hi=True)
    yield
    cum = cum * LOG2E
    tot = jnp.sum(lg, axis=0, keepdims=True) * LOG2E
    lane_s = lax.broadcasted_iota(jnp.int32, (SUB, WIDTH), 1) % c
    half = SUB // 2
    row_h = lax.broadcasted_iota(jnp.int32, (half, WIDTH), 0)
    blk = lambda a, i: a[i * SUB:(i + 1) * SUB]
    end_row = (lambda j: j * SUB) if reverse else (lambda j: j * SUB + SUB - 1)
    later = (lambda j: range(0, j)) if reverse else (lambda j: range(j + 1, nb))

    k_end = jnp.concatenate([blk(kk, j) * jnp.exp2(cum[end_row(j):end_row(j) + 1] - blk(cum, j))
                             for j in range(nb)], axis=0)
    q_parts, where_part = [], {}
    for j in range(nb):
        for i in later(j):
            where_part[(i, j)] = len(q_parts)
            q_parts.append(blk(q, i) * jnp.exp2(blk(cum, i) - cum[end_row(j):end_row(j) + 1]))
    cross = _dot_nt(jnp.concatenate(q_parts, axis=0).astype(BF16), (_tile4(k_end) * mbd).astype(BF16))
    o_state = _dot_nt((q * jnp.exp2(cum)).astype(BF16), st.astype(BF16))
    upd = _dot_tn(xi.astype(BF16), (kk * jnp.exp2(tot - cum)).astype(BF16))
    yield

    for i in range(nb):
        cb, qb = blk(cum, i), blk(q, i)
        for sl in range(SUB):
            s = i * SUB + sl
            parts = []
            for lo in (0, half):
                hi = lo + half - 1
                none_valid = lo > sl if reverse else hi < sl
                all_valid = hi <= sl if reverse else lo >= sl
                if none_valid:
                    parts.append(jnp.zeros((half, WIDTH), F32))
                    continue
                d = cb[lo:lo + half] - cum[s:s + 1]
                if not all_valid:
                    valid = (row_h + lo <= sl) if reverse else (row_h + lo >= sl)
                    d = jnp.where(valid, d, NEG)
                parts.append(jnp.exp2(d) * qb[lo:lo + half] * kk[s:s + 1])
            p_scr[s * SUB:(s + 1) * SUB, :] = jnp.concatenate(parts, axis=0).astype(BF16)
    same = _dot(p_scr[...], ebd.astype(BF16))
    yield

    att_rows = []
    for i in range(nb):
        att = jnp.zeros((SUB, WIDTH), F32)
        for sl in range(SUB):
            s = i * SUB + sl
            att = jnp.where(lane_s == s, same[s * SUB:(s + 1) * SUB], att)
        for j in range(nb):
            if (i, j) in where_part:
                n = where_part[(i, j)]
                att = jnp.where(lane_s // SUB == j, cross[n * SUB:(n + 1) * SUB], att)
        att_rows.append(att)
    att = jnp.concatenate(att_rows, axis=0)
    o = o_state + _dot(att.astype(BF16), (_tile4(xi) * mbd).astype(BF16))
    st_new = st * jnp.exp2(tot) + upd * mbd
    return o, st_new


def _mix_scan_kernel(st,
                     rf_ref, vf_ref, kapf_ref, lwf_ref, kdf_ref, bbf_ref,
                     rb_ref, vb_ref, kapb_ref, lwb_ref, kdb_ref, bbb_ref,
                     qf_ref, ff_ref, if_ref, qb_ref, fb_ref, ib_ref, lb_ref,
                     sa0_ref, sd0_ref, mbd_ref, eye_ref, lows_ref, lowi_ref, ups_ref, upi_ref, trif_ref, trib_ref,
                     af_ref, ab_ref, afin_f, afin_b, df_ref, db_ref, dfin_f, dfin_b,
                     sa_scr, sd_scr, p_scr):
    c = pl.program_id(0)
    _init_states(st, c, sa_scr, sa0_ref)
    _init_states(st, c, sd_scr, sd0_ref)
    mbd = mbd_ref[...]
    eye = eye_ref[...]
    gens = []
    for s in range(st.n_seg):
        gens.append(_rwkv_chunk(rf_ref[s], vf_ref[s], kapf_ref[s], lwf_ref[s], kdf_ref[s], bbf_ref[s],
                                sa_scr[s, 0], mbd, eye, lows_ref[...], lowi_ref[...], trif_ref[...]))
        gens.append(_hgrn_chunk(qf_ref[s], ff_ref[s], if_ref[s], lb_ref[0], sd_scr[s, 0], trif_ref[...],
                                mbd, mbd, p_scr.at[s, 0], False))
        gens.append(_rwkv_chunk(rb_ref[s], vb_ref[s], kapb_ref[s], lwb_ref[s], kdb_ref[s], bbb_ref[s],
                                sa_scr[s, 1], mbd, eye, ups_ref[...], upi_ref[...], trib_ref[...]))
        gens.append(_hgrn_chunk(qb_ref[s], fb_ref[s], ib_ref[s], lb_ref[1], sd_scr[s, 1], trib_ref[...],
                                mbd, mbd, p_scr.at[s, 1], True))
    outs = ((af_ref, sa_scr), (df_ref, sd_scr), (ab_ref, sa_scr), (db_ref, sd_scr))
    for k, (o, st_new) in enumerate(_interleave(gens)):
        s, kind = k // 4, k % 4
        o_ref, scr = outs[kind]
        o_ref[s] = o
        scr[s, kind // 2] = st_new

    @pl.when(st.ctx_last(c))
    def _():
        for h in range(N_HEADS):
            hs = slice(h * HEAD_DIM, (h + 1) * HEAD_DIM)
            afin_f[0, h] = sa_scr[0, 0, hs, hs]
            afin_b[0, h] = sa_scr[0, 1, hs, hs]
            dfin_f[0, h] = sd_scr[0, 0, hs, hs].T
            dfin_b[0, h] = sd_scr[0, 1, hs, hs].T


def _mix_scan(r, v, pre, zd, lb, sa0_dec, sd0_dec, hc, st):
    kap, lwf, lwb, kdf, kdb, bbf, bbb = pre
    consts = (hc['mbd'], hc['eye'], hc['low_s'], hc['low_i'], hc['up_s'], hc['up_i'], hc['tri_f'], hc['tri_b'])
    w = WIDTH
    o_shape = jax.ShapeDtypeStruct(kap.shape, F32)
    head_blocks = (N_HEADS, HEAD_DIM, HEAD_DIM)
    fin_shape = jax.ShapeDtypeStruct((st.n_ctx,) + head_blocks, F32)
    fin_specs = [st.fin_fwd(head_blocks), st.fin_bwd(head_blocks)]
    outs = pl.pallas_call(
        functools.partial(_mix_scan_kernel, st),
        grid=(st.n,),
        in_specs=[st.fwd(w)] * 6 + [st.bwd(w)] * 6 + [
                  st.fwd(w, 0), st.fwd(w, 1), st.fwd(w, 3), st.bwd(w, 0), st.bwd(w, 2), st.bwd(w, 3),
                  _full(lb), _full(sa0_dec), _full(sd0_dec)] + [_full(a) for a in consts],
        out_specs=[st.fwd(w), st.bwd(w)] + fin_specs + [st.fwd(w), st.bwd(w)] + fin_specs,
        out_shape=[o_shape, o_shape, fin_shape, fin_shape] * 2,
        scratch_shapes=[pltpu.VMEM((st.n_seg, 2, w, w), F32), pltpu.VMEM((st.n_seg, 2, w, w), F32),
                        pltpu.VMEM((st.n_seg, 2, CHUNK * SUB, w), BF16)],
        compiler_params=_cp(("arbitrary",)),
        name='mix_scan',
    )(r, v, kap, lwf, kdf, bbf, r, v, kap, lwb, kdb, bbb, zd, zd, zd, zd, zd, zd, lb, sa0_dec, sd0_dec, *consts)
    a_f, a_b, afin_f, afin_b, d_f, d_b, dfin_f, dfin_b = outs
    return ((a_f, a_b, jnp.stack([afin_f, afin_b], axis=1)), (d_f, d_b, jnp.stack([dfin_f, dfin_b], axis=1)))


def _lru_pre_kernel(seq_of_tile, xp_ref, x_ref, xn_ref, cw_ref, cb_ref, wg_ref, bg_ref, lam_ref,
                    af_o, bf_o, ab_o, bb_o, pad_scr):
    i = pl.program_id(0)
    tm = x_ref.shape[0]
    seqlen = seq_of_tile(i)
    pad_scr[0:8, :] = xp_ref[:, 0:WIDTH]
    pad_scr[8:8 + tm, :] = x_ref[:, 0:WIDTH]
    pad_scr[8 + tm:16 + tm, :] = xn_ref[:, 0:WIDTH]
    pos = jnp.bitwise_and(lax.broadcasted_iota(jnp.int32, (tm, WIDTH), 0) + i * tm, seqlen - 1)
    u = jnp.zeros((tm, WIDTH), F32) + cb_ref[...]
    for j in range(C_CONV):
        off = j - C_CONV // 2
        tap = pad_scr[pl.ds(8 + off, tm), :]
        ok = jnp.logical_and(pos + off >= 0, pos + off < seqlen)
        u += jnp.where(ok, tap, 0.0) * cw_ref[j:j + 1, :]
    gates = _sigmoid(_dot(u.astype(BF16), wg_ref[...].astype(BF16)) + bg_ref[...])
    for d, (a_o, b_o) in enumerate(((af_o, bf_o), (ab_o, bb_o))):
        r = gates[:, (2 * d) * WIDTH:(2 * d + 1) * WIDTH]
        ig = gates[:, (2 * d + 1) * WIDTH:(2 * d + 2) * WIDTH]
        log_a = -C_POW * r * _softplus(-lam_ref[d])
        a = jnp.exp(log_a)
        b = jnp.sqrt(-jnp.tanh(log_a) * (a * a + 1.0)) * (ig * u)
        for half in range(WIDTH // LANE):
            a_o[half] = a[:, half * LANE:(half + 1) * LANE]
            b_o[half] = b[:, half * LANE:(half + 1) * LANE]


def _lru_pre(zc, p, ctx_len, dec_len, tl):
    ps, tm = tl.per_seg, tl.tm
    per = tm // 8
    last8 = tl.seg_rows // 8 - 1
    seq_of_tile = lambda i: jnp.where(i < ps, ctx_len, dec_len)
    args = (p['conv_w'], p['conv_b'], p['wg'], p['bg'], p['lam'])
    nh = WIDTH // LANE
    prev8 = pl.BlockSpec((None, 8, WIDTH), lambda i: (i // ps, jnp.maximum((i % ps) * per - 1, 0), 0))
    next8 = pl.BlockSpec((None, 8, WIDTH), lambda i: (i // ps, jnp.minimum((i % ps + 1) * per, last8), 0))
    return pl.pallas_call(
        functools.partial(_lru_pre_kernel, seq_of_tile),
        grid=tl.grid,
        in_specs=[prev8, tl.row(WIDTH), next8] + [_full(a) for a in args],
        out_specs=[pl.BlockSpec((nh, None, tm, LANE), lambda i: (0, i // ps, i % ps, 0))] * 4,
        out_shape=[jax.ShapeDtypeStruct((nh, tl.n_seg, tl.seg_rows, LANE), F32)] * 4,
        scratch_shapes=[pltpu.VMEM((tm + 16, WIDTH), F32)],
        compiler_params=_cp(("parallel",)),
        name='rglru_pre',
    )(zc, zc, zc, *args)


def _lru_scan_kernel(st, af_ref, bf_ref, ab_ref, bb_ref, h0_ref, hf_ref, hb_ref, finf_ref, finb_ref,
                     h_scr, loc_scr, car_scr):
    c = pl.program_id(0)
    nh = WIDTH // LANE

    @pl.when(c == 0)
    def _():
        h_scr[:, 1:] = h0_ref[...]

    @pl.when(st.ctx_first(c))
    def _():
        h_scr[:, 0] = jnp.zeros((nh, 2, 1, LANE), F32)

    rr = LRU_ROWS
    ng = af_ref.shape[2] // rr
    chains = [(p, s, d) for s in range(st.n_seg) for d in range(2) for p in range(nh)]
    refs = ((af_ref, bf_ref, hf_ref), (ab_ref, bb_ref, hb_ref))
    order = (list(range(rr)), list(range(rr - 1, -1, -1)))

    for k, (p, s, d) in enumerate(chains):
        a_ref, b_ref, _ = refs[d]
        hloc = ploc = None
        for r in order[d]:
            a = a_ref[p, s, pl.ds(r, ng, stride=rr), :]
            b = b_ref[p, s, pl.ds(r, ng, stride=rr), :]
            hloc = b if hloc is None else a * hloc + b
            ploc = a if ploc is None else a * ploc
            loc_scr[k, 0, r] = hloc
            loc_scr[k, 1, r] = ploc

    def carry_step(j, carries):
        out = []
        for k, (p, s, d) in enumerate(chains):
            g = j if d == 0 else ng - 1 - j
            r_end = order[d][-1]
            car_scr[k, pl.ds(g, 1), :] = carries[k]
            out.append(loc_scr[k, 1, r_end, pl.ds(g, 1), :] * carries[k] + loc_scr[k, 0, r_end, pl.ds(g, 1), :])
        return tuple(out)

    carries = lax.fori_loop(0, ng, carry_step, tuple(h_scr[p, s, d] for p, s, d in chains))

    for k, (p, s, d) in enumerate(chains):
        h_scr[p, s, d] = carries[k]
        o_ref = refs[d][2]
        car = car_scr[k]
        for r in range(rr):
            o_ref[p, s, pl.ds(r, ng, stride=rr), :] = loc_scr[k, 0, r] + loc_scr[k, 1, r] * car

    @pl.when(st.ctx_last(c))
    def _():
        for p in range(nh):
            finf_ref[0, :, p * LANE:(p + 1) * LANE] = h_scr[p, 0, 0]
            finb_ref[0, :, p * LANE:(p + 1) * LANE] = h_scr[p, 0, 1]


def _lru_scan(coef, h0_dec, st):
    nh = WIDTH // LANE
    a_f, b_f, a_b, b_b = coef
    n_dec = h0_dec.shape[0]
    h0 = jnp.transpose(h0_dec.reshape(n_dec, 2, nh, 1, LANE), (2, 0, 1, 3, 4))
    n_chain = 2 * st.n_seg * nh
    ng = st.chunk // LRU_ROWS
    n = st.n
    fwd = pl.BlockSpec((nh, st.n_seg, st.chunk, LANE), lambda c: (0, 0, c, 0))
    bwd = pl.BlockSpec((nh, st.n_seg, st.chunk, LANE), lambda c: (0, 0, n - 1 - c, 0))
    h_f, h_b, fin_f, fin_b = pl.pallas_call(
        functools.partial(_lru_scan_kernel, st),
        grid=(st.n,),
        in_specs=[fwd, fwd, bwd, bwd, pl.BlockSpec(h0.shape, lambda c: (0,) * h0.ndim)],
        out_specs=[fwd, bwd, st.fin_fwd((1, WIDTH)), st.fin_bwd((1, WIDTH))],
        out_shape=[jax.ShapeDtypeStruct(a_f.shape, F32), jax.ShapeDtypeStruct(a_f.shape, F32),
                   jax.ShapeDtypeStruct((st.n_ctx, 1, WIDTH), F32), jax.ShapeDtypeStruct((st.n_ctx, 1, WIDTH), F32)],
        scratch_shapes=[pltpu.VMEM((nh, st.n_seg, 2, 1, LANE), F32),
                        pltpu.VMEM((n_chain, 2, LRU_ROWS, ng, LANE), F32),
                        pltpu.VMEM((n_chain, ng, LANE), F32)],
        compiler_params=_cp(("arbitrary",)),
        name='rglru_scan',
    )(a_f, b_f, a_b, b_b, h0)
    return h_f, h_b, jnp.concatenate([fin_f, fin_b], axis=1)


def _rope_slab(x, cos, sin):
    lane = lax.broadcasted_iota(jnp.int32, x.shape, 1)
    rot = jnp.where(lane % (B_ROPE // 2) < B_ROPE // 4, -pltpu.roll(x, LANE - B_ROPE // 4, 1),
                    pltpu.roll(x, B_ROPE // 4, 1))
    return x * cos + rot * sin


def _kv_up(ckv_bf, kpe_slab, wuk_ref, wuv_ref, k_o, v_o):
    kn = _dot(ckv_bf, wuk_ref[...])
    for h in range(N_HEADS):
        k_o[:, h * SLAB:(h + 1) * SLAB] = (kn[:, h * SLAB:(h + 1) * SLAB] + kpe_slab).astype(BF16)
    lane = lax.broadcasted_iota(jnp.int32, (1, N_HEADS * SLAB), 1)
    v_o[...] = (_dot(ckv_bf, wuv_ref[...]) + jnp.where(lane % SLAB >= B_VDIM, 1.0, 0.0)).astype(BF16)


def _mla_prep_kernel(zb_ref, cos_ref, sin_ref, qn_ref, kvn_ref, wuq_ref, wuk_ref, wuv_ref,
                     ckv_o, q_o, k_o, v_o):
    cq = zb_ref[:, 0:B_Q_RANK]
    ckv = zb_ref[:, B_Q_RANK:B_Q_RANK + B_KV_RANK]
    kpe = zb_ref[:, B_Q_RANK + B_KV_RANK:ZB]
    cos = cos_ref[...]
    sin = sin_ref[...]
    cqn = cq * lax.rsqrt(jnp.mean(cq * cq, axis=-1, keepdims=True) + RMS_EPS) * qn_ref[...]
    ckvn = ckv * lax.rsqrt(jnp.mean(ckv * ckv, axis=-1, keepdims=True) + RMS_EPS) * kvn_ref[...]
    ckv_o[...] = ckvn
    q = _dot(cqn.astype(BF16), wuq_ref[...])
    for h in range(N_HEADS):
        q_o[:, h * SLAB:(h + 1) * SLAB] = (_rope_slab(q[:, h * SLAB:(h + 1) * SLAB], cos, sin)
                                           * ATTN_LOG2_SCALE).astype(BF16)
    _kv_up(ckvn.astype(BF16), _rope_slab(kpe, cos, sin), wuk_ref, wuv_ref, k_o, v_o)


def _mla_cache_kernel(ckv_ref, kpe_ref, wuk_ref, wuv_ref, k_o, v_o):
    _kv_up(ckv_ref[...].astype(BF16), kpe_ref[...], wuk_ref, wuv_ref, k_o, v_o)


def _mla_cache(ckv, kpe_slab, p):
    rows = ckv.shape[0]
    return pl.pallas_call(
        _mla_cache_kernel,
        out_shape=[jax.ShapeDtypeStruct((rows, N_HEADS * SLAB), BF16)] * 2,
        compiler_params=pltpu.CompilerParams(vmem_limit_bytes=VMEM_LIMIT),
        name='mla_cache',
    )(ckv, kpe_slab, p['wuk'], p['wuv'])


def _attend_heads(q_ref, kv_refs, o_ref):
    tq = q_ref.shape[0]
    low = lax.broadcasted_iota(jnp.int32, (tq, SLAB), 1) < B_VDIM

    def head(h):
        qh = q_ref[:, h * SLAB:(h + 1) * SLAB]
        ss = [_dot_nt(qh, k_ref[:, h * SLAB:(h + 1) * SLAB]) for k_ref, _ in kv_refs]
        yield
        m = ss[0].max(axis=-1, keepdims=True)
        for s in ss[1:]:
            m = jnp.maximum(m, s.max(axis=-1, keepdims=True))
        es = [jnp.exp2((s - m).astype(BF16)) for s in ss]
        yield
        acc = jnp.zeros((tq, SLAB), F32)
        for e, (_, v_ref) in zip(es, kv_refs):
            acc += _dot(e, v_ref[:, h * SLAB:(h + 1) * SLAB])
        return acc / acc[:, B_VDIM:B_VDIM + 1]

    pairs = []
    for h in range(0, N_HEADS, 2):
        h0, h1 = _interleave([head(h), head(h + 1)])
        pairs.append(jnp.where(low, h0, pltpu.roll(h1, B_VDIM, 1)))
    o_ref[...] = jnp.concatenate(pairs, axis=-1)


def _attn_kernel(n_ctx, q_ref, kx_ref, vx_ref, k_ref, v_ref, kc_ref, vc_ref, o_ref):
    i = pl.program_id(0)

    @pl.when(i < n_ctx)
    def _():
        _attend_heads(q_ref, [(kx_ref, vx_ref)], o_ref)

    @pl.when(i >= n_ctx)
    def _():
        _attend_heads(q_ref, [(k_ref, v_ref), (kc_ref, vc_ref)], o_ref)


def _attn(q, k, v, kc, vc, n_ctx, ctx_len, n_dec, dec_len, past):
    tq = ctx_len
    nq = dec_len // tq
    slabs = N_HEADS * SLAB
    dec = lambda i: jnp.maximum(i - n_ctx, 0)

    def q_map(i):
        return (jnp.where(i < n_ctx, 0, 1 + dec(i) // nq), jnp.where(i < n_ctx, i, dec(i) % nq), 0)

    ctx_kv = pl.BlockSpec((None, ctx_len, slabs), lambda i: (0, jnp.minimum(i, n_ctx - 1), 0))
    dec_kv = pl.BlockSpec((None, dec_len, slabs), lambda i: (1 + dec(i) // nq, 0, 0))
    cache_kv = pl.BlockSpec((past, slabs), lambda i: (dec(i) // nq, 0))
    return pl.pallas_call(
        functools.partial(_attn_kernel, n_ctx),
        grid=(n_ctx + n_dec * nq,),
        in_specs=[pl.BlockSpec((None, tq, slabs), q_map), ctx_kv, ctx_kv, dec_kv, dec_kv, cache_kv, cache_kv],
        out_specs=pl.BlockSpec((None, tq, WIDTH), q_map),
        out_shape=jax.ShapeDtypeStruct(q.shape[:2] + (WIDTH,), F32),
        compiler_params=_cp(("arbitrary",)),
        name='attn',
    )(q, k, v, k, v, kc, vc)


def _rope_tables(n_ctx_rows, n_dec, dec_len):
    rows = dec_len // GRID_W
    row = np.repeat(np.arange(rows, dtype=np.float32), GRID_W)
    col = np.tile(np.arange(GRID_W, dtype=np.float32), rows)
    half = B_ROPE // 2
    inv = jnp.power(ROPE_BASE, -jnp.arange(0, half, 2, dtype=F32) / half)
    ang_r = jnp.asarray(row)[:, None] * inv
    ang_c = jnp.asarray(col)[:, None] * inv
    ang = jnp.concatenate([ang_r, ang_r, ang_c, ang_c], axis=-1)
    cos = jnp.pad(jnp.cos(ang), ((0, 0), (0, LANE - B_ROPE)), constant_values=1.0)
    sin = jnp.pad(jnp.sin(ang), ((0, 0), (0, LANE - B_ROPE)))
    cos = jnp.concatenate([jnp.ones((n_ctx_rows, LANE), F32)] + [cos] * n_dec, axis=0)
    sin = jnp.concatenate([jnp.zeros((n_ctx_rows, LANE), F32)] + [sin] * n_dec, axis=0)
    return cos, sin


def _block_diag(w):
    g, n, m = w.shape[-3:]
    eye = jnp.eye(g, dtype=w.dtype)
    out = w[..., :, :, None, :] * eye[:, None, :, None]
    return out.reshape(w.shape[:-3] + (g * n, g * m))


def kernel(x_prompt, x_sample, cache_mla_ckv, cache_mla_kpe, state_rwkv, state_rglru, state_hgrn, c, c_ctx,
           w_ada, b_ada, ln_g, ln_b, w_ffn_in, w_ffn_out, w_in, w_out,
           rwkv_w0, rwkv_w2, rwkv_a0, rwkv_a2, rwkv_g2, rwkv_kk, rwkv_ka, rwkv_rk, rwkv_gn_g, rwkv_gn_b,
           mla_qn_g, mla_w_uq, mla_kvn_g, mla_w_ukv,
           rglru_conv_w, rglru_conv_b, rglru_wa, rglru_ba, rglru_wx, rglru_bx, rglru_lam,
           hgrn_lb, hgrn_gn_g):
    n_ctx, ctx_len, _ = x_prompt.shape
    n_dec, dec_len, _ = x_sample.shape
    past = cache_mla_ckv.shape[2]
    seg = n_ctx * ctx_len
    assert seg == dec_len, "context rows must form one segment of the decode sequence length"
    assert ctx_len & (ctx_len - 1) == 0 and dec_len & (dec_len - 1) == 0, "sequence lengths must be powers of two"
    n_seg = 1 + n_dec
    tl = _Tiles(n_seg, seg, min(TM, seg))
    assert ctx_len % LRU_CHUNK == 0
    st = _Streams(n_seg, n_ctx, seg, ctx_len, CHUNK)
    lru_st = _Streams(n_seg, n_ctx, seg, ctx_len, LRU_CHUNK)

    hc = {k: jnp.asarray(v) for k, v in _head_consts().items()}
    ebd = hc['mbd']
    cos, sin = (t.reshape(n_seg, seg, LANE) for t in _rope_tables(seg, n_dec, dec_len))

    cond8 = jnp.zeros((8, D_MODEL), F32).at[0].set(c_ctx).at[1:1 + n_dec].set(c)
    mods = _ada(cond8, w_ada, b_ada).reshape(DEPTH, 8, N_MOD, D_MODEL)
    mods = jnp.transpose(mods, (0, 2, 1, 3))[:, :, :n_seg, None, :]

    w_ffn_in_bf = w_ffn_in.astype(BF16)
    w_ffn_out_bf = w_ffn_out.astype(BF16)
    w_out_bf = w_out.astype(BF16)
    kpe_end = ZA + B_Q_RANK + B_KV_RANK + B_ROPE
    w_in_bf = w_in.astype(BF16)
    w_in_p = jnp.concatenate([w_in_bf[:, :, :kpe_end], jnp.zeros((DEPTH, D_MODEL, LANE - B_ROPE), BF16),
                              w_in_bf[:, :, kpe_end:]], axis=-1)
    wuq = mla_w_uq.reshape(DEPTH, B_Q_RANK, N_HEADS, B_NOPE + B_ROPE)
    wuq_p = jnp.concatenate([wuq[..., B_NOPE:], wuq[..., :B_NOPE],
                             jnp.zeros((DEPTH, B_Q_RANK, N_HEADS, SLAB - B_NOPE - B_ROPE), F32)], axis=-1)
    wuq_p = wuq_p.reshape(DEPTH, B_Q_RANK, N_HEADS * SLAB).astype(BF16)
    wukv = mla_w_ukv.reshape(DEPTH, B_KV_RANK, N_HEADS, B_NOPE + B_VDIM)
    wuk_p = jnp.concatenate([jnp.zeros((DEPTH, B_KV_RANK, N_HEADS, B_ROPE), F32), wukv[..., :B_NOPE],
                             jnp.zeros((DEPTH, B_KV_RANK, N_HEADS, SLAB - B_NOPE - B_ROPE), F32)], axis=-1)
    wuk_p = wuk_p.reshape(DEPTH, B_KV_RANK, N_HEADS * SLAB).astype(BF16)
    wuv_p = jnp.concatenate([wukv[..., B_NOPE:], jnp.zeros((DEPTH, B_KV_RANK, N_HEADS, SLAB - B_VDIM), F32)], axis=-1)
    wuv_p = wuv_p.reshape(DEPTH, B_KV_RANK, N_HEADS * SLAB).astype(BF16)
    lru_wg = jnp.concatenate([_block_diag(rglru_wa[:, 0]), _block_diag(rglru_wx[:, 0]),
                              _block_diag(rglru_wa[:, 1]), _block_diag(rglru_wx[:, 1])], axis=-1)
    lru_bg = jnp.concatenate([rglru_ba[:, 0], rglru_bx[:, 0], rglru_ba[:, 1], rglru_bx[:, 1]], axis=-1)[:, None, :]
    lbs = _lower_bounds(hgrn_lb)

    rwkv_s0 = _block_diag(state_rwkv)
    hgrn_s0 = _block_diag(jnp.swapaxes(state_hgrn, -1, -2))
    cache_kpe_slab = jnp.pad(cache_mla_kpe, ((0, 0), (0, 0), (0, 0), (0, LANE - B_ROPE)))

    new_ckv, new_kpe, new_rwkv, new_lru, new_hgrn = [], [], [], [], []
    for l in range(DEPTH):
        m = mods[l]
        lng = ln_g[l][:, None, :]
        lnb = ln_b[l][:, None, :]
        ffn0 = functools.partial(_ffn, sh=m[0], sc=m[1], g=m[2], lng=lng[0], lnb=lnb[0], w_in_bf=w_ffn_in_bf,
                                 w_out_bf=w_ffn_out_bf, l=l, f=0, tl=tl)
        if l == 0:
            x = ffn0((x_prompt.reshape(1, seg, D_MODEL), x_sample), split='in')
        else:
            x = ffn0(x)
        pa = {'w0': rwkv_w0[l][:, None, :], 'w2': rwkv_w2[l], 'a0': rwkv_a0[l][:, None, :], 'a2': rwkv_a2[l],
              'g2': rwkv_g2[l], 'kk': rwkv_kk[l][None, :], 'ka': rwkv_ka[l][None, :], 'rk': rwkv_rk[l][None, :]}
        pb = {'qn_g': mla_qn_g[l][None, :], 'kvn_g': mla_kvn_g[l][None, :], 'wuq': wuq_p[l], 'wuk': wuk_p[l],
              'wuv': wuv_p[l]}
        a_r, a_v, pre, kpe_raw, (ckvn, q_all, k_all, v_all), zc, zd = _inproj(x, m[3], m[4], w_in_p, l, pa, pb,
                                                                                ebd, cos, sin, tl)

        (oa_f, oa_b, sa_fin), (od_f, od_b, sd_fin) = _mix_scan(a_r, a_v, pre[:7], zd, lbs[l], rwkv_s0[:, l],
                                                                hgrn_s0[:, l], hc, st)
        new_rwkv.append(sa_fin)
        new_hgrn.append(sd_fin)

        kc, vc = _mla_cache(cache_mla_ckv[:, l].reshape(n_dec * past, B_KV_RANK),
                            cache_kpe_slab[:, l].reshape(n_dec * past, LANE), pb)
        yb = _attn(q_all, k_all, v_all, kc, vc, n_ctx, ctx_len, n_dec, dec_len, past)
        new_ckv.append(ckvn[0].reshape(n_ctx, ctx_len, B_KV_RANK))
        new_kpe.append(kpe_raw[0, :, :B_ROPE].reshape(n_ctx, ctx_len, B_ROPE))

        pc = {'conv_w': rglru_conv_w[l], 'conv_b': rglru_conv_b[l][None, :], 'wg': lru_wg[l], 'bg': lru_bg[l],
              'lam': rglru_lam[l][:, None, :]}
        coef = _lru_pre(zc, pc, ctx_len, dec_len, tl)
        h_f, h_b, h_fin = _lru_scan(coef, state_rglru[:, l], lru_st)
        new_lru.append(h_fin)

        x = _outproj(x, oa_f, oa_b, pre[7], pre[8], yb, h_f, h_b, zc, od_f, od_b, zd,
                     rwkv_gn_g[l][None, :], rwkv_gn_b[l][None, :], hgrn_gn_g[l][None, :], ebd,
                     w_out_bf, m[5], lng[1], lnb[1], l, tl)
        ffn1 = functools.partial(_ffn, sh=m[6], sc=m[7], g=m[8], lng=lng[2], lnb=lnb[2], w_in_bf=w_ffn_in_bf,
                                 w_out_bf=w_ffn_out_bf, l=l, f=1, tl=tl)
        if l == DEPTH - 1:
            y_prompt, y_sample = ffn1(x, split='out')
            y_prompt = y_prompt.reshape(n_ctx, ctx_len, D_MODEL)
        else:
            x = ffn1(x)
    return (y_prompt, y_sample, jnp.stack(new_ckv, axis=1), jnp.stack(new_kpe, axis=1),
            jnp.stack(new_rwkv, axis=1), jnp.stack(new_lru, axis=1), jnp.stack(new_hgrn, axis=1))
```

```python
import functools

import numpy as np
import jax
import jax.numpy as jnp
from jax import lax
from jax.experimental import pallas as pl
from jax.experimental.pallas import tpu as pltpu

D_MODEL = 1024
DEPTH = 4
GRID_W = 64
HEAD_DIM = 64
N_HEADS = 4
WIDTH = N_HEADS * HEAD_DIM
A_DECAY_RANK = 64
A_ICLR_RANK = 64
A_GATE_RANK = 128
B_NOPE = 64
B_ROPE = 32
B_VDIM = 64
B_Q_RANK = 256
B_KV_RANK = 128
C_CONV = 4
C_POW = 8.0
D_FF = 2816
N_MOD = 9
ROPE_BASE = 10000.0
LN_EPS = 1e-5
RMS_EPS = 1e-6
RWKV_GN_EPS = 64e-5
DN_ALPHA = (2 * DEPTH) ** 0.25

LANE = 128
SLAB = 128
CHUNK = 64
SUB = 16
LRU_CHUNK = 256
LRU_ROWS = 8
TM = 512
TF = 256
VMEM_LIMIT = 56 * 1024 * 1024

F32 = jnp.float32
BF16 = jnp.bfloat16
HI = lax.Precision.HIGHEST
NEG = -1e30
LOG2E = float(np.log2(np.e))
ATTN_LOG2_SCALE = float((B_NOPE + B_ROPE) ** -0.5) * LOG2E
RWKV_HI = False


def _cp(sem):
    return pltpu.CompilerParams(dimension_semantics=sem, vmem_limit_bytes=VMEM_LIMIT)


def _dot(a, b, hi=False):
    return jnp.dot(a, b, preferred_element_type=F32, precision=HI if hi else None)


def _dot_nt(a, b, hi=False):
    return lax.dot_general(a, b, (((1,), (1,)), ((), ())), preferred_element_type=F32,
                           precision=HI if hi else None)


def _dot_tn(a, b, hi=False):
    return lax.dot_general(a, b, (((0,), (0,)), ((), ())), preferred_element_type=F32,
                           precision=HI if hi else None)


def _head_sums(x, ebd_bf):
    hi = x.astype(BF16)
    lo = (x - hi.astype(F32)).astype(BF16)
    return _dot(hi, ebd_bf) + _dot(lo, ebd_bf)


def _sigmoid(x):
    return 1.0 / (1.0 + jnp.exp(-x))


def _silu(x):
    return x * _sigmoid(x)


def _softplus(x):
    return jnp.maximum(x, 0.0) + jnp.log(1.0 + jnp.exp(-jnp.abs(x)))


def _tile4(y):
    return jnp.concatenate([y, y, y, y], axis=0)


_PAIRS = tuple(slice(p * LANE, (p + 1) * LANE) for p in range(WIDTH // LANE))


def _pair_bd(y, mask2):
    return [jnp.concatenate([y[:, sl]] * (LANE // HEAD_DIM), axis=0) * mask2 for sl in _PAIRS]


def _pair_dot(a, rhs_pairs, hi=False):
    return jnp.concatenate([_dot(a[:, sl], r, hi) for sl, r in zip(_PAIRS, rhs_pairs)], axis=-1)


def _pair_dot_nt(a, rhs_pairs, hi=False):
    return jnp.concatenate([_dot_nt(a[:, sl], r, hi) for sl, r in zip(_PAIRS, rhs_pairs)], axis=-1)


def _pair_blocks(blocks):
    z = jnp.zeros_like(blocks[0])
    rows = [jnp.concatenate([blocks[p] if q == p else z for q in range(len(blocks))], axis=-1)
            for p in range(len(blocks))]
    return jnp.concatenate(rows, axis=0)


def _head_consts():
    r = np.arange(WIDTH)
    same = (r[:, None] // HEAD_DIM) == (r[None, :] // HEAD_DIM)
    t = np.arange(CHUNK)[:, None]
    j = (np.arange(WIDTH) % CHUNK)[None, :]
    tt = np.arange(CHUNK)
    return {
        'mbd': same.astype(np.float32),
        'low_s': (j < t).astype(np.float32), 'low_i': (j <= t).astype(np.float32),
        'up_s': (j > t).astype(np.float32), 'up_i': (j >= t).astype(np.float32),
        'eye': (j == t).astype(np.float32),
        'tri_f': (tt[None, :] <= tt[:, None]).astype(np.float32),
        'tri_b': (tt[None, :] >= tt[:, None]).astype(np.float32),
    }


def _lb_kernel(x_ref, o_ref):
    x = x_ref[...]
    m = jnp.max(x, axis=0, keepdims=True)
    e = jnp.exp(x - m)
    sm = e / jnp.sum(e, axis=0, keepdims=True)
    run = sm[0:1]
    rows = [run - sm[0:1]]
    for l in range(1, DEPTH):
        run = run + sm[l:l + 1]
        rows.append(run - sm[0:1])
    o_ref[...] = jnp.concatenate(rows, axis=0)


def _lower_bounds(hgrn_lb):
    flat = hgrn_lb.reshape(DEPTH, 2 * WIDTH)
    out = pl.pallas_call(_lb_kernel, out_shape=jax.ShapeDtypeStruct(flat.shape, F32), name='hgrn_lb')(flat)
    return out.reshape(DEPTH, 2, 1, WIDTH)


def _ada_kernel(c_ref, w_ref, b_ref, o_ref):
    h = _silu(c_ref[...]).astype(BF16)
    o_ref[0] = _dot(h, w_ref[0].astype(BF16)) + b_ref[0]


def _ada(cond8, w_ada, b_ada):
    n = N_MOD * D_MODEL
    tn = 2304
    return pl.pallas_call(
        _ada_kernel,
        grid=(DEPTH, n // tn),
        in_specs=[pl.BlockSpec((8, D_MODEL), lambda l, j: (0, 0)),
                  pl.BlockSpec((1, D_MODEL, tn), lambda l, j: (l, 0, j)),
                  pl.BlockSpec((1, 1, tn), lambda l, j: (l, 0, j))],
        out_specs=pl.BlockSpec((1, 8, tn), lambda l, j: (l, 0, j)),
        out_shape=jax.ShapeDtypeStruct((DEPTH, 8, n), F32),
        compiler_params=_cp(("arbitrary", "arbitrary")),
        name='ada',
    )(cond8, w_ada, b_ada.reshape(DEPTH, 1, n))


def _layer_norm_rows(y, g, b):
    mu = jnp.mean(y, axis=-1, keepdims=True)
    d = y - mu
    var = jnp.mean(d * d, axis=-1, keepdims=True)
    return d * lax.rsqrt(var + LN_EPS) * g + b


def _ffn_tile(x, sh_ref, sc_ref, g_ref, lng_ref, lnb_ref, win_ref, wout_ref):
    h = (x * (1.0 + sc_ref[0]) + sh_ref[0]).astype(BF16)
    n_chunks = D_FF // TF

    def gate_up(c):
        return (_dot(h, win_ref[:, c * TF:(c + 1) * TF]),
                _dot(h, win_ref[:, D_FF + c * TF:D_FF + (c + 1) * TF]))

    y = jnp.zeros(x.shape, F32)
    cur = gate_up(0)
    for c in range(n_chunks):
        nxt = gate_up(c + 1) if c + 1 < n_chunks else None
        a = (_silu(cur[0]) * cur[1]).astype(BF16)
        y = y + _dot(a, wout_ref[c * TF:(c + 1) * TF, :])
        cur = nxt
    return _layer_norm_rows(DN_ALPHA * x + 0.5 * g_ref[0] * y, lng_ref[...], lnb_ref[...])


def _ffn_kernel(x_ref, *refs):
    o_ref = refs[-1]
    o_ref[...] = _ffn_tile(x_ref[...], *refs[:-1])


def _ffn_first_kernel(ctx_tiles, xp_ref, xs_ref, *refs):
    o_ref = refs[-1]
    x = jnp.where(pl.program_id(0) < ctx_tiles, xp_ref[...], xs_ref[...])
    o_ref[...] = _ffn_tile(x, *refs[:-1])


def _ffn_last_kernel(ctx_tiles, x_ref, *refs):
    yp_ref, ys_ref = refs[-2:]
    y = _ffn_tile(x_ref[...], *refs[:-2])
    is_ctx = pl.program_id(0) < ctx_tiles

    @pl.when(is_ctx)
    def _():
        yp_ref[...] = y

    @pl.when(jnp.logical_not(is_ctx))
    def _():
        ys_ref[...] = y


class _Tiles:
    def __init__(self, n_seg, seg_rows, tm):
        self.n_seg, self.seg_rows, self.tm = n_seg, seg_rows, tm
        self.per_seg = seg_rows // tm
        self.grid = (n_seg * self.per_seg,)

    def row(self, width, col=0):
        ps = self.per_seg
        return pl.BlockSpec((None, self.tm, width), lambda i: (i // ps, i % ps, col))

    def mod(self):
        ps = self.per_seg
        return pl.BlockSpec((1, 1, D_MODEL), lambda i: (i // ps, 0, 0))

    def shape(self, width, dtype=F32):
        return jax.ShapeDtypeStruct((self.n_seg, self.seg_rows, width), dtype)


def _full(a):
    return pl.BlockSpec(a.shape, lambda i: (0,) * a.ndim)


def _ffn(x, sh, sc, g, lng, lnb, w_in_bf, w_out_bf, l, f, tl, split=None):
    ps, tm = tl.per_seg, tl.tm
    mod = tl.mod()
    vec = pl.BlockSpec((1, D_MODEL), lambda i: (0, 0))
    ctx_blk = pl.BlockSpec((None, tm, D_MODEL), lambda i: (0, jnp.minimum(i, ps - 1), 0))
    dec_blk = pl.BlockSpec((None, tm, D_MODEL),
                           lambda i: (jnp.maximum(i - ps, 0) // ps, jnp.maximum(i - ps, 0) % ps, 0))
    weights = [pl.BlockSpec((None, None, D_MODEL, 2 * D_FF), lambda i: (l, f, 0, 0)),
               pl.BlockSpec((None, None, D_FF, D_MODEL), lambda i: (l, f, 0, 0))]
    rest = [mod, mod, mod, vec, vec] + weights
    args = (sh, sc, g, lng, lnb, w_in_bf, w_out_bf)
    n_dec = tl.n_seg - 1
    if split == 'in':
        body, xs, x_specs = functools.partial(_ffn_first_kernel, ps), tuple(x), [ctx_blk, dec_blk]
    else:
        body, xs, x_specs = _ffn_kernel, (x,), [tl.row(D_MODEL)]
    if split == 'out':
        body = functools.partial(_ffn_last_kernel, ps)
        out_specs = [ctx_blk, dec_blk]
        out_shape = [jax.ShapeDtypeStruct((1, tl.seg_rows, D_MODEL), F32),
                     jax.ShapeDtypeStruct((n_dec, tl.seg_rows, D_MODEL), F32)]
    else:
        out_specs, out_shape = tl.row(D_MODEL), tl.shape(D_MODEL)
    return pl.pallas_call(
        body,
        grid=tl.grid,
        in_specs=x_specs + rest,
        out_specs=out_specs,
        out_shape=out_shape,
        compiler_params=_cp(("arbitrary",) if split == 'out' else ("parallel",)),
        name='ffn',
    )(*xs, *args)


ZA = 3 * WIDTH + 2 * A_DECAY_RANK + 2 * A_ICLR_RANK + A_GATE_RANK
ZB = B_Q_RANK + B_KV_RANK + LANE
ZC = 2 * WIDTH
ZD = 5 * WIDTH


N_RWKV_PRE = 9
N_MLA_PREP = 4


def _inproj_kernel(x_ref, sh_ref, sc_ref, wab_ref, wcd_ref, *rest):
    n_a, n_b = 9, 7
    a_par, b_par = rest[:n_a], rest[n_a:n_a + n_b]
    outs = rest[n_a + n_b:]
    r_o, v_o = outs[0:2]
    a_out = outs[2:2 + N_RWKV_PRE]
    kpe_o = outs[2 + N_RWKV_PRE]
    b_out = outs[3 + N_RWKV_PRE:3 + N_RWKV_PRE + N_MLA_PREP]
    zc_ref, zd_ref, za_scr, zb_scr = outs[3 + N_RWKV_PRE + N_MLA_PREP:]
    h = (x_ref[...] * (1.0 + sc_ref[0]) + sh_ref[0]).astype(BF16)
    za_scr[...] = _dot(h, wab_ref[:, 0:ZA])
    zb_scr[...] = _dot(h, wab_ref[:, ZA:ZA + ZB])
    r_o[...] = za_scr[:, 0:WIDTH]
    v_o[...] = za_scr[:, 2 * WIDTH:3 * WIDTH]
    _rwkv_pre_kernel(za_scr, *a_par, *a_out)
    zc_ref[...] = _dot(h, wcd_ref[:, 0:ZC])
    kpe_o[...] = zb_scr[:, B_Q_RANK + B_KV_RANK:ZB]
    _mla_prep_kernel(zb_scr, *b_par, *b_out)
    zd_ref[...] = _dot(h, wcd_ref[:, ZC:ZC + ZD])


def _inproj(x, sh, sc, w_ab, w_cd, l, pa, pb, ebd, cos, sin, tl):
    mod = tl.mod()
    a_par = (pa['w0'], pa['w2'], pa['a0'], pa['a2'], pa['g2'], pa['kk'], pa['ka'], pa['rk'], ebd)
    b_par = (pb['qn_g'], pb['kvn_g'], pb['wuq'], pb['wuk'], pb['wuv'])
    slabs = N_HEADS * SLAB
    w = WIDTH
    out_specs = ([tl.row(w)] * (2 + N_RWKV_PRE) + [tl.row(LANE), tl.row(B_KV_RANK)] + [tl.row(slabs)] * 3
                 + [tl.row(ZC), tl.row(ZD)])
    out_shape = ([tl.shape(w)] * (2 + N_RWKV_PRE) + [tl.shape(LANE), tl.shape(B_KV_RANK)]
                 + [tl.shape(slabs, BF16)] * 3 + [tl.shape(ZC), tl.shape(ZD)])
    outs = pl.pallas_call(
        _inproj_kernel,
        grid=tl.grid,
        in_specs=[tl.row(D_MODEL), mod, mod, pl.BlockSpec((None, D_MODEL, ZA + ZB), lambda i: (l, 0, 0)),
                  pl.BlockSpec((None, D_MODEL, ZC + ZD), lambda i: (l, 0, 0))]
                 + [_full(a) for a in a_par] + [tl.row(LANE), tl.row(LANE)] + [_full(a) for a in b_par],
        out_specs=out_specs,
        out_shape=out_shape,
        scratch_shapes=[pltpu.VMEM((tl.tm, ZA), F32), pltpu.VMEM((tl.tm, ZB), F32)],
        compiler_params=_cp(("parallel",)),
        name='mixer_in',
    )(x, sh, sc, w_ab, w_cd, *a_par, cos, sin, *b_par)
    r, v = outs[0:2]
    pre = outs[2:2 + N_RWKV_PRE]
    kpe = outs[2 + N_RWKV_PRE]
    mla = outs[3 + N_RWKV_PRE:3 + N_RWKV_PRE + N_MLA_PREP]
    zc, zd = outs[-2:]
    return r, v, pre, kpe, mla, zc, zd


def _gelu_tanh(x):
    return 0.5 * x * (1.0 + jnp.tanh(np.sqrt(2.0 / np.pi) * (x + 0.044715 * (x * x * x))))


def _outproj_kernel(x_ref, af_ref, ab_ref, bon_ref, gate_ref, yb_ref, hf_ref, hb_ref, cg_ref, df_ref, db_ref,
                    dg_ref, agn_g_ref, agn_b_ref, dgn_g_ref, ebd_ref, w_ref, g_ref, lng_ref, lnb_ref, o_ref):
    avg = (ebd_ref[...] * (1.0 / HEAD_DIM)).astype(BF16)
    oa = af_ref[...] + ab_ref[...]
    da = oa - _head_sums(oa, avg)
    gn = da * lax.rsqrt(_head_sums(da * da, avg) + RWKV_GN_EPS) * agn_g_ref[...] + agn_b_ref[...]
    ya = (gn + bon_ref[...]) * gate_ref[...]
    h = jnp.concatenate([hf_ref[p] + hb_ref[p] for p in range(WIDTH // LANE)], axis=-1)
    yc = h * _gelu_tanh(cg_ref[...])
    od = df_ref[...] + db_ref[...]
    yd = od * lax.rsqrt(_head_sums(od * od, avg) + RMS_EPS) * dgn_g_ref[...] * _silu(dg_ref[...])
    y = _dot(ya.astype(BF16), w_ref[0:WIDTH, :])
    y += _dot(yb_ref[...].astype(BF16), w_ref[WIDTH:2 * WIDTH, :])
    y += _dot(yc.astype(BF16), w_ref[2 * WIDTH:3 * WIDTH, :])
    y += _dot(yd.astype(BF16), w_ref[3 * WIDTH:4 * WIDTH, :])
    o_ref[...] = _layer_norm_rows(DN_ALPHA * x_ref[...] + g_ref[0] * y, lng_ref[...], lnb_ref[...])


def _outproj(x, oa_f, oa_b, bonus, gate, yb, hf, hb, zc, od_f, od_b, zd, agn_g, agn_b, dgn_g, ebd,
             w_out_bf, g, lng, lnb, l, tl):
    vec = pl.BlockSpec((1, D_MODEL), lambda i: (0, 0))
    wvec = pl.BlockSpec((1, WIDTH), lambda i: (0, 0))
    ps = tl.per_seg
    halves = pl.BlockSpec((WIDTH // LANE, None, tl.tm, LANE), lambda i: (0, i // ps, i % ps, 0))
    w = WIDTH
    return pl.pallas_call(
        _outproj_kernel,
        grid=tl.grid,
        in_specs=[tl.row(D_MODEL), tl.row(w), tl.row(w), tl.row(w), tl.row(w), tl.row(w), halves, halves,
                  tl.row(w, 1), tl.row(w), tl.row(w), tl.row(w, 4), wvec, wvec, wvec, _full(ebd),
                  pl.BlockSpec((None, 4 * WIDTH, D_MODEL), lambda i: (l, 0, 0)), tl.mod(), vec, vec],
        out_specs=tl.row(D_MODEL),
        out_shape=tl.shape(D_MODEL),
        compiler_params=_cp(("parallel",)),
        name='mixer_out',
    )(x, oa_f, oa_b, bonus, gate, yb, hf, hb, zc, od_f, od_b, zd, agn_g, agn_b, dgn_g, ebd, w_out_bf, g, lng, lnb)


class _Streams:
    def __init__(self, n_seg, n_ctx, seg_rows, ctx_len, chunk):
        self.n_seg, self.n_ctx, self.chunk = n_seg, n_ctx, chunk
        self.n = seg_rows // chunk
        self.ctx_n = ctx_len // chunk

    def fwd(self, width, col=0):
        return pl.BlockSpec((self.n_seg, self.chunk, width), lambda c: (0, c, col))

    def bwd(self, width, col=0):
        n = self.n
        return pl.BlockSpec((self.n_seg, self.chunk, width), lambda c: (0, n - 1 - c, col))

    def fin_fwd(self, shape):
        ctx_n = self.ctx_n
        return pl.BlockSpec((1,) + shape, lambda c: (c // ctx_n,) + (0,) * len(shape))

    def fin_bwd(self, shape):
        ctx_n, n_ctx = self.ctx_n, self.n_ctx
        return pl.BlockSpec((1,) + shape, lambda c: (n_ctx - 1 - c // ctx_n,) + (0,) * len(shape))

    def ctx_first(self, c):
        return c % self.ctx_n == 0

    def ctx_last(self, c):
        return c % self.ctx_n == self.ctx_n - 1


def _rwkv_pre_kernel(za_ref, w0_ref, w2_ref, a0_ref, a2_ref, g2_ref, kkp_ref, kap_ref, rkp_ref, ebd_ref,
                     kap_o, lwf_o, lwb_o, kdf_o, kdb_o, bbf_o, bbb_o, bon_o, gate_o):
    r = za_ref[:, 0:WIDTH]
    k = za_ref[:, WIDTH:2 * WIDTH]
    v = za_ref[:, 2 * WIDTH:3 * WIDTH]
    o = 3 * WIDTH
    xw = (za_ref[:, o:o + A_DECAY_RANK], za_ref[:, o + A_DECAY_RANK:o + 2 * A_DECAY_RANK])
    o += 2 * A_DECAY_RANK
    xa = (za_ref[:, o:o + A_ICLR_RANK], za_ref[:, o + A_ICLR_RANK:o + 2 * A_ICLR_RANK])
    o += 2 * A_ICLR_RANK
    xg = za_ref[:, o:o + A_GATE_RANK]
    ebd = ebd_ref[...].astype(BF16)

    kk = k * kkp_ref[...]
    nrm = jnp.sqrt(_head_sums(kk * kk, ebd))
    kappa = kk / jnp.maximum(nrm, 1e-12)
    kap_o[...] = kappa
    k_sum = jnp.zeros_like(r)
    for d, (lw_o, kd_o, bb_o) in enumerate(((lwf_o, kdf_o, bbf_o), (lwb_o, kdb_o, bbb_o))):
        w_log = -_softplus(-(w0_ref[d] + _dot(jnp.tanh(xw[d]).astype(BF16), w2_ref[d].astype(BF16)))) - 0.5
        lw_o[...] = -jnp.exp(w_log)
        a = _sigmoid(a0_ref[d] + _dot(xa[d].astype(BF16), a2_ref[d].astype(BF16)))
        k_d = k * (1.0 + (a - 1.0) * kap_ref[...])
        kd_o[...] = k_d
        bb_o[...] = kappa * a
        k_sum += k_d
    bon_o[...] = _head_sums(r * k_sum * rkp_ref[...], ebd) * v
    gate_o[...] = _dot(_sigmoid(xg).astype(BF16), g2_ref[...].astype(BF16))


def _rwkv_chunk(r, v, kap, lw, kd, bb, st, mbd, eye, m_strict, m_incl, tri):
    cum = _dot(tri, lw, hi=True)
    yield
    tot = jnp.sum(lw, axis=0, keepdims=True)
    g_in = jnp.exp(cum)
    g_inv = jnp.exp(-cum)
    g_ex = jnp.exp(cum - lw)
    g_end = jnp.exp(tot - cum)
    cast = (lambda a: a) if RWKV_HI else (lambda a: a.astype(BF16))
    mask2 = cast(mbd[:LANE, :LANE])
    x = cast(jnp.concatenate([kap * g_ex, r * g_in], axis=0))
    ab = _pair_dot_nt(x, _pair_bd(cast(bb * g_inv), mask2), RWKV_HI)
    ak = _pair_dot_nt(x, _pair_bd(cast(kd * g_inv), mask2), RWKV_HI)
    xs = _pair_dot_nt(x, [cast(st[sl, sl]) for sl in _PAIRS], RWKV_HI)
    yield
    c = r.shape[0]
    strict = m_strict > 0.5
    incl = m_incl > 0.5
    a_ub = jnp.where(strict, ab[:c], 0.0)
    a_rb = jnp.where(incl, ab[c:], 0.0)
    a_uk = jnp.where(strict, ak[:c], 0.0)
    a_rk = jnp.where(incl, ak[c:], 0.0)

    def catmul(pc, q):
        return _pair_dot(cast(pc), _pair_bd(cast(q), mask2), RWKV_HI)

    xs = xs + catmul(jnp.concatenate([a_uk, a_rk], axis=0), v)
    rhs, o_v = xs[:c], xs[c:]
    xp = -a_ub
    inv = eye + xp
    xp = catmul(xp, xp)
    yield
    for _ in range(int(np.log2(c)) - 2):
        both = catmul(jnp.concatenate([inv, xp], axis=0), xp)
        yield
        inv = inv + both[:c]
        xp = both[c:]
    inv = inv + catmul(inv, xp)
    yield
    u = -catmul(inv, rhs)
    yield
    o = o_v + catmul(a_rb, u)
    uv = cast(jnp.concatenate([u, v], axis=0))
    bk = cast(jnp.concatenate([bb * g_end, kd * g_end], axis=0))
    upd = _pair_blocks([_dot_tn(uv[:, sl], bk[:, sl], hi=RWKV_HI) * mbd[:LANE, :LANE] for sl in _PAIRS])
    st_new = st * jnp.exp(tot) + upd
    return o, st_new


def _interleave(gens):
    results = [None] * len(gens)
    live = list(range(len(gens)))
    while live:
        still = []
        for k in live:
            try:
                next(gens[k])
                still.append(k)
            except StopIteration as done:
                results[k] = done.value
        live = still
    return results


def _init_states(st, c, st_scr, s0_ref):
    @pl.when(c == 0)
    def _():
        st_scr[1:] = s0_ref[...]

    @pl.when(st.ctx_first(c))
    def _():
        st_scr[0] = jnp.zeros(st_scr.shape[1:], F32)


def _hgrn_chunk(xq, xf, xi, lb, st, tri, ebd, mbd, p_scr, reverse):
    c = xq.shape[0]
    nb = c // SUB
    q = _silu(xq)
    gsig = lb + (1.0 - lb) * _sigmoid(xf)
    kk = 1.0 - gsig
    lg = jnp.log(gsig)
    cum = _dot(tri, lg, hi=True)
    yield
    cum = cum * LOG2E
    tot = jnp.sum(lg, axis=0, keepdims=True) * LOG2E
    lane_s = lax.broadcasted_iota(jnp.int32, (SUB, WIDTH), 1) % c
    half = SUB // 2
    row_h = lax.broadcasted_iota(jnp.int32, (half, WIDTH), 0)
    blk = lambda a, i: a[i * SUB:(i + 1) * SUB]
    end_row = (lambda j: j * SUB) if reverse else (lambda j: j * SUB + SUB - 1)
    later = (lambda j: range(0, j)) if reverse else (lambda j: range(j + 1, nb))

    k_end = jnp.concatenate([blk(kk, j) * jnp.exp2(cum[end_row(j):end_row(j) + 1] - blk(cum, j))
                             for j in range(nb)], axis=0)
    q_parts, where_part = [], {}
    for j in range(nb):
        for i in later(j):
            where_part[(i, j)] = len(q_parts)
            q_parts.append(blk(q, i) * jnp.exp2(blk(cum, i) - cum[end_row(j):end_row(j) + 1]))
    mask2 = mbd[:LANE, :LANE].astype(BF16)
    cross = _pair_dot_nt(jnp.concatenate(q_parts, axis=0).astype(BF16), _pair_bd(k_end.astype(BF16), mask2))
    o_state = _pair_dot_nt((q * jnp.exp2(cum)).astype(BF16), [st[sl, sl].astype(BF16) for sl in _PAIRS])
    xi_bf = xi.astype(BF16)
    k_out = (kk * jnp.exp2(tot - cum)).astype(BF16)
    upd = _pair_blocks([_dot_tn(xi_bf[:, sl], k_out[:, sl]) * mbd[:LANE, :LANE] for sl in _PAIRS])
    yield

    for i in range(nb):
        cb, qb = blk(cum, i), blk(q, i)
        for sl in range(SUB):
            s = i * SUB + sl
            parts = []
            for lo in (0, half):
                hi = lo + half - 1
                none_valid = lo > sl if reverse else hi < sl
                all_valid = hi <= sl if reverse else lo >= sl
                if none_valid:
                    parts.append(jnp.zeros((half, WIDTH), F32))
                    continue
                d = cb[lo:lo + half] - cum[s:s + 1]
                if not all_valid:
                    valid = (row_h + lo <= sl) if reverse else (row_h + lo >= sl)
                    d = jnp.where(valid, d, NEG)
                parts.append(jnp.exp2(d) * qb[lo:lo + half] * kk[s:s + 1])
            p_scr[s * SUB:(s + 1) * SUB, :] = jnp.concatenate(parts, axis=0).astype(BF16)
    same = _pair_dot(p_scr, [mask2] * len(_PAIRS))
    yield

    att_rows = []
    for i in range(nb):
        att = jnp.zeros((SUB, WIDTH), F32)
        for sl in range(SUB):
            s = i * SUB + sl
            att = jnp.where(lane_s == s, same[s * SUB:(s + 1) * SUB], att)
        for j in range(nb):
            if (i, j) in where_part:
                n = where_part[(i, j)]
                att = jnp.where(lane_s // SUB == j, cross[n * SUB:(n + 1) * SUB], att)
        att_rows.append(att)
    att = jnp.concatenate(att_rows, axis=0)
    o = o_state + _pair_dot(att.astype(BF16), _pair_bd(xi_bf, mask2))
    st_new = st * jnp.exp2(tot) + upd
    return o, st_new


def _mix_scan_kernel(st,
                     rf_ref, vf_ref, kapf_ref, lwf_ref, kdf_ref, bbf_ref,
                     rb_ref, vb_ref, kapb_ref, lwb_ref, kdb_ref, bbb_ref,
                     qf_ref, ff_ref, if_ref, qb_ref, fb_ref, ib_ref, lb_ref,
                     sa0_ref, sd0_ref, mbd_ref, eye_ref, lows_ref, lowi_ref, ups_ref, upi_ref, trif_ref, trib_ref,
                     af_ref, ab_ref, afin_f, afin_b, df_ref, db_ref, dfin_f, dfin_b,
                     sa_scr, sd_scr, p_scr):
    c = pl.program_id(0)
    _init_states(st, c, sa_scr, sa0_ref)
    _init_states(st, c, sd_scr, sd0_ref)
    mbd = mbd_ref[...]
    eye = eye_ref[...]
    gens = []
    for s in range(st.n_seg):
        gens.append(_rwkv_chunk(rf_ref[s], vf_ref[s], kapf_ref[s], lwf_ref[s], kdf_ref[s], bbf_ref[s],
                                sa_scr[s, 0], mbd, eye, lows_ref[...], lowi_ref[...], trif_ref[...]))
        gens.append(_hgrn_chunk(qf_ref[s], ff_ref[s], if_ref[s], lb_ref[0], sd_scr[s, 0], trif_ref[...],
                                mbd, mbd, p_scr.at[s, 0], False))
        gens.append(_rwkv_chunk(rb_ref[s], vb_ref[s], kapb_ref[s], lwb_ref[s], kdb_ref[s], bbb_ref[s],
                                sa_scr[s, 1], mbd, eye, ups_ref[...], upi_ref[...], trib_ref[...]))
        gens.append(_hgrn_chunk(qb_ref[s], fb_ref[s], ib_ref[s], lb_ref[1], sd_scr[s, 1], trib_ref[...],
                                mbd, mbd, p_scr.at[s, 1], True))
    outs = ((af_ref, sa_scr), (df_ref, sd_scr), (ab_ref, sa_scr), (db_ref, sd_scr))
    for k, (o, st_new) in enumerate(_interleave(gens)):
        s, kind = k // 4, k % 4
        o_ref, scr = outs[kind]
        o_ref[s] = o
        scr[s, kind // 2] = st_new

    @pl.when(st.ctx_last(c))
    def _():
        for h in range(N_HEADS):
            hs = slice(h * HEAD_DIM, (h + 1) * HEAD_DIM)
            afin_f[0, h] = sa_scr[0, 0, hs, hs]
            afin_b[0, h] = sa_scr[0, 1, hs, hs]
            dfin_f[0, h] = sd_scr[0, 0, hs, hs].T
            dfin_b[0, h] = sd_scr[0, 1, hs, hs].T


def _mix_scan(r, v, pre, zd, lb, sa0_dec, sd0_dec, hc, st):
    kap, lwf, lwb, kdf, kdb, bbf, bbb = pre
    consts = (hc['mbd'], hc['eye'], hc['low_s'], hc['low_i'], hc['up_s'], hc['up_i'], hc['tri_f'], hc['tri_b'])
    w = WIDTH
    o_shape = jax.ShapeDtypeStruct(kap.shape, F32)
    head_blocks = (N_HEADS, HEAD_DIM, HEAD_DIM)
    fin_shape = jax.ShapeDtypeStruct((st.n_ctx,) + head_blocks, F32)
    fin_specs = [st.fin_fwd(head_blocks), st.fin_bwd(head_blocks)]
    outs = pl.pallas_call(
        functools.partial(_mix_scan_kernel, st),
        grid=(st.n,),
        in_specs=[st.fwd(w)] * 6 + [st.bwd(w)] * 6 + [
                  st.fwd(w, 0), st.fwd(w, 1), st.fwd(w, 3), st.bwd(w, 0), st.bwd(w, 2), st.bwd(w, 3),
                  _full(lb), _full(sa0_dec), _full(sd0_dec)] + [_full(a) for a in consts],
        out_specs=[st.fwd(w), st.bwd(w)] + fin_specs + [st.fwd(w), st.bwd(w)] + fin_specs,
        out_shape=[o_shape, o_shape, fin_shape, fin_shape] * 2,
        scratch_shapes=[pltpu.VMEM((st.n_seg, 2, w, w), F32), pltpu.VMEM((st.n_seg, 2, w, w), F32),
                        pltpu.VMEM((st.n_seg, 2, CHUNK * SUB, w), BF16)],
        compiler_params=_cp(("arbitrary",)),
        name='mix_scan',
    )(r, v, kap, lwf, kdf, bbf, r, v, kap, lwb, kdb, bbb, zd, zd, zd, zd, zd, zd, lb, sa0_dec, sd0_dec, *consts)
    a_f, a_b, afin_f, afin_b, d_f, d_b, dfin_f, dfin_b = outs
    return ((a_f, a_b, jnp.stack([afin_f, afin_b], axis=1)), (d_f, d_b, jnp.stack([dfin_f, dfin_b], axis=1)))


def _lru_pre_kernel(seq_of_tile, xp_ref, x_ref, xn_ref, cw_ref, cb_ref, wg_ref, bg_ref, lam_ref,
                    af_o, bf_o, ab_o, bb_o, pad_scr):
    i = pl.program_id(0)
    tm = x_ref.shape[0]
    seqlen = seq_of_tile(i)
    pad_scr[0:8, :] = xp_ref[:, 0:WIDTH]
    pad_scr[8:8 + tm, :] = x_ref[:, 0:WIDTH]
    pad_scr[8 + tm:16 + tm, :] = xn_ref[:, 0:WIDTH]
    pos = jnp.bitwise_and(lax.broadcasted_iota(jnp.int32, (tm, WIDTH), 0) + i * tm, seqlen - 1)
    u = jnp.zeros((tm, WIDTH), F32) + cb_ref[...]
    for j in range(C_CONV):
        off = j - C_CONV // 2
        tap = pad_scr[pl.ds(8 + off, tm), :]
        ok = jnp.logical_and(pos + off >= 0, pos + off < seqlen)
        u += jnp.where(ok, tap, 0.0) * cw_ref[j:j + 1, :]
    gates = _sigmoid(_dot(u.astype(BF16), wg_ref[...].astype(BF16)) + bg_ref[...])
    for d, (a_o, b_o) in enumerate(((af_o, bf_o), (ab_o, bb_o))):
        r = gates[:, (2 * d) * WIDTH:(2 * d + 1) * WIDTH]
        ig = gates[:, (2 * d + 1) * WIDTH:(2 * d + 2) * WIDTH]
        log_a = -C_POW * r * _softplus(-lam_ref[d])
        a = jnp.exp(log_a)
        b = jnp.sqrt(-jnp.tanh(log_a) * (a * a + 1.0)) * (ig * u)
        for half in range(WIDTH // LANE):
            a_o[half] = a[:, half * LANE:(half + 1) * LANE]
            b_o[half] = b[:, half * LANE:(half + 1) * LANE]


def _lru_pre(zc, p, ctx_len, dec_len, tl):
    ps, tm = tl.per_seg, tl.tm
    per = tm // 8
    last8 = tl.seg_rows // 8 - 1
    seq_of_tile = lambda i: jnp.where(i < ps, ctx_len, dec_len)
    args = (p['conv_w'], p['conv_b'], p['wg'], p['bg'], p['lam'])
    nh = WIDTH // LANE
    prev8 = pl.BlockSpec((None, 8, WIDTH), lambda i: (i // ps, jnp.maximum((i % ps) * per - 1, 0), 0))
    next8 = pl.BlockSpec((None, 8, WIDTH), lambda i: (i // ps, jnp.minimum((i % ps + 1) * per, last8), 0))
    return pl.pallas_call(
        functools.partial(_lru_pre_kernel, seq_of_tile),
        grid=tl.grid,
        in_specs=[prev8, tl.row(WIDTH), next8] + [_full(a) for a in args],
        out_specs=[pl.BlockSpec((nh, None, tm, LANE), lambda i: (0, i // ps, i % ps, 0))] * 4,
        out_shape=[jax.ShapeDtypeStruct((nh, tl.n_seg, tl.seg_rows, LANE), F32)] * 4,
        scratch_shapes=[pltpu.VMEM((tm + 16, WIDTH), F32)],
        compiler_params=_cp(("parallel",)),
        name='rglru_pre',
    )(zc, zc, zc, *args)


def _lru_scan_kernel(st, af_ref, bf_ref, ab_ref, bb_ref, h0_ref, hf_ref, hb_ref, finf_ref, finb_ref,
                     h_scr, loc_scr, car_scr):
    c = pl.program_id(0)
    nh = WIDTH // LANE

    @pl.when(c == 0)
    def _():
        h_scr[:, 1:] = h0_ref[...]

    @pl.when(st.ctx_first(c))
    def _():
        h_scr[:, 0] = jnp.zeros((nh, 2, 1, LANE), F32)

    rr = LRU_ROWS
    ng = af_ref.shape[2] // rr
    chains = [(p, s, d) for s in range(st.n_seg) for d in range(2) for p in range(nh)]
    refs = ((af_ref, bf_ref, hf_ref), (ab_ref, bb_ref, hb_ref))
    order = (list(range(rr)), list(range(rr - 1, -1, -1)))

    for k, (p, s, d) in enumerate(chains):
        a_ref, b_ref, _ = refs[d]
        hloc = ploc = None
        for r in order[d]:
            a = a_ref[p, s, pl.ds(r, ng, stride=rr), :]
            b = b_ref[p, s, pl.ds(r, ng, stride=rr), :]
            hloc = b if hloc is None else a * hloc + b
            ploc = a if ploc is None else a * ploc
            loc_scr[k, 0, r] = hloc
            loc_scr[k, 1, r] = ploc

    def carry_step(j, carries):
        out = []
        for k, (p, s, d) in enumerate(chains):
            g = j if d == 0 else ng - 1 - j
            r_end = order[d][-1]
            car_scr[k, pl.ds(g, 1), :] = carries[k]
            out.append(loc_scr[k, 1, r_end, pl.ds(g, 1), :] * carries[k] + loc_scr[k, 0, r_end, pl.ds(g, 1), :])
        return tuple(out)

    carries = lax.fori_loop(0, ng, carry_step, tuple(h_scr[p, s, d] for p, s, d in chains))

    for k, (p, s, d) in enumerate(chains):
        h_scr[p, s, d] = carries[k]
        o_ref = refs[d][2]
        car = car_scr[k]
        for r in range(rr):
            o_ref[p, s, pl.ds(r, ng, stride=rr), :] = loc_scr[k, 0, r] + loc_scr[k, 1, r] * car

    @pl.when(st.ctx_last(c))
    def _():
        for p in range(nh):
            finf_ref[0, :, p * LANE:(p + 1) * LANE] = h_scr[p, 0, 0]
            finb_ref[0, :, p * LANE:(p + 1) * LANE] = h_scr[p, 0, 1]


def _lru_scan(coef, h0_dec, st):
    nh = WIDTH // LANE
    a_f, b_f, a_b, b_b = coef
    n_dec = h0_dec.shape[0]
    h0 = jnp.transpose(h0_dec.reshape(n_dec, 2, nh, 1, LANE), (2, 0, 1, 3, 4))
    n_chain = 2 * st.n_seg * nh
    ng = st.chunk // LRU_ROWS
    n = st.n
    fwd = pl.BlockSpec((nh, st.n_seg, st.chunk, LANE), lambda c: (0, 0, c, 0))
    bwd = pl.BlockSpec((nh, st.n_seg, st.chunk, LANE), lambda c: (0, 0, n - 1 - c, 0))
    h_f, h_b, fin_f, fin_b = pl.pallas_call(
        functools.partial(_lru_scan_kernel, st),
        grid=(st.n,),
        in_specs=[fwd, fwd, bwd, bwd, pl.BlockSpec(h0.shape, lambda c: (0,) * h0.ndim)],
        out_specs=[fwd, bwd, st.fin_fwd((1, WIDTH)), st.fin_bwd((1, WIDTH))],
        out_shape=[jax.ShapeDtypeStruct(a_f.shape, F32), jax.ShapeDtypeStruct(a_f.shape, F32),
                   jax.ShapeDtypeStruct((st.n_ctx, 1, WIDTH), F32), jax.ShapeDtypeStruct((st.n_ctx, 1, WIDTH), F32)],
        scratch_shapes=[pltpu.VMEM((nh, st.n_seg, 2, 1, LANE), F32),
                        pltpu.VMEM((n_chain, 2, LRU_ROWS, ng, LANE), F32),
                        pltpu.VMEM((n_chain, ng, LANE), F32)],
        compiler_params=_cp(("arbitrary",)),
        name='rglru_scan',
    )(a_f, b_f, a_b, b_b, h0)
    return h_f, h_b, jnp.concatenate([fin_f, fin_b], axis=1)


def _rope_slab(x, cos, sin):
    lane = lax.broadcasted_iota(jnp.int32, x.shape, 1)
    rot = jnp.where(lane % (B_ROPE // 2) < B_ROPE // 4, -pltpu.roll(x, LANE - B_ROPE // 4, 1),
                    pltpu.roll(x, B_ROPE // 4, 1))
    return x * cos + rot * sin


def _kv_up(ckv_bf, kpe_slab, wuk_ref, wuv_ref, k_o, v_o):
    kn = _dot(ckv_bf, wuk_ref[...])
    for h in range(N_HEADS):
        k_o[:, h * SLAB:(h + 1) * SLAB] = (kn[:, h * SLAB:(h + 1) * SLAB] + kpe_slab).astype(BF16)
    lane = lax.broadcasted_iota(jnp.int32, (1, N_HEADS * SLAB), 1)
    v_o[...] = (_dot(ckv_bf, wuv_ref[...]) + jnp.where(lane % SLAB >= B_VDIM, 1.0, 0.0)).astype(BF16)


def _mla_prep_kernel(zb_ref, cos_ref, sin_ref, qn_ref, kvn_ref, wuq_ref, wuk_ref, wuv_ref,
                     ckv_o, q_o, k_o, v_o):
    cq = zb_ref[:, 0:B_Q_RANK]
    ckv = zb_ref[:, B_Q_RANK:B_Q_RANK + B_KV_RANK]
    kpe = zb_ref[:, B_Q_RANK + B_KV_RANK:ZB]
    cos = cos_ref[...]
    sin = sin_ref[...]
    cqn = cq * lax.rsqrt(jnp.mean(cq * cq, axis=-1, keepdims=True) + RMS_EPS) * qn_ref[...]
    ckvn = ckv * lax.rsqrt(jnp.mean(ckv * ckv, axis=-1, keepdims=True) + RMS_EPS) * kvn_ref[...]
    ckv_o[...] = ckvn
    q = _dot(cqn.astype(BF16), wuq_ref[...])
    for h in range(N_HEADS):
        q_o[:, h * SLAB:(h + 1) * SLAB] = (_rope_slab(q[:, h * SLAB:(h + 1) * SLAB], cos, sin)
                                           * ATTN_LOG2_SCALE).astype(BF16)
    _kv_up(ckvn.astype(BF16), _rope_slab(kpe, cos, sin), wuk_ref, wuv_ref, k_o, v_o)


def _mla_cache_kernel(ckv_ref, kpe_ref, wuk_ref, wuv_ref, k_o, v_o):
    _kv_up(ckv_ref[...].astype(BF16), kpe_ref[...], wuk_ref, wuv_ref, k_o, v_o)


def _mla_cache(ckv, kpe_slab, p):
    rows = ckv.shape[0]
    return pl.pallas_call(
        _mla_cache_kernel,
        out_shape=[jax.ShapeDtypeStruct((rows, N_HEADS * SLAB), BF16)] * 2,
        compiler_params=pltpu.CompilerParams(vmem_limit_bytes=VMEM_LIMIT),
        name='mla_cache',
    )(ckv, kpe_slab, p['wuk'], p['wuv'])


def _attend_heads(q_ref, kv_refs, o_ref):
    tq = q_ref.shape[0]
    low = lax.broadcasted_iota(jnp.int32, (tq, SLAB), 1) < B_VDIM

    def head(h):
        qh = q_ref[:, h * SLAB:(h + 1) * SLAB]
        ss = [_dot_nt(qh, k_ref[:, h * SLAB:(h + 1) * SLAB]) for k_ref, _ in kv_refs]
        yield
        m = ss[0].max(axis=-1, keepdims=True)
        for s in ss[1:]:
            m = jnp.maximum(m, s.max(axis=-1, keepdims=True))
        es = [jnp.exp2((s - m).astype(BF16)) for s in ss]
        yield
        acc = jnp.zeros((tq, SLAB), F32)
        for e, (_, v_ref) in zip(es, kv_refs):
            acc += _dot(e, v_ref[:, h * SLAB:(h + 1) * SLAB])
        return acc / acc[:, B_VDIM:B_VDIM + 1]

    pairs = []
    for h in range(0, N_HEADS, 2):
        h0, h1 = _interleave([head(h), head(h + 1)])
        pairs.append(jnp.where(low, h0, pltpu.roll(h1, B_VDIM, 1)))
    o_ref[...] = jnp.concatenate(pairs, axis=-1)


def _attn_kernel(n_ctx, q_ref, kx_ref, vx_ref, k_ref, v_ref, kc_ref, vc_ref, o_ref):
    i = pl.program_id(0)

    @pl.when(i < n_ctx)
    def _():
        _attend_heads(q_ref, [(kx_ref, vx_ref)], o_ref)

    @pl.when(i >= n_ctx)
    def _():
        _attend_heads(q_ref, [(k_ref, v_ref), (kc_ref, vc_ref)], o_ref)


def _attn(q, k, v, kc, vc, n_ctx, ctx_len, n_dec, dec_len, past):
    tq = ctx_len
    nq = dec_len // tq
    slabs = N_HEADS * SLAB
    dec = lambda i: jnp.maximum(i - n_ctx, 0)

    def q_map(i):
        return (jnp.where(i < n_ctx, 0, 1 + dec(i) // nq), jnp.where(i < n_ctx, i, dec(i) % nq), 0)

    ctx_kv = pl.BlockSpec((None, ctx_len, slabs), lambda i: (0, jnp.minimum(i, n_ctx - 1), 0))
    dec_kv = pl.BlockSpec((None, dec_len, slabs), lambda i: (1 + dec(i) // nq, 0, 0))
    cache_kv = pl.BlockSpec((past, slabs), lambda i: (dec(i) // nq, 0))
    return pl.pallas_call(
        functools.partial(_attn_kernel, n_ctx),
        grid=(n_ctx + n_dec * nq,),
        in_specs=[pl.BlockSpec((None, tq, slabs), q_map), ctx_kv, ctx_kv, dec_kv, dec_kv, cache_kv, cache_kv],
        out_specs=pl.BlockSpec((None, tq, WIDTH), q_map),
        out_shape=jax.ShapeDtypeStruct(q.shape[:2] + (WIDTH,), F32),
        compiler_params=_cp(("arbitrary",)),
        name='attn',
    )(q, k, v, k, v, kc, vc)


def _rope_tables(n_ctx_rows, n_dec, dec_len):
    rows = dec_len // GRID_W
    row = np.repeat(np.arange(rows, dtype=np.float32), GRID_W)
    col = np.tile(np.arange(GRID_W, dtype=np.float32), rows)
    half = B_ROPE // 2
    inv = jnp.power(ROPE_BASE, -jnp.arange(0, half, 2, dtype=F32) / half)
    ang_r = jnp.asarray(row)[:, None] * inv
    ang_c = jnp.asarray(col)[:, None] * inv
    ang = jnp.concatenate([ang_r, ang_r, ang_c, ang_c], axis=-1)
    cos = jnp.pad(jnp.cos(ang), ((0, 0), (0, LANE - B_ROPE)), constant_values=1.0)
    sin = jnp.pad(jnp.sin(ang), ((0, 0), (0, LANE - B_ROPE)))
    cos = jnp.concatenate([jnp.ones((n_ctx_rows, LANE), F32)] + [cos] * n_dec, axis=0)
    sin = jnp.concatenate([jnp.zeros((n_ctx_rows, LANE), F32)] + [sin] * n_dec, axis=0)
    return cos, sin


def _block_diag(w):
    g, n, m = w.shape[-3:]
    eye = jnp.eye(g, dtype=w.dtype)
    out = w[..., :, :, None, :] * eye[:, None, :, None]
    return out.reshape(w.shape[:-3] + (g * n, g * m))


def kernel(x_prompt, x_sample, cache_mla_ckv, cache_mla_kpe, state_rwkv, state_rglru, state_hgrn, c, c_ctx,
           w_ada, b_ada, ln_g, ln_b, w_ffn_in, w_ffn_out, w_in, w_out,
           rwkv_w0, rwkv_w2, rwkv_a0, rwkv_a2, rwkv_g2, rwkv_kk, rwkv_ka, rwkv_rk, rwkv_gn_g, rwkv_gn_b,
           mla_qn_g, mla_w_uq, mla_kvn_g, mla_w_ukv,
           rglru_conv_w, rglru_conv_b, rglru_wa, rglru_ba, rglru_wx, rglru_bx, rglru_lam,
           hgrn_lb, hgrn_gn_g):
    n_ctx, ctx_len, _ = x_prompt.shape
    n_dec, dec_len, _ = x_sample.shape
    past = cache_mla_ckv.shape[2]
    seg = n_ctx * ctx_len
    assert seg == dec_len, "context rows must form one segment of the decode sequence length"
    assert ctx_len & (ctx_len - 1) == 0 and dec_len & (dec_len - 1) == 0, "sequence lengths must be powers of two"
    n_seg = 1 + n_dec
    tl = _Tiles(n_seg, seg, min(TM, seg))
    assert ctx_len % LRU_CHUNK == 0
    st = _Streams(n_seg, n_ctx, seg, ctx_len, CHUNK)
    lru_st = _Streams(n_seg, n_ctx, seg, ctx_len, LRU_CHUNK)

    hc = {k: jnp.asarray(v) for k, v in _head_consts().items()}
    ebd = hc['mbd']
    cos, sin = (t.reshape(n_seg, seg, LANE) for t in _rope_tables(seg, n_dec, dec_len))

    cond8 = jnp.zeros((8, D_MODEL), F32).at[0].set(c_ctx).at[1:1 + n_dec].set(c)
    mods = _ada(cond8, w_ada, b_ada).reshape(DEPTH, 8, N_MOD, D_MODEL)
    mods = jnp.transpose(mods, (0, 2, 1, 3))[:, :, :n_seg, None, :]

    w_ffn_in_bf = w_ffn_in.astype(BF16)
    w_ffn_out_bf = w_ffn_out.astype(BF16)
    w_out_bf = w_out.astype(BF16)
    kpe_end = ZA + B_Q_RANK + B_KV_RANK + B_ROPE
    w_ab = jnp.pad(w_in[:, :, :kpe_end].astype(BF16), ((0, 0), (0, 0), (0, LANE - B_ROPE)))
    w_cd = w_in[:, :, kpe_end:].astype(BF16)
    wuq = mla_w_uq.reshape(DEPTH, B_Q_RANK, N_HEADS, B_NOPE + B_ROPE)
    wuq_p = jnp.concatenate([wuq[..., B_NOPE:], wuq[..., :B_NOPE],
                             jnp.zeros((DEPTH, B_Q_RANK, N_HEADS, SLAB - B_NOPE - B_ROPE), F32)], axis=-1)
    wuq_p = wuq_p.reshape(DEPTH, B_Q_RANK, N_HEADS * SLAB).astype(BF16)
    wukv = mla_w_ukv.reshape(DEPTH, B_KV_RANK, N_HEADS, B_NOPE + B_VDIM)
    wuk_p = jnp.concatenate([jnp.zeros((DEPTH, B_KV_RANK, N_HEADS, B_ROPE), F32), wukv[..., :B_NOPE],
                             jnp.zeros((DEPTH, B_KV_RANK, N_HEADS, SLAB - B_NOPE - B_ROPE), F32)], axis=-1)
    wuk_p = wuk_p.reshape(DEPTH, B_KV_RANK, N_HEADS * SLAB).astype(BF16)
    wuv_p = jnp.concatenate([wukv[..., B_NOPE:], jnp.zeros((DEPTH, B_KV_RANK, N_HEADS, SLAB - B_VDIM), F32)], axis=-1)
    wuv_p = wuv_p.reshape(DEPTH, B_KV_RANK, N_HEADS * SLAB).astype(BF16)
    lru_wg = jnp.concatenate([_block_diag(rglru_wa[:, 0]), _block_diag(rglru_wx[:, 0]),
                              _block_diag(rglru_wa[:, 1]), _block_diag(rglru_wx[:, 1])], axis=-1)
    lru_bg = jnp.concatenate([rglru_ba[:, 0], rglru_bx[:, 0], rglru_ba[:, 1], rglru_bx[:, 1]], axis=-1)[:, None, :]
    lbs = _lower_bounds(hgrn_lb)

    rwkv_s0 = _block_diag(state_rwkv)
    hgrn_s0 = _block_diag(jnp.swapaxes(state_hgrn, -1, -2))
    cache_kpe_slab = jnp.pad(cache_mla_kpe, ((0, 0), (0, 0), (0, 0), (0, LANE - B_ROPE)))

    new_ckv, new_kpe, new_rwkv, new_lru, new_hgrn = [], [], [], [], []
    for l in range(DEPTH):
        m = mods[l]
        lng = ln_g[l][:, None, :]
        lnb = ln_b[l][:, None, :]
        ffn0 = functools.partial(_ffn, sh=m[0], sc=m[1], g=m[2], lng=lng[0], lnb=lnb[0], w_in_bf=w_ffn_in_bf,
                                 w_out_bf=w_ffn_out_bf, l=l, f=0, tl=tl)
        if l == 0:
            x = ffn0((x_prompt.reshape(1, seg, D_MODEL), x_sample), split='in')
        else:
            x = ffn0(x)
        pa = {'w0': rwkv_w0[l][:, None, :], 'w2': rwkv_w2[l], 'a0': rwkv_a0[l][:, None, :], 'a2': rwkv_a2[l],
              'g2': rwkv_g2[l], 'kk': rwkv_kk[l][None, :], 'ka': rwkv_ka[l][None, :], 'rk': rwkv_rk[l][None, :]}
        pb = {'qn_g': mla_qn_g[l][None, :], 'kvn_g': mla_kvn_g[l][None, :], 'wuq': wuq_p[l], 'wuk': wuk_p[l],
              'wuv': wuv_p[l]}
        a_r, a_v, pre, kpe_raw, (ckvn, q_all, k_all, v_all), zc, zd = _inproj(x, m[3], m[4], w_ab, w_cd, l, pa, pb,
                                                                                ebd, cos, sin, tl)

        (oa_f, oa_b, sa_fin), (od_f, od_b, sd_fin) = _mix_scan(a_r, a_v, pre[:7], zd, lbs[l], rwkv_s0[:, l],
                                                                hgrn_s0[:, l], hc, st)
        new_rwkv.append(sa_fin)
        new_hgrn.append(sd_fin)

        kc, vc = _mla_cache(cache_mla_ckv[:, l].reshape(n_dec * past, B_KV_RANK),
                            cache_kpe_slab[:, l].reshape(n_dec * past, LANE), pb)
        yb = _attn(q_all, k_all, v_all, kc, vc, n_ctx, ctx_len, n_dec, dec_len, past)
        new_ckv.append(ckvn[0].reshape(n_ctx, ctx_len, B_KV_RANK))
        new_kpe.append(kpe_raw[0, :, :B_ROPE].reshape(n_ctx, ctx_len, B_ROPE))

        pc = {'conv_w': rglru_conv_w[l], 'conv_b': rglru_conv_b[l][None, :], 'wg': lru_wg[l], 'bg': lru_bg[l],
              'lam': rglru_lam[l][:, None, :]}
        coef = _lru_pre(zc, pc, ctx_len, dec_len, tl)
        h_f, h_b, h_fin = _lru_scan(coef, state_rglru[:, l], lru_st)
        new_lru.append(h_fin)

        x = _outproj(x, oa_f, oa_b, pre[7], pre[8], yb, h_f, h_b, zc, od_f, od_b, zd,
                     rwkv_gn_g[l][None, :], rwkv_gn_b[l][None, :], hgrn_gn_g[l][None, :], ebd,
                     w_out_bf, m[5], lng[1], lnb[1], l, tl)
        ffn1 = functools.partial(_ffn, sh=m[6], sc=m[7], g=m[8], lng=lng[2], lnb=lnb[2], w_in_bf=w_ffn_in_bf,
                                 w_out_bf=w_ffn_out_bf, l=l, f=1, tl=tl)
        if l == DEPTH - 1:
            y_prompt, y_sample = ffn1(x, split='out')
            y_prompt = y_prompt.reshape(n_ctx, ctx_len, D_MODEL)
        else:
            x = ffn1(x)
    return (y_prompt, y_sample, jnp.stack(new_ckv, axis=1), jnp.stack(new_kpe, axis=1),
            jnp.stack(new_rwkv, axis=1), jnp.stack(new_lru, axis=1), jnp.stack(new_hgrn, axis=1))
```

```python
import functools

import numpy as np
import jax
import jax.numpy as jnp
from jax import lax
from jax.experimental import pallas as pl
from jax.experimental.pallas import tpu as pltpu

D_MODEL = 1024
DEPTH = 4
GRID_W = 64
HEAD_DIM = 64
N_HEADS = 4
WIDTH = N_HEADS * HEAD_DIM
A_DECAY_RANK = 64
A_ICLR_RANK = 64
A_GATE_RANK = 128
B_NOPE = 64
B_ROPE = 32
B_VDIM = 64
B_Q_RANK = 256
B_KV_RANK = 128
C_CONV = 4
C_POW = 8.0
D_FF = 2816
N_MOD = 9
ROPE_BASE = 10000.0
LN_EPS = 1e-5
RMS_EPS = 1e-6
RWKV_GN_EPS = 64e-5
DN_ALPHA = (2 * DEPTH) ** 0.25

LANE = 128
SLAB = 128
CHUNK = 64
SUB = 16
LRU_CHUNK = 256
LRU_ROWS = 8
TM = 512
TF = 256
VMEM_LIMIT = 56 * 1024 * 1024

F32 = jnp.float32
BF16 = jnp.bfloat16
HI = lax.Precision.HIGHEST
NEG = -1e30
LOG2E = float(np.log2(np.e))
ATTN_LOG2_SCALE = float((B_NOPE + B_ROPE) ** -0.5) * LOG2E
RWKV_HI = False


def _cp(sem):
    return pltpu.CompilerParams(dimension_semantics=sem, vmem_limit_bytes=VMEM_LIMIT)


def _dot(a, b, hi=False):
    return jnp.dot(a, b, preferred_element_type=F32, precision=HI if hi else None)


def _dot_nt(a, b, hi=False):
    return lax.dot_general(a, b, (((1,), (1,)), ((), ())), preferred_element_type=F32,
                           precision=HI if hi else None)


def _dot_tn(a, b, hi=False):
    return lax.dot_general(a, b, (((0,), (0,)), ((), ())), preferred_element_type=F32,
                           precision=HI if hi else None)


def _head_sums(x, ebd_bf):
    hi = x.astype(BF16)
    lo = (x - hi.astype(F32)).astype(BF16)
    return _dot(hi, ebd_bf) + _dot(lo, ebd_bf)


def _sigmoid(x):
    return 1.0 / (1.0 + jnp.exp(-x))


def _silu(x):
    return x * _sigmoid(x)


def _softplus(x):
    return jnp.maximum(x, 0.0) + jnp.log(1.0 + jnp.exp(-jnp.abs(x)))


def _tile4(y):
    return jnp.concatenate([y, y, y, y], axis=0)


_PAIRS = tuple(slice(p * LANE, (p + 1) * LANE) for p in range(WIDTH // LANE))


def _pair_bd(y, mask2):
    return [jnp.concatenate([y[:, sl]] * (LANE // HEAD_DIM), axis=0) * mask2 for sl in _PAIRS]


def _pair_dot(a, rhs_pairs, hi=False):
    return jnp.concatenate([_dot(a[:, sl], r, hi) for sl, r in zip(_PAIRS, rhs_pairs)], axis=-1)


def _pair_dot_nt(a, rhs_pairs, hi=False):
    return jnp.concatenate([_dot_nt(a[:, sl], r, hi) for sl, r in zip(_PAIRS, rhs_pairs)], axis=-1)


def _pair_blocks(blocks):
    z = jnp.zeros_like(blocks[0])
    rows = [jnp.concatenate([blocks[p] if q == p else z for q in range(len(blocks))], axis=-1)
            for p in range(len(blocks))]
    return jnp.concatenate(rows, axis=0)


def _head_consts():
    r = np.arange(WIDTH)
    same = (r[:, None] // HEAD_DIM) == (r[None, :] // HEAD_DIM)
    t = np.arange(CHUNK)[:, None]
    j = (np.arange(WIDTH) % CHUNK)[None, :]
    tt = np.arange(CHUNK)
    return {
        'mbd': same.astype(np.float32),
        'low_s': (j < t).astype(np.float32), 'low_i': (j <= t).astype(np.float32),
        'up_s': (j > t).astype(np.float32), 'up_i': (j >= t).astype(np.float32),
        'eye': (j == t).astype(np.float32),
        'tri_f': (tt[None, :] <= tt[:, None]).astype(np.float32),
        'tri_b': (tt[None, :] >= tt[:, None]).astype(np.float32),
    }


def _lb_kernel(x_ref, o_ref):
    x = x_ref[...]
    m = jnp.max(x, axis=0, keepdims=True)
    e = jnp.exp(x - m)
    sm = e / jnp.sum(e, axis=0, keepdims=True)
    run = sm[0:1]
    rows = [run - sm[0:1]]
    for l in range(1, DEPTH):
        run = run + sm[l:l + 1]
        rows.append(run - sm[0:1])
    o_ref[...] = jnp.concatenate(rows, axis=0)


def _lower_bounds(hgrn_lb):
    flat = hgrn_lb.reshape(DEPTH, 2 * WIDTH)
    out = pl.pallas_call(_lb_kernel, out_shape=jax.ShapeDtypeStruct(flat.shape, F32), name='hgrn_lb')(flat)
    return out.reshape(DEPTH, 2, 1, WIDTH)


def _ada_kernel(c_ref, w_ref, b_ref, o_ref):
    h = _silu(c_ref[...]).astype(BF16)
    o_ref[0] = _dot(h, w_ref[0].astype(BF16)) + b_ref[0]


def _ada(cond8, w_ada, b_ada):
    n = N_MOD * D_MODEL
    tn = 2304
    return pl.pallas_call(
        _ada_kernel,
        grid=(DEPTH, n // tn),
        in_specs=[pl.BlockSpec((8, D_MODEL), lambda l, j: (0, 0)),
                  pl.BlockSpec((1, D_MODEL, tn), lambda l, j: (l, 0, j)),
                  pl.BlockSpec((1, 1, tn), lambda l, j: (l, 0, j))],
        out_specs=pl.BlockSpec((1, 8, tn), lambda l, j: (l, 0, j)),
        out_shape=jax.ShapeDtypeStruct((DEPTH, 8, n), F32),
        compiler_params=_cp(("arbitrary", "arbitrary")),
        name='ada',
    )(cond8, w_ada, b_ada.reshape(DEPTH, 1, n))


def _layer_norm_rows(y, g, b):
    mu = jnp.mean(y, axis=-1, keepdims=True)
    d = y - mu
    var = jnp.mean(d * d, axis=-1, keepdims=True)
    return d * lax.rsqrt(var + LN_EPS) * g + b


def _ffn_tile(x, sh_ref, sc_ref, g_ref, lng_ref, lnb_ref, win_ref, wout_ref):
    h = (x * (1.0 + sc_ref[0]) + sh_ref[0]).astype(BF16)
    n_chunks = D_FF // TF

    def gate_up(c):
        return (_dot(h, win_ref[:, c * TF:(c + 1) * TF]),
                _dot(h, win_ref[:, D_FF + c * TF:D_FF + (c + 1) * TF]))

    y = jnp.zeros(x.shape, F32)
    cur = gate_up(0)
    for c in range(n_chunks):
        nxt = gate_up(c + 1) if c + 1 < n_chunks else None
        a = (_silu(cur[0]) * cur[1]).astype(BF16)
        y = y + _dot(a, wout_ref[c * TF:(c + 1) * TF, :])
        cur = nxt
    return _layer_norm_rows(DN_ALPHA * x + 0.5 * g_ref[0] * y, lng_ref[...], lnb_ref[...])


def _ffn_kernel(x_ref, *refs):
    o_ref = refs[-1]
    o_ref[...] = _ffn_tile(x_ref[...], *refs[:-1])


def _ffn_first_kernel(ctx_tiles, xp_ref, xs_ref, *refs):
    o_ref = refs[-1]
    x = jnp.where(pl.program_id(0) < ctx_tiles, xp_ref[...], xs_ref[...])
    o_ref[...] = _ffn_tile(x, *refs[:-1])


def _ffn_last_kernel(ctx_tiles, x_ref, *refs):
    yp_ref, ys_ref = refs[-2:]
    y = _ffn_tile(x_ref[...], *refs[:-2])
    is_ctx = pl.program_id(0) < ctx_tiles

    @pl.when(is_ctx)
    def _():
        yp_ref[...] = y

    @pl.when(jnp.logical_not(is_ctx))
    def _():
        ys_ref[...] = y


class _Tiles:
    def __init__(self, n_seg, seg_rows, tm):
        self.n_seg, self.seg_rows, self.tm = n_seg, seg_rows, tm
        self.per_seg = seg_rows // tm
        self.grid = (n_seg * self.per_seg,)

    def row(self, width, col=0):
        ps = self.per_seg
        return pl.BlockSpec((None, self.tm, width), lambda i: (i // ps, i % ps, col))

    def mod(self):
        ps = self.per_seg
        return pl.BlockSpec((1, 1, D_MODEL), lambda i: (i // ps, 0, 0))

    def shape(self, width, dtype=F32):
        return jax.ShapeDtypeStruct((self.n_seg, self.seg_rows, width), dtype)


def _full(a):
    return pl.BlockSpec(a.shape, lambda i: (0,) * a.ndim)


def _ffn(x, sh, sc, g, lng, lnb, w_in_bf, w_out_bf, l, f, tl, split=None):
    ps, tm = tl.per_seg, tl.tm
    mod = tl.mod()
    vec = pl.BlockSpec((1, D_MODEL), lambda i: (0, 0))
    ctx_blk = pl.BlockSpec((None, tm, D_MODEL), lambda i: (0, jnp.minimum(i, ps - 1), 0))
    dec_blk = pl.BlockSpec((None, tm, D_MODEL),
                           lambda i: (jnp.maximum(i - ps, 0) // ps, jnp.maximum(i - ps, 0) % ps, 0))
    weights = [pl.BlockSpec((None, None, D_MODEL, 2 * D_FF), lambda i: (l, f, 0, 0)),
               pl.BlockSpec((None, None, D_FF, D_MODEL), lambda i: (l, f, 0, 0))]
    rest = [mod, mod, mod, vec, vec] + weights
    args = (sh, sc, g, lng, lnb, w_in_bf, w_out_bf)
    n_dec = tl.n_seg - 1
    if split == 'in':
        body, xs, x_specs = functools.partial(_ffn_first_kernel, ps), tuple(x), [ctx_blk, dec_blk]
    else:
        body, xs, x_specs = _ffn_kernel, (x,), [tl.row(D_MODEL)]
    if split == 'out':
        body = functools.partial(_ffn_last_kernel, ps)
        out_specs = [ctx_blk, dec_blk]
        out_shape = [jax.ShapeDtypeStruct((1, tl.seg_rows, D_MODEL), F32),
                     jax.ShapeDtypeStruct((n_dec, tl.seg_rows, D_MODEL), F32)]
    else:
        out_specs, out_shape = tl.row(D_MODEL), tl.shape(D_MODEL)
    return pl.pallas_call(
        body,
        grid=tl.grid,
        in_specs=x_specs + rest,
        out_specs=out_specs,
        out_shape=out_shape,
        compiler_params=_cp(("arbitrary",) if split == 'out' else ("parallel",)),
        name='ffn',
    )(*xs, *args)


ZA = 3 * WIDTH + 2 * A_DECAY_RANK + 2 * A_ICLR_RANK + A_GATE_RANK
ZB = B_Q_RANK + B_KV_RANK + LANE
ZC = 2 * WIDTH
ZD = 5 * WIDTH


N_RWKV_PRE = 9
N_MLA_PREP = 4


def _inproj_kernel(x_ref, sh_ref, sc_ref, wab_ref, wcd_ref, *rest):
    n_a, n_b = 9, 7
    a_par, b_par = rest[:n_a], rest[n_a:n_a + n_b]
    outs = rest[n_a + n_b:]
    r_o, v_o = outs[0:2]
    a_out = outs[2:2 + N_RWKV_PRE]
    kpe_o = outs[2 + N_RWKV_PRE]
    b_out = outs[3 + N_RWKV_PRE:3 + N_RWKV_PRE + N_MLA_PREP]
    zc_ref, zd_ref, za_scr, zb_scr = outs[3 + N_RWKV_PRE + N_MLA_PREP:]
    h = (x_ref[...] * (1.0 + sc_ref[0]) + sh_ref[0]).astype(BF16)
    za_scr[...] = _dot(h, wab_ref[:, 0:ZA])
    zb_scr[...] = _dot(h, wab_ref[:, ZA:ZA + ZB])
    r_o[...] = za_scr[:, 0:WIDTH]
    v_o[...] = za_scr[:, 2 * WIDTH:3 * WIDTH]
    _rwkv_pre_kernel(za_scr, *a_par, *a_out)
    zc_ref[...] = _dot(h, wcd_ref[:, 0:ZC])
    kpe_o[...] = zb_scr[:, B_Q_RANK + B_KV_RANK:ZB]
    _mla_prep_kernel(zb_scr, *b_par, *b_out)
    zd_ref[...] = _dot(h, wcd_ref[:, ZC:ZC + ZD])


def _inproj(x, sh, sc, w_ab, w_cd, l, pa, pb, ebd, cos, sin, tl):
    mod = tl.mod()
    a_par = (pa['w0'], pa['w2'], pa['a0'], pa['a2'], pa['g2'], pa['kk'], pa['ka'], pa['rk'], ebd)
    b_par = (pb['qn_g'], pb['kvn_g'], pb['wuq'], pb['wuk'], pb['wuv'])
    slabs = N_HEADS * SLAB
    w = WIDTH
    out_specs = ([tl.row(w)] * (2 + N_RWKV_PRE) + [tl.row(LANE), tl.row(B_KV_RANK)] + [tl.row(slabs)] * 3
                 + [tl.row(ZC), tl.row(ZD)])
    out_shape = ([tl.shape(w)] * (2 + N_RWKV_PRE) + [tl.shape(LANE), tl.shape(B_KV_RANK)]
                 + [tl.shape(slabs, BF16)] * 3 + [tl.shape(ZC), tl.shape(ZD)])
    outs = pl.pallas_call(
        _inproj_kernel,
        grid=tl.grid,
        in_specs=[tl.row(D_MODEL), mod, mod, pl.BlockSpec((None, D_MODEL, ZA + ZB), lambda i: (l, 0, 0)),
                  pl.BlockSpec((None, D_MODEL, ZC + ZD), lambda i: (l, 0, 0))]
                 + [_full(a) for a in a_par] + [tl.row(LANE), tl.row(LANE)] + [_full(a) for a in b_par],
        out_specs=out_specs,
        out_shape=out_shape,
        scratch_shapes=[pltpu.VMEM((tl.tm, ZA), F32), pltpu.VMEM((tl.tm, ZB), F32)],
        compiler_params=_cp(("parallel",)),
        name='mixer_in',
    )(x, sh, sc, w_ab, w_cd, *a_par, cos, sin, *b_par)
    r, v = outs[0:2]
    pre = outs[2:2 + N_RWKV_PRE]
    kpe = outs[2 + N_RWKV_PRE]
    mla = outs[3 + N_RWKV_PRE:3 + N_RWKV_PRE + N_MLA_PREP]
    zc, zd = outs[-2:]
    return r, v, pre, kpe, mla, zc, zd


def _gelu_tanh(x):
    return 0.5 * x * (1.0 + jnp.tanh(np.sqrt(2.0 / np.pi) * (x + 0.044715 * (x * x * x))))


def _outproj_kernel(x_ref, af_ref, ab_ref, bon_ref, gate_ref, yb_ref, hf_ref, hb_ref, cg_ref, df_ref, db_ref,
                    dg_ref, agn_g_ref, agn_b_ref, dgn_g_ref, ebd_ref, w_ref, g_ref, lng_ref, lnb_ref, o_ref):
    avg = (ebd_ref[...] * (1.0 / HEAD_DIM)).astype(BF16)
    oa = af_ref[...] + ab_ref[...]
    da = oa - _head_sums(oa, avg)
    gn = da * lax.rsqrt(_head_sums(da * da, avg) + RWKV_GN_EPS) * agn_g_ref[...] + agn_b_ref[...]
    ya = (gn + bon_ref[...]) * gate_ref[...]
    h = jnp.concatenate([hf_ref[p] + hb_ref[p] for p in range(WIDTH // LANE)], axis=-1)
    yc = h * _gelu_tanh(cg_ref[...])
    od = df_ref[...] + db_ref[...]
    yd = od * lax.rsqrt(_head_sums(od * od, avg) + RMS_EPS) * dgn_g_ref[...] * _silu(dg_ref[...])
    y = _dot(ya.astype(BF16), w_ref[0:WIDTH, :])
    y += _dot(yb_ref[...].astype(BF16), w_ref[WIDTH:2 * WIDTH, :])
    y += _dot(yc.astype(BF16), w_ref[2 * WIDTH:3 * WIDTH, :])
    y += _dot(yd.astype(BF16), w_ref[3 * WIDTH:4 * WIDTH, :])
    o_ref[...] = _layer_norm_rows(DN_ALPHA * x_ref[...] + g_ref[0] * y, lng_ref[...], lnb_ref[...])


def _outproj(x, oa_f, oa_b, bonus, gate, yb, hf, hb, zc, od_f, od_b, zd, agn_g, agn_b, dgn_g, ebd,
             w_out_bf, g, lng, lnb, l, tl):
    vec = pl.BlockSpec((1, D_MODEL), lambda i: (0, 0))
    wvec = pl.BlockSpec((1, WIDTH), lambda i: (0, 0))
    ps = tl.per_seg
    halves = pl.BlockSpec((WIDTH // LANE, None, tl.tm, LANE), lambda i: (0, i // ps, i % ps, 0))
    w = WIDTH
    return pl.pallas_call(
        _outproj_kernel,
        grid=tl.grid,
        in_specs=[tl.row(D_MODEL), tl.row(w), tl.row(w), tl.row(w), tl.row(w), tl.row(w), halves, halves,
                  tl.row(w, 1), tl.row(w), tl.row(w), tl.row(w, 4), wvec, wvec, wvec, _full(ebd),
                  pl.BlockSpec((None, 4 * WIDTH, D_MODEL), lambda i: (l, 0, 0)), tl.mod(), vec, vec],
        out_specs=tl.row(D_MODEL),
        out_shape=tl.shape(D_MODEL),
        compiler_params=_cp(("parallel",)),
        name='mixer_out',
    )(x, oa_f, oa_b, bonus, gate, yb, hf, hb, zc, od_f, od_b, zd, agn_g, agn_b, dgn_g, ebd, w_out_bf, g, lng, lnb)


class _Streams:
    def __init__(self, n_seg, n_ctx, seg_rows, ctx_len, chunk):
        self.n_seg, self.n_ctx, self.chunk = n_seg, n_ctx, chunk
        self.n = seg_rows // chunk
        self.ctx_n = ctx_len // chunk

    def fwd(self, width, col=0):
        return pl.BlockSpec((self.n_seg, self.chunk, width), lambda c: (0, c, col))

    def bwd(self, width, col=0):
        n = self.n
        return pl.BlockSpec((self.n_seg, self.chunk, width), lambda c: (0, n - 1 - c, col))

    def fin_fwd(self, shape):
        ctx_n = self.ctx_n
        return pl.BlockSpec((1,) + shape, lambda c: (c // ctx_n,) + (0,) * len(shape))

    def fin_bwd(self, shape):
        ctx_n, n_ctx = self.ctx_n, self.n_ctx
        return pl.BlockSpec((1,) + shape, lambda c: (n_ctx - 1 - c // ctx_n,) + (0,) * len(shape))

    def ctx_first(self, c):
        return c % self.ctx_n == 0

    def ctx_last(self, c):
        return c % self.ctx_n == self.ctx_n - 1


def _rwkv_pre_kernel(za_ref, w0_ref, w2_ref, a0_ref, a2_ref, g2_ref, kkp_ref, kap_ref, rkp_ref, ebd_ref,
                     kap_o, lwf_o, lwb_o, kdf_o, kdb_o, bbf_o, bbb_o, bon_o, gate_o):
    r = za_ref[:, 0:WIDTH]
    k = za_ref[:, WIDTH:2 * WIDTH]
    v = za_ref[:, 2 * WIDTH:3 * WIDTH]
    o = 3 * WIDTH
    xw = (za_ref[:, o:o + A_DECAY_RANK], za_ref[:, o + A_DECAY_RANK:o + 2 * A_DECAY_RANK])
    o += 2 * A_DECAY_RANK
    xa = (za_ref[:, o:o + A_ICLR_RANK], za_ref[:, o + A_ICLR_RANK:o + 2 * A_ICLR_RANK])
    o += 2 * A_ICLR_RANK
    xg = za_ref[:, o:o + A_GATE_RANK]
    ebd = ebd_ref[...].astype(BF16)

    kk = k * kkp_ref[...]
    nrm = jnp.sqrt(_head_sums(kk * kk, ebd))
    kappa = kk / jnp.maximum(nrm, 1e-12)
    kap_o[...] = kappa
    k_sum = jnp.zeros_like(r)
    for d, (lw_o, kd_o, bb_o) in enumerate(((lwf_o, kdf_o, bbf_o), (lwb_o, kdb_o, bbb_o))):
        w_log = -_softplus(-(w0_ref[d] + _dot(jnp.tanh(xw[d]).astype(BF16), w2_ref[d].astype(BF16)))) - 0.5
        lw_o[...] = -jnp.exp(w_log)
        a = _sigmoid(a0_ref[d] + _dot(xa[d].astype(BF16), a2_ref[d].astype(BF16)))
        k_d = k * (1.0 + (a - 1.0) * kap_ref[...])
        kd_o[...] = k_d
        bb_o[...] = kappa * a
        k_sum += k_d
    bon_o[...] = _head_sums(r * k_sum * rkp_ref[...], ebd) * v
    gate_o[...] = _dot(_sigmoid(xg).astype(BF16), g2_ref[...].astype(BF16))


def _rwkv_chunk(r, v, kap, lw, kd, bb, st, mbd, eye, m_strict, m_incl, tri):
    cum = _dot(tri, lw, hi=True)
    yield
    tot = jnp.sum(lw, axis=0, keepdims=True)
    g_in = jnp.exp(cum)
    g_inv = jnp.exp(-cum)
    g_ex = jnp.exp(cum - lw)
    g_end = jnp.exp(tot - cum)
    cast = (lambda a: a) if RWKV_HI else (lambda a: a.astype(BF16))
    mask2 = cast(mbd[:LANE, :LANE])
    x = cast(jnp.concatenate([kap * g_ex, r * g_in], axis=0))
    ab = _pair_dot_nt(x, _pair_bd(cast(bb * g_inv), mask2), RWKV_HI)
    ak = _pair_dot_nt(x, _pair_bd(cast(kd * g_inv), mask2), RWKV_HI)
    xs = _pair_dot_nt(x, [cast(st[sl, sl]) for sl in _PAIRS], RWKV_HI)
    yield
    c = r.shape[0]
    strict = m_strict > 0.5
    incl = m_incl > 0.5
    a_ub = jnp.where(strict, ab[:c], 0.0)
    a_rb = jnp.where(incl, ab[c:], 0.0)
    a_uk = jnp.where(strict, ak[:c], 0.0)
    a_rk = jnp.where(incl, ak[c:], 0.0)

    def catmul(pc, q):
        return _pair_dot(cast(pc), _pair_bd(cast(q), mask2), RWKV_HI)

    xs = xs + catmul(jnp.concatenate([a_uk, a_rk], axis=0), v)
    rhs, o_v = xs[:c], xs[c:]
    xp = -a_ub
    inv = eye + xp
    xp = catmul(xp, xp)
    yield
    for _ in range(int(np.log2(c)) - 2):
        both = catmul(jnp.concatenate([inv, xp], axis=0), xp)
        yield
        inv = inv + both[:c]
        xp = both[c:]
    inv = inv + catmul(inv, xp)
    yield
    u = -catmul(inv, rhs)
    yield
    o = o_v + catmul(a_rb, u)
    uv = cast(jnp.concatenate([u, v], axis=0))
    bk = cast(jnp.concatenate([bb * g_end, kd * g_end], axis=0))
    upd = _pair_blocks([_dot_tn(uv[:, sl], bk[:, sl], hi=RWKV_HI) * mbd[:LANE, :LANE] for sl in _PAIRS])
    st_new = st * jnp.exp(tot) + upd
    return o, st_new


def _interleave(gens):
    results = [None] * len(gens)
    live = list(range(len(gens)))
    while live:
        still = []
        for k in live:
            try:
                next(gens[k])
                still.append(k)
            except StopIteration as done:
                results[k] = done.value
        live = still
    return results


def _init_states(st, c, st_scr, s0_ref):
    @pl.when(c == 0)
    def _():
        st_scr[1:] = s0_ref[...]

    @pl.when(st.ctx_first(c))
    def _():
        st_scr[0] = jnp.zeros(st_scr.shape[1:], F32)


def _hgrn_chunk(xq, xf, xi, lb, st, tri, ebd, mbd, p_scr, reverse):
    c = xq.shape[0]
    nb = c // SUB
    q = _silu(xq)
    gsig = lb + (1.0 - lb) * _sigmoid(xf)
    kk = 1.0 - gsig
    lg = jnp.log(gsig)
    cum = _dot(tri, lg, hi=True)
    yield
    cum = cum * LOG2E
    tot = jnp.sum(lg, axis=0, keepdims=True) * LOG2E
    lane_s = lax.broadcasted_iota(jnp.int32, (SUB, WIDTH), 1) % c
    half = SUB // 2
    row_h = lax.broadcasted_iota(jnp.int32, (half, WIDTH), 0)
    blk = lambda a, i: a[i * SUB:(i + 1) * SUB]
    end_row = (lambda j: j * SUB) if reverse else (lambda j: j * SUB + SUB - 1)
    later = (lambda j: range(0, j)) if reverse else (lambda j: range(j + 1, nb))

    k_end = jnp.concatenate([blk(kk, j) * jnp.exp2(cum[end_row(j):end_row(j) + 1] - blk(cum, j))
                             for j in range(nb)], axis=0)
    q_parts, where_part = [], {}
    for j in range(nb):
        for i in later(j):
            where_part[(i, j)] = len(q_parts)
            q_parts.append(blk(q, i) * jnp.exp2(blk(cum, i) - cum[end_row(j):end_row(j) + 1]))
    mask2 = mbd[:LANE, :LANE].astype(BF16)
    cross = _pair_dot_nt(jnp.concatenate(q_parts, axis=0).astype(BF16), _pair_bd(k_end.astype(BF16), mask2))
    o_state = _pair_dot_nt((q * jnp.exp2(cum)).astype(BF16), [st[sl, sl].astype(BF16) for sl in _PAIRS])
    xi_bf = xi.astype(BF16)
    k_out = (kk * jnp.exp2(tot - cum)).astype(BF16)
    upd = _pair_blocks([_dot_tn(xi_bf[:, sl], k_out[:, sl]) * mbd[:LANE, :LANE] for sl in _PAIRS])
    yield

    for i in range(nb):
        cb, qb = blk(cum, i), blk(q, i)
        for sl in range(SUB):
            s = i * SUB + sl
            parts = []
            for lo in (0, half):
                hi = lo + half - 1
                none_valid = lo > sl if reverse else hi < sl
                all_valid = hi <= sl if reverse else lo >= sl
                if none_valid:
                    parts.append(jnp.zeros((half, WIDTH), F32))
                    continue
                d = cb[lo:lo + half] - cum[s:s + 1]
                if not all_valid:
                    valid = (row_h + lo <= sl) if reverse else (row_h + lo >= sl)
                    d = jnp.where(valid, d, NEG)
                parts.append(jnp.exp2(d) * qb[lo:lo + half] * kk[s:s + 1])
            p_scr[s * SUB:(s + 1) * SUB, :] = jnp.concatenate(parts, axis=0).astype(BF16)
    same = _pair_dot(p_scr, [mask2] * len(_PAIRS))
    yield

    att_rows = []
    for i in range(nb):
        att = jnp.zeros((SUB, WIDTH), F32)
        for sl in range(SUB):
            s = i * SUB + sl
            att = jnp.where(lane_s == s, same[s * SUB:(s + 1) * SUB], att)
        for j in range(nb):
            if (i, j) in where_part:
                n = where_part[(i, j)]
                att = jnp.where(lane_s // SUB == j, cross[n * SUB:(n + 1) * SUB], att)
        att_rows.append(att)
    att = jnp.concatenate(att_rows, axis=0)
    o = o_state + _pair_dot(att.astype(BF16), _pair_bd(xi_bf, mask2))
    st_new = st * jnp.exp2(tot) + upd
    return o, st_new


def _mix_scan_kernel(st,
                     rf_ref, vf_ref, kapf_ref, lwf_ref, kdf_ref, bbf_ref,
                     rb_ref, vb_ref, kapb_ref, lwb_ref, kdb_ref, bbb_ref,
                     qf_ref, ff_ref, if_ref, qb_ref, fb_ref, ib_ref, lb_ref,
                     sa0_ref, sd0_ref, mbd_ref, eye_ref, lows_ref, lowi_ref, ups_ref, upi_ref, trif_ref, trib_ref,
                     af_ref, ab_ref, afin_f, afin_b, df_ref, db_ref, dfin_f, dfin_b,
                     sa_scr, sd_scr, p_scr):
    c = pl.program_id(0)
    _init_states(st, c, sa_scr, sa0_ref)
    _init_states(st, c, sd_scr, sd0_ref)
    mbd = mbd_ref[...]
    eye = eye_ref[...]
    gens = []
    for s in range(st.n_seg):
        gens.append(_rwkv_chunk(rf_ref[s], vf_ref[s], kapf_ref[s], lwf_ref[s], kdf_ref[s], bbf_ref[s],
                                sa_scr[s, 0], mbd, eye, lows_ref[...], lowi_ref[...], trif_ref[...]))
        gens.append(_hgrn_chunk(qf_ref[s], ff_ref[s], if_ref[s], lb_ref[0], sd_scr[s, 0], trif_ref[...],
                                mbd, mbd, p_scr.at[s, 0], False))
        gens.append(_rwkv_chunk(rb_ref[s], vb_ref[s], kapb_ref[s], lwb_ref[s], kdb_ref[s], bbb_ref[s],
                                sa_scr[s, 1], mbd, eye, ups_ref[...], upi_ref[...], trib_ref[...]))
        gens.append(_hgrn_chunk(qb_ref[s], fb_ref[s], ib_ref[s], lb_ref[1], sd_scr[s, 1], trib_ref[...],
                                mbd, mbd, p_scr.at[s, 1], True))
    outs = ((af_ref, sa_scr), (df_ref, sd_scr), (ab_ref, sa_scr), (db_ref, sd_scr))
    for k, (o, st_new) in enumerate(_interleave(gens)):
        s, kind = k // 4, k % 4
        o_ref, scr = outs[kind]
        o_ref[s] = o
        scr[s, kind // 2] = st_new

    @pl.when(st.ctx_last(c))
    def _():
        for h in range(N_HEADS):
            hs = slice(h * HEAD_DIM, (h + 1) * HEAD_DIM)
            afin_f[0, h] = sa_scr[0, 0, hs, hs]
            afin_b[0, h] = sa_scr[0, 1, hs, hs]
            dfin_f[0, h] = sd_scr[0, 0, hs, hs].T
            dfin_b[0, h] = sd_scr[0, 1, hs, hs].T


def _mix_scan(r, v, pre, zd, lb, sa0_dec, sd0_dec, hc, st):
    kap, lwf, lwb, kdf, kdb, bbf, bbb = pre
    consts = (hc['mbd'], hc['eye'], hc['low_s'], hc['low_i'], hc['up_s'], hc['up_i'], hc['tri_f'], hc['tri_b'])
    w = WIDTH
    o_shape = jax.ShapeDtypeStruct(kap.shape, F32)
    head_blocks = (N_HEADS, HEAD_DIM, HEAD_DIM)
    fin_shape = jax.ShapeDtypeStruct((st.n_ctx,) + head_blocks, F32)
    fin_specs = [st.fin_fwd(head_blocks), st.fin_bwd(head_blocks)]
    outs = pl.pallas_call(
        functools.partial(_mix_scan_kernel, st),
        grid=(st.n,),
        in_specs=[st.fwd(w)] * 6 + [st.bwd(w)] * 6 + [
                  st.fwd(w, 0), st.fwd(w, 1), st.fwd(w, 3), st.bwd(w, 0), st.bwd(w, 2), st.bwd(w, 3),
                  _full(lb), _full(sa0_dec), _full(sd0_dec)] + [_full(a) for a in consts],
        out_specs=[st.fwd(w), st.bwd(w)] + fin_specs + [st.fwd(w), st.bwd(w)] + fin_specs,
        out_shape=[o_shape, o_shape, fin_shape, fin_shape] * 2,
        scratch_shapes=[pltpu.VMEM((st.n_seg, 2, w, w), F32), pltpu.VMEM((st.n_seg, 2, w, w), F32),
                        pltpu.VMEM((st.n_seg, 2, CHUNK * SUB, w), BF16)],
        compiler_params=_cp(("arbitrary",)),
        name='mix_scan',
    )(r, v, kap, lwf, kdf, bbf, r, v, kap, lwb, kdb, bbb, zd, zd, zd, zd, zd, zd, lb, sa0_dec, sd0_dec, *consts)
    a_f, a_b, afin_f, afin_b, d_f, d_b, dfin_f, dfin_b = outs
    return ((a_f, a_b, jnp.stack([afin_f, afin_b], axis=1)), (d_f, d_b, jnp.stack([dfin_f, dfin_b], axis=1)))


def _lru_coefficients(d, base, seqlen, xp, x, xn, cw_ref, cb_ref, wg_ref, bg_ref, lam_ref, pad, coef):
    tm = x.shape[0]
    pad[0:8, :] = xp
    pad[8:8 + tm, :] = x
    pad[8 + tm:16 + tm, :] = xn
    pos = jnp.bitwise_and(lax.broadcasted_iota(jnp.int32, (tm, WIDTH), 0) + base, seqlen - 1)
    u = jnp.zeros((tm, WIDTH), F32) + cb_ref[...]
    for j in range(C_CONV):
        off = j - C_CONV // 2
        tap = pad[pl.ds(8 + off, tm), :]
        ok = jnp.logical_and(pos + off >= 0, pos + off < seqlen)
        u += jnp.where(ok, tap, 0.0) * cw_ref[j:j + 1, :]
    cols = slice(2 * d * WIDTH, (2 * d + 2) * WIDTH)
    pre = _dot(u.astype(BF16), wg_ref[:, cols].astype(BF16))
    yield
    gates = _sigmoid(pre + bg_ref[:, cols])
    r, ig = gates[:, :WIDTH], gates[:, WIDTH:]
    log_a = -C_POW * r * _softplus(-lam_ref[d])
    a = jnp.exp(log_a)
    b = jnp.sqrt(-jnp.tanh(log_a) * (a * a + 1.0)) * (ig * u)
    for half in range(WIDTH // LANE):
        coef[0, half] = a[:, half * LANE:(half + 1) * LANE]
        coef[1, half] = b[:, half * LANE:(half + 1) * LANE]


def _lru_scan_kernel(st, seqlens, xf_ref, xpf_ref, xnf_ref, xb_ref, xpb_ref, xnb_ref,
                     cw_ref, cb_ref, wg_ref, bg_ref, lam_ref, h0_ref,
                     hf_ref, hb_ref, finf_ref, finb_ref, h_scr, loc_scr, car_scr, pad_scr, coef_scr):
    c = pl.program_id(0)
    nh = WIDTH // LANE

    @pl.when(c == 0)
    def _():
        h_scr[:, 1:] = h0_ref[...]

    @pl.when(st.ctx_first(c))
    def _():
        h_scr[:, 0] = jnp.zeros((nh, 2, 1, LANE), F32)

    chunk = xf_ref.shape[1]
    x_refs = ((xf_ref, xpf_ref, xnf_ref), (xb_ref, xpb_ref, xnb_ref))
    bases = (c * chunk, (st.n - 1 - c) * chunk)
    gens = []
    for s in range(st.n_seg):
        for d in range(2):
            x_ref, xp_ref, xn_ref = x_refs[d]
            gens.append(_lru_coefficients(d, bases[d], seqlens[s], xp_ref[s], x_ref[s], xn_ref[s], cw_ref, cb_ref,
                                          wg_ref, bg_ref, lam_ref, pad_scr.at[2 * s + d], coef_scr.at[2 * s + d]))
    _interleave(gens)

    rr = LRU_ROWS
    ng = chunk // rr
    chains = [(p, s, d) for s in range(st.n_seg) for d in range(2) for p in range(nh)]
    o_refs = (hf_ref, hb_ref)
    order = (list(range(rr)), list(range(rr - 1, -1, -1)))

    for k, (p, s, d) in enumerate(chains):
        hloc = ploc = None
        for r in order[d]:
            a = coef_scr[2 * s + d, 0, p, pl.ds(r, ng, stride=rr), :]
            b = coef_scr[2 * s + d, 1, p, pl.ds(r, ng, stride=rr), :]
            hloc = b if hloc is None else a * hloc + b
            ploc = a if ploc is None else a * ploc
            loc_scr[k, 0, r] = hloc
            loc_scr[k, 1, r] = ploc

    def carry_step(j, carries):
        out = []
        for k, (p, s, d) in enumerate(chains):
            g = j if d == 0 else ng - 1 - j
            r_end = order[d][-1]
            car_scr[k, pl.ds(g, 1), :] = carries[k]
            out.append(loc_scr[k, 1, r_end, pl.ds(g, 1), :] * carries[k] + loc_scr[k, 0, r_end, pl.ds(g, 1), :])
        return tuple(out)

    carries = lax.fori_loop(0, ng, carry_step, tuple(h_scr[p, s, d] for p, s, d in chains))

    for k, (p, s, d) in enumerate(chains):
        h_scr[p, s, d] = carries[k]
        o_ref = o_refs[d]
        car = car_scr[k]
        for r in range(rr):
            o_ref[p, s, pl.ds(r, ng, stride=rr), :] = loc_scr[k, 0, r] + loc_scr[k, 1, r] * car

    @pl.when(st.ctx_last(c))
    def _():
        for p in range(nh):
            finf_ref[0, :, p * LANE:(p + 1) * LANE] = h_scr[p, 0, 0]
            finb_ref[0, :, p * LANE:(p + 1) * LANE] = h_scr[p, 0, 1]


def _lru_scan(zc, p, h0_dec, ctx_len, dec_len, st):
    nh = WIDTH // LANE
    n_seg, seg_rows = zc.shape[:2]
    n_dec = h0_dec.shape[0]
    h0 = jnp.transpose(h0_dec.reshape(n_dec, 2, nh, 1, LANE), (2, 0, 1, 3, 4))
    n_chain = 2 * n_seg * nh
    chunk = st.chunk
    ng = chunk // LRU_ROWS
    n = st.n
    per = chunk // 8
    last8 = seg_rows // 8 - 1
    x_spec = lambda blk: [pl.BlockSpec((n_seg, chunk, WIDTH), lambda c: (0, blk(c), 0)),
                          pl.BlockSpec((n_seg, 8, WIDTH), lambda c: (0, jnp.maximum(blk(c) * per - 1, 0), 0)),
                          pl.BlockSpec((n_seg, 8, WIDTH), lambda c: (0, jnp.minimum((blk(c) + 1) * per, last8), 0))]
    h_fwd = pl.BlockSpec((nh, n_seg, chunk, LANE), lambda c: (0, 0, c, 0))
    h_bwd = pl.BlockSpec((nh, n_seg, chunk, LANE), lambda c: (0, 0, n - 1 - c, 0))
    args = (p['conv_w'], p['conv_b'], p['wg'], p['bg'], p['lam'], h0)
    seqlens = (ctx_len,) + (dec_len,) * n_dec
    h_shape = jax.ShapeDtypeStruct((nh, n_seg, seg_rows, LANE), F32)
    fin_shape = jax.ShapeDtypeStruct((st.n_ctx, 1, WIDTH), F32)
    h_f, h_b, fin_f, fin_b = pl.pallas_call(
        functools.partial(_lru_scan_kernel, st, seqlens),
        grid=(n,),
        in_specs=x_spec(lambda c: c) + x_spec(lambda c: n - 1 - c) + [_full(a) for a in args],
        out_specs=[h_fwd, h_bwd, st.fin_fwd((1, WIDTH)), st.fin_bwd((1, WIDTH))],
        out_shape=[h_shape, h_shape, fin_shape, fin_shape],
        scratch_shapes=[pltpu.VMEM((nh, n_seg, 2, 1, LANE), F32),
                        pltpu.VMEM((n_chain, 2, LRU_ROWS, ng, LANE), F32),
                        pltpu.VMEM((n_chain, ng, LANE), F32),
                        pltpu.VMEM((2 * n_seg, chunk + 16, WIDTH), F32),
                        pltpu.VMEM((2 * n_seg, 2, nh, chunk, LANE), F32)],
        compiler_params=_cp(("arbitrary",)),
        name='rglru',
    )(zc, zc, zc, zc, zc, zc, *args)
    return h_f, h_b, jnp.concatenate([fin_f, fin_b], axis=1)


def _rope_slab(x, cos, sin):
    lane = lax.broadcasted_iota(jnp.int32, x.shape, 1)
    rot = jnp.where(lane % (B_ROPE // 2) < B_ROPE // 4, -pltpu.roll(x, LANE - B_ROPE // 4, 1),
                    pltpu.roll(x, B_ROPE // 4, 1))
    return x * cos + rot * sin


def _kv_up(ckv_bf, kpe_slab, wuk_ref, wuv_ref, k_o, v_o):
    kn = _dot(ckv_bf, wuk_ref[...])
    for h in range(N_HEADS):
        k_o[:, h * SLAB:(h + 1) * SLAB] = (kn[:, h * SLAB:(h + 1) * SLAB] + kpe_slab).astype(BF16)
    lane = lax.broadcasted_iota(jnp.int32, (1, N_HEADS * SLAB), 1)
    v_o[...] = (_dot(ckv_bf, wuv_ref[...]) + jnp.where(lane % SLAB >= B_VDIM, 1.0, 0.0)).astype(BF16)


def _mla_prep_kernel(zb_ref, cos_ref, sin_ref, qn_ref, kvn_ref, wuq_ref, wuk_ref, wuv_ref,
                     ckv_o, q_o, k_o, v_o):
    cq = zb_ref[:, 0:B_Q_RANK]
    ckv = zb_ref[:, B_Q_RANK:B_Q_RANK + B_KV_RANK]
    kpe = zb_ref[:, B_Q_RANK + B_KV_RANK:ZB]
    cos = cos_ref[...]
    sin = sin_ref[...]
    cqn = cq * lax.rsqrt(jnp.mean(cq * cq, axis=-1, keepdims=True) + RMS_EPS) * qn_ref[...]
    ckvn = ckv * lax.rsqrt(jnp.mean(ckv * ckv, axis=-1, keepdims=True) + RMS_EPS) * kvn_ref[...]
    ckv_o[...] = ckvn
    q = _dot(cqn.astype(BF16), wuq_ref[...])
    for h in range(N_HEADS):
        q_o[:, h * SLAB:(h + 1) * SLAB] = (_rope_slab(q[:, h * SLAB:(h + 1) * SLAB], cos, sin)
                                           * ATTN_LOG2_SCALE).astype(BF16)
    _kv_up(ckvn.astype(BF16), _rope_slab(kpe, cos, sin), wuk_ref, wuv_ref, k_o, v_o)


def _mla_cache_kernel(ckv_ref, kpe_ref, wuk_ref, wuv_ref, k_o, v_o):
    _kv_up(ckv_ref[...].astype(BF16), kpe_ref[...], wuk_ref, wuv_ref, k_o, v_o)


def _mla_cache(ckv, kpe_slab, p):
    rows = ckv.shape[0]
    return pl.pallas_call(
        _mla_cache_kernel,
        out_shape=[jax.ShapeDtypeStruct((rows, N_HEADS * SLAB), BF16)] * 2,
        compiler_params=pltpu.CompilerParams(vmem_limit_bytes=VMEM_LIMIT),
        name='mla_cache',
    )(ckv, kpe_slab, p['wuk'], p['wuv'])


def _attend_heads(q_ref, kv_refs, o_ref):
    tq = q_ref.shape[0]
    low = lax.broadcasted_iota(jnp.int32, (tq, SLAB), 1) < B_VDIM

    def head(h):
        qh = q_ref[:, h * SLAB:(h + 1) * SLAB]
        ss = [_dot_nt(qh, k_ref[:, h * SLAB:(h + 1) * SLAB]) for k_ref, _ in kv_refs]
        yield
        m = ss[0].max(axis=-1, keepdims=True)
        for s in ss[1:]:
            m = jnp.maximum(m, s.max(axis=-1, keepdims=True))
        es = [jnp.exp2((s - m).astype(BF16)) for s in ss]
        yield
        acc = jnp.zeros((tq, SLAB), F32)
        for e, (_, v_ref) in zip(es, kv_refs):
            acc += _dot(e, v_ref[:, h * SLAB:(h + 1) * SLAB])
        return acc / acc[:, B_VDIM:B_VDIM + 1]

    pairs = []
    for h in range(0, N_HEADS, 2):
        h0, h1 = _interleave([head(h), head(h + 1)])
        pairs.append(jnp.where(low, h0, pltpu.roll(h1, B_VDIM, 1)))
    o_ref[...] = jnp.concatenate(pairs, axis=-1)


def _attn_kernel(n_ctx, q_ref, kx_ref, vx_ref, k_ref, v_ref, kc_ref, vc_ref, o_ref):
    i = pl.program_id(0)

    @pl.when(i < n_ctx)
    def _():
        _attend_heads(q_ref, [(kx_ref, vx_ref)], o_ref)

    @pl.when(i >= n_ctx)
    def _():
        _attend_heads(q_ref, [(k_ref, v_ref), (kc_ref, vc_ref)], o_ref)


def _attn(q, k, v, kc, vc, n_ctx, ctx_len, n_dec, dec_len, past):
    tq = ctx_len
    nq = dec_len // tq
    slabs = N_HEADS * SLAB
    dec = lambda i: jnp.maximum(i - n_ctx, 0)

    def q_map(i):
        return (jnp.where(i < n_ctx, 0, 1 + dec(i) // nq), jnp.where(i < n_ctx, i, dec(i) % nq), 0)

    ctx_kv = pl.BlockSpec((None, ctx_len, slabs), lambda i: (0, jnp.minimum(i, n_ctx - 1), 0))
    dec_kv = pl.BlockSpec((None, dec_len, slabs), lambda i: (1 + dec(i) // nq, 0, 0))
    cache_kv = pl.BlockSpec((past, slabs), lambda i: (dec(i) // nq, 0))
    return pl.pallas_call(
        functools.partial(_attn_kernel, n_ctx),
        grid=(n_ctx + n_dec * nq,),
        in_specs=[pl.BlockSpec((None, tq, slabs), q_map), ctx_kv, ctx_kv, dec_kv, dec_kv, cache_kv, cache_kv],
        out_specs=pl.BlockSpec((None, tq, WIDTH), q_map),
        out_shape=jax.ShapeDtypeStruct(q.shape[:2] + (WIDTH,), F32),
        compiler_params=_cp(("arbitrary",)),
        name='attn',
    )(q, k, v, k, v, kc, vc)


def _rope_tables(n_ctx_rows, n_dec, dec_len):
    rows = dec_len // GRID_W
    row = np.repeat(np.arange(rows, dtype=np.float32), GRID_W)
    col = np.tile(np.arange(GRID_W, dtype=np.float32), rows)
    half = B_ROPE // 2
    inv = jnp.power(ROPE_BASE, -jnp.arange(0, half, 2, dtype=F32) / half)
    ang_r = jnp.asarray(row)[:, None] * inv
    ang_c = jnp.asarray(col)[:, None] * inv
    ang = jnp.concatenate([ang_r, ang_r, ang_c, ang_c], axis=-1)
    cos = jnp.pad(jnp.cos(ang), ((0, 0), (0, LANE - B_ROPE)), constant_values=1.0)
    sin = jnp.pad(jnp.sin(ang), ((0, 0), (0, LANE - B_ROPE)))
    cos = jnp.concatenate([jnp.ones((n_ctx_rows, LANE), F32)] + [cos] * n_dec, axis=0)
    sin = jnp.concatenate([jnp.zeros((n_ctx_rows, LANE), F32)] + [sin] * n_dec, axis=0)
    return cos, sin


def _block_diag(w):
    g, n, m = w.shape[-3:]
    eye = jnp.eye(g, dtype=w.dtype)
    out = w[..., :, :, None, :] * eye[:, None, :, None]
    return out.reshape(w.shape[:-3] + (g * n, g * m))


def kernel(x_prompt, x_sample, cache_mla_ckv, cache_mla_kpe, state_rwkv, state_rglru, state_hgrn, c, c_ctx,
           w_ada, b_ada, ln_g, ln_b, w_ffn_in, w_ffn_out, w_in, w_out,
           rwkv_w0, rwkv_w2, rwkv_a0, rwkv_a2, rwkv_g2, rwkv_kk, rwkv_ka, rwkv_rk, rwkv_gn_g, rwkv_gn_b,
           mla_qn_g, mla_w_uq, mla_kvn_g, mla_w_ukv,
           rglru_conv_w, rglru_conv_b, rglru_wa, rglru_ba, rglru_wx, rglru_bx, rglru_lam,
           hgrn_lb, hgrn_gn_g):
    n_ctx, ctx_len, _ = x_prompt.shape
    n_dec, dec_len, _ = x_sample.shape
    past = cache_mla_ckv.shape[2]
    seg = n_ctx * ctx_len
    assert seg == dec_len, "context rows must form one segment of the decode sequence length"
    assert ctx_len & (ctx_len - 1) == 0 and dec_len & (dec_len - 1) == 0, "sequence lengths must be powers of two"
    n_seg = 1 + n_dec
    tl = _Tiles(n_seg, seg, min(TM, seg))
    assert ctx_len % LRU_CHUNK == 0
    st = _Streams(n_seg, n_ctx, seg, ctx_len, CHUNK)
    lru_st = _Streams(n_seg, n_ctx, seg, ctx_len, LRU_CHUNK)

    hc = {k: jnp.asarray(v) for k, v in _head_consts().items()}
    ebd = hc['mbd']
    cos, sin = (t.reshape(n_seg, seg, LANE) for t in _rope_tables(seg, n_dec, dec_len))

    cond8 = jnp.zeros((8, D_MODEL), F32).at[0].set(c_ctx).at[1:1 + n_dec].set(c)
    mods = _ada(cond8, w_ada, b_ada).reshape(DEPTH, 8, N_MOD, D_MODEL)
    mods = jnp.transpose(mods, (0, 2, 1, 3))[:, :, :n_seg, None, :]

    w_ffn_in_bf = w_ffn_in.astype(BF16)
    w_ffn_out_bf = w_ffn_out.astype(BF16)
    w_out_bf = w_out.astype(BF16)
    kpe_end = ZA + B_Q_RANK + B_KV_RANK + B_ROPE
    w_ab = jnp.pad(w_in[:, :, :kpe_end].astype(BF16), ((0, 0), (0, 0), (0, LANE - B_ROPE)))
    w_cd = w_in[:, :, kpe_end:].astype(BF16)
    wuq = mla_w_uq.reshape(DEPTH, B_Q_RANK, N_HEADS, B_NOPE + B_ROPE)
    wuq_p = jnp.concatenate([wuq[..., B_NOPE:], wuq[..., :B_NOPE],
                             jnp.zeros((DEPTH, B_Q_RANK, N_HEADS, SLAB - B_NOPE - B_ROPE), F32)], axis=-1)
    wuq_p = wuq_p.reshape(DEPTH, B_Q_RANK, N_HEADS * SLAB).astype(BF16)
    wukv = mla_w_ukv.reshape(DEPTH, B_KV_RANK, N_HEADS, B_NOPE + B_VDIM)
    wuk_p = jnp.concatenate([jnp.zeros((DEPTH, B_KV_RANK, N_HEADS, B_ROPE), F32), wukv[..., :B_NOPE],
                             jnp.zeros((DEPTH, B_KV_RANK, N_HEADS, SLAB - B_NOPE - B_ROPE), F32)], axis=-1)
    wuk_p = wuk_p.reshape(DEPTH, B_KV_RANK, N_HEADS * SLAB).astype(BF16)
    wuv_p = jnp.concatenate([wukv[..., B_NOPE:], jnp.zeros((DEPTH, B_KV_RANK, N_HEADS, SLAB - B_VDIM), F32)], axis=-1)
    wuv_p = wuv_p.reshape(DEPTH, B_KV_RANK, N_HEADS * SLAB).astype(BF16)
    lru_wg = jnp.concatenate([_block_diag(rglru_wa[:, 0]), _block_diag(rglru_wx[:, 0]),
                              _block_diag(rglru_wa[:, 1]), _block_diag(rglru_wx[:, 1])], axis=-1)
    lru_bg = jnp.concatenate([rglru_ba[:, 0], rglru_bx[:, 0], rglru_ba[:, 1], rglru_bx[:, 1]], axis=-1)[:, None, :]
    lbs = _lower_bounds(hgrn_lb)

    rwkv_s0 = _block_diag(state_rwkv)
    hgrn_s0 = _block_diag(jnp.swapaxes(state_hgrn, -1, -2))
    cache_kpe_slab = jnp.pad(cache_mla_kpe, ((0, 0), (0, 0), (0, 0), (0, LANE - B_ROPE)))

    new_ckv, new_kpe, new_rwkv, new_lru, new_hgrn = [], [], [], [], []
    for l in range(DEPTH):
        m = mods[l]
        lng = ln_g[l][:, None, :]
        lnb = ln_b[l][:, None, :]
        ffn0 = functools.partial(_ffn, sh=m[0], sc=m[1], g=m[2], lng=lng[0], lnb=lnb[0], w_in_bf=w_ffn_in_bf,
                                 w_out_bf=w_ffn_out_bf, l=l, f=0, tl=tl)
        if l == 0:
            x = ffn0((x_prompt.reshape(1, seg, D_MODEL), x_sample), split='in')
        else:
            x = ffn0(x)
        pa = {'w0': rwkv_w0[l][:, None, :], 'w2': rwkv_w2[l], 'a0': rwkv_a0[l][:, None, :], 'a2': rwkv_a2[l],
              'g2': rwkv_g2[l], 'kk': rwkv_kk[l][None, :], 'ka': rwkv_ka[l][None, :], 'rk': rwkv_rk[l][None, :]}
        pb = {'qn_g': mla_qn_g[l][None, :], 'kvn_g': mla_kvn_g[l][None, :], 'wuq': wuq_p[l], 'wuk': wuk_p[l],
              'wuv': wuv_p[l]}
        a_r, a_v, pre, kpe_raw, (ckvn, q_all, k_all, v_all), zc, zd = _inproj(x, m[3], m[4], w_ab, w_cd, l, pa, pb,
                                                                                ebd, cos, sin, tl)

        (oa_f, oa_b, sa_fin), (od_f, od_b, sd_fin) = _mix_scan(a_r, a_v, pre[:7], zd, lbs[l], rwkv_s0[:, l],
                                                                hgrn_s0[:, l], hc, st)
        new_rwkv.append(sa_fin)
        new_hgrn.append(sd_fin)

        kc, vc = _mla_cache(cache_mla_ckv[:, l].reshape(n_dec * past, B_KV_RANK),
                            cache_kpe_slab[:, l].reshape(n_dec * past, LANE), pb)
        yb = _attn(q_all, k_all, v_all, kc, vc, n_ctx, ctx_len, n_dec, dec_len, past)
        new_ckv.append(ckvn[0].reshape(n_ctx, ctx_len, B_KV_RANK))
        new_kpe.append(kpe_raw[0, :, :B_ROPE].reshape(n_ctx, ctx_len, B_ROPE))

        pc = {'conv_w': rglru_conv_w[l], 'conv_b': rglru_conv_b[l][None, :], 'wg': lru_wg[l], 'bg': lru_bg[l],
              'lam': rglru_lam[l][:, None, :]}
        h_f, h_b, h_fin = _lru_scan(zc, pc, state_rglru[:, l], ctx_len, dec_len, lru_st)
        new_lru.append(h_fin)

        x = _outproj(x, oa_f, oa_b, pre[7], pre[8], yb, h_f, h_b, zc, od_f, od_b, zd,
                     rwkv_gn_g[l][None, :], rwkv_gn_b[l][None, :], hgrn_gn_g[l][None, :], ebd,
                     w_out_bf, m[5], lng[1], lnb[1], l, tl)
        ffn1 = functools.partial(_ffn, sh=m[6], sc=m[7], g=m[8], lng=lng[2], lnb=lnb[2], w_in_bf=w_ffn_in_bf,
                                 w_out_bf=w_ffn_out_bf, l=l, f=1, tl=tl)
        if l == DEPTH - 1:
            y_prompt, y_sample = ffn1(x, split='out')
            y_prompt = y_prompt.reshape(n_ctx, ctx_len, D_MODEL)
        else:
            x = ffn1(x)
    return (y_prompt, y_sample, jnp.stack(new_ckv, axis=1), jnp.stack(new_kpe, axis=1),
            jnp.stack(new_rwkv, axis=1), jnp.stack(new_lru, axis=1), jnp.stack(new_hgrn, axis=1))
```

```python
import functools

import numpy as np
import jax
import jax.numpy as jnp
from jax import lax
from jax.experimental import pallas as pl
from jax.experimental.pallas import tpu as pltpu

D_MODEL = 1024
DEPTH = 4
GRID_W = 64
HEAD_DIM = 64
N_HEADS = 4
WIDTH = N_HEADS * HEAD_DIM
A_DECAY_RANK = 64
A_ICLR_RANK = 64
A_GATE_RANK = 128
B_NOPE = 64
B_ROPE = 32
B_VDIM = 64
B_Q_RANK = 256
B_KV_RANK = 128
C_CONV = 4
C_POW = 8.0
D_FF = 2816
N_MOD = 9
ROPE_BASE = 10000.0
LN_EPS = 1e-5
RMS_EPS = 1e-6
RWKV_GN_EPS = 64e-5
DN_ALPHA = (2 * DEPTH) ** 0.25

LANE = 128
SLAB = 128
CHUNK = 64
SUB = 16
SUBLANE = 8
LRU_CHUNK = 256
LRU_ROWS = SUBLANE
TM = 512
TF = 256
ADA_TN = 2304
VMEM_LIMIT = 56 * 1024 * 1024

F32 = jnp.float32
BF16 = jnp.bfloat16
HI = lax.Precision.HIGHEST
NEG = -1e30
LOG2E = float(np.log2(np.e))
ATTN_LOG2_SCALE = float((B_NOPE + B_ROPE) ** -0.5) * LOG2E


def _cp(sem):
    return pltpu.CompilerParams(dimension_semantics=sem, vmem_limit_bytes=VMEM_LIMIT)


def _dot(a, b, hi=False):
    return jnp.dot(a, b, preferred_element_type=F32, precision=HI if hi else None)


def _dot_nt(a, b, hi=False):
    return lax.dot_general(a, b, (((1,), (1,)), ((), ())), preferred_element_type=F32,
                           precision=HI if hi else None)


def _dot_tn(a, b, hi=False):
    return lax.dot_general(a, b, (((0,), (0,)), ((), ())), preferred_element_type=F32,
                           precision=HI if hi else None)


def _head_sums(x, ebd_bf):
    hi = x.astype(BF16)
    lo = (x - hi.astype(F32)).astype(BF16)
    return _dot(hi, ebd_bf) + _dot(lo, ebd_bf)


def _sigmoid(x):
    return 1.0 / (1.0 + jnp.exp(-x))


def _silu(x):
    return x * _sigmoid(x)


def _softplus(x):
    return jnp.maximum(x, 0.0) + jnp.log(1.0 + jnp.exp(-jnp.abs(x)))


_PAIRS = tuple(slice(p * LANE, (p + 1) * LANE) for p in range(WIDTH // LANE))


def _pair_bd(y, mask2):
    return [jnp.concatenate([y[:, sl]] * (LANE // HEAD_DIM), axis=0) * mask2 for sl in _PAIRS]


def _pair_dot(a, rhs_pairs, hi=False):
    return jnp.concatenate([_dot(a[:, sl], r, hi) for sl, r in zip(_PAIRS, rhs_pairs)], axis=-1)


def _pair_dot_nt(a, rhs_pairs, hi=False):
    return jnp.concatenate([_dot_nt(a[:, sl], r, hi) for sl, r in zip(_PAIRS, rhs_pairs)], axis=-1)


def _pair_blocks(blocks):
    z = jnp.zeros_like(blocks[0])
    rows = [jnp.concatenate([blocks[p] if q == p else z for q in range(len(blocks))], axis=-1)
            for p in range(len(blocks))]
    return jnp.concatenate(rows, axis=0)


def _head_consts():
    r = np.arange(WIDTH)
    same = (r[:, None] // HEAD_DIM) == (r[None, :] // HEAD_DIM)
    t = np.arange(CHUNK)[:, None]
    j = (np.arange(WIDTH) % CHUNK)[None, :]
    tt = np.arange(CHUNK)
    return {
        'mbd': same.astype(np.float32),
        'low_s': (j < t).astype(np.float32), 'low_i': (j <= t).astype(np.float32),
        'up_s': (j > t).astype(np.float32), 'up_i': (j >= t).astype(np.float32),
        'eye': (j == t).astype(np.float32),
        'tri_f': (tt[None, :] <= tt[:, None]).astype(np.float32),
        'tri_b': (tt[None, :] >= tt[:, None]).astype(np.float32),
    }


def _lb_kernel(x_ref, o_ref):
    x = x_ref[...]
    m = jnp.max(x, axis=0, keepdims=True)
    e = jnp.exp(x - m)
    sm = e / jnp.sum(e, axis=0, keepdims=True)
    run = sm[0:1]
    rows = [run - sm[0:1]]
    for l in range(1, DEPTH):
        run = run + sm[l:l + 1]
        rows.append(run - sm[0:1])
    o_ref[...] = jnp.concatenate(rows, axis=0)


def _lower_bounds(hgrn_lb):
    flat = hgrn_lb.reshape(DEPTH, 2 * WIDTH)
    out = pl.pallas_call(_lb_kernel, out_shape=jax.ShapeDtypeStruct(flat.shape, F32), name='hgrn_lb')(flat)
    return out.reshape(DEPTH, 2, 1, WIDTH)


def _ada_kernel(c_ref, w_ref, b_ref, o_ref):
    h = _silu(c_ref[...]).astype(BF16)
    o_ref[0] = _dot(h, w_ref[0].astype(BF16)) + b_ref[0]


def _ada(cond8, w_ada, b_ada):
    n = N_MOD * D_MODEL
    tn = ADA_TN
    return pl.pallas_call(
        _ada_kernel,
        grid=(DEPTH, n // tn),
        in_specs=[pl.BlockSpec((SUBLANE, D_MODEL), lambda l, j: (0, 0)),
                  pl.BlockSpec((1, D_MODEL, tn), lambda l, j: (l, 0, j)),
                  pl.BlockSpec((1, 1, tn), lambda l, j: (l, 0, j))],
        out_specs=pl.BlockSpec((1, SUBLANE, tn), lambda l, j: (l, 0, j)),
        out_shape=jax.ShapeDtypeStruct((DEPTH, SUBLANE, n), F32),
        compiler_params=_cp(("arbitrary", "arbitrary")),
        name='ada',
    )(cond8, w_ada, b_ada.reshape(DEPTH, 1, n))


def _layer_norm_rows(y, g, b):
    mu = jnp.mean(y, axis=-1, keepdims=True)
    d = y - mu
    var = jnp.mean(d * d, axis=-1, keepdims=True)
    return d * lax.rsqrt(var + LN_EPS) * g + b


def _ffn_tile(x, sh_ref, sc_ref, g_ref, lng_ref, lnb_ref, win_ref, wout_ref):
    h = (x * (1.0 + sc_ref[0]) + sh_ref[0]).astype(BF16)
    n_chunks = D_FF // TF

    def gate_up(c):
        return (_dot(h, win_ref[:, c * TF:(c + 1) * TF]),
                _dot(h, win_ref[:, D_FF + c * TF:D_FF + (c + 1) * TF]))

    y = jnp.zeros(x.shape, F32)
    cur = gate_up(0)
    for c in range(n_chunks):
        nxt = gate_up(c + 1) if c + 1 < n_chunks else None
        a = (_silu(cur[0]) * cur[1]).astype(BF16)
        y = y + _dot(a, wout_ref[c * TF:(c + 1) * TF, :])
        cur = nxt
    return _layer_norm_rows(DN_ALPHA * x + 0.5 * g_ref[0] * y, lng_ref[...], lnb_ref[...])


def _ffn_kernel(x_ref, *refs):
    o_ref = refs[-1]
    o_ref[...] = _ffn_tile(x_ref[...], *refs[:-1])


def _ffn_first_kernel(ctx_tiles, xp_ref, xs_ref, *refs):
    o_ref = refs[-1]
    x = jnp.where(pl.program_id(0) < ctx_tiles, xp_ref[...], xs_ref[...])
    o_ref[...] = _ffn_tile(x, *refs[:-1])


def _ffn_last_kernel(ctx_tiles, x_ref, *refs):
    yp_ref, ys_ref = refs[-2:]
    y = _ffn_tile(x_ref[...], *refs[:-2])
    is_ctx = pl.program_id(0) < ctx_tiles

    @pl.when(is_ctx)
    def _():
        yp_ref[...] = y

    @pl.when(jnp.logical_not(is_ctx))
    def _():
        ys_ref[...] = y


class _Tiles:
    def __init__(self, n_seg, seg_rows, tm):
        self.n_seg, self.seg_rows, self.tm = n_seg, seg_rows, tm
        self.per_seg = seg_rows // tm
        self.grid = (n_seg * self.per_seg,)

    def row(self, width, col=0):
        ps = self.per_seg
        return pl.BlockSpec((None, self.tm, width), lambda i: (i // ps, i % ps, col))

    def mod(self):
        ps = self.per_seg
        return pl.BlockSpec((1, 1, D_MODEL), lambda i: (i // ps, 0, 0))

    def shape(self, width, dtype=F32):
        return jax.ShapeDtypeStruct((self.n_seg, self.seg_rows, width), dtype)


def _full(a):
    return pl.BlockSpec(a.shape, lambda i: (0,) * a.ndim)


def _ffn(x, sh, sc, g, lng, lnb, w_in_bf, w_out_bf, l, f, tl, split=None):
    ps, tm = tl.per_seg, tl.tm
    mod = tl.mod()
    vec = pl.BlockSpec((1, D_MODEL), lambda i: (0, 0))
    ctx_blk = pl.BlockSpec((None, tm, D_MODEL), lambda i: (0, jnp.minimum(i, ps - 1), 0))
    dec_blk = pl.BlockSpec((None, tm, D_MODEL),
                           lambda i: (jnp.maximum(i - ps, 0) // ps, jnp.maximum(i - ps, 0) % ps, 0))
    weights = [pl.BlockSpec((None, None, D_MODEL, 2 * D_FF), lambda i: (l, f, 0, 0)),
               pl.BlockSpec((None, None, D_FF, D_MODEL), lambda i: (l, f, 0, 0))]
    rest = [mod, mod, mod, vec, vec] + weights
    args = (sh, sc, g, lng, lnb, w_in_bf, w_out_bf)
    n_dec = tl.n_seg - 1
    if split == 'in':
        body, xs, x_specs = functools.partial(_ffn_first_kernel, ps), tuple(x), [ctx_blk, dec_blk]
    else:
        body, xs, x_specs = _ffn_kernel, (x,), [tl.row(D_MODEL)]
    if split == 'out':
        body = functools.partial(_ffn_last_kernel, ps)
        out_specs = [ctx_blk, dec_blk]
        out_shape = [jax.ShapeDtypeStruct((1, tl.seg_rows, D_MODEL), F32),
                     jax.ShapeDtypeStruct((n_dec, tl.seg_rows, D_MODEL), F32)]
    else:
        out_specs, out_shape = tl.row(D_MODEL), tl.shape(D_MODEL)
    return pl.pallas_call(
        body,
        grid=tl.grid,
        in_specs=x_specs + rest,
        out_specs=out_specs,
        out_shape=out_shape,
        compiler_params=_cp(("arbitrary",) if split == 'out' else ("parallel",)),
        name='ffn',
    )(*xs, *args)


ZA = 3 * WIDTH + 2 * A_DECAY_RANK + 2 * A_ICLR_RANK + A_GATE_RANK
ZB = B_Q_RANK + B_KV_RANK + LANE
ZC = 2 * WIDTH
ZD = 5 * WIDTH


N_RWKV_PRE = 9
N_RWKV_PAR = 9
N_MLA_PREP = 4
N_MLA_PAR = 7


def _inproj_kernel(x_ref, sh_ref, sc_ref, wab_ref, wcd_ref, *rest):
    a_par, b_par = rest[:N_RWKV_PAR], rest[N_RWKV_PAR:N_RWKV_PAR + N_MLA_PAR]
    outs = rest[N_RWKV_PAR + N_MLA_PAR:]
    r_o, v_o = outs[0:2]
    a_out = outs[2:2 + N_RWKV_PRE]
    kpe_o = outs[2 + N_RWKV_PRE]
    b_out = outs[3 + N_RWKV_PRE:3 + N_RWKV_PRE + N_MLA_PREP]
    zc_ref, zd_ref, za_scr, zb_scr = outs[3 + N_RWKV_PRE + N_MLA_PREP:]
    h = (x_ref[...] * (1.0 + sc_ref[0]) + sh_ref[0]).astype(BF16)
    za_scr[...] = _dot(h, wab_ref[:, 0:ZA])
    zb_scr[...] = _dot(h, wab_ref[:, ZA:ZA + ZB])
    r_o[...] = za_scr[:, 0:WIDTH]
    v_o[...] = za_scr[:, 2 * WIDTH:3 * WIDTH]
    _rwkv_pre_kernel(za_scr, *a_par, *a_out)
    zc_ref[...] = _dot(h, wcd_ref[:, 0:ZC])
    kpe_o[...] = zb_scr[:, B_Q_RANK + B_KV_RANK:ZB]
    _mla_prep_kernel(zb_scr, *b_par, *b_out)
    zd_ref[...] = _dot(h, wcd_ref[:, ZC:ZC + ZD])


def _inproj(x, sh, sc, w_ab, w_cd, l, pa, pb, ebd, cos, sin, tl):
    mod = tl.mod()
    a_par = (pa['w0'], pa['w2'], pa['a0'], pa['a2'], pa['g2'], pa['kk'], pa['ka'], pa['rk'], ebd)
    b_par = (pb['qn_g'], pb['kvn_g'], pb['wuq'], pb['wuk'], pb['wuv'])
    slabs = N_HEADS * SLAB
    w = WIDTH
    out_specs = ([tl.row(w)] * (2 + N_RWKV_PRE) + [tl.row(LANE), tl.row(B_KV_RANK)] + [tl.row(slabs)] * 3
                 + [tl.row(ZC), tl.row(ZD)])
    out_shape = ([tl.shape(w)] * (2 + N_RWKV_PRE) + [tl.shape(LANE), tl.shape(B_KV_RANK)]
                 + [tl.shape(slabs, BF16)] * 3 + [tl.shape(ZC), tl.shape(ZD)])
    outs = pl.pallas_call(
        _inproj_kernel,
        grid=tl.grid,
        in_specs=[tl.row(D_MODEL), mod, mod, pl.BlockSpec((None, D_MODEL, ZA + ZB), lambda i: (l, 0, 0)),
                  pl.BlockSpec((None, D_MODEL, ZC + ZD), lambda i: (l, 0, 0))]
                 + [_full(a) for a in a_par] + [tl.row(LANE), tl.row(LANE)] + [_full(a) for a in b_par],
        out_specs=out_specs,
        out_shape=out_shape,
        scratch_shapes=[pltpu.VMEM((tl.tm, ZA), F32), pltpu.VMEM((tl.tm, ZB), F32)],
        compiler_params=_cp(("parallel",)),
        name='mixer_in',
    )(x, sh, sc, w_ab, w_cd, *a_par, cos, sin, *b_par)
    r, v = outs[0:2]
    pre = outs[2:2 + N_RWKV_PRE]
    kpe = outs[2 + N_RWKV_PRE]
    mla = outs[3 + N_RWKV_PRE:3 + N_RWKV_PRE + N_MLA_PREP]
    zc, zd = outs[-2:]
    return r, v, pre, kpe, mla, zc, zd


def _gelu_tanh(x):
    return 0.5 * x * (1.0 + jnp.tanh(np.sqrt(2.0 / np.pi) * (x + 0.044715 * (x * x * x))))


def _outproj_kernel(x_ref, af_ref, ab_ref, bon_ref, gate_ref, yb_ref, hf_ref, hb_ref, cg_ref, df_ref, db_ref,
                    dg_ref, agn_g_ref, agn_b_ref, dgn_g_ref, ebd_ref, w_ref, g_ref, lng_ref, lnb_ref, o_ref):
    avg = (ebd_ref[...] * (1.0 / HEAD_DIM)).astype(BF16)
    oa = af_ref[...] + ab_ref[...]
    da = oa - _head_sums(oa, avg)
    gn = da * lax.rsqrt(_head_sums(da * da, avg) + RWKV_GN_EPS) * agn_g_ref[...] + agn_b_ref[...]
    ya = (gn + bon_ref[...]) * gate_ref[...]
    h = jnp.concatenate([hf_ref[p] + hb_ref[p] for p in range(WIDTH // LANE)], axis=-1)
    yc = h * _gelu_tanh(cg_ref[...])
    od = df_ref[...] + db_ref[...]
    yd = od * lax.rsqrt(_head_sums(od * od, avg) + RMS_EPS) * dgn_g_ref[...] * _silu(dg_ref[...])
    y = _dot(ya.astype(BF16), w_ref[0:WIDTH, :])
    y += _dot(yb_ref[...].astype(BF16), w_ref[WIDTH:2 * WIDTH, :])
    y += _dot(yc.astype(BF16), w_ref[2 * WIDTH:3 * WIDTH, :])
    y += _dot(yd.astype(BF16), w_ref[3 * WIDTH:4 * WIDTH, :])
    o_ref[...] = _layer_norm_rows(DN_ALPHA * x_ref[...] + g_ref[0] * y, lng_ref[...], lnb_ref[...])


def _outproj(x, oa_f, oa_b, bonus, gate, yb, hf, hb, zc, od_f, od_b, zd, agn_g, agn_b, dgn_g, ebd,
             w_out_bf, g, lng, lnb, l, tl):
    vec = pl.BlockSpec((1, D_MODEL), lambda i: (0, 0))
    wvec = pl.BlockSpec((1, WIDTH), lambda i: (0, 0))
    ps = tl.per_seg
    halves = pl.BlockSpec((WIDTH // LANE, None, tl.tm, LANE), lambda i: (0, i // ps, i % ps, 0))
    w = WIDTH
    return pl.pallas_call(
        _outproj_kernel,
        grid=tl.grid,
        in_specs=[tl.row(D_MODEL), tl.row(w), tl.row(w), tl.row(w), tl.row(w), tl.row(w), halves, halves,
                  tl.row(w, 1), tl.row(w), tl.row(w), tl.row(w, 4), wvec, wvec, wvec, _full(ebd),
                  pl.BlockSpec((None, 4 * WIDTH, D_MODEL), lambda i: (l, 0, 0)), tl.mod(), vec, vec],
        out_specs=tl.row(D_MODEL),
        out_shape=tl.shape(D_MODEL),
        compiler_params=_cp(("parallel",)),
        name='mixer_out',
    )(x, oa_f, oa_b, bonus, gate, yb, hf, hb, zc, od_f, od_b, zd, agn_g, agn_b, dgn_g, ebd, w_out_bf, g, lng, lnb)


class _Streams:
    def __init__(self, n_seg, n_ctx, seg_rows, ctx_len, chunk):
        self.n_seg, self.n_ctx, self.chunk = n_seg, n_ctx, chunk
        self.n = seg_rows // chunk
        self.ctx_n = ctx_len // chunk

    def fwd(self, width, col=0):
        return pl.BlockSpec((self.n_seg, self.chunk, width), lambda c: (0, c, col))

    def bwd(self, width, col=0):
        n = self.n
        return pl.BlockSpec((self.n_seg, self.chunk, width), lambda c: (0, n - 1 - c, col))

    def fin_fwd(self, shape):
        ctx_n = self.ctx_n
        return pl.BlockSpec((1,) + shape, lambda c: (c // ctx_n,) + (0,) * len(shape))

    def fin_bwd(self, shape):
        ctx_n, n_ctx = self.ctx_n, self.n_ctx
        return pl.BlockSpec((1,) + shape, lambda c: (n_ctx - 1 - c // ctx_n,) + (0,) * len(shape))

    def ctx_first(self, c):
        return c % self.ctx_n == 0

    def ctx_last(self, c):
        return c % self.ctx_n == self.ctx_n - 1


def _rwkv_pre_kernel(za_ref, w0_ref, w2_ref, a0_ref, a2_ref, g2_ref, kkp_ref, kap_ref, rkp_ref, ebd_ref,
                     kap_o, lwf_o, lwb_o, kdf_o, kdb_o, bbf_o, bbb_o, bon_o, gate_o):
    r = za_ref[:, 0:WIDTH]
    k = za_ref[:, WIDTH:2 * WIDTH]
    v = za_ref[:, 2 * WIDTH:3 * WIDTH]
    o = 3 * WIDTH
    xw = (za_ref[:, o:o + A_DECAY_RANK], za_ref[:, o + A_DECAY_RANK:o + 2 * A_DECAY_RANK])
    o += 2 * A_DECAY_RANK
    xa = (za_ref[:, o:o + A_ICLR_RANK], za_ref[:, o + A_ICLR_RANK:o + 2 * A_ICLR_RANK])
    o += 2 * A_ICLR_RANK
    xg = za_ref[:, o:o + A_GATE_RANK]
    ebd = ebd_ref[...].astype(BF16)

    kk = k * kkp_ref[...]
    nrm = jnp.sqrt(_head_sums(kk * kk, ebd))
    kappa = kk / jnp.maximum(nrm, 1e-12)
    kap_o[...] = kappa
    k_sum = jnp.zeros_like(r)
    for d, (lw_o, kd_o, bb_o) in enumerate(((lwf_o, kdf_o, bbf_o), (lwb_o, kdb_o, bbb_o))):
        w_log = -_softplus(-(w0_ref[d] + _dot(jnp.tanh(xw[d]).astype(BF16), w2_ref[d].astype(BF16)))) - 0.5
        lw_o[...] = -jnp.exp(w_log)
        a = _sigmoid(a0_ref[d] + _dot(xa[d].astype(BF16), a2_ref[d].astype(BF16)))
        k_d = k * (1.0 + (a - 1.0) * kap_ref[...])
        kd_o[...] = k_d
        bb_o[...] = kappa * a
        k_sum += k_d
    bon_o[...] = _head_sums(r * k_sum * rkp_ref[...], ebd) * v
    gate_o[...] = _dot(_sigmoid(xg).astype(BF16), g2_ref[...].astype(BF16))


def _rwkv_chunk(r, v, kap, lw, kd, bb, st, mbd, eye, m_strict, m_incl, tri):
    cum = _dot(tri, lw, hi=True)
    yield
    tot = jnp.sum(lw, axis=0, keepdims=True)
    g_in = jnp.exp(cum)
    g_inv = jnp.exp(-cum)
    g_ex = jnp.exp(cum - lw)
    g_end = jnp.exp(tot - cum)
    cast = lambda a: a.astype(BF16)
    mask2 = cast(mbd[:LANE, :LANE])
    x = cast(jnp.concatenate([kap * g_ex, r * g_in], axis=0))
    ab = _pair_dot_nt(x, _pair_bd(cast(bb * g_inv), mask2))
    ak = _pair_dot_nt(x, _pair_bd(cast(kd * g_inv), mask2))
    xs = _pair_dot_nt(x, [cast(st[sl, sl]) for sl in _PAIRS])
    yield
    c = r.shape[0]
    strict = m_strict > 0.5
    incl = m_incl > 0.5
    a_ub = jnp.where(strict, ab[:c], 0.0)
    a_rb = jnp.where(incl, ab[c:], 0.0)
    a_uk = jnp.where(strict, ak[:c], 0.0)
    a_rk = jnp.where(incl, ak[c:], 0.0)

    def catmul(pc, q):
        return _pair_dot(cast(pc), _pair_bd(cast(q), mask2))

    xs = xs + catmul(jnp.concatenate([a_uk, a_rk], axis=0), v)
    rhs, o_v = xs[:c], xs[c:]
    xp = -a_ub
    inv = eye + xp
    xp = catmul(xp, xp)
    yield
    for _ in range(int(np.log2(c)) - 2):
        both = catmul(jnp.concatenate([inv, xp], axis=0), xp)
        yield
        inv = inv + both[:c]
        xp = both[c:]
    inv = inv + catmul(inv, xp)
    yield
    u = -catmul(inv, rhs)
    yield
    o = o_v + catmul(a_rb, u)
    uv = cast(jnp.concatenate([u, v], axis=0))
    bk = cast(jnp.concatenate([bb * g_end, kd * g_end], axis=0))
    upd = _pair_blocks([_dot_tn(uv[:, sl], bk[:, sl]) * mbd[:LANE, :LANE] for sl in _PAIRS])
    st_new = st * jnp.exp(tot) + upd
    return o, st_new


def _interleave(gens):
    results = [None] * len(gens)
    live = list(range(len(gens)))
    while live:
        still = []
        for k in live:
            try:
                next(gens[k])
                still.append(k)
            except StopIteration as done:
                results[k] = done.value
        live = still
    return results


def _init_states(st, c, st_scr, s0_ref):
    @pl.when(c == 0)
    def _():
        st_scr[1:] = s0_ref[...]

    @pl.when(st.ctx_first(c))
    def _():
        st_scr[0] = jnp.zeros(st_scr.shape[1:], F32)


def _hgrn_chunk(xq, xf, xi, lb, st, tri, ebd, mbd, p_scr, reverse):
    c = xq.shape[0]
    nb = c // SUB
    q = _silu(xq)
    gsig = lb + (1.0 - lb) * _sigmoid(xf)
    kk = 1.0 - gsig
    lg = jnp.log(gsig)
    cum = _dot(tri, lg, hi=True)
    yield
    cum = cum * LOG2E
    tot = jnp.sum(lg, axis=0, keepdims=True) * LOG2E
    lane_s = lax.broadcasted_iota(jnp.int32, (SUB, WIDTH), 1) % c
    half = SUB // 2
    row_h = lax.broadcasted_iota(jnp.int32, (half, WIDTH), 0)
    blk = lambda a, i: a[i * SUB:(i + 1) * SUB]
    end_row = (lambda j: j * SUB) if reverse else (lambda j: j * SUB + SUB - 1)
    later = (lambda j: range(0, j)) if reverse else (lambda j: range(j + 1, nb))

    k_end = jnp.concatenate([blk(kk, j) * jnp.exp2(cum[end_row(j):end_row(j) + 1] - blk(cum, j))
                             for j in range(nb)], axis=0)
    q_parts, where_part = [], {}
    for j in range(nb):
        for i in later(j):
            where_part[(i, j)] = len(q_parts)
            q_parts.append(blk(q, i) * jnp.exp2(blk(cum, i) - cum[end_row(j):end_row(j) + 1]))
    mask2 = mbd[:LANE, :LANE].astype(BF16)
    cross = _pair_dot_nt(jnp.concatenate(q_parts, axis=0).astype(BF16), _pair_bd(k_end.astype(BF16), mask2))
    o_state = _pair_dot_nt((q * jnp.exp2(cum)).astype(BF16), [st[sl, sl].astype(BF16) for sl in _PAIRS])
    xi_bf = xi.astype(BF16)
    k_out = (kk * jnp.exp2(tot - cum)).astype(BF16)
    upd = _pair_blocks([_dot_tn(xi_bf[:, sl], k_out[:, sl]) * mbd[:LANE, :LANE] for sl in _PAIRS])
    yield

    for i in range(nb):
        cb, qb = blk(cum, i), blk(q, i)
        for sl in range(SUB):
            s = i * SUB + sl
            parts = []
            for lo in (0, half):
                hi = lo + half - 1
                none_valid = lo > sl if reverse else hi < sl
                all_valid = hi <= sl if reverse else lo >= sl
                if none_valid:
                    parts.append(jnp.zeros((half, WIDTH), F32))
                    continue
                d = cb[lo:lo + half] - cum[s:s + 1]
                if not all_valid:
                    valid = (row_h + lo <= sl) if reverse else (row_h + lo >= sl)
                    d = jnp.where(valid, d, NEG)
                parts.append(jnp.exp2(d) * qb[lo:lo + half] * kk[s:s + 1])
            p_scr[s * SUB:(s + 1) * SUB, :] = jnp.concatenate(parts, axis=0).astype(BF16)
    same = _pair_dot(p_scr, [mask2] * len(_PAIRS))
    yield

    att_rows = []
    for i in range(nb):
        att = jnp.zeros((SUB, WIDTH), F32)
        for sl in range(SUB):
            s = i * SUB + sl
            att = jnp.where(lane_s == s, same[s * SUB:(s + 1) * SUB], att)
        for j in range(nb):
            if (i, j) in where_part:
                n = where_part[(i, j)]
                att = jnp.where(lane_s // SUB == j, cross[n * SUB:(n + 1) * SUB], att)
        att_rows.append(att)
    att = jnp.concatenate(att_rows, axis=0)
    o = o_state + _pair_dot(att.astype(BF16), _pair_bd(xi_bf, mask2))
    st_new = st * jnp.exp2(tot) + upd
    return o, st_new


def _mix_scan_kernel(st,
                     rf_ref, vf_ref, kapf_ref, lwf_ref, kdf_ref, bbf_ref,
                     rb_ref, vb_ref, kapb_ref, lwb_ref, kdb_ref, bbb_ref,
                     qf_ref, ff_ref, if_ref, qb_ref, fb_ref, ib_ref, lb_ref,
                     sa0_ref, sd0_ref, mbd_ref, eye_ref, lows_ref, lowi_ref, ups_ref, upi_ref, trif_ref, trib_ref,
                     af_ref, ab_ref, afin_f, afin_b, df_ref, db_ref, dfin_f, dfin_b,
                     sa_scr, sd_scr, p_scr):
    c = pl.program_id(0)
    _init_states(st, c, sa_scr, sa0_ref)
    _init_states(st, c, sd_scr, sd0_ref)
    mbd = mbd_ref[...]
    eye = eye_ref[...]
    gens = []
    for s in range(st.n_seg):
        gens.append(_rwkv_chunk(rf_ref[s], vf_ref[s], kapf_ref[s], lwf_ref[s], kdf_ref[s], bbf_ref[s],
                                sa_scr[s, 0], mbd, eye, lows_ref[...], lowi_ref[...], trif_ref[...]))
        gens.append(_hgrn_chunk(qf_ref[s], ff_ref[s], if_ref[s], lb_ref[0], sd_scr[s, 0], trif_ref[...],
                                mbd, mbd, p_scr.at[s, 0], False))
        gens.append(_rwkv_chunk(rb_ref[s], vb_ref[s], kapb_ref[s], lwb_ref[s], kdb_ref[s], bbb_ref[s],
                                sa_scr[s, 1], mbd, eye, ups_ref[...], upi_ref[...], trib_ref[...]))
        gens.append(_hgrn_chunk(qb_ref[s], fb_ref[s], ib_ref[s], lb_ref[1], sd_scr[s, 1], trib_ref[...],
                                mbd, mbd, p_scr.at[s, 1], True))
    outs = ((af_ref, sa_scr), (df_ref, sd_scr), (ab_ref, sa_scr), (db_ref, sd_scr))
    for k, (o, st_new) in enumerate(_interleave(gens)):
        s, kind = k // 4, k % 4
        o_ref, scr = outs[kind]
        o_ref[s] = o
        scr[s, kind // 2] = st_new

    @pl.when(st.ctx_last(c))
    def _():
        for h in range(N_HEADS):
            hs = slice(h * HEAD_DIM, (h + 1) * HEAD_DIM)
            afin_f[0, h] = sa_scr[0, 0, hs, hs]
            afin_b[0, h] = sa_scr[0, 1, hs, hs]
            dfin_f[0, h] = sd_scr[0, 0, hs, hs].T
            dfin_b[0, h] = sd_scr[0, 1, hs, hs].T


def _mix_scan(r, v, pre, zd, lb, sa0_dec, sd0_dec, hc, st):
    kap, lwf, lwb, kdf, kdb, bbf, bbb = pre
    consts = (hc['mbd'], hc['eye'], hc['low_s'], hc['low_i'], hc['up_s'], hc['up_i'], hc['tri_f'], hc['tri_b'])
    w = WIDTH
    o_shape = jax.ShapeDtypeStruct(kap.shape, F32)
    head_blocks = (N_HEADS, HEAD_DIM, HEAD_DIM)
    fin_shape = jax.ShapeDtypeStruct((st.n_ctx,) + head_blocks, F32)
    fin_specs = [st.fin_fwd(head_blocks), st.fin_bwd(head_blocks)]
    outs = pl.pallas_call(
        functools.partial(_mix_scan_kernel, st),
        grid=(st.n,),
        in_specs=[st.fwd(w)] * 6 + [st.bwd(w)] * 6 + [
                  st.fwd(w, 0), st.fwd(w, 1), st.fwd(w, 3), st.bwd(w, 0), st.bwd(w, 2), st.bwd(w, 3),
                  _full(lb), _full(sa0_dec), _full(sd0_dec)] + [_full(a) for a in consts],
        out_specs=[st.fwd(w), st.bwd(w)] + fin_specs + [st.fwd(w), st.bwd(w)] + fin_specs,
        out_shape=[o_shape, o_shape, fin_shape, fin_shape] * 2,
        scratch_shapes=[pltpu.VMEM((st.n_seg, 2, w, w), F32), pltpu.VMEM((st.n_seg, 2, w, w), F32),
                        pltpu.VMEM((st.n_seg, 2, CHUNK * SUB, w), BF16)],
        compiler_params=_cp(("arbitrary",)),
        name='mix_scan',
    )(r, v, kap, lwf, kdf, bbf, r, v, kap, lwb, kdb, bbb, zd, zd, zd, zd, zd, zd, lb, sa0_dec, sd0_dec, *consts)
    a_f, a_b, afin_f, afin_b, d_f, d_b, dfin_f, dfin_b = outs
    return ((a_f, a_b, jnp.stack([afin_f, afin_b], axis=1)), (d_f, d_b, jnp.stack([dfin_f, dfin_b], axis=1)))


def _lru_coefficients(d, base, seqlen, xp, x, xn, cw_ref, cb_ref, wg_ref, bg_ref, lam_ref, pad, coef):
    tm = x.shape[0]
    halo = SUBLANE
    pad[0:halo, :] = xp
    pad[halo:halo + tm, :] = x
    pad[halo + tm:2 * halo + tm, :] = xn
    pos = jnp.bitwise_and(lax.broadcasted_iota(jnp.int32, (tm, WIDTH), 0) + base, seqlen - 1)
    u = jnp.zeros((tm, WIDTH), F32) + cb_ref[...]
    for j in range(C_CONV):
        off = j - C_CONV // 2
        tap = pad[pl.ds(halo + off, tm), :]
        ok = jnp.logical_and(pos + off >= 0, pos + off < seqlen)
        u += jnp.where(ok, tap, 0.0) * cw_ref[j:j + 1, :]
    cols = slice(2 * d * WIDTH, (2 * d + 2) * WIDTH)
    pre = _dot(u.astype(BF16), wg_ref[:, cols].astype(BF16))
    yield
    gates = _sigmoid(pre + bg_ref[:, cols])
    r, ig = gates[:, :WIDTH], gates[:, WIDTH:]
    log_a = -C_POW * r * _softplus(-lam_ref[d])
    a = jnp.exp(log_a)
    b = jnp.sqrt(-jnp.tanh(log_a) * (a * a + 1.0)) * (ig * u)
    for half in range(WIDTH // LANE):
        coef[0, half] = a[:, half * LANE:(half + 1) * LANE]
        coef[1, half] = b[:, half * LANE:(half + 1) * LANE]


def _lru_scan_kernel(st, seqlens, xf_ref, xpf_ref, xnf_ref, xb_ref, xpb_ref, xnb_ref,
                     cw_ref, cb_ref, wg_ref, bg_ref, lam_ref, h0_ref,
                     hf_ref, hb_ref, finf_ref, finb_ref, h_scr, loc_scr, car_scr, pad_scr, coef_scr):
    c = pl.program_id(0)
    nh = WIDTH // LANE

    @pl.when(c == 0)
    def _():
        h_scr[:, 1:] = h0_ref[...]

    @pl.when(st.ctx_first(c))
    def _():
        h_scr[:, 0] = jnp.zeros((nh, 2, 1, LANE), F32)

    chunk = xf_ref.shape[1]
    x_refs = ((xf_ref, xpf_ref, xnf_ref), (xb_ref, xpb_ref, xnb_ref))
    bases = (c * chunk, (st.n - 1 - c) * chunk)
    gens = []
    for s in range(st.n_seg):
        for d in range(2):
            x_ref, xp_ref, xn_ref = x_refs[d]
            gens.append(_lru_coefficients(d, bases[d], seqlens[s], xp_ref[s], x_ref[s], xn_ref[s], cw_ref, cb_ref,
                                          wg_ref, bg_ref, lam_ref, pad_scr.at[2 * s + d], coef_scr.at[2 * s + d]))
    _interleave(gens)

    rr = LRU_ROWS
    ng = chunk // rr
    chains = [(p, s, d) for s in range(st.n_seg) for d in range(2) for p in range(nh)]
    o_refs = (hf_ref, hb_ref)
    order = (list(range(rr)), list(range(rr - 1, -1, -1)))

    for k, (p, s, d) in enumerate(chains):
        hloc = ploc = None
        for r in order[d]:
            a = coef_scr[2 * s + d, 0, p, pl.ds(r, ng, stride=rr), :]
            b = coef_scr[2 * s + d, 1, p, pl.ds(r, ng, stride=rr), :]
            hloc = b if hloc is None else a * hloc + b
            ploc = a if ploc is None else a * ploc
            loc_scr[k, 0, r] = hloc
            loc_scr[k, 1, r] = ploc

    def carry_step(j, carries):
        out = []
        for k, (p, s, d) in enumerate(chains):
            g = j if d == 0 else ng - 1 - j
            r_end = order[d][-1]
            car_scr[k, pl.ds(g, 1), :] = carries[k]
            out.append(loc_scr[k, 1, r_end, pl.ds(g, 1), :] * carries[k] + loc_scr[k, 0, r_end, pl.ds(g, 1), :])
        return tuple(out)

    carries = lax.fori_loop(0, ng, carry_step, tuple(h_scr[p, s, d] for p, s, d in chains))

    for k, (p, s, d) in enumerate(chains):
        h_scr[p, s, d] = carries[k]
        o_ref = o_refs[d]
        car = car_scr[k]
        for r in range(rr):
            o_ref[p, s, pl.ds(r, ng, stride=rr), :] = loc_scr[k, 0, r] + loc_scr[k, 1, r] * car

    @pl.when(st.ctx_last(c))
    def _():
        for p in range(nh):
            finf_ref[0, :, p * LANE:(p + 1) * LANE] = h_scr[p, 0, 0]
            finb_ref[0, :, p * LANE:(p + 1) * LANE] = h_scr[p, 0, 1]


def _lru_scan(zc, p, h0_dec, ctx_len, dec_len, st):
    nh = WIDTH // LANE
    n_seg, seg_rows = zc.shape[:2]
    n_dec = h0_dec.shape[0]
    h0 = jnp.transpose(h0_dec.reshape(n_dec, 2, nh, 1, LANE), (2, 0, 1, 3, 4))
    n_chain = 2 * n_seg * nh
    chunk = st.chunk
    ng = chunk // LRU_ROWS
    n = st.n
    halo = SUBLANE
    per = chunk // halo
    last = seg_rows // halo - 1
    x_spec = lambda blk: [pl.BlockSpec((n_seg, chunk, WIDTH), lambda c: (0, blk(c), 0)),
                          pl.BlockSpec((n_seg, halo, WIDTH), lambda c: (0, jnp.maximum(blk(c) * per - 1, 0), 0)),
                          pl.BlockSpec((n_seg, halo, WIDTH),
                                       lambda c: (0, jnp.minimum((blk(c) + 1) * per, last), 0))]
    h_fwd = pl.BlockSpec((nh, n_seg, chunk, LANE), lambda c: (0, 0, c, 0))
    h_bwd = pl.BlockSpec((nh, n_seg, chunk, LANE), lambda c: (0, 0, n - 1 - c, 0))
    args = (p['conv_w'], p['conv_b'], p['wg'], p['bg'], p['lam'], h0)
    seqlens = (ctx_len,) + (dec_len,) * n_dec
    h_shape = jax.ShapeDtypeStruct((nh, n_seg, seg_rows, LANE), F32)
    fin_shape = jax.ShapeDtypeStruct((st.n_ctx, 1, WIDTH), F32)
    h_f, h_b, fin_f, fin_b = pl.pallas_call(
        functools.partial(_lru_scan_kernel, st, seqlens),
        grid=(n,),
        in_specs=x_spec(lambda c: c) + x_spec(lambda c: n - 1 - c) + [_full(a) for a in args],
        out_specs=[h_fwd, h_bwd, st.fin_fwd((1, WIDTH)), st.fin_bwd((1, WIDTH))],
        out_shape=[h_shape, h_shape, fin_shape, fin_shape],
        scratch_shapes=[pltpu.VMEM((nh, n_seg, 2, 1, LANE), F32),
                        pltpu.VMEM((n_chain, 2, LRU_ROWS, ng, LANE), F32),
                        pltpu.VMEM((n_chain, ng, LANE), F32),
                        pltpu.VMEM((2 * n_seg, chunk + 2 * halo, WIDTH), F32),
                        pltpu.VMEM((2 * n_seg, 2, nh, chunk, LANE), F32)],
        compiler_params=_cp(("arbitrary",)),
        name='rglru',
    )(zc, zc, zc, zc, zc, zc, *args)
    return h_f, h_b, jnp.concatenate([fin_f, fin_b], axis=1)


def _rope_slab(x, cos, sin):
    lane = lax.broadcasted_iota(jnp.int32, x.shape, 1)
    rot = jnp.where(lane % (B_ROPE // 2) < B_ROPE // 4, -pltpu.roll(x, LANE - B_ROPE // 4, 1),
                    pltpu.roll(x, B_ROPE // 4, 1))
    return x * cos + rot * sin


def _kv_up(ckv_bf, kpe_slab, wuk_ref, wuv_ref, k_o, v_o):
    kn = _dot(ckv_bf, wuk_ref[...])
    for h in range(N_HEADS):
        k_o[:, h * SLAB:(h + 1) * SLAB] = (kn[:, h * SLAB:(h + 1) * SLAB] + kpe_slab).astype(BF16)
    lane = lax.broadcasted_iota(jnp.int32, (1, N_HEADS * SLAB), 1)
    v_o[...] = (_dot(ckv_bf, wuv_ref[...]) + jnp.where(lane % SLAB >= B_VDIM, 1.0, 0.0)).astype(BF16)


def _mla_prep_kernel(zb_ref, cos_ref, sin_ref, qn_ref, kvn_ref, wuq_ref, wuk_ref, wuv_ref,
                     ckv_o, q_o, k_o, v_o):
    cq = zb_ref[:, 0:B_Q_RANK]
    ckv = zb_ref[:, B_Q_RANK:B_Q_RANK + B_KV_RANK]
    kpe = zb_ref[:, B_Q_RANK + B_KV_RANK:ZB]
    cos = cos_ref[...]
    sin = sin_ref[...]
    cqn = cq * lax.rsqrt(jnp.mean(cq * cq, axis=-1, keepdims=True) + RMS_EPS) * qn_ref[...]
    ckvn = ckv * lax.rsqrt(jnp.mean(ckv * ckv, axis=-1, keepdims=True) + RMS_EPS) * kvn_ref[...]
    ckv_o[...] = ckvn
    q = _dot(cqn.astype(BF16), wuq_ref[...])
    for h in range(N_HEADS):
        q_o[:, h * SLAB:(h + 1) * SLAB] = (_rope_slab(q[:, h * SLAB:(h + 1) * SLAB], cos, sin)
                                           * ATTN_LOG2_SCALE).astype(BF16)
    _kv_up(ckvn.astype(BF16), _rope_slab(kpe, cos, sin), wuk_ref, wuv_ref, k_o, v_o)


def _mla_cache_kernel(ckv_ref, kpe_ref, wuk_ref, wuv_ref, k_o, v_o):
    _kv_up(ckv_ref[...].astype(BF16), kpe_ref[...], wuk_ref, wuv_ref, k_o, v_o)


def _mla_cache(ckv, kpe_slab, p):
    rows = ckv.shape[0]
    return pl.pallas_call(
        _mla_cache_kernel,
        out_shape=[jax.ShapeDtypeStruct((rows, N_HEADS * SLAB), BF16)] * 2,
        compiler_params=pltpu.CompilerParams(vmem_limit_bytes=VMEM_LIMIT),
        name='mla_cache',
    )(ckv, kpe_slab, p['wuk'], p['wuv'])


def _attend_heads(q_ref, kv_refs, o_ref):
    tq = q_ref.shape[0]
    low = lax.broadcasted_iota(jnp.int32, (tq, SLAB), 1) < B_VDIM

    def head(h):
        qh = q_ref[:, h * SLAB:(h + 1) * SLAB]
        ss = [_dot_nt(qh, k_ref[:, h * SLAB:(h + 1) * SLAB]) for k_ref, _ in kv_refs]
        yield
        m = ss[0].max(axis=-1, keepdims=True)
        for s in ss[1:]:
            m = jnp.maximum(m, s.max(axis=-1, keepdims=True))
        es = [jnp.exp2((s - m).astype(BF16)) for s in ss]
        yield
        acc = jnp.zeros((tq, SLAB), F32)
        for e, (_, v_ref) in zip(es, kv_refs):
            acc += _dot(e, v_ref[:, h * SLAB:(h + 1) * SLAB])
        return acc / acc[:, B_VDIM:B_VDIM + 1]

    pairs = []
    for h in range(0, N_HEADS, 2):
        h0, h1 = _interleave([head(h), head(h + 1)])
        pairs.append(jnp.where(low, h0, pltpu.roll(h1, B_VDIM, 1)))
    o_ref[...] = jnp.concatenate(pairs, axis=-1)


def _attn_kernel(n_ctx, q_ref, kx_ref, vx_ref, k_ref, v_ref, kc_ref, vc_ref, o_ref):
    i = pl.program_id(0)

    @pl.when(i < n_ctx)
    def _():
        _attend_heads(q_ref, [(kx_ref, vx_ref)], o_ref)

    @pl.when(i >= n_ctx)
    def _():
        _attend_heads(q_ref, [(k_ref, v_ref), (kc_ref, vc_ref)], o_ref)


def _attn(q, k, v, kc, vc, n_ctx, ctx_len, n_dec, dec_len, past):
    tq = ctx_len
    nq = dec_len // tq
    slabs = N_HEADS * SLAB
    dec = lambda i: jnp.maximum(i - n_ctx, 0)

    def q_map(i):
        return (jnp.where(i < n_ctx, 0, 1 + dec(i) // nq), jnp.where(i < n_ctx, i, dec(i) % nq), 0)

    ctx_kv = pl.BlockSpec((None, ctx_len, slabs), lambda i: (0, jnp.minimum(i, n_ctx - 1), 0))
    dec_kv = pl.BlockSpec((None, dec_len, slabs), lambda i: (1 + dec(i) // nq, 0, 0))
    cache_kv = pl.BlockSpec((past, slabs), lambda i: (dec(i) // nq, 0))
    return pl.pallas_call(
        functools.partial(_attn_kernel, n_ctx),
        grid=(n_ctx + n_dec * nq,),
        in_specs=[pl.BlockSpec((None, tq, slabs), q_map), ctx_kv, ctx_kv, dec_kv, dec_kv, cache_kv, cache_kv],
        out_specs=pl.BlockSpec((None, tq, WIDTH), q_map),
        out_shape=jax.ShapeDtypeStruct(q.shape[:2] + (WIDTH,), F32),
        compiler_params=_cp(("arbitrary",)),
        name='attn',
    )(q, k, v, k, v, kc, vc)


def _rope_tables(n_ctx_rows, n_dec, dec_len):
    rows = dec_len // GRID_W
    row = np.repeat(np.arange(rows, dtype=np.float32), GRID_W)
    col = np.tile(np.arange(GRID_W, dtype=np.float32), rows)
    half = B_ROPE // 2
    inv = jnp.power(ROPE_BASE, -jnp.arange(0, half, 2, dtype=F32) / half)
    ang_r = jnp.asarray(row)[:, None] * inv
    ang_c = jnp.asarray(col)[:, None] * inv
    ang = jnp.concatenate([ang_r, ang_r, ang_c, ang_c], axis=-1)
    cos = jnp.pad(jnp.cos(ang), ((0, 0), (0, LANE - B_ROPE)), constant_values=1.0)
    sin = jnp.pad(jnp.sin(ang), ((0, 0), (0, LANE - B_ROPE)))
    cos = jnp.concatenate([jnp.ones((n_ctx_rows, LANE), F32)] + [cos] * n_dec, axis=0)
    sin = jnp.concatenate([jnp.zeros((n_ctx_rows, LANE), F32)] + [sin] * n_dec, axis=0)
    return cos, sin


def _block_diag(w):
    g, n, m = w.shape[-3:]
    eye = jnp.eye(g, dtype=w.dtype)
    out = w[..., :, :, None, :] * eye[:, None, :, None]
    return out.reshape(w.shape[:-3] + (g * n, g * m))


def kernel(x_prompt, x_sample, cache_mla_ckv, cache_mla_kpe, state_rwkv, state_rglru, state_hgrn, c, c_ctx,
           w_ada, b_ada, ln_g, ln_b, w_ffn_in, w_ffn_out, w_in, w_out,
           rwkv_w0, rwkv_w2, rwkv_a0, rwkv_a2, rwkv_g2, rwkv_kk, rwkv_ka, rwkv_rk, rwkv_gn_g, rwkv_gn_b,
           mla_qn_g, mla_w_uq, mla_kvn_g, mla_w_ukv,
           rglru_conv_w, rglru_conv_b, rglru_wa, rglru_ba, rglru_wx, rglru_bx, rglru_lam,
           hgrn_lb, hgrn_gn_g):
    n_ctx, ctx_len, _ = x_prompt.shape
    n_dec, dec_len, _ = x_sample.shape
    past = cache_mla_ckv.shape[2]
    seg = n_ctx * ctx_len
    assert seg == dec_len, "context rows must form one segment of the decode sequence length"
    assert ctx_len & (ctx_len - 1) == 0 and dec_len & (dec_len - 1) == 0, "sequence lengths must be powers of two"
    n_seg = 1 + n_dec
    tl = _Tiles(n_seg, seg, min(TM, seg))
    assert ctx_len % LRU_CHUNK == 0
    st = _Streams(n_seg, n_ctx, seg, ctx_len, CHUNK)
    lru_st = _Streams(n_seg, n_ctx, seg, ctx_len, LRU_CHUNK)

    hc = {k: jnp.asarray(v) for k, v in _head_consts().items()}
    ebd = hc['mbd']
    cos, sin = (t.reshape(n_seg, seg, LANE) for t in _rope_tables(seg, n_dec, dec_len))

    assert n_seg <= SUBLANE
    cond8 = jnp.zeros((SUBLANE, D_MODEL), F32).at[0].set(c_ctx).at[1:1 + n_dec].set(c)
    mods = _ada(cond8, w_ada, b_ada).reshape(DEPTH, SUBLANE, N_MOD, D_MODEL)
    mods = jnp.transpose(mods, (0, 2, 1, 3))[:, :, :n_seg, None, :]

    w_ffn_in_bf = w_ffn_in.astype(BF16)
    w_ffn_out_bf = w_ffn_out.astype(BF16)
    w_out_bf = w_out.astype(BF16)
    kpe_end = ZA + B_Q_RANK + B_KV_RANK + B_ROPE
    w_ab = jnp.pad(w_in[:, :, :kpe_end].astype(BF16), ((0, 0), (0, 0), (0, LANE - B_ROPE)))
    w_cd = w_in[:, :, kpe_end:].astype(BF16)
    wuq = mla_w_uq.reshape(DEPTH, B_Q_RANK, N_HEADS, B_NOPE + B_ROPE)
    wuq_p = jnp.concatenate([wuq[..., B_NOPE:], wuq[..., :B_NOPE],
                             jnp.zeros((DEPTH, B_Q_RANK, N_HEADS, SLAB - B_NOPE - B_ROPE), F32)], axis=-1)
    wuq_p = wuq_p.reshape(DEPTH, B_Q_RANK, N_HEADS * SLAB).astype(BF16)
    wukv = mla_w_ukv.reshape(DEPTH, B_KV_RANK, N_HEADS, B_NOPE + B_VDIM)
    wuk_p = jnp.concatenate([jnp.zeros((DEPTH, B_KV_RANK, N_HEADS, B_ROPE), F32), wukv[..., :B_NOPE],
                             jnp.zeros((DEPTH, B_KV_RANK, N_HEADS, SLAB - B_NOPE - B_ROPE), F32)], axis=-1)
    wuk_p = wuk_p.reshape(DEPTH, B_KV_RANK, N_HEADS * SLAB).astype(BF16)
    wuv_p = jnp.concatenate([wukv[..., B_NOPE:], jnp.zeros((DEPTH, B_KV_RANK, N_HEADS, SLAB - B_VDIM), F32)], axis=-1)
    wuv_p = wuv_p.reshape(DEPTH, B_KV_RANK, N_HEADS * SLAB).astype(BF16)
    lru_wg = jnp.concatenate([_block_diag(rglru_wa[:, 0]), _block_diag(rglru_wx[:, 0]),
                              _block_diag(rglru_wa[:, 1]), _block_diag(rglru_wx[:, 1])], axis=-1)
    lru_bg = jnp.concatenate([rglru_ba[:, 0], rglru_bx[:, 0], rglru_ba[:, 1], rglru_bx[:, 1]], axis=-1)[:, None, :]
    lbs = _lower_bounds(hgrn_lb)

    rwkv_s0 = _block_diag(state_rwkv)
    hgrn_s0 = _block_diag(jnp.swapaxes(state_hgrn, -1, -2))
    cache_kpe_slab = jnp.pad(cache_mla_kpe, ((0, 0), (0, 0), (0, 0), (0, LANE - B_ROPE)))

    new_ckv, new_kpe, new_rwkv, new_lru, new_hgrn = [], [], [], [], []
    for l in range(DEPTH):
        m = mods[l]
        lng = ln_g[l][:, None, :]
        lnb = ln_b[l][:, None, :]
        ffn0 = functools.partial(_ffn, sh=m[0], sc=m[1], g=m[2], lng=lng[0], lnb=lnb[0], w_in_bf=w_ffn_in_bf,
                                 w_out_bf=w_ffn_out_bf, l=l, f=0, tl=tl)
        if l == 0:
            x = ffn0((x_prompt.reshape(1, seg, D_MODEL), x_sample), split='in')
        else:
            x = ffn0(x)
        pa = {'w0': rwkv_w0[l][:, None, :], 'w2': rwkv_w2[l], 'a0': rwkv_a0[l][:, None, :], 'a2': rwkv_a2[l],
              'g2': rwkv_g2[l], 'kk': rwkv_kk[l][None, :], 'ka': rwkv_ka[l][None, :], 'rk': rwkv_rk[l][None, :]}
        pb = {'qn_g': mla_qn_g[l][None, :], 'kvn_g': mla_kvn_g[l][None, :], 'wuq': wuq_p[l], 'wuk': wuk_p[l],
              'wuv': wuv_p[l]}
        a_r, a_v, pre, kpe_raw, (ckvn, q_all, k_all, v_all), zc, zd = _inproj(x, m[3], m[4], w_ab, w_cd, l, pa, pb,
                                                                                ebd, cos, sin, tl)

        (oa_f, oa_b, sa_fin), (od_f, od_b, sd_fin) = _mix_scan(a_r, a_v, pre[:7], zd, lbs[l], rwkv_s0[:, l],
                                                                hgrn_s0[:, l], hc, st)
        new_rwkv.append(sa_fin)
        new_hgrn.append(sd_fin)

        kc, vc = _mla_cache(cache_mla_ckv[:, l].reshape(n_dec * past, B_KV_RANK),
                            cache_kpe_slab[:, l].reshape(n_dec * past, LANE), pb)
        yb = _attn(q_all, k_all, v_all, kc, vc, n_ctx, ctx_len, n_dec, dec_len, past)
        new_ckv.append(ckvn[0].reshape(n_ctx, ctx_len, B_KV_RANK))
        new_kpe.append(kpe_raw[0, :, :B_ROPE].reshape(n_ctx, ctx_len, B_ROPE))

        pc = {'conv_w': rglru_conv_w[l], 'conv_b': rglru_conv_b[l][None, :], 'wg': lru_wg[l], 'bg': lru_bg[l],
              'lam': rglru_lam[l][:, None, :]}
        h_f, h_b, h_fin = _lru_scan(zc, pc, state_rglru[:, l], ctx_len, dec_len, lru_st)
        new_lru.append(h_fin)

        x = _outproj(x, oa_f, oa_b, pre[7], pre[8], yb, h_f, h_b, zc, od_f, od_b, zd,
                     rwkv_gn_g[l][None, :], rwkv_gn_b[l][None, :], hgrn_gn_g[l][None, :], ebd,
                     w_out_bf, m[5], lng[1], lnb[1], l, tl)
        ffn1 = functools.partial(_ffn, sh=m[6], sc=m[7], g=m[8], lng=lng[2], lnb=lnb[2], w_in_bf=w_ffn_in_bf,
                                 w_out_bf=w_ffn_out_bf, l=l, f=1, tl=tl)
        if l == DEPTH - 1:
            y_prompt, y_sample = ffn1(x, split='out')
            y_prompt = y_prompt.reshape(n_ctx, ctx_len, D_MODEL)
        else:
            x = ffn1(x)
    return (y_prompt, y_sample, jnp.stack(new_ckv, axis=1), jnp.stack(new_kpe, axis=1),
            jnp.stack(new_rwkv, axis=1), jnp.stack(new_lru, axis=1), jnp.stack(new_hgrn, axis=1))
```

```python
import functools

import numpy as np
import jax
import jax.numpy as jnp
from jax import lax
from jax.experimental import pallas as pl
from jax.experimental.pallas import tpu as pltpu

D_MODEL = 1024
DEPTH = 4
GRID_W = 64
HEAD_DIM = 64
N_HEADS = 4
WIDTH = N_HEADS * HEAD_DIM
A_DECAY_RANK = 64
A_ICLR_RANK = 64
A_GATE_RANK = 128
B_NOPE = 64
B_ROPE = 32
B_VDIM = 64
B_Q_RANK = 256
B_KV_RANK = 128
C_CONV = 4
C_POW = 8.0
D_FF = 2816
N_MOD = 9
ROPE_BASE = 10000.0
LN_EPS = 1e-5
RMS_EPS = 1e-6
RWKV_GN_EPS = 64e-5
DN_ALPHA = (2 * DEPTH) ** 0.25

LANE = 128
SLAB = 128
CHUNK = 64
SUB = 16
SUBLANE = 8
LRU_CHUNK = 256
LRU_ROWS = SUBLANE
TM = 512
TF = 256
ADA_TN = 2304
VMEM_LIMIT = 56 * 1024 * 1024

F32 = jnp.float32
BF16 = jnp.bfloat16
HI = lax.Precision.HIGHEST
NEG = -1e30
LOG2E = float(np.log2(np.e))
ATTN_LOG2_SCALE = float((B_NOPE + B_ROPE) ** -0.5) * LOG2E


def _cp(sem):
    return pltpu.CompilerParams(dimension_semantics=sem, vmem_limit_bytes=VMEM_LIMIT)


def _dot(a, b, hi=False):
    return jnp.dot(a, b, preferred_element_type=F32, precision=HI if hi else None)


def _dot_nt(a, b, hi=False):
    return lax.dot_general(a, b, (((1,), (1,)), ((), ())), preferred_element_type=F32,
                           precision=HI if hi else None)


def _dot_tn(a, b, hi=False):
    return lax.dot_general(a, b, (((0,), (0,)), ((), ())), preferred_element_type=F32,
                           precision=HI if hi else None)


def _head_sums(x, ebd_bf):
    hi = x.astype(BF16)
    lo = (x - hi.astype(F32)).astype(BF16)
    return _dot(hi, ebd_bf) + _dot(lo, ebd_bf)


def _sigmoid(x):
    return 1.0 / (1.0 + jnp.exp(-x))


def _silu(x):
    return x * _sigmoid(x)


def _softplus(x):
    return jnp.maximum(x, 0.0) + jnp.log(1.0 + jnp.exp(-jnp.abs(x)))


_PAIRS = tuple(slice(p * LANE, (p + 1) * LANE) for p in range(WIDTH // LANE))


def _pair_bd(y, mask2):
    return [jnp.concatenate([y[:, sl]] * (LANE // HEAD_DIM), axis=0) * mask2 for sl in _PAIRS]


def _pair_dot(a, rhs_pairs, hi=False):
    return jnp.concatenate([_dot(a[:, sl], r, hi) for sl, r in zip(_PAIRS, rhs_pairs)], axis=-1)


def _pair_dot_nt(a, rhs_pairs, hi=False):
    return jnp.concatenate([_dot_nt(a[:, sl], r, hi) for sl, r in zip(_PAIRS, rhs_pairs)], axis=-1)


def _pair_blocks(blocks):
    z = jnp.zeros_like(blocks[0])
    rows = [jnp.concatenate([blocks[p] if q == p else z for q in range(len(blocks))], axis=-1)
            for p in range(len(blocks))]
    return jnp.concatenate(rows, axis=0)


def _head_consts():
    r = np.arange(WIDTH)
    same = (r[:, None] // HEAD_DIM) == (r[None, :] // HEAD_DIM)
    t = np.arange(CHUNK)[:, None]
    j = (np.arange(WIDTH) % CHUNK)[None, :]
    tt = np.arange(CHUNK)
    return {
        'mbd': same.astype(np.float32),
        'low_s': (j < t).astype(np.float32), 'low_i': (j <= t).astype(np.float32),
        'up_s': (j > t).astype(np.float32), 'up_i': (j >= t).astype(np.float32),
        'eye': (j == t).astype(np.float32),
        'tri_f': (tt[None, :] <= tt[:, None]).astype(np.float32),
        'tri_b': (tt[None, :] >= tt[:, None]).astype(np.float32),
    }


def _lb_kernel(x_ref, o_ref):
    x = x_ref[...]
    m = jnp.max(x, axis=0, keepdims=True)
    e = jnp.exp(x - m)
    sm = e / jnp.sum(e, axis=0, keepdims=True)
    run = sm[0:1]
    rows = [run - sm[0:1]]
    for l in range(1, DEPTH):
        run = run + sm[l:l + 1]
        rows.append(run - sm[0:1])
    o_ref[...] = jnp.concatenate(rows, axis=0)


def _lower_bounds(hgrn_lb):
    flat = hgrn_lb.reshape(DEPTH, 2 * WIDTH)
    out = pl.pallas_call(_lb_kernel, out_shape=jax.ShapeDtypeStruct(flat.shape, F32), name='hgrn_lb')(flat)
    return out.reshape(DEPTH, 2, 1, WIDTH)


def _ada_kernel(c_ref, w_ref, b_ref, o_ref):
    h = _silu(c_ref[...]).astype(BF16)
    o_ref[0] = _dot(h, w_ref[0].astype(BF16)) + b_ref[0]


def _ada(cond8, w_ada, b_ada):
    n = N_MOD * D_MODEL
    tn = ADA_TN
    return pl.pallas_call(
        _ada_kernel,
        grid=(DEPTH, n // tn),
        in_specs=[pl.BlockSpec((SUBLANE, D_MODEL), lambda l, j: (0, 0)),
                  pl.BlockSpec((1, D_MODEL, tn), lambda l, j: (l, 0, j)),
                  pl.BlockSpec((1, 1, tn), lambda l, j: (l, 0, j))],
        out_specs=pl.BlockSpec((1, SUBLANE, tn), lambda l, j: (l, 0, j)),
        out_shape=jax.ShapeDtypeStruct((DEPTH, SUBLANE, n), F32),
        compiler_params=_cp(("arbitrary", "arbitrary")),
        name='ada',
    )(cond8, w_ada, b_ada.reshape(DEPTH, 1, n))


def _layer_norm_rows(y, g, b):
    mu = jnp.mean(y, axis=-1, keepdims=True)
    d = y - mu
    var = jnp.mean(d * d, axis=-1, keepdims=True)
    return d * lax.rsqrt(var + LN_EPS) * g + b


def _ffn_tile(x, sh_ref, sc_ref, g_ref, lng_ref, lnb_ref, win_ref, wout_ref):
    h = (x * (1.0 + sc_ref[0]) + sh_ref[0]).astype(BF16)
    n_chunks = D_FF // TF

    def gate_up(c):
        return (_dot(h, win_ref[:, c * TF:(c + 1) * TF]),
                _dot(h, win_ref[:, D_FF + c * TF:D_FF + (c + 1) * TF]))

    y = jnp.zeros(x.shape, F32)
    cur = gate_up(0)
    for c in range(n_chunks):
        nxt = gate_up(c + 1) if c + 1 < n_chunks else None
        a = (_silu(cur[0]) * cur[1]).astype(BF16)
        y = y + _dot(a, wout_ref[c * TF:(c + 1) * TF, :])
        cur = nxt
    return _layer_norm_rows(DN_ALPHA * x + 0.5 * g_ref[0] * y, lng_ref[...], lnb_ref[...])


def _ffn_kernel(x_ref, *refs):
    o_ref = refs[-1]
    o_ref[...] = _ffn_tile(x_ref[...], *refs[:-1])


def _ffn_first_kernel(ctx_tiles, xp_ref, xs_ref, *refs):
    o_ref = refs[-1]
    x = jnp.where(pl.program_id(0) < ctx_tiles, xp_ref[...], xs_ref[...])
    o_ref[...] = _ffn_tile(x, *refs[:-1])


def _ffn_last_kernel(ctx_tiles, x_ref, *refs):
    yp_ref, ys_ref = refs[-2:]
    y = _ffn_tile(x_ref[...], *refs[:-2])
    is_ctx = pl.program_id(0) < ctx_tiles

    @pl.when(is_ctx)
    def _():
        yp_ref[...] = y

    @pl.when(jnp.logical_not(is_ctx))
    def _():
        ys_ref[...] = y


class _Tiles:
    def __init__(self, n_seg, seg_rows, tm):
        self.n_seg, self.seg_rows, self.tm = n_seg, seg_rows, tm
        self.per_seg = seg_rows // tm
        self.grid = (n_seg * self.per_seg,)

    def row(self, width, col=0):
        ps = self.per_seg
        return pl.BlockSpec((None, self.tm, width), lambda i: (i // ps, i % ps, col))

    def mod(self):
        ps = self.per_seg
        return pl.BlockSpec((1, 1, D_MODEL), lambda i: (i // ps, 0, 0))

    def shape(self, width, dtype=F32):
        return jax.ShapeDtypeStruct((self.n_seg, self.seg_rows, width), dtype)


def _full(a):
    return pl.BlockSpec(a.shape, lambda i: (0,) * a.ndim)


def _ffn(x, sh, sc, g, lng, lnb, w_in_bf, w_out_bf, l, f, tl, split=None):
    ps, tm = tl.per_seg, tl.tm
    mod = tl.mod()
    vec = pl.BlockSpec((1, D_MODEL), lambda i: (0, 0))
    ctx_blk = pl.BlockSpec((None, tm, D_MODEL), lambda i: (0, jnp.minimum(i, ps - 1), 0))
    dec_blk = pl.BlockSpec((None, tm, D_MODEL),
                           lambda i: (jnp.maximum(i - ps, 0) // ps, jnp.maximum(i - ps, 0) % ps, 0))
    weights = [pl.BlockSpec((None, None, D_MODEL, 2 * D_FF), lambda i: (l, f, 0, 0)),
               pl.BlockSpec((None, None, D_FF, D_MODEL), lambda i: (l, f, 0, 0))]
    rest = [mod, mod, mod, vec, vec] + weights
    args = (sh, sc, g, lng, lnb, w_in_bf, w_out_bf)
    n_dec = tl.n_seg - 1
    if split == 'in':
        body, xs, x_specs = functools.partial(_ffn_first_kernel, ps), tuple(x), [ctx_blk, dec_blk]
    else:
        body, xs, x_specs = _ffn_kernel, (x,), [tl.row(D_MODEL)]
    if split == 'out':
        body = functools.partial(_ffn_last_kernel, ps)
        out_specs = [ctx_blk, dec_blk]
        out_shape = [jax.ShapeDtypeStruct((1, tl.seg_rows, D_MODEL), F32),
                     jax.ShapeDtypeStruct((n_dec, tl.seg_rows, D_MODEL), F32)]
    else:
        out_specs, out_shape = tl.row(D_MODEL), tl.shape(D_MODEL)
    return pl.pallas_call(
        body,
        grid=tl.grid,
        in_specs=x_specs + rest,
        out_specs=out_specs,
        out_shape=out_shape,
        compiler_params=_cp(("arbitrary",) if split == 'out' else ("parallel",)),
        name='ffn',
    )(*xs, *args)


ZA = 3 * WIDTH + 2 * A_DECAY_RANK + 2 * A_ICLR_RANK + A_GATE_RANK
ZB = B_Q_RANK + B_KV_RANK + LANE
ZC = 2 * WIDTH
ZD = 5 * WIDTH


N_RWKV_PRE = 9
N_RWKV_PAR = 9
N_MLA_PREP = 4
N_MLA_PAR = 7


def _inproj_kernel(x_ref, sh_ref, sc_ref, wab_ref, wcd_ref, *rest):
    a_par, b_par = rest[:N_RWKV_PAR], rest[N_RWKV_PAR:N_RWKV_PAR + N_MLA_PAR]
    outs = rest[N_RWKV_PAR + N_MLA_PAR:]
    r_o, v_o = outs[0:2]
    a_out = outs[2:2 + N_RWKV_PRE]
    kpe_o = outs[2 + N_RWKV_PRE]
    b_out = outs[3 + N_RWKV_PRE:3 + N_RWKV_PRE + N_MLA_PREP]
    zc_ref, zd_ref, za_scr, zb_scr = outs[3 + N_RWKV_PRE + N_MLA_PREP:]
    h = (x_ref[...] * (1.0 + sc_ref[0]) + sh_ref[0]).astype(BF16)
    za_scr[...] = _dot(h, wab_ref[:, 0:ZA])
    zb_scr[...] = _dot(h, wab_ref[:, ZA:ZA + ZB])
    r_o[...] = za_scr[:, 0:WIDTH]
    v_o[...] = za_scr[:, 2 * WIDTH:3 * WIDTH]
    _rwkv_pre_kernel(za_scr, *a_par, *a_out)
    zc_ref[...] = _dot(h, wcd_ref[:, 0:ZC])
    kpe_o[...] = zb_scr[:, B_Q_RANK + B_KV_RANK:ZB]
    _mla_prep_kernel(zb_scr, *b_par, *b_out)
    zd_ref[...] = _dot(h, wcd_ref[:, ZC:ZC + ZD])


def _inproj(x, sh, sc, w_ab, w_cd, l, pa, pb, ebd, cos, sin, tl):
    mod = tl.mod()
    a_par = (pa['w0'], pa['w2'], pa['a0'], pa['a2'], pa['g2'], pa['kk'], pa['ka'], pa['rk'], ebd)
    b_par = (pb['qn_g'], pb['kvn_g'], pb['wuq'], pb['wuk'], pb['wuv'])
    slabs = N_HEADS * SLAB
    w = WIDTH
    out_specs = ([tl.row(w)] * (2 + N_RWKV_PRE) + [tl.row(LANE), tl.row(B_KV_RANK)] + [tl.row(slabs)] * 3
                 + [tl.row(ZC), tl.row(ZD)])
    out_shape = ([tl.shape(w)] * (2 + N_RWKV_PRE) + [tl.shape(LANE), tl.shape(B_KV_RANK)]
                 + [tl.shape(slabs, BF16)] * 3 + [tl.shape(ZC), tl.shape(ZD)])
    outs = pl.pallas_call(
        _inproj_kernel,
        grid=tl.grid,
        in_specs=[tl.row(D_MODEL), mod, mod, pl.BlockSpec((None, D_MODEL, ZA + ZB), lambda i: (l, 0, 0)),
                  pl.BlockSpec((None, D_MODEL, ZC + ZD), lambda i: (l, 0, 0))]
                 + [_full(a) for a in a_par] + [tl.row(LANE), tl.row(LANE)] + [_full(a) for a in b_par],
        out_specs=out_specs,
        out_shape=out_shape,
        scratch_shapes=[pltpu.VMEM((tl.tm, ZA), F32), pltpu.VMEM((tl.tm, ZB), F32)],
        compiler_params=_cp(("parallel",)),
        name='mixer_in',
    )(x, sh, sc, w_ab, w_cd, *a_par, cos, sin, *b_par)
    r, v = outs[0:2]
    pre = outs[2:2 + N_RWKV_PRE]
    kpe = outs[2 + N_RWKV_PRE]
    mla = outs[3 + N_RWKV_PRE:3 + N_RWKV_PRE + N_MLA_PREP]
    zc, zd = outs[-2:]
    return r, v, pre, kpe, mla, zc, zd


def _gelu_tanh(x):
    return 0.5 * x * (1.0 + jnp.tanh(np.sqrt(2.0 / np.pi) * (x + 0.044715 * (x * x * x))))


def _outproj_kernel(x_ref, af_ref, ab_ref, bon_ref, gate_ref, yb_ref, hf_ref, hb_ref, cg_ref, df_ref, db_ref,
                    dg_ref, agn_g_ref, agn_b_ref, dgn_g_ref, ebd_ref, w_ref, g_ref, lng_ref, lnb_ref, o_ref):
    avg = (ebd_ref[...] * (1.0 / HEAD_DIM)).astype(BF16)
    oa = af_ref[...] + ab_ref[...]
    da = oa - _head_sums(oa, avg)
    gn = da * lax.rsqrt(_head_sums(da * da, avg) + RWKV_GN_EPS) * agn_g_ref[...] + agn_b_ref[...]
    ya = (gn + bon_ref[...]) * gate_ref[...]
    h = jnp.concatenate([hf_ref[p] + hb_ref[p] for p in range(WIDTH // LANE)], axis=-1)
    yc = h * _gelu_tanh(cg_ref[...])
    od = df_ref[...] + db_ref[...]
    yd = od * lax.rsqrt(_head_sums(od * od, avg) + RMS_EPS) * dgn_g_ref[...] * _silu(dg_ref[...])
    y = _dot(ya.astype(BF16), w_ref[0:WIDTH, :])
    y += _dot(yb_ref[...].astype(BF16), w_ref[WIDTH:2 * WIDTH, :])
    y += _dot(yc.astype(BF16), w_ref[2 * WIDTH:3 * WIDTH, :])
    y += _dot(yd.astype(BF16), w_ref[3 * WIDTH:4 * WIDTH, :])
    o_ref[...] = _layer_norm_rows(DN_ALPHA * x_ref[...] + g_ref[0] * y, lng_ref[...], lnb_ref[...])


def _outproj(x, oa_f, oa_b, bonus, gate, yb, hf, hb, zc, od_f, od_b, zd, agn_g, agn_b, dgn_g, ebd,
             w_out_bf, g, lng, lnb, l, tl):
    vec = pl.BlockSpec((1, D_MODEL), lambda i: (0, 0))
    wvec = pl.BlockSpec((1, WIDTH), lambda i: (0, 0))
    ps = tl.per_seg
    halves = pl.BlockSpec((WIDTH // LANE, None, tl.tm, LANE), lambda i: (0, i // ps, i % ps, 0))
    w = WIDTH
    return pl.pallas_call(
        _outproj_kernel,
        grid=tl.grid,
        in_specs=[tl.row(D_MODEL), tl.row(w), tl.row(w), tl.row(w), tl.row(w), tl.row(w), halves, halves,
                  tl.row(w, 1), tl.row(w), tl.row(w), tl.row(w, 4), wvec, wvec, wvec, _full(ebd),
                  pl.BlockSpec((None, 4 * WIDTH, D_MODEL), lambda i: (l, 0, 0)), tl.mod(), vec, vec],
        out_specs=tl.row(D_MODEL),
        out_shape=tl.shape(D_MODEL),
        compiler_params=_cp(("parallel",)),
        name='mixer_out',
    )(x, oa_f, oa_b, bonus, gate, yb, hf, hb, zc, od_f, od_b, zd, agn_g, agn_b, dgn_g, ebd, w_out_bf, g, lng, lnb)


class _Streams:
    def __init__(self, n_seg, n_ctx, seg_rows, ctx_len, chunk):
        self.n_seg, self.n_ctx, self.chunk = n_seg, n_ctx, chunk
        self.n = seg_rows // chunk
        self.ctx_n = ctx_len // chunk

    def fwd(self, width, col=0):
        return pl.BlockSpec((self.n_seg, self.chunk, width), lambda c: (0, c, col))

    def bwd(self, width, col=0):
        n = self.n
        return pl.BlockSpec((self.n_seg, self.chunk, width), lambda c: (0, n - 1 - c, col))

    def fin_fwd(self, shape):
        ctx_n = self.ctx_n
        return pl.BlockSpec((1,) + shape, lambda c: (c // ctx_n,) + (0,) * len(shape))

    def fin_bwd(self, shape):
        ctx_n, n_ctx = self.ctx_n, self.n_ctx
        return pl.BlockSpec((1,) + shape, lambda c: (n_ctx - 1 - c // ctx_n,) + (0,) * len(shape))

    def ctx_first(self, c):
        return c % self.ctx_n == 0

    def ctx_last(self, c):
        return c % self.ctx_n == self.ctx_n - 1


def _rwkv_pre_kernel(za_ref, w0_ref, w2_ref, a0_ref, a2_ref, g2_ref, kkp_ref, kap_ref, rkp_ref, ebd_ref,
                     kap_o, lwf_o, lwb_o, kdf_o, kdb_o, bbf_o, bbb_o, bon_o, gate_o):
    r = za_ref[:, 0:WIDTH]
    k = za_ref[:, WIDTH:2 * WIDTH]
    v = za_ref[:, 2 * WIDTH:3 * WIDTH]
    o = 3 * WIDTH
    xw = (za_ref[:, o:o + A_DECAY_RANK], za_ref[:, o + A_DECAY_RANK:o + 2 * A_DECAY_RANK])
    o += 2 * A_DECAY_RANK
    xa = (za_ref[:, o:o + A_ICLR_RANK], za_ref[:, o + A_ICLR_RANK:o + 2 * A_ICLR_RANK])
    o += 2 * A_ICLR_RANK
    xg = za_ref[:, o:o + A_GATE_RANK]
    ebd = ebd_ref[...].astype(BF16)

    kk = k * kkp_ref[...]
    nrm = jnp.sqrt(_head_sums(kk * kk, ebd))
    kappa = kk / jnp.maximum(nrm, 1e-12)
    kap_o[...] = kappa
    k_sum = jnp.zeros_like(r)
    for d, (lw_o, kd_o, bb_o) in enumerate(((lwf_o, kdf_o, bbf_o), (lwb_o, kdb_o, bbb_o))):
        w_log = -_softplus(-(w0_ref[d] + _dot(jnp.tanh(xw[d]).astype(BF16), w2_ref[d].astype(BF16)))) - 0.5
        lw_o[...] = -jnp.exp(w_log)
        a = _sigmoid(a0_ref[d] + _dot(xa[d].astype(BF16), a2_ref[d].astype(BF16)))
        k_d = k * (1.0 + (a - 1.0) * kap_ref[...])
        kd_o[...] = k_d
        bb_o[...] = kappa * a
        k_sum += k_d
    bon_o[...] = _head_sums(r * k_sum * rkp_ref[...], ebd) * v
    gate_o[...] = _dot(_sigmoid(xg).astype(BF16), g2_ref[...].astype(BF16))


def _rwkv_chunk(r, v, kap, lw, kd, bb, st, mbd, eye, m_strict, m_incl, tri):
    cum = _dot(tri, lw, hi=True)
    yield
    tot = jnp.sum(lw, axis=0, keepdims=True)
    g_in = jnp.exp(cum)
    g_inv = jnp.exp(-cum)
    g_ex = jnp.exp(cum - lw)
    g_end = jnp.exp(tot - cum)
    cast = lambda a: a.astype(BF16)
    mask2 = cast(mbd[:LANE, :LANE])
    x = cast(jnp.concatenate([kap * g_ex, r * g_in], axis=0))
    ab = _pair_dot_nt(x, _pair_bd(cast(bb * g_inv), mask2))
    ak = _pair_dot_nt(x, _pair_bd(cast(kd * g_inv), mask2))
    xs = _pair_dot_nt(x, [cast(st[sl, sl]) for sl in _PAIRS])
    yield
    c = r.shape[0]
    strict = m_strict > 0.5
    incl = m_incl > 0.5
    a_ub = jnp.where(strict, ab[:c], 0.0)
    a_rb = jnp.where(incl, ab[c:], 0.0)
    a_uk = jnp.where(strict, ak[:c], 0.0)
    a_rk = jnp.where(incl, ak[c:], 0.0)

    def catmul(pc, q):
        return _pair_dot(cast(pc), _pair_bd(cast(q), mask2))

    xs = xs + catmul(jnp.concatenate([a_uk, a_rk], axis=0), v)
    rhs, o_v = xs[:c], xs[c:]
    xp = -a_ub
    inv = eye + xp
    xp = catmul(xp, xp)
    yield
    for _ in range(int(np.log2(c)) - 2):
        both = catmul(jnp.concatenate([inv, xp], axis=0), xp)
        yield
        inv = inv + both[:c]
        xp = both[c:]
    inv = inv + catmul(inv, xp)
    yield
    u = -catmul(inv, rhs)
    yield
    o = o_v + catmul(a_rb, u)
    uv = cast(jnp.concatenate([u, v], axis=0))
    bk = cast(jnp.concatenate([bb * g_end, kd * g_end], axis=0))
    upd = _pair_blocks([_dot_tn(uv[:, sl], bk[:, sl]) * mbd[:LANE, :LANE] for sl in _PAIRS])
    st_new = st * jnp.exp(tot) + upd
    return o, st_new


def _interleave(gens):
    results = [None] * len(gens)
    live = list(range(len(gens)))
    while live:
        still = []
        for k in live:
            try:
                next(gens[k])
                still.append(k)
            except StopIteration as done:
                results[k] = done.value
        live = still
    return results


def _init_states(st, c, st_scr, s0_ref):
    @pl.when(c == 0)
    def _():
        st_scr[1:] = s0_ref[...]

    @pl.when(st.ctx_first(c))
    def _():
        st_scr[0] = jnp.zeros(st_scr.shape[1:], F32)


def _hgrn_chunk(xq, xf, xi, lb, st, tri, ebd, mbd, p_scr, reverse):
    c = xq.shape[0]
    nb = c // SUB
    q = _silu(xq)
    gsig = lb + (1.0 - lb) * _sigmoid(xf)
    kk = 1.0 - gsig
    lg = jnp.log(gsig)
    cum = _dot(tri, lg, hi=True)
    yield
    cum = cum * LOG2E
    tot = jnp.sum(lg, axis=0, keepdims=True) * LOG2E
    lane_s = lax.broadcasted_iota(jnp.int32, (SUB, WIDTH), 1) % c
    half = SUB // 2
    row_h = lax.broadcasted_iota(jnp.int32, (half, WIDTH), 0)
    blk = lambda a, i: a[i * SUB:(i + 1) * SUB]
    end_row = (lambda j: j * SUB) if reverse else (lambda j: j * SUB + SUB - 1)
    later = (lambda j: range(0, j)) if reverse else (lambda j: range(j + 1, nb))

    k_end = jnp.concatenate([blk(kk, j) * jnp.exp2(cum[end_row(j):end_row(j) + 1] - blk(cum, j))
                             for j in range(nb)], axis=0)
    q_parts, where_part = [], {}
    for j in range(nb):
        for i in later(j):
            where_part[(i, j)] = len(q_parts)
            q_parts.append(blk(q, i) * jnp.exp2(blk(cum, i) - cum[end_row(j):end_row(j) + 1]))
    mask2 = mbd[:LANE, :LANE].astype(BF16)
    cross = _pair_dot_nt(jnp.concatenate(q_parts, axis=0).astype(BF16), _pair_bd(k_end.astype(BF16), mask2))
    o_state = _pair_dot_nt((q * jnp.exp2(cum)).astype(BF16), [st[sl, sl].astype(BF16) for sl in _PAIRS])
    xi_bf = xi.astype(BF16)
    k_out = (kk * jnp.exp2(tot - cum)).astype(BF16)
    upd = _pair_blocks([_dot_tn(xi_bf[:, sl], k_out[:, sl]) * mbd[:LANE, :LANE] for sl in _PAIRS])
    yield

    for i in range(nb):
        cb, qb = blk(cum, i), blk(q, i)
        for sl in range(SUB):
            s = i * SUB + sl
            parts = []
            for lo in (0, half):
                hi = lo + half - 1
                none_valid = lo > sl if reverse else hi < sl
                all_valid = hi <= sl if reverse else lo >= sl
                if none_valid:
                    parts.append(jnp.zeros((half, WIDTH), F32))
                    continue
                d = cb[lo:lo + half] - cum[s:s + 1]
                if not all_valid:
                    valid = (row_h + lo <= sl) if reverse else (row_h + lo >= sl)
                    d = jnp.where(valid, d, NEG)
                parts.append(jnp.exp2(d) * qb[lo:lo + half] * kk[s:s + 1])
            p_scr[s * SUB:(s + 1) * SUB, :] = jnp.concatenate(parts, axis=0).astype(BF16)
        yield
    same = _pair_dot(p_scr, [mask2] * len(_PAIRS))
    yield

    att_rows = []
    for i in range(nb):
        att = jnp.zeros((SUB, WIDTH), F32)
        for sl in range(SUB):
            s = i * SUB + sl
            att = jnp.where(lane_s == s, same[s * SUB:(s + 1) * SUB], att)
        for j in range(nb):
            if (i, j) in where_part:
                n = where_part[(i, j)]
                att = jnp.where(lane_s // SUB == j, cross[n * SUB:(n + 1) * SUB], att)
        att_rows.append(att)
    att = jnp.concatenate(att_rows, axis=0)
    o = o_state + _pair_dot(att.astype(BF16), _pair_bd(xi_bf, mask2))
    st_new = st * jnp.exp2(tot) + upd
    return o, st_new


def _mix_scan_kernel(st,
                     rf_ref, vf_ref, kapf_ref, lwf_ref, kdf_ref, bbf_ref,
                     rb_ref, vb_ref, kapb_ref, lwb_ref, kdb_ref, bbb_ref,
                     qf_ref, ff_ref, if_ref, qb_ref, fb_ref, ib_ref, lb_ref,
                     sa0_ref, sd0_ref, mbd_ref, eye_ref, lows_ref, lowi_ref, ups_ref, upi_ref, trif_ref, trib_ref,
                     af_ref, ab_ref, afin_f, afin_b, df_ref, db_ref, dfin_f, dfin_b,
                     sa_scr, sd_scr, p_scr):
    c = pl.program_id(0)
    _init_states(st, c, sa_scr, sa0_ref)
    _init_states(st, c, sd_scr, sd0_ref)
    mbd = mbd_ref[...]
    eye = eye_ref[...]
    gens = []
    for s in range(st.n_seg):
        gens.append(_rwkv_chunk(rf_ref[s], vf_ref[s], kapf_ref[s], lwf_ref[s], kdf_ref[s], bbf_ref[s],
                                sa_scr[s, 0], mbd, eye, lows_ref[...], lowi_ref[...], trif_ref[...]))
        gens.append(_hgrn_chunk(qf_ref[s], ff_ref[s], if_ref[s], lb_ref[0], sd_scr[s, 0], trif_ref[...],
                                mbd, mbd, p_scr.at[s, 0], False))
        gens.append(_rwkv_chunk(rb_ref[s], vb_ref[s], kapb_ref[s], lwb_ref[s], kdb_ref[s], bbb_ref[s],
                                sa_scr[s, 1], mbd, eye, ups_ref[...], upi_ref[...], trib_ref[...]))
        gens.append(_hgrn_chunk(qb_ref[s], fb_ref[s], ib_ref[s], lb_ref[1], sd_scr[s, 1], trib_ref[...],
                                mbd, mbd, p_scr.at[s, 1], True))
    outs = ((af_ref, sa_scr), (df_ref, sd_scr), (ab_ref, sa_scr), (db_ref, sd_scr))
    for k, (o, st_new) in enumerate(_interleave(gens)):
        s, kind = k // 4, k % 4
        o_ref, scr = outs[kind]
        o_ref[s] = o
        scr[s, kind // 2] = st_new

    @pl.when(st.ctx_last(c))
    def _():
        for h in range(N_HEADS):
            hs = slice(h * HEAD_DIM, (h + 1) * HEAD_DIM)
            afin_f[0, h] = sa_scr[0, 0, hs, hs]
            afin_b[0, h] = sa_scr[0, 1, hs, hs]
            dfin_f[0, h] = sd_scr[0, 0, hs, hs].T
            dfin_b[0, h] = sd_scr[0, 1, hs, hs].T


def _mix_scan(r, v, pre, zd, lb, sa0_dec, sd0_dec, hc, st):
    kap, lwf, lwb, kdf, kdb, bbf, bbb = pre
    consts = (hc['mbd'], hc['eye'], hc['low_s'], hc['low_i'], hc['up_s'], hc['up_i'], hc['tri_f'], hc['tri_b'])
    w = WIDTH
    o_shape = jax.ShapeDtypeStruct(kap.shape, F32)
    head_blocks = (N_HEADS, HEAD_DIM, HEAD_DIM)
    fin_shape = jax.ShapeDtypeStruct((st.n_ctx,) + head_blocks, F32)
    fin_specs = [st.fin_fwd(head_blocks), st.fin_bwd(head_blocks)]
    outs = pl.pallas_call(
        functools.partial(_mix_scan_kernel, st),
        grid=(st.n,),
        in_specs=[st.fwd(w)] * 6 + [st.bwd(w)] * 6 + [
                  st.fwd(w, 0), st.fwd(w, 1), st.fwd(w, 3), st.bwd(w, 0), st.bwd(w, 2), st.bwd(w, 3),
                  _full(lb), _full(sa0_dec), _full(sd0_dec)] + [_full(a) for a in consts],
        out_specs=[st.fwd(w), st.bwd(w)] + fin_specs + [st.fwd(w), st.bwd(w)] + fin_specs,
        out_shape=[o_shape, o_shape, fin_shape, fin_shape] * 2,
        scratch_shapes=[pltpu.VMEM((st.n_seg, 2, w, w), F32), pltpu.VMEM((st.n_seg, 2, w, w), F32),
                        pltpu.VMEM((st.n_seg, 2, CHUNK * SUB, w), BF16)],
        compiler_params=_cp(("arbitrary",)),
        name='mix_scan',
    )(r, v, kap, lwf, kdf, bbf, r, v, kap, lwb, kdb, bbb, zd, zd, zd, zd, zd, zd, lb, sa0_dec, sd0_dec, *consts)
    a_f, a_b, afin_f, afin_b, d_f, d_b, dfin_f, dfin_b = outs
    return ((a_f, a_b, jnp.stack([afin_f, afin_b], axis=1)), (d_f, d_b, jnp.stack([dfin_f, dfin_b], axis=1)))


def _lru_coefficients(d, base, seqlen, xp, x, xn, cw_ref, cb_ref, wg_ref, bg_ref, lam_ref, pad, coef):
    tm = x.shape[0]
    halo = SUBLANE
    pad[0:halo, :] = xp
    pad[halo:halo + tm, :] = x
    pad[halo + tm:2 * halo + tm, :] = xn
    pos = jnp.bitwise_and(lax.broadcasted_iota(jnp.int32, (tm, WIDTH), 0) + base, seqlen - 1)
    u = jnp.zeros((tm, WIDTH), F32) + cb_ref[...]
    for j in range(C_CONV):
        off = j - C_CONV // 2
        tap = pad[pl.ds(halo + off, tm), :]
        ok = jnp.logical_and(pos + off >= 0, pos + off < seqlen)
        u += jnp.where(ok, tap, 0.0) * cw_ref[j:j + 1, :]
    cols = slice(2 * d * WIDTH, (2 * d + 2) * WIDTH)
    pre = _dot(u.astype(BF16), wg_ref[:, cols].astype(BF16))
    yield
    gates = _sigmoid(pre + bg_ref[:, cols])
    r, ig = gates[:, :WIDTH], gates[:, WIDTH:]
    log_a = -C_POW * r * _softplus(-lam_ref[d])
    a = jnp.exp(log_a)
    b = jnp.sqrt(-jnp.tanh(log_a) * (a * a + 1.0)) * (ig * u)
    for half in range(WIDTH // LANE):
        coef[0, half] = a[:, half * LANE:(half + 1) * LANE]
        coef[1, half] = b[:, half * LANE:(half + 1) * LANE]


def _lru_scan_kernel(st, seqlens, xf_ref, xpf_ref, xnf_ref, xb_ref, xpb_ref, xnb_ref,
                     cw_ref, cb_ref, wg_ref, bg_ref, lam_ref, h0_ref,
                     hf_ref, hb_ref, finf_ref, finb_ref, h_scr, loc_scr, car_scr, pad_scr, coef_scr):
    c = pl.program_id(0)
    nh = WIDTH // LANE

    @pl.when(c == 0)
    def _():
        h_scr[:, 1:] = h0_ref[...]

    @pl.when(st.ctx_first(c))
    def _():
        h_scr[:, 0] = jnp.zeros((nh, 2, 1, LANE), F32)

    chunk = xf_ref.shape[1]
    x_refs = ((xf_ref, xpf_ref, xnf_ref), (xb_ref, xpb_ref, xnb_ref))
    bases = (c * chunk, (st.n - 1 - c) * chunk)
    gens = []
    for s in range(st.n_seg):
        for d in range(2):
            x_ref, xp_ref, xn_ref = x_refs[d]
            gens.append(_lru_coefficients(d, bases[d], seqlens[s], xp_ref[s], x_ref[s], xn_ref[s], cw_ref, cb_ref,
                                          wg_ref, bg_ref, lam_ref, pad_scr.at[2 * s + d], coef_scr.at[2 * s + d]))
    _interleave(gens)

    rr = LRU_ROWS
    ng = chunk // rr
    chains = [(p, s, d) for s in range(st.n_seg) for d in range(2) for p in range(nh)]
    o_refs = (hf_ref, hb_ref)
    order = (list(range(rr)), list(range(rr - 1, -1, -1)))

    for k, (p, s, d) in enumerate(chains):
        hloc = ploc = None
        for r in order[d]:
            a = coef_scr[2 * s + d, 0, p, pl.ds(r, ng, stride=rr), :]
            b = coef_scr[2 * s + d, 1, p, pl.ds(r, ng, stride=rr), :]
            hloc = b if hloc is None else a * hloc + b
            ploc = a if ploc is None else a * ploc
            loc_scr[k, 0, r] = hloc
            loc_scr[k, 1, r] = ploc

    def carry_step(j, carries):
        out = []
        for k, (p, s, d) in enumerate(chains):
            g = j if d == 0 else ng - 1 - j
            r_end = order[d][-1]
            car_scr[k, pl.ds(g, 1), :] = carries[k]
            out.append(loc_scr[k, 1, r_end, pl.ds(g, 1), :] * carries[k] + loc_scr[k, 0, r_end, pl.ds(g, 1), :])
        return tuple(out)

    carries = lax.fori_loop(0, ng, carry_step, tuple(h_scr[p, s, d] for p, s, d in chains))

    for k, (p, s, d) in enumerate(chains):
        h_scr[p, s, d] = carries[k]
        o_ref = o_refs[d]
        car = car_scr[k]
        for r in range(rr):
            o_ref[p, s, pl.ds(r, ng, stride=rr), :] = loc_scr[k, 0, r] + loc_scr[k, 1, r] * car

    @pl.when(st.ctx_last(c))
    def _():
        for p in range(nh):
            finf_ref[0, :, p * LANE:(p + 1) * LANE] = h_scr[p, 0, 0]
            finb_ref[0, :, p * LANE:(p + 1) * LANE] = h_scr[p, 0, 1]


def _lru_scan(zc, p, h0_dec, ctx_len, dec_len, st):
    nh = WIDTH // LANE
    n_seg, seg_rows = zc.shape[:2]
    n_dec = h0_dec.shape[0]
    h0 = jnp.transpose(h0_dec.reshape(n_dec, 2, nh, 1, LANE), (2, 0, 1, 3, 4))
    n_chain = 2 * n_seg * nh
    chunk = st.chunk
    ng = chunk // LRU_ROWS
    n = st.n
    halo = SUBLANE
    per = chunk // halo
    last = seg_rows // halo - 1
    x_spec = lambda blk: [pl.BlockSpec((n_seg, chunk, WIDTH), lambda c: (0, blk(c), 0)),
                          pl.BlockSpec((n_seg, halo, WIDTH), lambda c: (0, jnp.maximum(blk(c) * per - 1, 0), 0)),
                          pl.BlockSpec((n_seg, halo, WIDTH),
                                       lambda c: (0, jnp.minimum((blk(c) + 1) * per, last), 0))]
    h_fwd = pl.BlockSpec((nh, n_seg, chunk, LANE), lambda c: (0, 0, c, 0))
    h_bwd = pl.BlockSpec((nh, n_seg, chunk, LANE), lambda c: (0, 0, n - 1 - c, 0))
    args = (p['conv_w'], p['conv_b'], p['wg'], p['bg'], p['lam'], h0)
    seqlens = (ctx_len,) + (dec_len,) * n_dec
    h_shape = jax.ShapeDtypeStruct((nh, n_seg, seg_rows, LANE), F32)
    fin_shape = jax.ShapeDtypeStruct((st.n_ctx, 1, WIDTH), F32)
    h_f, h_b, fin_f, fin_b = pl.pallas_call(
        functools.partial(_lru_scan_kernel, st, seqlens),
        grid=(n,),
        in_specs=x_spec(lambda c: c) + x_spec(lambda c: n - 1 - c) + [_full(a) for a in args],
        out_specs=[h_fwd, h_bwd, st.fin_fwd((1, WIDTH)), st.fin_bwd((1, WIDTH))],
        out_shape=[h_shape, h_shape, fin_shape, fin_shape],
        scratch_shapes=[pltpu.VMEM((nh, n_seg, 2, 1, LANE), F32),
                        pltpu.VMEM((n_chain, 2, LRU_ROWS, ng, LANE), F32),
                        pltpu.VMEM((n_chain, ng, LANE), F32),
                        pltpu.VMEM((2 * n_seg, chunk + 2 * halo, WIDTH), F32),
                        pltpu.VMEM((2 * n_seg, 2, nh, chunk, LANE), F32)],
        compiler_params=_cp(("arbitrary",)),
        name='rglru',
    )(zc, zc, zc, zc, zc, zc, *args)
    return h_f, h_b, jnp.concatenate([fin_f, fin_b], axis=1)


def _rope_slab(x, cos, sin):
    lane = lax.broadcasted_iota(jnp.int32, x.shape, 1)
    rot = jnp.where(lane % (B_ROPE // 2) < B_ROPE // 4, -pltpu.roll(x, LANE - B_ROPE // 4, 1),
                    pltpu.roll(x, B_ROPE // 4, 1))
    return x * cos + rot * sin


def _kv_up(ckv_bf, kpe_slab, wuk_ref, wuv_ref, k_o, v_o):
    kn = _dot(ckv_bf, wuk_ref[...])
    for h in range(N_HEADS):
        k_o[:, h * SLAB:(h + 1) * SLAB] = (kn[:, h * SLAB:(h + 1) * SLAB] + kpe_slab).astype(BF16)
    lane = lax.broadcasted_iota(jnp.int32, (1, N_HEADS * SLAB), 1)
    v_o[...] = (_dot(ckv_bf, wuv_ref[...]) + jnp.where(lane % SLAB >= B_VDIM, 1.0, 0.0)).astype(BF16)


def _mla_prep_kernel(zb_ref, cos_ref, sin_ref, qn_ref, kvn_ref, wuq_ref, wuk_ref, wuv_ref,
                     ckv_o, q_o, k_o, v_o):
    cq = zb_ref[:, 0:B_Q_RANK]
    ckv = zb_ref[:, B_Q_RANK:B_Q_RANK + B_KV_RANK]
    kpe = zb_ref[:, B_Q_RANK + B_KV_RANK:ZB]
    cos = cos_ref[...]
    sin = sin_ref[...]
    cqn = cq * lax.rsqrt(jnp.mean(cq * cq, axis=-1, keepdims=True) + RMS_EPS) * qn_ref[...]
    ckvn = ckv * lax.rsqrt(jnp.mean(ckv * ckv, axis=-1, keepdims=True) + RMS_EPS) * kvn_ref[...]
    ckv_o[...] = ckvn
    q = _dot(cqn.astype(BF16), wuq_ref[...])
    for h in range(N_HEADS):
        q_o[:, h * SLAB:(h + 1) * SLAB] = (_rope_slab(q[:, h * SLAB:(h + 1) * SLAB], cos, sin)
                                           * ATTN_LOG2_SCALE).astype(BF16)
    _kv_up(ckvn.astype(BF16), _rope_slab(kpe, cos, sin), wuk_ref, wuv_ref, k_o, v_o)


def _mla_cache_kernel(ckv_ref, kpe_ref, wuk_ref, wuv_ref, k_o, v_o):
    _kv_up(ckv_ref[...].astype(BF16), kpe_ref[...], wuk_ref, wuv_ref, k_o, v_o)


def _mla_cache(ckv, kpe_slab, p):
    rows = ckv.shape[0]
    return pl.pallas_call(
        _mla_cache_kernel,
        out_shape=[jax.ShapeDtypeStruct((rows, N_HEADS * SLAB), BF16)] * 2,
        compiler_params=pltpu.CompilerParams(vmem_limit_bytes=VMEM_LIMIT),
        name='mla_cache',
    )(ckv, kpe_slab, p['wuk'], p['wuv'])


def _attend_heads(q_ref, kv_refs, o_ref):
    tq = q_ref.shape[0]
    low = lax.broadcasted_iota(jnp.int32, (tq, SLAB), 1) < B_VDIM

    def head(h):
        qh = q_ref[:, h * SLAB:(h + 1) * SLAB]
        ss = [_dot_nt(qh, k_ref[:, h * SLAB:(h + 1) * SLAB]) for k_ref, _ in kv_refs]
        yield
        m = ss[0].max(axis=-1, keepdims=True)
        for s in ss[1:]:
            m = jnp.maximum(m, s.max(axis=-1, keepdims=True))
        es = [jnp.exp2((s - m).astype(BF16)) for s in ss]
        yield
        acc = jnp.zeros((tq, SLAB), F32)
        for e, (_, v_ref) in zip(es, kv_refs):
            acc += _dot(e, v_ref[:, h * SLAB:(h + 1) * SLAB])
        return acc / acc[:, B_VDIM:B_VDIM + 1]

    outs = _interleave([head(h) for h in range(N_HEADS)])
    pairs = [jnp.where(low, outs[h], pltpu.roll(outs[h + 1], B_VDIM, 1)) for h in range(0, N_HEADS, 2)]
    o_ref[...] = jnp.concatenate(pairs, axis=-1)


def _attn_kernel(n_ctx, q_ref, kx_ref, vx_ref, k_ref, v_ref, kc_ref, vc_ref, o_ref):
    i = pl.program_id(0)

    @pl.when(i < n_ctx)
    def _():
        _attend_heads(q_ref, [(kx_ref, vx_ref)], o_ref)

    @pl.when(i >= n_ctx)
    def _():
        _attend_heads(q_ref, [(k_ref, v_ref), (kc_ref, vc_ref)], o_ref)


def _attn(q, k, v, kc, vc, n_ctx, ctx_len, n_dec, dec_len, past):
    tq = ctx_len
    nq = dec_len // tq
    slabs = N_HEADS * SLAB
    dec = lambda i: jnp.maximum(i - n_ctx, 0)

    def q_map(i):
        return (jnp.where(i < n_ctx, 0, 1 + dec(i) // nq), jnp.where(i < n_ctx, i, dec(i) % nq), 0)

    ctx_kv = pl.BlockSpec((None, ctx_len, slabs), lambda i: (0, jnp.minimum(i, n_ctx - 1), 0))
    dec_kv = pl.BlockSpec((None, dec_len, slabs), lambda i: (1 + dec(i) // nq, 0, 0))
    cache_kv = pl.BlockSpec((past, slabs), lambda i: (dec(i) // nq, 0))
    return pl.pallas_call(
        functools.partial(_attn_kernel, n_ctx),
        grid=(n_ctx + n_dec * nq,),
        in_specs=[pl.BlockSpec((None, tq, slabs), q_map), ctx_kv, ctx_kv, dec_kv, dec_kv, cache_kv, cache_kv],
        out_specs=pl.BlockSpec((None, tq, WIDTH), q_map),
        out_shape=jax.ShapeDtypeStruct(q.shape[:2] + (WIDTH,), F32),
        compiler_params=_cp(("arbitrary",)),
        name='attn',
    )(q, k, v, k, v, kc, vc)


def _rope_tables(n_ctx_rows, n_dec, dec_len):
    rows = dec_len // GRID_W
    row = np.repeat(np.arange(rows, dtype=np.float32), GRID_W)
    col = np.tile(np.arange(GRID_W, dtype=np.float32), rows)
    half = B_ROPE // 2
    inv = jnp.power(ROPE_BASE, -jnp.arange(0, half, 2, dtype=F32) / half)
    ang_r = jnp.asarray(row)[:, None] * inv
    ang_c = jnp.asarray(col)[:, None] * inv
    ang = jnp.concatenate([ang_r, ang_r, ang_c, ang_c], axis=-1)
    cos = jnp.pad(jnp.cos(ang), ((0, 0), (0, LANE - B_ROPE)), constant_values=1.0)
    sin = jnp.pad(jnp.sin(ang), ((0, 0), (0, LANE - B_ROPE)))
    cos = jnp.concatenate([jnp.ones((n_ctx_rows, LANE), F32)] + [cos] * n_dec, axis=0)
    sin = jnp.concatenate([jnp.zeros((n_ctx_rows, LANE), F32)] + [sin] * n_dec, axis=0)
    return cos, sin


def _block_diag(w):
    g, n, m = w.shape[-3:]
    eye = jnp.eye(g, dtype=w.dtype)
    out = w[..., :, :, None, :] * eye[:, None, :, None]
    return out.reshape(w.shape[:-3] + (g * n, g * m))


def kernel(x_prompt, x_sample, cache_mla_ckv, cache_mla_kpe, state_rwkv, state_rglru, state_hgrn, c, c_ctx,
           w_ada, b_ada, ln_g, ln_b, w_ffn_in, w_ffn_out, w_in, w_out,
           rwkv_w0, rwkv_w2, rwkv_a0, rwkv_a2, rwkv_g2, rwkv_kk, rwkv_ka, rwkv_rk, rwkv_gn_g, rwkv_gn_b,
           mla_qn_g, mla_w_uq, mla_kvn_g, mla_w_ukv,
           rglru_conv_w, rglru_conv_b, rglru_wa, rglru_ba, rglru_wx, rglru_bx, rglru_lam,
           hgrn_lb, hgrn_gn_g):
    n_ctx, ctx_len, _ = x_prompt.shape
    n_dec, dec_len, _ = x_sample.shape
    past = cache_mla_ckv.shape[2]
    seg = n_ctx * ctx_len
    assert seg == dec_len, "context rows must form one segment of the decode sequence length"
    assert ctx_len & (ctx_len - 1) == 0 and dec_len & (dec_len - 1) == 0, "sequence lengths must be powers of two"
    n_seg = 1 + n_dec
    tl = _Tiles(n_seg, seg, min(TM, seg))
    assert ctx_len % LRU_CHUNK == 0
    st = _Streams(n_seg, n_ctx, seg, ctx_len, CHUNK)
    lru_st = _Streams(n_seg, n_ctx, seg, ctx_len, LRU_CHUNK)

    hc = {k: jnp.asarray(v) for k, v in _head_consts().items()}
    ebd = hc['mbd']
    cos, sin = (t.reshape(n_seg, seg, LANE) for t in _rope_tables(seg, n_dec, dec_len))

    assert n_seg <= SUBLANE
    cond8 = jnp.zeros((SUBLANE, D_MODEL), F32).at[0].set(c_ctx).at[1:1 + n_dec].set(c)
    mods = _ada(cond8, w_ada, b_ada).reshape(DEPTH, SUBLANE, N_MOD, D_MODEL)
    mods = jnp.transpose(mods, (0, 2, 1, 3))[:, :, :n_seg, None, :]

    w_ffn_in_bf = w_ffn_in.astype(BF16)
    w_ffn_out_bf = w_ffn_out.astype(BF16)
    w_out_bf = w_out.astype(BF16)
    kpe_end = ZA + B_Q_RANK + B_KV_RANK + B_ROPE
    w_ab = jnp.pad(w_in[:, :, :kpe_end].astype(BF16), ((0, 0), (0, 0), (0, LANE - B_ROPE)))
    w_cd = w_in[:, :, kpe_end:].astype(BF16)
    wuq = mla_w_uq.reshape(DEPTH, B_Q_RANK, N_HEADS, B_NOPE + B_ROPE)
    wuq_p = jnp.concatenate([wuq[..., B_NOPE:], wuq[..., :B_NOPE],
                             jnp.zeros((DEPTH, B_Q_RANK, N_HEADS, SLAB - B_NOPE - B_ROPE), F32)], axis=-1)
    wuq_p = wuq_p.reshape(DEPTH, B_Q_RANK, N_HEADS * SLAB).astype(BF16)
    wukv = mla_w_ukv.reshape(DEPTH, B_KV_RANK, N_HEADS, B_NOPE + B_VDIM)
    wuk_p = jnp.concatenate([jnp.zeros((DEPTH, B_KV_RANK, N_HEADS, B_ROPE), F32), wukv[..., :B_NOPE],
                             jnp.zeros((DEPTH, B_KV_RANK, N_HEADS, SLAB - B_NOPE - B_ROPE), F32)], axis=-1)
    wuk_p = wuk_p.reshape(DEPTH, B_KV_RANK, N_HEADS * SLAB).astype(BF16)
    wuv_p = jnp.concatenate([wukv[..., B_NOPE:], jnp.zeros((DEPTH, B_KV_RANK, N_HEADS, SLAB - B_VDIM), F32)], axis=-1)
    wuv_p = wuv_p.reshape(DEPTH, B_KV_RANK, N_HEADS * SLAB).astype(BF16)
    lru_wg = jnp.concatenate([_block_diag(rglru_wa[:, 0]), _block_diag(rglru_wx[:, 0]),
                              _block_diag(rglru_wa[:, 1]), _block_diag(rglru_wx[:, 1])], axis=-1)
    lru_bg = jnp.concatenate([rglru_ba[:, 0], rglru_bx[:, 0], rglru_ba[:, 1], rglru_bx[:, 1]], axis=-1)[:, None, :]
    lbs = _lower_bounds(hgrn_lb)

    rwkv_s0 = _block_diag(state_rwkv)
    hgrn_s0 = _block_diag(jnp.swapaxes(state_hgrn, -1, -2))
    cache_kpe_slab = jnp.pad(cache_mla_kpe, ((0, 0), (0, 0), (0, 0), (0, LANE - B_ROPE)))

    new_ckv, new_kpe, new_rwkv, new_lru, new_hgrn = [], [], [], [], []
    for l in range(DEPTH):
        m = mods[l]
        lng = ln_g[l][:, None, :]
        lnb = ln_b[l][:, None, :]
        ffn0 = functools.partial(_ffn, sh=m[0], sc=m[1], g=m[2], lng=lng[0], lnb=lnb[0], w_in_bf=w_ffn_in_bf,
                                 w_out_bf=w_ffn_out_bf, l=l, f=0, tl=tl)
        if l == 0:
            x = ffn0((x_prompt.reshape(1, seg, D_MODEL), x_sample), split='in')
        else:
            x = ffn0(x)
        pa = {'w0': rwkv_w0[l][:, None, :], 'w2': rwkv_w2[l], 'a0': rwkv_a0[l][:, None, :], 'a2': rwkv_a2[l],
              'g2': rwkv_g2[l], 'kk': rwkv_kk[l][None, :], 'ka': rwkv_ka[l][None, :], 'rk': rwkv_rk[l][None, :]}
        pb = {'qn_g': mla_qn_g[l][None, :], 'kvn_g': mla_kvn_g[l][None, :], 'wuq': wuq_p[l], 'wuk': wuk_p[l],
              'wuv': wuv_p[l]}
        a_r, a_v, pre, kpe_raw, (ckvn, q_all, k_all, v_all), zc, zd = _inproj(x, m[3], m[4], w_ab, w_cd, l, pa, pb,
                                                                                ebd, cos, sin, tl)

        (oa_f, oa_b, sa_fin), (od_f, od_b, sd_fin) = _mix_scan(a_r, a_v, pre[:7], zd, lbs[l], rwkv_s0[:, l],
                                                                hgrn_s0[:, l], hc, st)
        new_rwkv.append(sa_fin)
        new_hgrn.append(sd_fin)

        kc, vc = _mla_cache(cache_mla_ckv[:, l].reshape(n_dec * past, B_KV_RANK),
                            cache_kpe_slab[:, l].reshape(n_dec * past, LANE), pb)
        yb = _attn(q_all, k_all, v_all, kc, vc, n_ctx, ctx_len, n_dec, dec_len, past)
        new_ckv.append(ckvn[0].reshape(n_ctx, ctx_len, B_KV_RANK))
        new_kpe.append(kpe_raw[0, :, :B_ROPE].reshape(n_ctx, ctx_len, B_ROPE))

        pc = {'conv_w': rglru_conv_w[l], 'conv_b': rglru_conv_b[l][None, :], 'wg': lru_wg[l], 'bg': lru_bg[l],
              'lam': rglru_lam[l][:, None, :]}
        h_f, h_b, h_fin = _lru_scan(zc, pc, state_rglru[:, l], ctx_len, dec_len, lru_st)
        new_lru.append(h_fin)

        x = _outproj(x, oa_f, oa_b, pre[7], pre[8], yb, h_f, h_b, zc, od_f, od_b, zd,
                     rwkv_gn_g[l][None, :], rwkv_gn_b[l][None, :], hgrn_gn_g[l][None, :], ebd,
                     w_out_bf, m[5], lng[1], lnb[1], l, tl)
        ffn1 = functools.partial(_ffn, sh=m[6], sc=m[7], g=m[8], lng=lng[2], lnb=lnb[2], w_in_bf=w_ffn_in_bf,
                                 w_out_bf=w_ffn_out_bf, l=l, f=1, tl=tl)
        if l == DEPTH - 1:
            y_prompt, y_sample = ffn1(x, split='out')
            y_prompt = y_prompt.reshape(n_ctx, ctx_len, D_MODEL)
        else:
            x = ffn1(x)
    return (y_prompt, y_sample, jnp.stack(new_ckv, axis=1), jnp.stack(new_kpe, axis=1),
            jnp.stack(new_rwkv, axis=1), jnp.stack(new_lru, axis=1), jnp.stack(new_hgrn, axis=1))
```

```python
import functools

import numpy as np
import jax
import jax.numpy as jnp
from jax import lax
from jax.experimental import pallas as pl
from jax.experimental.pallas import tpu as pltpu

D_MODEL = 1024
DEPTH = 4
GRID_W = 64
HEAD_DIM = 64
N_HEADS = 4
WIDTH = N_HEADS * HEAD_DIM
A_DECAY_RANK = 64
A_ICLR_RANK = 64
A_GATE_RANK = 128
B_NOPE = 64
B_ROPE = 32
B_VDIM = 64
B_Q_RANK = 256
B_KV_RANK = 128
C_CONV = 4
C_POW = 8.0
D_FF = 2816
N_MOD = 9
ROPE_BASE = 10000.0
LN_EPS = 1e-5
RMS_EPS = 1e-6
RWKV_GN_EPS = 64e-5
DN_ALPHA = (2 * DEPTH) ** 0.25

LANE = 128
SLAB = 128
CHUNK = 64
SUB = 16
SUBLANE = 8
LRU_CHUNK = 256
LRU_ROWS = SUBLANE
TM = 512
TF = 256
ADA_TN = 2304
VMEM_LIMIT = 56 * 1024 * 1024

F32 = jnp.float32
BF16 = jnp.bfloat16
HI = lax.Precision.HIGHEST
NEG = -1e30
LOG2E = float(np.log2(np.e))
ATTN_LOG2_SCALE = float((B_NOPE + B_ROPE) ** -0.5) * LOG2E


def _cp(sem):
    return pltpu.CompilerParams(dimension_semantics=sem, vmem_limit_bytes=VMEM_LIMIT)


def _dot(a, b, hi=False):
    return jnp.dot(a, b, preferred_element_type=F32, precision=HI if hi else None)


def _dot_nt(a, b, hi=False):
    return lax.dot_general(a, b, (((1,), (1,)), ((), ())), preferred_element_type=F32,
                           precision=HI if hi else None)


def _dot_tn(a, b, hi=False):
    return lax.dot_general(a, b, (((0,), (0,)), ((), ())), preferred_element_type=F32,
                           precision=HI if hi else None)


def _head_sums(x, ebd_bf):
    hi = x.astype(BF16)
    lo = (x - hi.astype(F32)).astype(BF16)
    return _dot(hi, ebd_bf) + _dot(lo, ebd_bf)


def _sigmoid(x):
    return 1.0 / (1.0 + jnp.exp(-x))


def _silu(x):
    return x * _sigmoid(x)


def _softplus(x):
    return jnp.maximum(x, 0.0) + jnp.log(1.0 + jnp.exp(-jnp.abs(x)))


_PAIRS = tuple(slice(p * LANE, (p + 1) * LANE) for p in range(WIDTH // LANE))


def _pair_bd(y, mask2):
    return [jnp.concatenate([y[:, sl]] * (LANE // HEAD_DIM), axis=0) * mask2 for sl in _PAIRS]


def _pair_dot(a, rhs_pairs, hi=False):
    return jnp.concatenate([_dot(a[:, sl], r, hi) for sl, r in zip(_PAIRS, rhs_pairs)], axis=-1)


def _pair_dot_nt(a, rhs_pairs, hi=False):
    return jnp.concatenate([_dot_nt(a[:, sl], r, hi) for sl, r in zip(_PAIRS, rhs_pairs)], axis=-1)


def _pair_blocks(blocks):
    z = jnp.zeros_like(blocks[0])
    rows = [jnp.concatenate([blocks[p] if q == p else z for q in range(len(blocks))], axis=-1)
            for p in range(len(blocks))]
    return jnp.concatenate(rows, axis=0)


def _head_consts():
    r = np.arange(WIDTH)
    same = (r[:, None] // HEAD_DIM) == (r[None, :] // HEAD_DIM)
    t = np.arange(CHUNK)[:, None]
    j = (np.arange(WIDTH) % CHUNK)[None, :]
    tt = np.arange(CHUNK)
    return {
        'mbd': same.astype(np.float32),
        'low_s': (j < t).astype(np.float32), 'low_i': (j <= t).astype(np.float32),
        'up_s': (j > t).astype(np.float32), 'up_i': (j >= t).astype(np.float32),
        'eye': (j == t).astype(np.float32),
        'tri_f': (tt[None, :] <= tt[:, None]).astype(np.float32),
        'tri_b': (tt[None, :] >= tt[:, None]).astype(np.float32),
    }


def _lb_kernel(x_ref, o_ref):
    x = x_ref[...]
    m = jnp.max(x, axis=0, keepdims=True)
    e = jnp.exp(x - m)
    sm = e / jnp.sum(e, axis=0, keepdims=True)
    run = sm[0:1]
    rows = [run - sm[0:1]]
    for l in range(1, DEPTH):
        run = run + sm[l:l + 1]
        rows.append(run - sm[0:1])
    o_ref[...] = jnp.concatenate(rows, axis=0)


def _lower_bounds(hgrn_lb):
    flat = hgrn_lb.reshape(DEPTH, 2 * WIDTH)
    out = pl.pallas_call(_lb_kernel, out_shape=jax.ShapeDtypeStruct(flat.shape, F32), name='hgrn_lb')(flat)
    return out.reshape(DEPTH, 2, 1, WIDTH)


def _ada_kernel(c_ref, w_ref, b_ref, o_ref):
    h = _silu(c_ref[...]).astype(BF16)
    o_ref[0] = _dot(h, w_ref[0].astype(BF16)) + b_ref[0]


def _ada(cond8, w_ada, b_ada):
    n = N_MOD * D_MODEL
    tn = ADA_TN
    return pl.pallas_call(
        _ada_kernel,
        grid=(DEPTH, n // tn),
        in_specs=[pl.BlockSpec((SUBLANE, D_MODEL), lambda l, j: (0, 0)),
                  pl.BlockSpec((1, D_MODEL, tn), lambda l, j: (l, 0, j)),
                  pl.BlockSpec((1, 1, tn), lambda l, j: (l, 0, j))],
        out_specs=pl.BlockSpec((1, SUBLANE, tn), lambda l, j: (l, 0, j)),
        out_shape=jax.ShapeDtypeStruct((DEPTH, SUBLANE, n), F32),
        compiler_params=_cp(("arbitrary", "arbitrary")),
        name='ada',
    )(cond8, w_ada, b_ada.reshape(DEPTH, 1, n))


def _layer_norm_rows(y, g, b):
    mu = jnp.mean(y, axis=-1, keepdims=True)
    d = y - mu
    var = jnp.mean(d * d, axis=-1, keepdims=True)
    return d * lax.rsqrt(var + LN_EPS) * g + b


def _delayed(gen, stages):
    for _ in range(stages):
        yield
    return (yield from gen)


def _ffn_tile(x, *refs):
    half = x.shape[0] // 2
    top, bot = _interleave([_ffn_rows(x[:half], *refs),
                            _delayed(_ffn_rows(x[half:], *refs), D_FF // TF // 2)])
    return jnp.concatenate([top, bot], axis=0)


def _ffn_rows(x, sh_ref, sc_ref, g_ref, lng_ref, lnb_ref, win_ref, wout_ref):
    h = (x * (1.0 + sc_ref[0]) + sh_ref[0]).astype(BF16)
    n_chunks = D_FF // TF

    def gate_up(c):
        return (_dot(h, win_ref[:, c * TF:(c + 1) * TF]),
                _dot(h, win_ref[:, D_FF + c * TF:D_FF + (c + 1) * TF]))

    y = jnp.zeros(x.shape, F32)
    cur = gate_up(0)
    for c in range(n_chunks):
        nxt = gate_up(c + 1) if c + 1 < n_chunks else None
        a = (_silu(cur[0]) * cur[1]).astype(BF16)
        y = y + _dot(a, wout_ref[c * TF:(c + 1) * TF, :])
        cur = nxt
        yield
    return _layer_norm_rows(DN_ALPHA * x + 0.5 * g_ref[0] * y, lng_ref[...], lnb_ref[...])


def _ffn_kernel(x_ref, *refs):
    o_ref = refs[-1]
    o_ref[...] = _ffn_tile(x_ref[...], *refs[:-1])


def _ffn_first_kernel(ctx_tiles, xp_ref, xs_ref, *refs):
    o_ref = refs[-1]
    x = jnp.where(pl.program_id(0) < ctx_tiles, xp_ref[...], xs_ref[...])
    o_ref[...] = _ffn_tile(x, *refs[:-1])


def _ffn_last_kernel(ctx_tiles, x_ref, *refs):
    yp_ref, ys_ref = refs[-2:]
    y = _ffn_tile(x_ref[...], *refs[:-2])
    is_ctx = pl.program_id(0) < ctx_tiles

    @pl.when(is_ctx)
    def _():
        yp_ref[...] = y

    @pl.when(jnp.logical_not(is_ctx))
    def _():
        ys_ref[...] = y


class _Tiles:
    def __init__(self, n_seg, seg_rows, tm):
        self.n_seg, self.seg_rows, self.tm = n_seg, seg_rows, tm
        self.per_seg = seg_rows // tm
        self.grid = (n_seg * self.per_seg,)

    def row(self, width, col=0):
        ps = self.per_seg
        return pl.BlockSpec((None, self.tm, width), lambda i: (i // ps, i % ps, col))

    def mod(self):
        ps = self.per_seg
        return pl.BlockSpec((1, 1, D_MODEL), lambda i: (i // ps, 0, 0))

    def shape(self, width, dtype=F32):
        return jax.ShapeDtypeStruct((self.n_seg, self.seg_rows, width), dtype)


def _full(a):
    return pl.BlockSpec(a.shape, lambda i: (0,) * a.ndim)


def _ffn(x, sh, sc, g, lng, lnb, w_in_bf, w_out_bf, l, f, tl, split=None):
    ps, tm = tl.per_seg, tl.tm
    mod = tl.mod()
    vec = pl.BlockSpec((1, D_MODEL), lambda i: (0, 0))
    ctx_blk = pl.BlockSpec((None, tm, D_MODEL), lambda i: (0, jnp.minimum(i, ps - 1), 0))
    dec_blk = pl.BlockSpec((None, tm, D_MODEL),
                           lambda i: (jnp.maximum(i - ps, 0) // ps, jnp.maximum(i - ps, 0) % ps, 0))
    weights = [pl.BlockSpec((None, None, D_MODEL, 2 * D_FF), lambda i: (l, f, 0, 0)),
               pl.BlockSpec((None, None, D_FF, D_MODEL), lambda i: (l, f, 0, 0))]
    rest = [mod, mod, mod, vec, vec] + weights
    args = (sh, sc, g, lng, lnb, w_in_bf, w_out_bf)
    n_dec = tl.n_seg - 1
    if split == 'in':
        body, xs, x_specs = functools.partial(_ffn_first_kernel, ps), tuple(x), [ctx_blk, dec_blk]
    else:
        body, xs, x_specs = _ffn_kernel, (x,), [tl.row(D_MODEL)]
    if split == 'out':
        body = functools.partial(_ffn_last_kernel, ps)
        out_specs = [ctx_blk, dec_blk]
        out_shape = [jax.ShapeDtypeStruct((1, tl.seg_rows, D_MODEL), F32),
                     jax.ShapeDtypeStruct((n_dec, tl.seg_rows, D_MODEL), F32)]
    else:
        out_specs, out_shape = tl.row(D_MODEL), tl.shape(D_MODEL)
    return pl.pallas_call(
        body,
        grid=tl.grid,
        in_specs=x_specs + rest,
        out_specs=out_specs,
        out_shape=out_shape,
        compiler_params=_cp(("arbitrary",) if split == 'out' else ("parallel",)),
        name='ffn',
    )(*xs, *args)


ZA = 3 * WIDTH + 2 * A_DECAY_RANK + 2 * A_ICLR_RANK + A_GATE_RANK
ZB = B_Q_RANK + B_KV_RANK + LANE
ZC = 2 * WIDTH
ZD = 5 * WIDTH


N_RWKV_PRE = 9
N_RWKV_PAR = 9
N_MLA_PREP = 4
N_MLA_PAR = 7


def _inproj_kernel(x_ref, sh_ref, sc_ref, wab_ref, wcd_ref, *rest):
    a_par, b_par = rest[:N_RWKV_PAR], rest[N_RWKV_PAR:N_RWKV_PAR + N_MLA_PAR]
    outs = rest[N_RWKV_PAR + N_MLA_PAR:]
    r_o, v_o = outs[0:2]
    a_out = outs[2:2 + N_RWKV_PRE]
    kpe_o = outs[2 + N_RWKV_PRE]
    b_out = outs[3 + N_RWKV_PRE:3 + N_RWKV_PRE + N_MLA_PREP]
    zc_ref, zd_ref, za_scr, zb_scr = outs[3 + N_RWKV_PRE + N_MLA_PREP:]
    h = (x_ref[...] * (1.0 + sc_ref[0]) + sh_ref[0]).astype(BF16)
    za_scr[...] = _dot(h, wab_ref[:, 0:ZA])
    zb_scr[...] = _dot(h, wab_ref[:, ZA:ZA + ZB])
    r_o[...] = za_scr[:, 0:WIDTH]
    v_o[...] = za_scr[:, 2 * WIDTH:3 * WIDTH]
    _rwkv_pre_kernel(za_scr, *a_par, *a_out)
    zc_ref[...] = _dot(h, wcd_ref[:, 0:ZC])
    kpe_o[...] = zb_scr[:, B_Q_RANK + B_KV_RANK:ZB]
    _mla_prep_kernel(zb_scr, *b_par, *b_out)
    zd_ref[...] = _dot(h, wcd_ref[:, ZC:ZC + ZD])


def _inproj(x, sh, sc, w_ab, w_cd, l, pa, pb, ebd, cos, sin, tl):
    mod = tl.mod()
    a_par = (pa['w0'], pa['w2'], pa['a0'], pa['a2'], pa['g2'], pa['kk'], pa['ka'], pa['rk'], ebd)
    b_par = (pb['qn_g'], pb['kvn_g'], pb['wuq'], pb['wuk'], pb['wuv'])
    slabs = N_HEADS * SLAB
    w = WIDTH
    out_specs = ([tl.row(w)] * (2 + N_RWKV_PRE) + [tl.row(LANE), tl.row(B_KV_RANK)] + [tl.row(slabs)] * 3
                 + [tl.row(ZC), tl.row(ZD)])
    out_shape = ([tl.shape(w)] * (2 + N_RWKV_PRE) + [tl.shape(LANE), tl.shape(B_KV_RANK)]
                 + [tl.shape(slabs, BF16)] * 3 + [tl.shape(ZC), tl.shape(ZD)])
    outs = pl.pallas_call(
        _inproj_kernel,
        grid=tl.grid,
        in_specs=[tl.row(D_MODEL), mod, mod, pl.BlockSpec((None, D_MODEL, ZA + ZB), lambda i: (l, 0, 0)),
                  pl.BlockSpec((None, D_MODEL, ZC + ZD), lambda i: (l, 0, 0))]
                 + [_full(a) for a in a_par] + [tl.row(LANE), tl.row(LANE)] + [_full(a) for a in b_par],
        out_specs=out_specs,
        out_shape=out_shape,
        scratch_shapes=[pltpu.VMEM((tl.tm, ZA), F32), pltpu.VMEM((tl.tm, ZB), F32)],
        compiler_params=_cp(("parallel",)),
        name='mixer_in',
    )(x, sh, sc, w_ab, w_cd, *a_par, cos, sin, *b_par)
    r, v = outs[0:2]
    pre = outs[2:2 + N_RWKV_PRE]
    kpe = outs[2 + N_RWKV_PRE]
    mla = outs[3 + N_RWKV_PRE:3 + N_RWKV_PRE + N_MLA_PREP]
    zc, zd = outs[-2:]
    return r, v, pre, kpe, mla, zc, zd


def _gelu_tanh(x):
    return 0.5 * x * (1.0 + jnp.tanh(np.sqrt(2.0 / np.pi) * (x + 0.044715 * (x * x * x))))


def _outproj_kernel(x_ref, af_ref, ab_ref, bon_ref, gate_ref, yb_ref, hf_ref, hb_ref, cg_ref, df_ref, db_ref,
                    dg_ref, agn_g_ref, agn_b_ref, dgn_g_ref, ebd_ref, w_ref, g_ref, lng_ref, lnb_ref, o_ref):
    avg = (ebd_ref[...] * (1.0 / HEAD_DIM)).astype(BF16)
    oa = af_ref[...] + ab_ref[...]
    da = oa - _head_sums(oa, avg)
    gn = da * lax.rsqrt(_head_sums(da * da, avg) + RWKV_GN_EPS) * agn_g_ref[...] + agn_b_ref[...]
    ya = (gn + bon_ref[...]) * gate_ref[...]
    h = jnp.concatenate([hf_ref[p] + hb_ref[p] for p in range(WIDTH // LANE)], axis=-1)
    yc = h * _gelu_tanh(cg_ref[...])
    od = df_ref[...] + db_ref[...]
    yd = od * lax.rsqrt(_head_sums(od * od, avg) + RMS_EPS) * dgn_g_ref[...] * _silu(dg_ref[...])
    y = _dot(ya.astype(BF16), w_ref[0:WIDTH, :])
    y += _dot(yb_ref[...].astype(BF16), w_ref[WIDTH:2 * WIDTH, :])
    y += _dot(yc.astype(BF16), w_ref[2 * WIDTH:3 * WIDTH, :])
    y += _dot(yd.astype(BF16), w_ref[3 * WIDTH:4 * WIDTH, :])
    o_ref[...] = _layer_norm_rows(DN_ALPHA * x_ref[...] + g_ref[0] * y, lng_ref[...], lnb_ref[...])


def _outproj(x, oa_f, oa_b, bonus, gate, yb, hf, hb, zc, od_f, od_b, zd, agn_g, agn_b, dgn_g, ebd,
             w_out_bf, g, lng, lnb, l, tl):
    vec = pl.BlockSpec((1, D_MODEL), lambda i: (0, 0))
    wvec = pl.BlockSpec((1, WIDTH), lambda i: (0, 0))
    ps = tl.per_seg
    halves = pl.BlockSpec((WIDTH // LANE, None, tl.tm, LANE), lambda i: (0, i // ps, i % ps, 0))
    w = WIDTH
    return pl.pallas_call(
        _outproj_kernel,
        grid=tl.grid,
        in_specs=[tl.row(D_MODEL), tl.row(w), tl.row(w), tl.row(w), tl.row(w), tl.row(w), halves, halves,
                  tl.row(w, 1), tl.row(w), tl.row(w), tl.row(w, 4), wvec, wvec, wvec, _full(ebd),
                  pl.BlockSpec((None, 4 * WIDTH, D_MODEL), lambda i: (l, 0, 0)), tl.mod(), vec, vec],
        out_specs=tl.row(D_MODEL),
        out_shape=tl.shape(D_MODEL),
        compiler_params=_cp(("parallel",)),
        name='mixer_out',
    )(x, oa_f, oa_b, bonus, gate, yb, hf, hb, zc, od_f, od_b, zd, agn_g, agn_b, dgn_g, ebd, w_out_bf, g, lng, lnb)


class _Streams:
    def __init__(self, n_seg, n_ctx, seg_rows, ctx_len, chunk):
        self.n_seg, self.n_ctx, self.chunk = n_seg, n_ctx, chunk
        self.n = seg_rows // chunk
        self.ctx_n = ctx_len // chunk

    def fwd(self, width, col=0):
        return pl.BlockSpec((self.n_seg, self.chunk, width), lambda c: (0, c, col))

    def bwd(self, width, col=0):
        n = self.n
        return pl.BlockSpec((self.n_seg, self.chunk, width), lambda c: (0, n - 1 - c, col))

    def fin_fwd(self, shape):
        ctx_n = self.ctx_n
        return pl.BlockSpec((1,) + shape, lambda c: (c // ctx_n,) + (0,) * len(shape))

    def fin_bwd(self, shape):
        ctx_n, n_ctx = self.ctx_n, self.n_ctx
        return pl.BlockSpec((1,) + shape, lambda c: (n_ctx - 1 - c // ctx_n,) + (0,) * len(shape))

    def ctx_first(self, c):
        return c % self.ctx_n == 0

    def ctx_last(self, c):
        return c % self.ctx_n == self.ctx_n - 1


def _rwkv_pre_kernel(za_ref, w0_ref, w2_ref, a0_ref, a2_ref, g2_ref, kkp_ref, kap_ref, rkp_ref, ebd_ref,
                     kap_o, lwf_o, lwb_o, kdf_o, kdb_o, bbf_o, bbb_o, bon_o, gate_o):
    r = za_ref[:, 0:WIDTH]
    k = za_ref[:, WIDTH:2 * WIDTH]
    v = za_ref[:, 2 * WIDTH:3 * WIDTH]
    o = 3 * WIDTH
    xw = (za_ref[:, o:o + A_DECAY_RANK], za_ref[:, o + A_DECAY_RANK:o + 2 * A_DECAY_RANK])
    o += 2 * A_DECAY_RANK
    xa = (za_ref[:, o:o + A_ICLR_RANK], za_ref[:, o + A_ICLR_RANK:o + 2 * A_ICLR_RANK])
    o += 2 * A_ICLR_RANK
    xg = za_ref[:, o:o + A_GATE_RANK]
    ebd = ebd_ref[...].astype(BF16)

    kk = k * kkp_ref[...]
    nrm = jnp.sqrt(_head_sums(kk * kk, ebd))
    kappa = kk / jnp.maximum(nrm, 1e-12)
    kap_o[...] = kappa
    k_sum = jnp.zeros_like(r)
    for d, (lw_o, kd_o, bb_o) in enumerate(((lwf_o, kdf_o, bbf_o), (lwb_o, kdb_o, bbb_o))):
        w_log = -_softplus(-(w0_ref[d] + _dot(jnp.tanh(xw[d]).astype(BF16), w2_ref[d].astype(BF16)))) - 0.5
        lw_o[...] = -jnp.exp(w_log)
        a = _sigmoid(a0_ref[d] + _dot(xa[d].astype(BF16), a2_ref[d].astype(BF16)))
        k_d = k * (1.0 + (a - 1.0) * kap_ref[...])
        kd_o[...] = k_d
        bb_o[...] = kappa * a
        k_sum += k_d
    bon_o[...] = _head_sums(r * k_sum * rkp_ref[...], ebd) * v
    gate_o[...] = _dot(_sigmoid(xg).astype(BF16), g2_ref[...].astype(BF16))


def _rwkv_chunk(r, v, kap, lw, kd, bb, st, mbd, eye, m_strict, m_incl, tri):
    cum = _dot(tri, lw, hi=True)
    yield
    tot = jnp.sum(lw, axis=0, keepdims=True)
    g_in = jnp.exp(cum)
    g_inv = jnp.exp(-cum)
    g_ex = jnp.exp(cum - lw)
    g_end = jnp.exp(tot - cum)
    cast = lambda a: a.astype(BF16)
    mask2 = cast(mbd[:LANE, :LANE])
    x = cast(jnp.concatenate([kap * g_ex, r * g_in], axis=0))
    ab = _pair_dot_nt(x, _pair_bd(cast(bb * g_inv), mask2))
    ak = _pair_dot_nt(x, _pair_bd(cast(kd * g_inv), mask2))
    xs = _pair_dot_nt(x, [cast(st[sl, sl]) for sl in _PAIRS])
    yield
    c = r.shape[0]
    strict = m_strict > 0.5
    incl = m_incl > 0.5
    a_ub = jnp.where(strict, ab[:c], 0.0)
    a_rb = jnp.where(incl, ab[c:], 0.0)
    a_uk = jnp.where(strict, ak[:c], 0.0)
    a_rk = jnp.where(incl, ak[c:], 0.0)

    def catmul(pc, q):
        return _pair_dot(cast(pc), _pair_bd(cast(q), mask2))

    xs = xs + catmul(jnp.concatenate([a_uk, a_rk], axis=0), v)
    rhs, o_v = xs[:c], xs[c:]
    xp = -a_ub
    inv = eye + xp
    xp = catmul(xp, xp)
    yield
    for _ in range(int(np.log2(c)) - 2):
        both = catmul(jnp.concatenate([inv, xp], axis=0), xp)
        yield
        inv = inv + both[:c]
        xp = both[c:]
    inv = inv + catmul(inv, xp)
    yield
    u = -catmul(inv, rhs)
    yield
    o = o_v + catmul(a_rb, u)
    uv = cast(jnp.concatenate([u, v], axis=0))
    bk = cast(jnp.concatenate([bb * g_end, kd * g_end], axis=0))
    upd = _pair_blocks([_dot_tn(uv[:, sl], bk[:, sl]) * mbd[:LANE, :LANE] for sl in _PAIRS])
    st_new = st * jnp.exp(tot) + upd
    return o, st_new


def _interleave(gens):
    results = [None] * len(gens)
    live = list(range(len(gens)))
    while live:
        still = []
        for k in live:
            try:
                next(gens[k])
                still.append(k)
            except StopIteration as done:
                results[k] = done.value
        live = still
    return results


def _init_states(st, c, st_scr, s0_ref):
    @pl.when(c == 0)
    def _():
        st_scr[1:] = s0_ref[...]

    @pl.when(st.ctx_first(c))
    def _():
        st_scr[0] = jnp.zeros(st_scr.shape[1:], F32)


def _hgrn_chunk(xq, xf, xi, lb, st, tri, ebd, mbd, p_scr, reverse):
    c = xq.shape[0]
    nb = c // SUB
    q = _silu(xq)
    gsig = lb + (1.0 - lb) * _sigmoid(xf)
    kk = 1.0 - gsig
    lg = jnp.log(gsig)
    cum = _dot(tri, lg, hi=True)
    yield
    cum = cum * LOG2E
    tot = jnp.sum(lg, axis=0, keepdims=True) * LOG2E
    lane_s = lax.broadcasted_iota(jnp.int32, (SUB, WIDTH), 1) % c
    half = SUB // 2
    row_h = lax.broadcasted_iota(jnp.int32, (half, WIDTH), 0)
    blk = lambda a, i: a[i * SUB:(i + 1) * SUB]
    end_row = (lambda j: j * SUB) if reverse else (lambda j: j * SUB + SUB - 1)
    later = (lambda j: range(0, j)) if reverse else (lambda j: range(j + 1, nb))

    k_end = jnp.concatenate([blk(kk, j) * jnp.exp2(cum[end_row(j):end_row(j) + 1] - blk(cum, j))
                             for j in range(nb)], axis=0)
    q_parts, where_part = [], {}
    for j in range(nb):
        for i in later(j):
            where_part[(i, j)] = len(q_parts)
            q_parts.append(blk(q, i) * jnp.exp2(blk(cum, i) - cum[end_row(j):end_row(j) + 1]))
    mask2 = mbd[:LANE, :LANE].astype(BF16)
    cross = _pair_dot_nt(jnp.concatenate(q_parts, axis=0).astype(BF16), _pair_bd(k_end.astype(BF16), mask2))
    o_state = _pair_dot_nt((q * jnp.exp2(cum)).astype(BF16), [st[sl, sl].astype(BF16) for sl in _PAIRS])
    xi_bf = xi.astype(BF16)
    k_out = (kk * jnp.exp2(tot - cum)).astype(BF16)
    upd = _pair_blocks([_dot_tn(xi_bf[:, sl], k_out[:, sl]) * mbd[:LANE, :LANE] for sl in _PAIRS])
    yield

    for i in range(nb):
        cb, qb = blk(cum, i), blk(q, i)
        for sl in range(SUB):
            s = i * SUB + sl
            parts = []
            for lo in (0, half):
                hi = lo + half - 1
                none_valid = lo > sl if reverse else hi < sl
                all_valid = hi <= sl if reverse else lo >= sl
                if none_valid:
                    parts.append(jnp.zeros((half, WIDTH), F32))
                    continue
                d = cb[lo:lo + half] - cum[s:s + 1]
                if not all_valid:
                    valid = (row_h + lo <= sl) if reverse else (row_h + lo >= sl)
                    d = jnp.where(valid, d, NEG)
                parts.append(jnp.exp2(d) * qb[lo:lo + half] * kk[s:s + 1])
            p_scr[s * SUB:(s + 1) * SUB, :] = jnp.concatenate(parts, axis=0).astype(BF16)
        yield
    same = _pair_dot(p_scr, [mask2] * len(_PAIRS))
    yield

    att_rows = []
    for i in range(nb):
        att = jnp.zeros((SUB, WIDTH), F32)
        for sl in range(SUB):
            s = i * SUB + sl
            att = jnp.where(lane_s == s, same[s * SUB:(s + 1) * SUB], att)
        for j in range(nb):
            if (i, j) in where_part:
                n = where_part[(i, j)]
                att = jnp.where(lane_s // SUB == j, cross[n * SUB:(n + 1) * SUB], att)
        att_rows.append(att)
    att = jnp.concatenate(att_rows, axis=0)
    o = o_state + _pair_dot(att.astype(BF16), _pair_bd(xi_bf, mask2))
    st_new = st * jnp.exp2(tot) + upd
    return o, st_new


def _mix_scan_kernel(st,
                     rf_ref, vf_ref, kapf_ref, lwf_ref, kdf_ref, bbf_ref,
                     rb_ref, vb_ref, kapb_ref, lwb_ref, kdb_ref, bbb_ref,
                     qf_ref, ff_ref, if_ref, qb_ref, fb_ref, ib_ref, lb_ref,
                     sa0_ref, sd0_ref, mbd_ref, eye_ref, lows_ref, lowi_ref, ups_ref, upi_ref, trif_ref, trib_ref,
                     af_ref, ab_ref, afin_f, afin_b, df_ref, db_ref, dfin_f, dfin_b,
                     sa_scr, sd_scr, p_scr):
    c = pl.program_id(0)
    _init_states(st, c, sa_scr, sa0_ref)
    _init_states(st, c, sd_scr, sd0_ref)
    mbd = mbd_ref[...]
    eye = eye_ref[...]
    gens = []
    for s in range(st.n_seg):
        gens.append(_rwkv_chunk(rf_ref[s], vf_ref[s], kapf_ref[s], lwf_ref[s], kdf_ref[s], bbf_ref[s],
                                sa_scr[s, 0], mbd, eye, lows_ref[...], lowi_ref[...], trif_ref[...]))
        gens.append(_hgrn_chunk(qf_ref[s], ff_ref[s], if_ref[s], lb_ref[0], sd_scr[s, 0], trif_ref[...],
                                mbd, mbd, p_scr.at[s, 0], False))
        gens.append(_rwkv_chunk(rb_ref[s], vb_ref[s], kapb_ref[s], lwb_ref[s], kdb_ref[s], bbb_ref[s],
                                sa_scr[s, 1], mbd, eye, ups_ref[...], upi_ref[...], trib_ref[...]))
        gens.append(_hgrn_chunk(qb_ref[s], fb_ref[s], ib_ref[s], lb_ref[1], sd_scr[s, 1], trib_ref[...],
                                mbd, mbd, p_scr.at[s, 1], True))
    outs = ((af_ref, sa_scr), (df_ref, sd_scr), (ab_ref, sa_scr), (db_ref, sd_scr))
    for k, (o, st_new) in enumerate(_interleave(gens)):
        s, kind = k // 4, k % 4
        o_ref, scr = outs[kind]
        o_ref[s] = o
        scr[s, kind // 2] = st_new

    @pl.when(st.ctx_last(c))
    def _():
        for h in range(N_HEADS):
            hs = slice(h * HEAD_DIM, (h + 1) * HEAD_DIM)
            afin_f[0, h] = sa_scr[0, 0, hs, hs]
            afin_b[0, h] = sa_scr[0, 1, hs, hs]
            dfin_f[0, h] = sd_scr[0, 0, hs, hs].T
            dfin_b[0, h] = sd_scr[0, 1, hs, hs].T


def _mix_scan(r, v, pre, zd, lb, sa0_dec, sd0_dec, hc, st):
    kap, lwf, lwb, kdf, kdb, bbf, bbb = pre
    consts = (hc['mbd'], hc['eye'], hc['low_s'], hc['low_i'], hc['up_s'], hc['up_i'], hc['tri_f'], hc['tri_b'])
    w = WIDTH
    o_shape = jax.ShapeDtypeStruct(kap.shape, F32)
    head_blocks = (N_HEADS, HEAD_DIM, HEAD_DIM)
    fin_shape = jax.ShapeDtypeStruct((st.n_ctx,) + head_blocks, F32)
    fin_specs = [st.fin_fwd(head_blocks), st.fin_bwd(head_blocks)]
    outs = pl.pallas_call(
        functools.partial(_mix_scan_kernel, st),
        grid=(st.n,),
        in_specs=[st.fwd(w)] * 6 + [st.bwd(w)] * 6 + [
                  st.fwd(w, 0), st.fwd(w, 1), st.fwd(w, 3), st.bwd(w, 0), st.bwd(w, 2), st.bwd(w, 3),
                  _full(lb), _full(sa0_dec), _full(sd0_dec)] + [_full(a) for a in consts],
        out_specs=[st.fwd(w), st.bwd(w)] + fin_specs + [st.fwd(w), st.bwd(w)] + fin_specs,
        out_shape=[o_shape, o_shape, fin_shape, fin_shape] * 2,
        scratch_shapes=[pltpu.VMEM((st.n_seg, 2, w, w), F32), pltpu.VMEM((st.n_seg, 2, w, w), F32),
                        pltpu.VMEM((st.n_seg, 2, CHUNK * SUB, w), BF16)],
        compiler_params=_cp(("arbitrary",)),
        name='mix_scan',
    )(r, v, kap, lwf, kdf, bbf, r, v, kap, lwb, kdb, bbb, zd, zd, zd, zd, zd, zd, lb, sa0_dec, sd0_dec, *consts)
    a_f, a_b, afin_f, afin_b, d_f, d_b, dfin_f, dfin_b = outs
    return ((a_f, a_b, jnp.stack([afin_f, afin_b], axis=1)), (d_f, d_b, jnp.stack([dfin_f, dfin_b], axis=1)))


def _lru_coefficients(d, base, seqlen, xp, x, xn, cw_ref, cb_ref, wg_ref, bg_ref, lam_ref, pad, coef):
    tm = x.shape[0]
    halo = SUBLANE
    pad[0:halo, :] = xp
    pad[halo:halo + tm, :] = x
    pad[halo + tm:2 * halo + tm, :] = xn
    pos = jnp.bitwise_and(lax.broadcasted_iota(jnp.int32, (tm, WIDTH), 0) + base, seqlen - 1)
    u = jnp.zeros((tm, WIDTH), F32) + cb_ref[...]
    for j in range(C_CONV):
        off = j - C_CONV // 2
        tap = pad[pl.ds(halo + off, tm), :]
        ok = jnp.logical_and(pos + off >= 0, pos + off < seqlen)
        u += jnp.where(ok, tap, 0.0) * cw_ref[j:j + 1, :]
    cols = slice(2 * d * WIDTH, (2 * d + 2) * WIDTH)
    pre = _dot(u.astype(BF16), wg_ref[:, cols].astype(BF16))
    yield
    gates = _sigmoid(pre + bg_ref[:, cols])
    r, ig = gates[:, :WIDTH], gates[:, WIDTH:]
    log_a = -C_POW * r * _softplus(-lam_ref[d])
    a = jnp.exp(log_a)
    b = jnp.sqrt(-jnp.tanh(log_a) * (a * a + 1.0)) * (ig * u)
    for half in range(WIDTH // LANE):
        coef[0, half] = a[:, half * LANE:(half + 1) * LANE]
        coef[1, half] = b[:, half * LANE:(half + 1) * LANE]


def _lru_scan_kernel(st, seqlens, xf_ref, xpf_ref, xnf_ref, xb_ref, xpb_ref, xnb_ref,
                     cw_ref, cb_ref, wg_ref, bg_ref, lam_ref, h0_ref,
                     hf_ref, hb_ref, finf_ref, finb_ref, h_scr, loc_scr, car_scr, pad_scr, coef_scr):
    c = pl.program_id(0)
    nh = WIDTH // LANE

    @pl.when(c == 0)
    def _():
        h_scr[:, 1:] = h0_ref[...]

    @pl.when(st.ctx_first(c))
    def _():
        h_scr[:, 0] = jnp.zeros((nh, 2, 1, LANE), F32)

    chunk = xf_ref.shape[1]
    x_refs = ((xf_ref, xpf_ref, xnf_ref), (xb_ref, xpb_ref, xnb_ref))
    bases = (c * chunk, (st.n - 1 - c) * chunk)
    gens = []
    for s in range(st.n_seg):
        for d in range(2):
            x_ref, xp_ref, xn_ref = x_refs[d]
            gens.append(_lru_coefficients(d, bases[d], seqlens[s], xp_ref[s], x_ref[s], xn_ref[s], cw_ref, cb_ref,
                                          wg_ref, bg_ref, lam_ref, pad_scr.at[2 * s + d], coef_scr.at[2 * s + d]))
    _interleave(gens)

    rr = LRU_ROWS
    ng = chunk // rr
    chains = [(p, s, d) for s in range(st.n_seg) for d in range(2) for p in range(nh)]
    o_refs = (hf_ref, hb_ref)
    order = (list(range(rr)), list(range(rr - 1, -1, -1)))

    for k, (p, s, d) in enumerate(chains):
        hloc = ploc = None
        for r in order[d]:
            a = coef_scr[2 * s + d, 0, p, pl.ds(r, ng, stride=rr), :]
            b = coef_scr[2 * s + d, 1, p, pl.ds(r, ng, stride=rr), :]
            hloc = b if hloc is None else a * hloc + b
            ploc = a if ploc is None else a * ploc
            loc_scr[k, 0, r] = hloc
            loc_scr[k, 1, r] = ploc

    def carry_step(j, carries):
        out = []
        for k, (p, s, d) in enumerate(chains):
            g = j if d == 0 else ng - 1 - j
            r_end = order[d][-1]
            car_scr[k, pl.ds(g, 1), :] = carries[k]
            out.append(loc_scr[k, 1, r_end, pl.ds(g, 1), :] * carries[k] + loc_scr[k, 0, r_end, pl.ds(g, 1), :])
        return tuple(out)

    carries = lax.fori_loop(0, ng, carry_step, tuple(h_scr[p, s, d] for p, s, d in chains))

    for k, (p, s, d) in enumerate(chains):
        h_scr[p, s, d] = carries[k]
        o_ref = o_refs[d]
        car = car_scr[k]
        for r in range(rr):
            o_ref[p, s, pl.ds(r, ng, stride=rr), :] = loc_scr[k, 0, r] + loc_scr[k, 1, r] * car

    @pl.when(st.ctx_last(c))
    def _():
        for p in range(nh):
            finf_ref[0, :, p * LANE:(p + 1) * LANE] = h_scr[p, 0, 0]
            finb_ref[0, :, p * LANE:(p + 1) * LANE] = h_scr[p, 0, 1]


def _lru_scan(zc, p, h0_dec, ctx_len, dec_len, st):
    nh = WIDTH // LANE
    n_seg, seg_rows = zc.shape[:2]
    n_dec = h0_dec.shape[0]
    h0 = jnp.transpose(h0_dec.reshape(n_dec, 2, nh, 1, LANE), (2, 0, 1, 3, 4))
    n_chain = 2 * n_seg * nh
    chunk = st.chunk
    ng = chunk // LRU_ROWS
    n = st.n
    halo = SUBLANE
    per = chunk // halo
    last = seg_rows // halo - 1
    x_spec = lambda blk: [pl.BlockSpec((n_seg, chunk, WIDTH), lambda c: (0, blk(c), 0)),
                          pl.BlockSpec((n_seg, halo, WIDTH), lambda c: (0, jnp.maximum(blk(c) * per - 1, 0), 0)),
                          pl.BlockSpec((n_seg, halo, WIDTH),
                                       lambda c: (0, jnp.minimum((blk(c) + 1) * per, last), 0))]
    h_fwd = pl.BlockSpec((nh, n_seg, chunk, LANE), lambda c: (0, 0, c, 0))
    h_bwd = pl.BlockSpec((nh, n_seg, chunk, LANE), lambda c: (0, 0, n - 1 - c, 0))
    args = (p['conv_w'], p['conv_b'], p['wg'], p['bg'], p['lam'], h0)
    seqlens = (ctx_len,) + (dec_len,) * n_dec
    h_shape = jax.ShapeDtypeStruct((nh, n_seg, seg_rows, LANE), F32)
    fin_shape = jax.ShapeDtypeStruct((st.n_ctx, 1, WIDTH), F32)
    h_f, h_b, fin_f, fin_b = pl.pallas_call(
        functools.partial(_lru_scan_kernel, st, seqlens),
        grid=(n,),
        in_specs=x_spec(lambda c: c) + x_spec(lambda c: n - 1 - c) + [_full(a) for a in args],
        out_specs=[h_fwd, h_bwd, st.fin_fwd((1, WIDTH)), st.fin_bwd((1, WIDTH))],
        out_shape=[h_shape, h_shape, fin_shape, fin_shape],
        scratch_shapes=[pltpu.VMEM((nh, n_seg, 2, 1, LANE), F32),
                        pltpu.VMEM((n_chain, 2, LRU_ROWS, ng, LANE), F32),
                        pltpu.VMEM((n_chain, ng, LANE), F32),
                        pltpu.VMEM((2 * n_seg, chunk + 2 * halo, WIDTH), F32),
                        pltpu.VMEM((2 * n_seg, 2, nh, chunk, LANE), F32)],
        compiler_params=_cp(("arbitrary",)),
        name='rglru',
    )(zc, zc, zc, zc, zc, zc, *args)
    return h_f, h_b, jnp.concatenate([fin_f, fin_b], axis=1)


def _rope_slab(x, cos, sin):
    lane = lax.broadcasted_iota(jnp.int32, x.shape, 1)
    rot = jnp.where(lane % (B_ROPE // 2) < B_ROPE // 4, -pltpu.roll(x, LANE - B_ROPE // 4, 1),
                    pltpu.roll(x, B_ROPE // 4, 1))
    return x * cos + rot * sin


def _kv_up(ckv_bf, kpe_slab, wuk_ref, wuv_ref, k_o, v_o):
    kn = _dot(ckv_bf, wuk_ref[...])
    for h in range(N_HEADS):
        k_o[:, h * SLAB:(h + 1) * SLAB] = (kn[:, h * SLAB:(h + 1) * SLAB] + kpe_slab).astype(BF16)
    lane = lax.broadcasted_iota(jnp.int32, (1, N_HEADS * SLAB), 1)
    v_o[...] = (_dot(ckv_bf, wuv_ref[...]) + jnp.where(lane % SLAB >= B_VDIM, 1.0, 0.0)).astype(BF16)


def _mla_prep_kernel(zb_ref, cos_ref, sin_ref, qn_ref, kvn_ref, wuq_ref, wuk_ref, wuv_ref,
                     ckv_o, q_o, k_o, v_o):
    cq = zb_ref[:, 0:B_Q_RANK]
    ckv = zb_ref[:, B_Q_RANK:B_Q_RANK + B_KV_RANK]
    kpe = zb_ref[:, B_Q_RANK + B_KV_RANK:ZB]
    cos = cos_ref[...]
    sin = sin_ref[...]
    cqn = cq * lax.rsqrt(jnp.mean(cq * cq, axis=-1, keepdims=True) + RMS_EPS) * qn_ref[...]
    ckvn = ckv * lax.rsqrt(jnp.mean(ckv * ckv, axis=-1, keepdims=True) + RMS_EPS) * kvn_ref[...]
    ckv_o[...] = ckvn
    q = _dot(cqn.astype(BF16), wuq_ref[...])
    for h in range(N_HEADS):
        q_o[:, h * SLAB:(h + 1) * SLAB] = (_rope_slab(q[:, h * SLAB:(h + 1) * SLAB], cos, sin)
                                           * ATTN_LOG2_SCALE).astype(BF16)
    _kv_up(ckvn.astype(BF16), _rope_slab(kpe, cos, sin), wuk_ref, wuv_ref, k_o, v_o)


def _mla_cache_kernel(ckv_ref, kpe_ref, wuk_ref, wuv_ref, k_o, v_o):
    _kv_up(ckv_ref[...].astype(BF16), kpe_ref[...], wuk_ref, wuv_ref, k_o, v_o)


def _mla_cache(ckv, kpe_slab, p):
    rows = ckv.shape[0]
    return pl.pallas_call(
        _mla_cache_kernel,
        out_shape=[jax.ShapeDtypeStruct((rows, N_HEADS * SLAB), BF16)] * 2,
        compiler_params=pltpu.CompilerParams(vmem_limit_bytes=VMEM_LIMIT),
        name='mla_cache',
    )(ckv, kpe_slab, p['wuk'], p['wuv'])


def _attend_heads(q_ref, kv_refs, o_ref):
    tq = q_ref.shape[0]
    low = lax.broadcasted_iota(jnp.int32, (tq, SLAB), 1) < B_VDIM

    def head(h):
        qh = q_ref[:, h * SLAB:(h + 1) * SLAB]
        ss = [_dot_nt(qh, k_ref[:, h * SLAB:(h + 1) * SLAB]) for k_ref, _ in kv_refs]
        yield
        m = ss[0].max(axis=-1, keepdims=True)
        for s in ss[1:]:
            m = jnp.maximum(m, s.max(axis=-1, keepdims=True))
        es = [jnp.exp2((s - m).astype(BF16)) for s in ss]
        yield
        acc = jnp.zeros((tq, SLAB), F32)
        for e, (_, v_ref) in zip(es, kv_refs):
            acc += _dot(e, v_ref[:, h * SLAB:(h + 1) * SLAB])
        return acc / acc[:, B_VDIM:B_VDIM + 1]

    outs = _interleave([head(h) for h in range(N_HEADS)])
    pairs = [jnp.where(low, outs[h], pltpu.roll(outs[h + 1], B_VDIM, 1)) for h in range(0, N_HEADS, 2)]
    o_ref[...] = jnp.concatenate(pairs, axis=-1)


def _attn_kernel(n_ctx, q_ref, kx_ref, vx_ref, k_ref, v_ref, kc_ref, vc_ref, o_ref):
    i = pl.program_id(0)

    @pl.when(i < n_ctx)
    def _():
        _attend_heads(q_ref, [(kx_ref, vx_ref)], o_ref)

    @pl.when(i >= n_ctx)
    def _():
        _attend_heads(q_ref, [(k_ref, v_ref), (kc_ref, vc_ref)], o_ref)


def _attn(q, k, v, kc, vc, n_ctx, ctx_len, n_dec, dec_len, past):
    tq = ctx_len
    nq = dec_len // tq
    slabs = N_HEADS * SLAB
    dec = lambda i: jnp.maximum(i - n_ctx, 0)

    def q_map(i):
        return (jnp.where(i < n_ctx, 0, 1 + dec(i) // nq), jnp.where(i < n_ctx, i, dec(i) % nq), 0)

    ctx_kv = pl.BlockSpec((None, ctx_len, slabs), lambda i: (0, jnp.minimum(i, n_ctx - 1), 0))
    dec_kv = pl.BlockSpec((None, dec_len, slabs), lambda i: (1 + dec(i) // nq, 0, 0))
    cache_kv = pl.BlockSpec((past, slabs), lambda i: (dec(i) // nq, 0))
    return pl.pallas_call(
        functools.partial(_attn_kernel, n_ctx),
        grid=(n_ctx + n_dec * nq,),
        in_specs=[pl.BlockSpec((None, tq, slabs), q_map), ctx_kv, ctx_kv, dec_kv, dec_kv, cache_kv, cache_kv],
        out_specs=pl.BlockSpec((None, tq, WIDTH), q_map),
        out_shape=jax.ShapeDtypeStruct(q.shape[:2] + (WIDTH,), F32),
        compiler_params=_cp(("arbitrary",)),
        name='attn',
    )(q, k, v, k, v, kc, vc)


def _rope_tables(n_ctx_rows, n_dec, dec_len):
    rows = dec_len // GRID_W
    row = np.repeat(np.arange(rows, dtype=np.float32), GRID_W)
    col = np.tile(np.arange(GRID_W, dtype=np.float32), rows)
    half = B_ROPE // 2
    inv = jnp.power(ROPE_BASE, -jnp.arange(0, half, 2, dtype=F32) / half)
    ang_r = jnp.asarray(row)[:, None] * inv
    ang_c = jnp.asarray(col)[:, None] * inv
    ang = jnp.concatenate([ang_r, ang_r, ang_c, ang_c], axis=-1)
    cos = jnp.pad(jnp.cos(ang), ((0, 0), (0, LANE - B_ROPE)), constant_values=1.0)
    sin = jnp.pad(jnp.sin(ang), ((0, 0), (0, LANE - B_ROPE)))
    cos = jnp.concatenate([jnp.ones((n_ctx_rows, LANE), F32)] + [cos] * n_dec, axis=0)
    sin = jnp.concatenate([jnp.zeros((n_ctx_rows, LANE), F32)] + [sin] * n_dec, axis=0)
    return cos, sin


def _block_diag(w):
    g, n, m = w.shape[-3:]
    eye = jnp.eye(g, dtype=w.dtype)
    out = w[..., :, :, None, :] * eye[:, None, :, None]
    return out.reshape(w.shape[:-3] + (g * n, g * m))


def kernel(x_prompt, x_sample, cache_mla_ckv, cache_mla_kpe, state_rwkv, state_rglru, state_hgrn, c, c_ctx,
           w_ada, b_ada, ln_g, ln_b, w_ffn_in, w_ffn_out, w_in, w_out,
           rwkv_w0, rwkv_w2, rwkv_a0, rwkv_a2, rwkv_g2, rwkv_kk, rwkv_ka, rwkv_rk, rwkv_gn_g, rwkv_gn_b,
           mla_qn_g, mla_w_uq, mla_kvn_g, mla_w_ukv,
           rglru_conv_w, rglru_conv_b, rglru_wa, rglru_ba, rglru_wx, rglru_bx, rglru_lam,
           hgrn_lb, hgrn_gn_g):
    n_ctx, ctx_len, _ = x_prompt.shape
    n_dec, dec_len, _ = x_sample.shape
    past = cache_mla_ckv.shape[2]
    seg = n_ctx * ctx_len
    assert seg == dec_len, "context rows must form one segment of the decode sequence length"
    assert ctx_len & (ctx_len - 1) == 0 and dec_len & (dec_len - 1) == 0, "sequence lengths must be powers of two"
    n_seg = 1 + n_dec
    tl = _Tiles(n_seg, seg, min(TM, seg))
    assert ctx_len % LRU_CHUNK == 0
    st = _Streams(n_seg, n_ctx, seg, ctx_len, CHUNK)
    lru_st = _Streams(n_seg, n_ctx, seg, ctx_len, LRU_CHUNK)

    hc = {k: jnp.asarray(v) for k, v in _head_consts().items()}
    ebd = hc['mbd']
    cos, sin = (t.reshape(n_seg, seg, LANE) for t in _rope_tables(seg, n_dec, dec_len))

    assert n_seg <= SUBLANE
    cond8 = jnp.zeros((SUBLANE, D_MODEL), F32).at[0].set(c_ctx).at[1:1 + n_dec].set(c)
    mods = _ada(cond8, w_ada, b_ada).reshape(DEPTH, SUBLANE, N_MOD, D_MODEL)
    mods = jnp.transpose(mods, (0, 2, 1, 3))[:, :, :n_seg, None, :]

    w_ffn_in_bf = w_ffn_in.astype(BF16)
    w_ffn_out_bf = w_ffn_out.astype(BF16)
    w_out_bf = w_out.astype(BF16)
    kpe_end = ZA + B_Q_RANK + B_KV_RANK + B_ROPE
    w_ab = jnp.pad(w_in[:, :, :kpe_end].astype(BF16), ((0, 0), (0, 0), (0, LANE - B_ROPE)))
    w_cd = w_in[:, :, kpe_end:].astype(BF16)
    wuq = mla_w_uq.reshape(DEPTH, B_Q_RANK, N_HEADS, B_NOPE + B_ROPE)
    wuq_p = jnp.concatenate([wuq[..., B_NOPE:], wuq[..., :B_NOPE],
                             jnp.zeros((DEPTH, B_Q_RANK, N_HEADS, SLAB - B_NOPE - B_ROPE), F32)], axis=-1)
    wuq_p = wuq_p.reshape(DEPTH, B_Q_RANK, N_HEADS * SLAB).astype(BF16)
    wukv = mla_w_ukv.reshape(DEPTH, B_KV_RANK, N_HEADS, B_NOPE + B_VDIM)
    wuk_p = jnp.concatenate([jnp.zeros((DEPTH, B_KV_RANK, N_HEADS, B_ROPE), F32), wukv[..., :B_NOPE],
                             jnp.zeros((DEPTH, B_KV_RANK, N_HEADS, SLAB - B_NOPE - B_ROPE), F32)], axis=-1)
    wuk_p = wuk_p.reshape(DEPTH, B_KV_RANK, N_HEADS * SLAB).astype(BF16)
    wuv_p = jnp.concatenate([wukv[..., B_NOPE:], jnp.zeros((DEPTH, B_KV_RANK, N_HEADS, SLAB - B_VDIM), F32)], axis=-1)
    wuv_p = wuv_p.reshape(DEPTH, B_KV_RANK, N_HEADS * SLAB).astype(BF16)
    lru_wg = jnp.concatenate([_block_diag(rglru_wa[:, 0]), _block_diag(rglru_wx[:, 0]),
                              _block_diag(rglru_wa[:, 1]), _block_diag(rglru_wx[:, 1])], axis=-1)
    lru_bg = jnp.concatenate([rglru_ba[:, 0], rglru_bx[:, 0], rglru_ba[:, 1], rglru_bx[:, 1]], axis=-1)[:, None, :]
    lbs = _lower_bounds(hgrn_lb)

    rwkv_s0 = _block_diag(state_rwkv)
    hgrn_s0 = _block_diag(jnp.swapaxes(state_hgrn, -1, -2))
    cache_kpe_slab = jnp.pad(cache_mla_kpe, ((0, 0), (0, 0), (0, 0), (0, LANE - B_ROPE)))

    new_ckv, new_kpe, new_rwkv, new_lru, new_hgrn = [], [], [], [], []
    for l in range(DEPTH):
        m = mods[l]
        lng = ln_g[l][:, None, :]
        lnb = ln_b[l][:, None, :]
        ffn0 = functools.partial(_ffn, sh=m[0], sc=m[1], g=m[2], lng=lng[0], lnb=lnb[0], w_in_bf=w_ffn_in_bf,
                                 w_out_bf=w_ffn_out_bf, l=l, f=0, tl=tl)
        if l == 0:
            x = ffn0((x_prompt.reshape(1, seg, D_MODEL), x_sample), split='in')
        else:
            x = ffn0(x)
        pa = {'w0': rwkv_w0[l][:, None, :], 'w2': rwkv_w2[l], 'a0': rwkv_a0[l][:, None, :], 'a2': rwkv_a2[l],
              'g2': rwkv_g2[l], 'kk': rwkv_kk[l][None, :], 'ka': rwkv_ka[l][None, :], 'rk': rwkv_rk[l][None, :]}
        pb = {'qn_g': mla_qn_g[l][None, :], 'kvn_g': mla_kvn_g[l][None, :], 'wuq': wuq_p[l], 'wuk': wuk_p[l],
              'wuv': wuv_p[l]}
        a_r, a_v, pre, kpe_raw, (ckvn, q_all, k_all, v_all), zc, zd = _inproj(x, m[3], m[4], w_ab, w_cd, l, pa, pb,
                                                                                ebd, cos, sin, tl)

        (oa_f, oa_b, sa_fin), (od_f, od_b, sd_fin) = _mix_scan(a_r, a_v, pre[:7], zd, lbs[l], rwkv_s0[:, l],
                                                                hgrn_s0[:, l], hc, st)
        new_rwkv.append(sa_fin)
        new_hgrn.append(sd_fin)

        kc, vc = _mla_cache(cache_mla_ckv[:, l].reshape(n_dec * past, B_KV_RANK),
                            cache_kpe_slab[:, l].reshape(n_dec * past, LANE), pb)
        yb = _attn(q_all, k_all, v_all, kc, vc, n_ctx, ctx_len, n_dec, dec_len, past)
        new_ckv.append(ckvn[0].reshape(n_ctx, ctx_len, B_KV_RANK))
        new_kpe.append(kpe_raw[0, :, :B_ROPE].reshape(n_ctx, ctx_len, B_ROPE))

        pc = {'conv_w': rglru_conv_w[l], 'conv_b': rglru_conv_b[l][None, :], 'wg': lru_wg[l], 'bg': lru_bg[l],
              'lam': rglru_lam[l][:, None, :]}
        h_f, h_b, h_fin = _lru_scan(zc, pc, state_rglru[:, l], ctx_len, dec_len, lru_st)
        new_lru.append(h_fin)

        x = _outproj(x, oa_f, oa_b, pre[7], pre[8], yb, h_f, h_b, zc, od_f, od_b, zd,
                     rwkv_gn_g[l][None, :], rwkv_gn_b[l][None, :], hgrn_gn_g[l][None, :], ebd,
                     w_out_bf, m[5], lng[1], lnb[1], l, tl)
        ffn1 = functools.partial(_ffn, sh=m[6], sc=m[7], g=m[8], lng=lng[2], lnb=lnb[2], w_in_bf=w_ffn_in_bf,
                                 w_out_bf=w_ffn_out_bf, l=l, f=1, tl=tl)
        if l == DEPTH - 1:
            y_prompt, y_sample = ffn1(x, split='out')
            y_prompt = y_prompt.reshape(n_ctx, ctx_len, D_MODEL)
        else:
            x = ffn1(x)
    return (y_prompt, y_sample, jnp.stack(new_ckv, axis=1), jnp.stack(new_kpe, axis=1),
            jnp.stack(new_rwkv, axis=1), jnp.stack(new_lru, axis=1), jnp.stack(new_hgrn, axis=1))
```
